```python
import jax, jax.numpy as jnp
from jax import lax
import numpy as np

D_MODEL = 2048
BATCH = 4
SEQ = 2048
DEPTH = 1

D_MIX = D_MODEL
HGRN_HEADS = 8
HGRN_DK = 128
HGRN_DV = 128
HGRN_WIDTH = HGRN_HEADS * HGRN_DV
HGRN_KWIDTH = HGRN_HEADS * HGRN_DK
HGRN_CHUNK = 64
MLA_HEADS = 8
MLA_Q_LORA = 512
MLA_KV_LORA = 256
MLA_NOPE = 128
MLA_ROPE = 64
MLA_V = 128
MLA_QK = MLA_NOPE + MLA_ROPE
MLA_WIDTH = MLA_HEADS * MLA_V
ROPE_BASE = 10000.0
ATTN_BLOCK = 128
N_EXPERTS = 16
EC_CAPACITY = 2
EXPERT_FF = 2048
EPS = 1e-6
IN_SIZES = (HGRN_KWIDTH, HGRN_KWIDTH, HGRN_KWIDTH, HGRN_WIDTH, HGRN_WIDTH, MLA_Q_LORA, MLA_KV_LORA, MLA_ROPE)
D_IN = HGRN_KWIDTH * 3 + HGRN_WIDTH * 2 + MLA_Q_LORA + MLA_KV_LORA + MLA_ROPE

kernel_name = "hymba_hgrn2_mla_expert_choice_encoder_layer"


def _split_points():
    pts, acc = [], 0
    for s in IN_SIZES[:-1]:
        acc += s
        pts.append(acc)
    return pts


def rmsnorm(x, g):
    xf = x.astype(jnp.float32)
    y = xf * lax.rsqrt(jnp.mean(xf * xf, axis=-1, keepdims=True) + EPS)
    return (y * g.astype(jnp.float32)).astype(x.dtype)


def rope_tables(positions):
    inv_freq = 1.0 / (ROPE_BASE ** (jnp.arange(0, MLA_ROPE, 2, dtype=jnp.float32) / MLA_ROPE))
    ang = positions.astype(jnp.float32)[..., None] * inv_freq
    return jnp.cos(ang)[:, :, None, :], jnp.sin(ang)[:, :, None, :]


def apply_rope(x, cos, sin):
    xf = x.astype(jnp.float32)
    half = MLA_ROPE // 2
    x1, x2 = xf[..., :half], xf[..., half:]
    return jnp.concatenate([x1 * cos - x2 * sin, x2 * cos + x1 * sin], axis=-1).astype(x.dtype)


def hgrn2_bidirectional(q, i, zf_fwd, zf_bwd, lb_fwd, lb_bwd):
    B, S, _ = q.shape
    f32 = jnp.float32
    L = HGRN_CHUNK
    N = S // L

    def log_forget(z, lb):
        lb = lb.astype(f32)
        return jnp.log(lb + (1.0 - lb) * jax.nn.sigmoid(z.astype(f32)))

    qf = q.astype(f32)
    vf = i.astype(f32)
    q2 = jnp.concatenate([qf, qf[:, ::-1]], axis=0)
    v2 = jnp.concatenate([vf, vf[:, ::-1]], axis=0)
    g2 = jnp.concatenate([log_forget(zf_fwd, lb_fwd), log_forget(zf_bwd, lb_bwd)[:, ::-1]], axis=0)
    k2 = -jnp.expm1(g2)

    def to_chunks(t, d):
        return t.reshape(2 * B, N, L, HGRN_HEADS, d).transpose(1, 0, 3, 2, 4)

    xs = (to_chunks(q2, HGRN_DK), to_chunks(k2, HGRN_DK), to_chunks(v2, HGRN_DV), to_chunks(g2, HGRN_DK))
    mask = jnp.tril(jnp.ones((L, L), dtype=bool))[:, :, None]

    def step(state, inp):
        qc, kc, vc, gc = inp
        b = jnp.cumsum(gc, axis=2)
        o_inter = jnp.einsum('zhtk,zhkv->zhtv', qc * jnp.exp(b), state)
        diff = b[:, :, :, None, :] - b[:, :, None, :, :]
        decay = jnp.exp(jnp.where(mask, diff, -jnp.inf))
        att = jnp.einsum('zhtsk,zhsk->zhts', decay * qc[:, :, :, None, :], kc)
        o = o_inter + jnp.einsum('zhts,zhsv->zhtv', att, vc)
        b_last = b[:, :, -1:, :]
        state = jnp.exp(b_last[:, :, 0, :])[..., None] * state + \
            jnp.einsum('zhsk,zhsv->zhkv', kc * jnp.exp(b_last - b), vc)
        return state, o

    state0 = jnp.zeros((2 * B, HGRN_HEADS, HGRN_DK, HGRN_DV), f32)
    _, o = lax.scan(step, state0, xs)
    o = o.transpose(1, 0, 3, 2, 4).reshape(2 * B, S, HGRN_HEADS, HGRN_DV)
    return o[:B] + o[B:, ::-1]


def mla_bidirectional(cq, ckv, kpe, positions, qa_norm_g, w_uq, kva_norm_g, w_ukv, q_head_g, k_head_g):
    B, S, _ = cq.shape
    q = (rmsnorm(cq, qa_norm_g) @ w_uq).reshape(B, S, MLA_HEADS, MLA_QK)
    kv = (rmsnorm(ckv, kva_norm_g) @ w_ukv).reshape(B, S, MLA_HEADS, MLA_NOPE + MLA_V)
    k_nope, v = kv[..., :MLA_NOPE], kv[..., MLA_NOPE:]
    k = jnp.concatenate([k_nope, jnp.broadcast_to(kpe[:, :, None, :], (B, S, MLA_HEADS, MLA_ROPE))], axis=-1)
    q = rmsnorm(q, q_head_g)
    k = rmsnorm(k, k_head_g)
    cos, sin = rope_tables(positions)
    q = jnp.concatenate([q[..., :MLA_NOPE], apply_rope(q[..., MLA_NOPE:], cos, sin)], axis=-1)
    k = jnp.concatenate([k[..., :MLA_NOPE], apply_rope(k[..., MLA_NOPE:], cos, sin)], axis=-1)

    nb = S // ATTN_BLOCK
    scale = MLA_QK ** -0.5
    qb = q.reshape(B, nb, ATTN_BLOCK, MLA_HEADS, MLA_QK).transpose(1, 0, 3, 2, 4)
    kt = k.transpose(0, 2, 1, 3)
    vt = v.transpose(0, 2, 1, 3)

    def attend(qblk):
        s = jnp.einsum('bhqd,bhkd->bhqk', qblk, kt).astype(jnp.float32) * scale
        p = jax.nn.softmax(s, axis=-1).astype(vt.dtype)
        return jnp.einsum('bhqk,bhkd->bhqd', p, vt)

    o = lax.map(attend, qb)
    return o.transpose(1, 0, 3, 2, 4).reshape(B, S, MLA_WIDTH)


def expert_choice_ffn(h, w_router, w_gate, w_up, w_down):
    B, S, D = h.shape
    cap = EC_CAPACITY * S // N_EXPERTS
    aff = jax.nn.softmax((h @ w_router).astype(jnp.float32), axis=-1)
    gates, idx = lax.top_k(aff.transpose(0, 2, 1), cap)
    xe = jax.vmap(lambda hb, ib: hb[ib])(h, idx)
    a = jnp.einsum('becd,edf->becf', xe, w_gate)
    u = jnp.einsum('becd,edf->becf', xe, w_up)
    y = jnp.einsum('becf,efd->becd', jax.nn.silu(a) * u, w_down)
    y = y * gates[..., None].astype(y.dtype)
    return jax.vmap(lambda ib, yb: jnp.zeros((S, D), yb.dtype).at[ib.reshape(-1)].add(yb.reshape(-1, D)))(idx, y)


def setup_inputs(seed: int = 0) -> dict:
    key = jax.random.key(seed)
    ks = jax.random.split(key, 24)
    f32 = jnp.float32
    D = D_MODEL

    def nrm(k, shape, fan_in):
        return jax.random.normal(k, shape, f32) * (fan_in ** -0.5)

    def gain(k, shape):
        return 1.0 + 0.02 * jax.random.normal(k, shape, f32)

    x = jax.random.normal(ks[0], (BATCH, SEQ, D), f32)
    c = jax.random.normal(ks[1], (BATCH, D), f32)
    offsets = jax.random.randint(ks[2], (BATCH, 1), 0, 1024, dtype=jnp.int32)
    positions = jnp.arange(SEQ, dtype=jnp.int32)[None, :] + offsets
    return {
        "x": x,
        "c": c,
        "positions": positions,
        "w_ada": 0.5 * nrm(ks[3], (DEPTH, D, 6 * D), D),
        "b_ada": 0.02 * jax.random.normal(ks[4], (DEPTH, 6 * D), f32),
        "norm1_g": gain(ks[5], (DEPTH, D)),
        "w_in": nrm(ks[6], (DEPTH, D, D_IN), D),
        "lb_logits": 0.5 * jax.random.normal(ks[7], (2, DEPTH + 1, HGRN_KWIDTH), f32),
        "hgrn_out_g": gain(ks[8], (DEPTH, HGRN_HEADS, HGRN_DV)),
        "qa_norm_g": gain(ks[9], (DEPTH, MLA_Q_LORA)),
        "w_uq": nrm(ks[10], (DEPTH, MLA_Q_LORA, MLA_HEADS * MLA_QK), MLA_Q_LORA),
        "kva_norm_g": gain(ks[11], (DEPTH, MLA_KV_LORA)),
        "w_ukv": nrm(ks[12], (DEPTH, MLA_KV_LORA, MLA_HEADS * (MLA_NOPE + MLA_V)), MLA_KV_LORA),
        "q_head_g": gain(ks[13], (DEPTH, MLA_QK)),
        "k_head_g": gain(ks[14], (DEPTH, MLA_QK)),
        "w_out": nrm(ks[15], (DEPTH, D_MIX, D), D_MIX),
        "norm2_g": gain(ks[16], (DEPTH, D)),
        "w_router": nrm(ks[17], (DEPTH, D, N_EXPERTS), D),
        "w_gate": nrm(ks[18], (DEPTH, N_EXPERTS, D, EXPERT_FF), D),
        "w_up": nrm(ks[19], (DEPTH, N_EXPERTS, D, EXPERT_FF), D),
        "w_down": nrm(ks[20], (DEPTH, N_EXPERTS, EXPERT_FF, D), EXPERT_FF),
    }


def reference(x, c, positions, w_ada, b_ada, norm1_g, w_in, lb_logits, hgrn_out_g, qa_norm_g, w_uq,
              kva_norm_g, w_ukv, q_head_g, k_head_g, w_out, norm2_g, w_router, w_gate, w_up, w_down):
    B, S, D = x.shape
    lb = jnp.cumsum(jax.nn.softmax(lb_logits.astype(jnp.float32), axis=1), axis=1)
    split_pts = _split_points()
    for l in range(DEPTH):
        mod = (jax.nn.silu(c) @ w_ada[l] + b_ada[l])[:, None, :]
        shift1, scale1, gate1, shift2, scale2, gate2 = jnp.split(mod, 6, axis=-1)

        h = rmsnorm(x, norm1_g[l]) * (1.0 + scale1) + shift1
        proj = h @ w_in[l]
        hq, hf_fwd, hf_bwd, hi, hg, cq, ckv, kpe = jnp.split(proj, split_pts, axis=-1)

        o_hgrn = hgrn2_bidirectional(hq, hi, hf_fwd, hf_bwd, lb[0, l], lb[1, l])
        o_hgrn = rmsnorm(o_hgrn, hgrn_out_g[l]).astype(x.dtype) * \
            jax.nn.silu(hg).reshape(B, S, HGRN_HEADS, HGRN_DV)
        o_hgrn = o_hgrn.reshape(B, S, HGRN_WIDTH)

        o_mla = mla_bidirectional(cq, ckv, kpe, positions, qa_norm_g[l], w_uq[l], kva_norm_g[l],
                                  w_ukv[l], q_head_g[l], k_head_g[l])

        mixed = jnp.concatenate([o_hgrn, o_mla], axis=-1)
        x = x + gate1 * (mixed @ w_out[l])

        h2 = rmsnorm(x, norm2_g[l]) * (1.0 + scale2) + shift2
        x = x + gate2 * expert_choice_ffn(h2, w_router[l], w_gate[l], w_up[l], w_down[l])
    return x
```

```python
import functools
import math

import jax
import jax.numpy as jnp
from jax import lax
from jax.experimental import pallas as pl
from jax.experimental.pallas import tpu as pltpu

F32 = jnp.float32
BF16 = jnp.bfloat16
EPS = 1e-6
ROPE_BASE = 10000.0
EC_CAPACITY = 2
LANES = 128
SUB = 16
BLK = 256
VMEM_LIMIT = 56 * 1024 * 1024


def _cparams(*sem):
    return pltpu.CompilerParams(dimension_semantics=sem, vmem_limit_bytes=VMEM_LIMIT)


def _dot(a, b):
    return jnp.dot(a, b, preferred_element_type=F32)


def _dot_nt(a, b):
    return lax.dot_general(a, b, (((1,), (1,)), ((), ())), preferred_element_type=F32)


def _dot_tn(a, b):
    return lax.dot_general(a, b, (((0,), (0,)), ((), ())), preferred_element_type=F32)


def _split2(a):
    hi = a.astype(BF16)
    lo = (a - hi.astype(F32)).astype(BF16)
    return hi, lo


def _split3(a):
    p1 = a.astype(BF16)
    r1 = a - p1.astype(F32)
    p2 = r1.astype(BF16)
    p3 = (r1 - p2.astype(F32)).astype(BF16)
    return p1, p2, p3


def _dot_hi(a, b):
    ah, al = _split2(a)
    bh, bl = _split2(b)
    return _dot(ah, bh) + (_dot(ah, bl) + _dot(al, bh))


def _silu(x):
    return x * jax.nn.sigmoid(x)


def _rope_kernel(pos_ref, cs_ref, *, half):
    pos = pos_ref[...].astype(F32)
    lane = lax.broadcasted_iota(jnp.int32, (1, 4 * half), 1)
    j = (lane & (half - 1)).astype(F32)
    inv_freq = jnp.exp(j * (-2.0 * math.log(ROPE_BASE) / (2 * half)))
    ang = pos * inv_freq
    c = jnp.cos(ang)
    s = jnp.sin(ang)
    cs_ref[...] = jnp.where(lane < 2 * half, c, jnp.where(lane < 3 * half, -s, s))


def _rope_tables(positions, rope):
    m = positions.size
    tm = min(m, 1024)
    half = rope // 2
    return pl.pallas_call(
        functools.partial(_rope_kernel, half=half),
        out_shape=jax.ShapeDtypeStruct((m, 2 * rope), F32),
        grid=(m // tm,),
        in_specs=[pl.BlockSpec((tm, 1), lambda i: (i, 0))],
        out_specs=pl.BlockSpec((tm, 2 * rope), lambda i: (i, 0)),
        compiler_params=_cparams("parallel"),
        name="rope_tables",
    )(positions.reshape(m, 1))


def _ada_kernel(c_ref, w_ref, b_ref, o_ref):
    o_ref[...] = _dot_hi(_silu(c_ref[...]), w_ref[...]) + b_ref[...]


def _ada(c8, w, b):
    d, n = w.shape
    tn = min(d, 1024)
    assert n % tn == 0
    return pl.pallas_call(
        _ada_kernel,
        out_shape=jax.ShapeDtypeStruct((c8.shape[0], n), F32),
        grid=(n // tn,),
        in_specs=[pl.BlockSpec((c8.shape[0], d), lambda j: (0, 0)),
                  pl.BlockSpec((d, tn), lambda j: (0, j)),
                  pl.BlockSpec((1, tn), lambda j: (0, j))],
        out_specs=pl.BlockSpec((c8.shape[0], tn), lambda j: (0, j)),
        compiler_params=_cparams("parallel"),
        name="ada_mod",
    )(c8, w, b.reshape(1, n))


def _inproj_kernel(x_ref, mod_ref, g_ref, w_ref, wk_ref, lbl_ref,
                   proj_ref, glog_ref, kpe_ref, h_scr, *, layer):
    j = pl.program_id(1)

    @pl.when(j == 0)
    def _():
        x = x_ref[...]
        r = lax.rsqrt(jnp.mean(x * x, axis=-1, keepdims=True) + EPS)
        h = x * r * g_ref[...] * (1.0 + mod_ref[1:2, :]) + mod_ref[0:1, :]
        hb = h.astype(BF16)
        h_scr[...] = hb
        kpe_ref[...] = _dot(hb, wk_ref[...])

    acc = _dot(h_scr[...], w_ref[...])
    proj_ref[...] = acc.astype(BF16)

    @pl.when((j == 1) | (j == 2))
    def _():
        lg = lbl_ref[...]
        e = jnp.exp(lg - jnp.max(lg, axis=0, keepdims=True))
        lb = jnp.sum(e[:layer + 1], axis=0, keepdims=True) / jnp.sum(e, axis=0, keepdims=True)
        glog_ref[...] = jnp.log(lb + (1.0 - lb) * jax.nn.sigmoid(acc))


def _inproj(xf, mod6, g1, w_in_b, wk_b, lb_logits, seq, hk, layer):
    m, d = xf.shape
    d_in = w_in_b.shape[1]
    tm = min(seq, 512)
    tpb = seq // tm
    tn = hk
    nj = pl.cdiv(d_in, tn)
    nl = lb_logits.shape[1]
    fdir = lambda j: jnp.clip(j - 1, 0, 1)
    return pl.pallas_call(
        functools.partial(_inproj_kernel, layer=layer),
        out_shape=(jax.ShapeDtypeStruct((m, d_in), BF16),
                   jax.ShapeDtypeStruct((m, 2 * hk), F32),
                   jax.ShapeDtypeStruct((m, LANES), F32)),
        grid=(m // tm, nj),
        in_specs=[pl.BlockSpec((tm, d), lambda i, j: (i, 0)),
                  pl.BlockSpec((None, 6, d), lambda i, j: (i // tpb, 0, 0)),
                  pl.BlockSpec((1, d), lambda i, j: (0, 0)),
                  pl.BlockSpec((d, tn), lambda i, j: (0, j)),
                  pl.BlockSpec((d, LANES), lambda i, j: (0, 0)),
                  pl.BlockSpec((None, nl, tn), lambda i, j: (fdir(j), 0, 0))],
        out_specs=(pl.BlockSpec((tm, tn), lambda i, j: (i, j)),
                   pl.BlockSpec((tm, tn), lambda i, j: (i, fdir(j))),
                   pl.BlockSpec((tm, LANES), lambda i, j: (i, 0))),
        scratch_shapes=[pltpu.VMEM((tm, d), BF16)],
        compiler_params=_cparams("parallel", "arbitrary"),
        name="norm1_inproj",
    )(xf, mod6, g1, w_in_b, wk_b, lb_logits)


def _hgrn_kernel(q_ref, v_ref, gf_ref, gb_ref, of_ref, ob_ref,
                 tri_scr, b_scr, tot_scr, qf_scr, kk_scr, vf_scr, qt_scr, kt_scr, vb_scr, st_scr):
    seq = q_ref.shape[0]
    nblk = seq // BLK
    nsub = BLK // SUB
    g_refs = (gf_ref, gb_ref)
    o_refs = (of_ref, ob_ref)

    r = lax.broadcasted_iota(jnp.int32, (BLK, BLK), 0)
    c = lax.broadcasted_iota(jnp.int32, (BLK, BLK), 1)
    shift = SUB.bit_length() - 1
    same = (r >> shift) == (c >> shift)
    ind = lambda mask: jnp.where(same & mask, 1.0, 0.0).astype(BF16)
    tri_scr[0] = ind(c <= r)
    tri_scr[1] = ind(c > r)
    tri_scr[2] = ind(c >= r)
    tri_scr[3] = ind(c < r)
    st_scr[...] = jnp.zeros_like(st_scr)

    row = lax.broadcasted_iota(jnp.int32, (SUB, 1), 0)
    half = SUB // 2

    def cum(tri, parts):
        return _dot(tri, parts[0]) + (_dot(tri, parts[1]) + _dot(tri, parts[2]))

    def blk_body(i, carry):
        for d in (0, 1):
            blk = i if d == 0 else nblk - 1 - i
            r0 = pl.multiple_of(blk * BLK, BLK)
            g = g_refs[d][pl.ds(r0, BLK), :]
            parts = _split3(g)
            b = cum(tri_scr[2 * d], parts)
            rest = cum(tri_scr[2 * d + 1], parts)
            qf = q_ref[pl.ds(r0, BLK), :].astype(F32)
            vb = v_ref[pl.ds(r0, BLK), :]
            kk = 1.0 - jnp.exp(g)
            b_scr[d] = b
            tot_scr[d] = b + rest
            qf_scr[d] = qf
            kk_scr[d] = kk
            vb_scr[d] = vb
            vf_scr[d] = vb.astype(F32)
            qt_scr[d] = (qf * jnp.exp(b)).astype(BF16)
            kt_scr[d] = (kk * jnp.exp(rest)).astype(BF16)

        def sub_body(j, carry2):
            for d in (0, 1):
                jj = j if d == 0 else nsub - 1 - j
                blk = i if d == 0 else nblk - 1 - i
                rl = pl.multiple_of(jj * SUB, SUB)
                rg = pl.multiple_of(blk * BLK + jj * SUB, SUB)
                st = st_scr[d]
                o_int = _dot_nt(qt_scr[d, pl.ds(rl, SUB), :], st.astype(BF16))
                u = _dot_tn(vb_scr[d, pl.ds(rl, SUB), :], kt_scr[d, pl.ds(rl, SUB), :])
                dvec = jnp.exp(tot_scr[d, pl.ds(rl, 1), :])
                st_scr[d] = st * dvec + u

                bj = b_scr[d, pl.ds(rl, SUB), :]
                qj = qf_scr[d, pl.ds(rl, SUB), :]
                acc = [jnp.zeros((half, LANES), F32), jnp.zeros((half, LANES), F32)]
                for s in range(SUB):
                    bs = b_scr[d, pl.ds(rl + s, 1), :]
                    ks = kk_scr[d, pl.ds(rl + s, 1), :]
                    vs = vf_scr[d, pl.ds(rl + s, 1), :]
                    if d == 0:
                        halves = (0, 1) if s < half else (1,)
                    else:
                        halves = (0,) if s < half else (0, 1)
                    for hf in halves:
                        lo = hf * half
                        e = jnp.exp(bj[lo:lo + half] - bs)
                        a = jnp.sum(qj[lo:lo + half] * e * ks, axis=-1, keepdims=True)
                        rows = row[lo:lo + half]
                        keep = (rows >= s) if d == 0 else (rows <= s)
                        acc[hf] = acc[hf] + jnp.where(keep, a, 0.0) * vs
                o_refs[d][pl.ds(rg, SUB), :] = o_int + jnp.concatenate(acc, axis=0)
            return carry2

        lax.fori_loop(0, nsub, sub_body, 0)
        return carry

    lax.fori_loop(0, nblk, blk_body, 0)


def _hgrn(proj, glog, batch, seq, heads, hk):
    m = proj.shape[0]
    nh = hk // LANES
    vcol = 3 * nh
    blk = lambda off: pl.BlockSpec((seq, LANES), lambda b, h: (b, off + h))
    out = jax.ShapeDtypeStruct((m, hk), F32)
    return pl.pallas_call(
        _hgrn_kernel,
        out_shape=(out, out),
        grid=(batch, heads),
        in_specs=[blk(0), blk(vcol), blk(0), blk(nh)],
        out_specs=(blk(0), blk(0)),
        scratch_shapes=[pltpu.VMEM((4, BLK, BLK), BF16),
                        pltpu.VMEM((2, BLK, LANES), F32),
                        pltpu.VMEM((2, BLK, LANES), F32),
                        pltpu.VMEM((2, BLK, LANES), F32),
                        pltpu.VMEM((2, BLK, LANES), F32),
                        pltpu.VMEM((2, BLK, LANES), F32),
                        pltpu.VMEM((2, BLK, LANES), BF16),
                        pltpu.VMEM((2, BLK, LANES), BF16),
                        pltpu.VMEM((2, BLK, LANES), BF16),
                        pltpu.VMEM((2, LANES, LANES), F32)],
        compiler_params=_cparams("parallel", "parallel"),
        name="hgrn2_scan",
    )(proj, proj, glog, glog)


def _mla_proj_kernel(cq_ref, ckv_ref, kpe_ref, cs_ref, qag_ref, kvag_ref,
                     wqn_ref, wqr_ref, wkn_ref, wv_ref, qgn_ref, qgr_ref, kgn_ref, kgr_ref,
                     q_out, k_out, v_out, cq_scr, ckv_scr, *, scale, qk_dim, rope):
    h = pl.program_id(1)

    @pl.when(h == 0)
    def _():
        cq = cq_ref[...].astype(F32)
        cq_scr[...] = (cq * lax.rsqrt(jnp.mean(cq * cq, axis=-1, keepdims=True) + EPS)
                       * qag_ref[...]).astype(BF16)
        ckv = ckv_ref[...].astype(F32)
        ckv_scr[...] = (ckv * lax.rsqrt(jnp.mean(ckv * ckv, axis=-1, keepdims=True) + EPS)
                        * kvag_ref[...]).astype(BF16)

    cs = cs_ref[...]
    lane = lax.broadcasted_iota(jnp.int32, cs.shape, 1)
    lo = lane < rope

    def head(nope, rr, gn, gr, mult):
        ss = (jnp.sum(nope * nope, axis=-1, keepdims=True)
              + jnp.sum(jnp.where(lo, rr * rr, 0.0), axis=-1, keepdims=True))
        rinv = lax.rsqrt(ss / qk_dim + EPS) * mult
        y = rr * gr * cs
        y = y + pltpu.roll(y, rope, 1)
        return (nope * gn * rinv).astype(BF16), jnp.where(lo, y * rinv, 0.0).astype(BF16)

    a = cq_scr[...]
    qn, qr = head(_dot(a, wqn_ref[...]), _dot(a, wqr_ref[...]), qgn_ref[...], qgr_ref[...], scale)
    q_out[:, :LANES] = qn
    q_out[:, LANES:] = qr
    kv = ckv_scr[...]
    kn, kr = head(_dot(kv, wkn_ref[...]), kpe_ref[...], kgn_ref[...], kgr_ref[...], 1.0)
    k_out[:, :LANES] = kn
    k_out[:, LANES:] = kr
    v_out[...] = _dot(kv, wv_ref[...]).astype(BF16)


def _mla_proj(proj, kpe2, cs, qag, kvag, wqn, wqr, wkn, wv, qgn, qgr, kgn, kgr,
              batch, seq, cq_off, ckv_off, qk_dim, rope):
    m = proj.shape[0]
    mh, ql, _ = wqn.shape
    kvl = wkn.shape[1]
    tm = min(seq, 512)
    tpb = seq // tm
    assert cq_off % ql == 0 and ckv_off % kvl == 0
    wspec = lambda k: pl.BlockSpec((None, k, LANES), lambda i, h: (h, 0, 0))
    vec = lambda n: pl.BlockSpec((1, n), lambda i, h: (0, 0))
    hspec = lambda n: pl.BlockSpec((None, None, tm, n), lambda i, h: (i // tpb, h, i % tpb, 0))
    return pl.pallas_call(
        functools.partial(_mla_proj_kernel, scale=qk_dim ** -0.5, qk_dim=float(qk_dim), rope=rope),
        out_shape=(jax.ShapeDtypeStruct((batch, mh, seq, 2 * LANES), BF16),
                   jax.ShapeDtypeStruct((batch, mh, seq, 2 * LANES), BF16),
                   jax.ShapeDtypeStruct((batch, mh, seq, LANES), BF16)),
        grid=(m // tm, mh),
        in_specs=[pl.BlockSpec((tm, ql), lambda i, h: (i, cq_off // ql)),
                  pl.BlockSpec((tm, kvl), lambda i, h: (i, ckv_off // kvl)),
                  pl.BlockSpec((tm, LANES), lambda i, h: (i, 0)),
                  pl.BlockSpec((tm, LANES), lambda i, h: (i, 0)),
                  vec(ql), vec(kvl), wspec(ql), wspec(ql), wspec(kvl), wspec(kvl),
                  vec(LANES), vec(LANES), vec(LANES), vec(LANES)],
        out_specs=(hspec(2 * LANES), hspec(2 * LANES), hspec(LANES)),
        scratch_shapes=[pltpu.VMEM((tm, ql), BF16), pltpu.VMEM((tm, kvl), BF16)],
        compiler_params=_cparams("parallel", "arbitrary"),
        name="mla_head_proj",
    )(proj, proj, kpe2, cs, qag, kvag, wqn, wqr, wkn, wv, qgn, qgr, kgn, kgr)


def _attn_kernel(q_ref, k_ref, v_ref, o_ref):
    s = _dot_nt(q_ref[...], k_ref[...])
    p = jnp.exp(s - jnp.max(s, axis=-1, keepdims=True))
    l = jnp.sum(p, axis=-1, keepdims=True)
    o_ref[...] = (_dot(p.astype(BF16), v_ref[...]) / l).astype(BF16)


def _attention(qh, kh, vh):
    batch, mh, seq, dq = qh.shape
    dv = vh.shape[-1]
    tq = min(seq, 512)
    nq = seq // tq
    return pl.pallas_call(
        _attn_kernel,
        out_shape=jax.ShapeDtypeStruct((batch * seq, mh * dv), BF16),
        grid=(batch, mh, nq),
        in_specs=[pl.BlockSpec((None, None, tq, dq), lambda b, h, i: (b, h, i, 0)),
                  pl.BlockSpec((None, None, seq, dq), lambda b, h, i: (b, h, 0, 0)),
                  pl.BlockSpec((None, None, seq, dv), lambda b, h, i: (b, h, 0, 0))],
        out_specs=pl.BlockSpec((tq, dv), lambda b, h, i: (b * nq + i, h)),
        compiler_params=_cparams("parallel", "parallel", "arbitrary"),
        name="mla_attention",
    )(qh, kh, vh)


def _eye(rows, cols):
    r = lax.broadcasted_iota(jnp.int32, (rows, cols), 0)
    c = lax.broadcasted_iota(jnp.int32, (rows, cols), 1)
    return jnp.where(r == c, 1.0, 0.0).astype(BF16)


def _outproj_kernel(of_ref, ob_ref, hg_ref, og_ref, om_ref, w_ref, x_ref, mod_ref, g2_ref, wr_ref,
                    x1_ref, h2_ref, aff_ref, lat_ref, mix_scr, *, heads, n_exp):
    hw = of_ref.shape[1]
    o = of_ref[...] + ob_ref[...]
    gate = _silu(hg_ref[...].astype(F32))
    for h in range(heads):
        sl = slice(h * LANES, (h + 1) * LANES)
        oh = o[:, sl]
        r = lax.rsqrt(jnp.mean(oh * oh, axis=-1, keepdims=True) + EPS)
        mix_scr[:, sl] = (oh * r * og_ref[:, sl] * gate[:, sl]).astype(BF16)
    mix_scr[:, hw:] = om_ref[...]
    x1 = x_ref[...] + mod_ref[2:3, :] * _dot(mix_scr[...], w_ref[...])
    x1_ref[...] = x1
    r2 = lax.rsqrt(jnp.mean(x1 * x1, axis=-1, keepdims=True) + EPS)
    h2 = x1 * r2 * g2_ref[...] * (1.0 + mod_ref[4:5, :]) + mod_ref[3:4, :]
    h2_ref[...] = h2.astype(BF16)
    logits = _dot_hi(h2, wr_ref[...])
    lane = lax.broadcasted_iota(jnp.int32, logits.shape, 1)
    logits = jnp.where(lane < n_exp, logits, -jnp.inf)
    z = logits - jnp.max(logits, axis=-1, keepdims=True)
    p = jnp.exp(z)
    sp = jnp.sum(p, axis=-1, keepdims=True)
    aff_ref[...] = p / sp
    la = jnp.where(lane < n_exp, z - jnp.log(sp), 0.0)
    eye = _eye(n_exp, la.shape[1])
    p1, p2, p3 = _split3(la)
    lat_ref[...] = (_dot_nt(eye, p1) + _dot_nt(eye, p2)) + _dot_nt(eye, p3)


def _outproj(o_f, o_b, proj, og, o_mla, w_out_b, xf, mod6, g2, wr_pad, seq, heads, hk, n_exp):
    m, d = xf.shape
    hw = o_f.shape[1]
    mw = o_mla.shape[1]
    tm = min(seq, 256)
    tpb = seq // tm
    gcol = (3 * hk + hw) // hw
    assert (3 * hk + hw) % hw == 0
    row = lambda n: pl.BlockSpec((tm, n), lambda i: (i, 0))
    return pl.pallas_call(
        functools.partial(_outproj_kernel, heads=heads, n_exp=n_exp),
        out_shape=(jax.ShapeDtypeStruct((m, d), F32),
                   jax.ShapeDtypeStruct((m, d), BF16),
                   jax.ShapeDtypeStruct((m, LANES), F32),
                   jax.ShapeDtypeStruct((m // seq, n_exp, seq), F32)),
        grid=(m // tm,),
        in_specs=[row(hw), row(hw),
                  pl.BlockSpec((tm, hw), lambda i: (i, gcol)),
                  pl.BlockSpec((1, hw), lambda i: (0, 0)),
                  row(mw),
                  pl.BlockSpec((hw + mw, d), lambda i: (0, 0)),
                  row(d),
                  pl.BlockSpec((None, 6, d), lambda i: (i // tpb, 0, 0)),
                  pl.BlockSpec((1, d), lambda i: (0, 0)),
                  pl.BlockSpec((d, LANES), lambda i: (0, 0))],
        out_specs=(row(d), row(d), row(LANES),
                   pl.BlockSpec((None, n_exp, tm), lambda i: (i // tpb, 0, i % tpb))),
        scratch_shapes=[pltpu.VMEM((tm, hw + mw), BF16)],
        compiler_params=_cparams("parallel"),
        name="outproj_norm2_router",
    )(o_f, o_b, proj, og, o_mla, w_out_b, xf, mod6, g2, wr_pad)


BISECT_STEPS = 64


def _topk_kernel(la_ref, slot_es_ref, slot_se_ref, tri_scr, *, cap, n_exp):
    nrow, seq = la_ref.shape
    ep = slot_se_ref.shape[1]
    rows = 256
    for k in range(seq // rows):
        r = lax.broadcasted_iota(jnp.int32, (rows, seq), 0) + k * rows
        c = lax.broadcasted_iota(jnp.int32, (rows, seq), 1)
        tri_scr[k * rows:(k + 1) * rows, :] = jnp.where(r < c, 1.0, 0.0).astype(BF16)

    def count(mask):
        return jnp.sum(jnp.where(mask, 1.0, 0.0), axis=-1, keepdims=True)

    def body(_, lh):
        lo, hi = lh
        mid = 0.5 * (lo + hi)
        ok = count(la_ref[...] >= mid) >= cap
        return jnp.where(ok, mid, lo), jnp.where(ok, hi, mid)

    la = la_ref[...]
    lo0 = jnp.min(la, axis=-1, keepdims=True)
    lo, hi = lax.fori_loop(0, BISECT_STEPS, body, (lo0, jnp.ones_like(lo0)))
    above = la >= hi
    tie = (la >= lo) & (la < hi)
    need = cap - count(above)
    tri = tri_scr[...]
    rank = _dot(jnp.where(tie, 1.0, 0.0).astype(BF16), tri)
    sel = above | (tie & (rank < need))
    pos = _dot(jnp.where(sel, 1.0, 0.0).astype(BF16), tri)
    slot = jnp.where(sel, pos, -1.0)
    slot_es_ref[...] = slot
    eye = _eye(n_exp, ep)
    for b in range(nrow // n_exp):
        slot_se_ref[b * seq:(b + 1) * seq, :] = _dot_tn(
            slot[b * n_exp:(b + 1) * n_exp, :].astype(BF16), eye)


def _topk(lat, batch, seq, n_exp, cap):
    return pl.pallas_call(
        functools.partial(_topk_kernel, cap=cap, n_exp=n_exp),
        out_shape=(jax.ShapeDtypeStruct((batch * n_exp, seq), F32),
                   jax.ShapeDtypeStruct((batch * seq, LANES), F32)),
        scratch_shapes=[pltpu.VMEM((seq, seq), BF16)],
        compiler_params=pltpu.CompilerParams(vmem_limit_bytes=VMEM_LIMIT),
        name="expert_choice_topk",
    )(lat.reshape(batch * n_exp, seq))


def _gather_kernel(slot_ref, h2_ref, xe_ref, *, cap):
    e = pl.program_id(1)
    seq = h2_ref.shape[0]
    srow = slot_ref[pl.ds(e, 1), :]
    cidx = lax.broadcasted_iota(jnp.int32, (cap, seq), 0).astype(F32)
    onehot = jnp.where(cidx == srow, 1.0, 0.0).astype(BF16)
    xe_ref[...] = _dot(onehot, h2_ref[...]).astype(BF16)


def _gather(slot_es, h2, batch, seq, n_exp, cap):
    d = h2.shape[1]
    return pl.pallas_call(
        functools.partial(_gather_kernel, cap=cap),
        out_shape=jax.ShapeDtypeStruct((n_exp, batch, cap, d), BF16),
        grid=(batch, n_exp),
        in_specs=[pl.BlockSpec((None, n_exp, seq), lambda b, e: (b, 0, 0)),
                  pl.BlockSpec((seq, d), lambda b, e: (b, 0))],
        out_specs=pl.BlockSpec((None, None, cap, d), lambda b, e: (e, b, 0, 0)),
        compiler_params=_cparams("parallel", "arbitrary"),
        name="expert_gather",
    )(slot_es, h2)


def _ffn_kernel(xe_ref, wg_ref, wu_ref, wd_ref, ye_ref, acc_scr):
    f = pl.program_id(1)
    xe = xe_ref[...]
    a = _dot(xe, wg_ref[...].astype(BF16))
    u = _dot(xe, wu_ref[...].astype(BF16))
    hmid = (_silu(a) * u).astype(BF16)
    y = _dot(hmid, wd_ref[...].astype(BF16))

    @pl.when(f == 0)
    def _():
        acc_scr[...] = y

    @pl.when(f > 0)
    def _():
        acc_scr[...] += y

    @pl.when(f == pl.num_programs(1) - 1)
    def _():
        ye_ref[...] = acc_scr[...].astype(BF16)


def _ffn(xe, w_gate, w_up, w_down):
    n_exp, rows, d = xe.shape
    ff = w_gate.shape[2]
    tf = min(ff, 256)
    return pl.pallas_call(
        _ffn_kernel,
        out_shape=jax.ShapeDtypeStruct((n_exp, rows, d), BF16),
        grid=(n_exp, ff // tf),
        in_specs=[pl.BlockSpec((None, rows, d), lambda e, f: (e, 0, 0)),
                  pl.BlockSpec((None, d, tf), lambda e, f: (e, 0, f)),
                  pl.BlockSpec((None, d, tf), lambda e, f: (e, 0, f)),
                  pl.BlockSpec((None, tf, d), lambda e, f: (e, f, 0))],
        out_specs=pl.BlockSpec((None, rows, d), lambda e, f: (e, 0, 0)),
        scratch_shapes=[pltpu.VMEM((rows, d), F32)],
        compiler_params=_cparams("parallel", "arbitrary"),
        name="expert_swiglu",
    )(xe, w_gate, w_up, w_down)


def _combine_kernel(slot_ref, aff_ref, ye_ref, x1_ref, mod_ref, out_ref, *, n_exp, cap):
    tt = x1_ref.shape[0]
    cidx = lax.broadcasted_iota(jnp.int32, (tt, cap), 1).astype(F32)
    acc = jnp.zeros(x1_ref.shape, F32)
    for e in range(n_exp):
        onehot = jnp.where(cidx == slot_ref[:, e:e + 1], 1.0, 0.0).astype(BF16)
        acc = acc + aff_ref[:, e:e + 1] * _dot(onehot, ye_ref[e])
    out_ref[...] = x1_ref[...] + mod_ref[5:6, :] * acc


def _combine(slot_se, aff, ye4, x1, mod6, seq, n_exp, cap):
    m, d = x1.shape
    ep = aff.shape[1]
    batch = m // seq
    tt = min(seq, 256)
    tpb = seq // tt
    return pl.pallas_call(
        functools.partial(_combine_kernel, n_exp=n_exp, cap=cap),
        out_shape=jax.ShapeDtypeStruct((m, d), F32),
        grid=(batch, tpb),
        in_specs=[pl.BlockSpec((tt, ep), lambda b, t: (b * tpb + t, 0)),
                  pl.BlockSpec((tt, ep), lambda b, t: (b * tpb + t, 0)),
                  pl.BlockSpec((n_exp, None, cap, d), lambda b, t: (0, b, 0, 0)),
                  pl.BlockSpec((tt, d), lambda b, t: (b * tpb + t, 0)),
                  pl.BlockSpec((None, 6, d), lambda b, t: (b, 0, 0))],
        out_specs=pl.BlockSpec((tt, d), lambda b, t: (b * tpb + t, 0)),
        compiler_params=_cparams("parallel", "arbitrary"),
        name="expert_combine",
    )(slot_se, aff, ye4, x1, mod6)


def kernel(x, c, positions, w_ada, b_ada, norm1_g, w_in, lb_logits, hgrn_out_g, qa_norm_g, w_uq,
           kva_norm_g, w_ukv, q_head_g, k_head_g, w_out, norm2_g, w_router, w_gate, w_up, w_down):
    batch, seq, d = x.shape
    depth = w_ada.shape[0]
    m = batch * seq
    hk = lb_logits.shape[2]
    heads, dv = hgrn_out_g.shape[1], hgrn_out_g.shape[2]
    hw = heads * dv
    ql, kvl = qa_norm_g.shape[1], kva_norm_g.shape[1]
    qk_dim = q_head_g.shape[1]
    mh = w_uq.shape[2] // qk_dim
    d_in = w_in.shape[2]
    rope = d_in - (3 * hk + 2 * hw + ql + kvl)
    nope = qk_dim - rope
    vdim = w_ukv.shape[2] // mh - nope
    n_exp = w_router.shape[2]
    cap = EC_CAPACITY * seq // n_exp
    assert dv == LANES and hk == hw and nope == LANES and vdim == LANES and 2 * rope == LANES
    assert ql + kvl + rope <= hk and seq % BLK == 0 and n_exp <= LANES and cap % 8 == 0

    cq_off = 3 * hk + 2 * hw
    ckv_off = cq_off + ql
    kpe_off = ckv_off + kvl
    swap = jnp.concatenate([jnp.arange(rope // 2, rope), jnp.arange(0, rope // 2)])

    def both(v):
        return jnp.concatenate([v, v[..., swap]], axis=-1)

    cs = _rope_tables(positions, rope)
    c8 = jnp.pad(c, ((0, (-batch) % 8), (0, 0)))
    xf = x.reshape(m, d)
    for l in range(depth):
        mod6 = _ada(c8, w_ada[l], b_ada[l])[:batch].reshape(batch, 6, d)

        w_in_b = w_in[l].astype(BF16)
        wk_b = both(w_in[l][:, kpe_off:kpe_off + rope]).astype(BF16)
        proj, glog, kpe2 = _inproj(xf, mod6, norm1_g[l].reshape(1, d), w_in_b, wk_b, lb_logits,
                                   seq, hk, l)

        o_f, o_b = _hgrn(proj, glog, batch, seq, heads, hk)

        wq = w_uq[l].reshape(ql, mh, qk_dim).transpose(1, 0, 2)
        wkv = w_ukv[l].reshape(kvl, mh, nope + vdim).transpose(1, 0, 2)
        qh, kh, vh = _mla_proj(
            proj, kpe2, cs, qa_norm_g[l].reshape(1, ql), kva_norm_g[l].reshape(1, kvl),
            wq[..., :nope].astype(BF16), both(wq[..., nope:]).astype(BF16),
            wkv[..., :nope].astype(BF16), wkv[..., nope:].astype(BF16),
            q_head_g[l][:nope].reshape(1, nope), both(q_head_g[l][nope:]).reshape(1, 2 * rope),
            k_head_g[l][:nope].reshape(1, nope), both(k_head_g[l][nope:]).reshape(1, 2 * rope),
            batch, seq, cq_off, ckv_off, qk_dim, rope)
        o_mla = _attention(qh, kh, vh)

        wr_pad = jnp.pad(w_router[l], ((0, 0), (0, LANES - n_exp)))
        x1, h2, aff, lat = _outproj(o_f, o_b, proj, hgrn_out_g[l].reshape(1, hw), o_mla,
                                    w_out[l].astype(BF16), xf, mod6, norm2_g[l].reshape(1, d),
                                    wr_pad, seq, heads, hk, n_exp)

        slot_es, slot_se = _topk(lat, batch, seq, n_exp, cap)
        xe = _gather(slot_es.reshape(batch, n_exp, seq), h2, batch, seq, n_exp, cap)
        ye = _ffn(xe.reshape(n_exp, batch * cap, d), w_gate[l], w_up[l], w_down[l])
        xf = _combine(slot_se, aff, ye.reshape(n_exp, batch, cap, d), x1, mod6, seq, n_exp, cap)
    return xf.reshape(batch, seq, d)
```

```python
import functools
import math

import jax
import jax.numpy as jnp
from jax import lax
from jax.experimental import pallas as pl
from jax.experimental.pallas import tpu as pltpu

F32 = jnp.float32
BF16 = jnp.bfloat16
EPS = 1e-6
ROPE_BASE = 10000.0
EC_CAPACITY = 2
LANES = 128
TILE = 8
GRP = 128
VMEM_LIMIT = 56 * 1024 * 1024


def _cparams(*sem):
    return pltpu.CompilerParams(dimension_semantics=sem, vmem_limit_bytes=VMEM_LIMIT)


def _dot(a, b):
    return jnp.dot(a, b, preferred_element_type=F32)


def _dot_nt(a, b):
    return lax.dot_general(a, b, (((1,), (1,)), ((), ())), preferred_element_type=F32)


def _dot_tn(a, b):
    return lax.dot_general(a, b, (((0,), (0,)), ((), ())), preferred_element_type=F32)


def _split2(a):
    hi = a.astype(BF16)
    lo = (a - hi.astype(F32)).astype(BF16)
    return hi, lo


def _split3(a):
    p1 = a.astype(BF16)
    r1 = a - p1.astype(F32)
    p2 = r1.astype(BF16)
    p3 = (r1 - p2.astype(F32)).astype(BF16)
    return p1, p2, p3


def _dot_hi(a, b):
    ah, al = _split2(a)
    bh, bl = _split2(b)
    return _dot(ah, bh) + (_dot(ah, bl) + _dot(al, bh))


def _silu(x):
    return x * jax.nn.sigmoid(x)


def _rope_kernel(pos_ref, cs_ref, *, half):
    pos = pos_ref[...].astype(F32)
    lane = lax.broadcasted_iota(jnp.int32, (1, 4 * half), 1)
    j = (lane & (half - 1)).astype(F32)
    inv_freq = jnp.exp(j * (-2.0 * math.log(ROPE_BASE) / (2 * half)))
    ang = pos * inv_freq
    c = jnp.cos(ang)
    s = jnp.sin(ang)
    cs_ref[...] = jnp.where(lane < 2 * half, c, jnp.where(lane < 3 * half, -s, s))


def _rope_tables(positions, rope):
    m = positions.size
    tm = min(m, 1024)
    half = rope // 2
    return pl.pallas_call(
        functools.partial(_rope_kernel, half=half),
        out_shape=jax.ShapeDtypeStruct((m, 2 * rope), F32),
        grid=(m // tm,),
        in_specs=[pl.BlockSpec((tm, 1), lambda i: (i, 0))],
        out_specs=pl.BlockSpec((tm, 2 * rope), lambda i: (i, 0)),
        compiler_params=_cparams("parallel"),
        name="rope_tables",
    )(positions.reshape(m, 1))


def _ada_kernel(c_ref, w_ref, b_ref, o_ref):
    o_ref[...] = _dot_hi(_silu(c_ref[...]), w_ref[...]) + b_ref[...]


def _ada(c8, w, b):
    d, n = w.shape
    tn = min(d, 1024)
    assert n % tn == 0
    return pl.pallas_call(
        _ada_kernel,
        out_shape=jax.ShapeDtypeStruct((c8.shape[0], n), F32),
        grid=(n // tn,),
        in_specs=[pl.BlockSpec((c8.shape[0], d), lambda j: (0, 0)),
                  pl.BlockSpec((d, tn), lambda j: (0, j)),
                  pl.BlockSpec((1, tn), lambda j: (0, j))],
        out_specs=pl.BlockSpec((c8.shape[0], tn), lambda j: (0, j)),
        compiler_params=_cparams("parallel"),
        name="ada_mod",
    )(c8, w, b.reshape(1, n))


def _inproj_kernel(x_ref, mod_ref, g_ref, w_ref, wk_ref, lbl_ref,
                   proj_ref, glog_ref, kpe_ref, h_scr, *, layer):
    j = pl.program_id(1)

    @pl.when(j == 0)
    def _():
        x = x_ref[...]
        r = lax.rsqrt(jnp.mean(x * x, axis=-1, keepdims=True) + EPS)
        h = x * r * g_ref[...] * (1.0 + mod_ref[1:2, :]) + mod_ref[0:1, :]
        hb = h.astype(BF16)
        h_scr[...] = hb
        kpe_ref[...] = _dot(hb, wk_ref[...])

    acc = _dot(h_scr[...], w_ref[...])
    proj_ref[...] = acc.astype(BF16)

    @pl.when((j == 1) | (j == 2))
    def _():
        lg = lbl_ref[...]
        e = jnp.exp(lg - jnp.max(lg, axis=0, keepdims=True))
        lb = jnp.sum(e[:layer + 1], axis=0, keepdims=True) / jnp.sum(e, axis=0, keepdims=True)
        glog_ref[...] = jnp.log(lb + (1.0 - lb) * jax.nn.sigmoid(acc))


def _inproj(xf, mod6, g1, w_in_b, wk_b, lb_logits, seq, hk, layer):
    m, d = xf.shape
    d_in = w_in_b.shape[1]
    tm = min(seq, 512)
    tpb = seq // tm
    tn = hk
    nj = pl.cdiv(d_in, tn)
    nl = lb_logits.shape[1]
    fdir = lambda j: jnp.clip(j - 1, 0, 1)
    return pl.pallas_call(
        functools.partial(_inproj_kernel, layer=layer),
        out_shape=(jax.ShapeDtypeStruct((m, d_in), BF16),
                   jax.ShapeDtypeStruct((m, 2 * hk), F32),
                   jax.ShapeDtypeStruct((m, LANES), F32)),
        grid=(m // tm, nj),
        in_specs=[pl.BlockSpec((tm, d), lambda i, j: (i, 0)),
                  pl.BlockSpec((None, 6, d), lambda i, j: (i // tpb, 0, 0)),
                  pl.BlockSpec((1, d), lambda i, j: (0, 0)),
                  pl.BlockSpec((d, tn), lambda i, j: (0, j)),
                  pl.BlockSpec((d, LANES), lambda i, j: (0, 0)),
                  pl.BlockSpec((None, nl, tn), lambda i, j: (fdir(j), 0, 0))],
        out_specs=(pl.BlockSpec((tm, tn), lambda i, j: (i, j)),
                   pl.BlockSpec((tm, tn), lambda i, j: (i, fdir(j))),
                   pl.BlockSpec((tm, LANES), lambda i, j: (i, 0))),
        scratch_shapes=[pltpu.VMEM((tm, d), BF16)],
        compiler_params=_cparams("parallel", "arbitrary"),
        name="norm1_inproj",
    )(xf, mod6, g1, w_in_b, wk_b, lb_logits)


def _group_cumsum(g, d):
    rin = lax.broadcasted_iota(jnp.int32, g.shape, 0) & (TILE - 1)
    b = g
    step = 1
    while step < TILE:
        if d == 0:
            b = b + jnp.where(rin >= step, pltpu.roll(b, step, 0), 0.0)
        else:
            b = b + jnp.where(rin < TILE - step, pltpu.roll(b, GRP - step, 0), 0.0)
        step *= 2
    ntile = GRP // TILE
    order = range(ntile) if d == 0 else range(ntile - 1, -1, -1)
    edge = TILE - 1 if d == 0 else 0
    out = [None] * ntile
    carry = None
    for i in order:
        t = b[i * TILE:(i + 1) * TILE]
        out[i] = t if carry is None else t + carry
        tot = t[edge:edge + 1]
        carry = tot if carry is None else carry + tot
    return jnp.concatenate(out, axis=0)


def _boundary(b, h, d):
    idx = h - 1 if d == 0 else h
    if 2 * h >= TILE:
        b3 = b.reshape(GRP // (2 * h), 2 * h, b.shape[1])
        return jnp.broadcast_to(b3[:, idx:idx + 1, :], b3.shape).reshape(b.shape)
    p = lax.broadcasted_iota(jnp.int32, b.shape, 0) & (2 * h - 1)
    out = b
    for pos in range(2 * h):
        shift = pos - idx
        if shift != 0:
            out = jnp.where(p == pos, pltpu.roll(b, shift % GRP, 0), out)
    return out


def _hgrn_kernel(q_ref, v_ref, gf_ref, gb_ref, of_ref, ob_ref, lv_scr, st_scr):
    seq = q_ref.shape[0]
    ngrp = seq // GRP
    nlev = GRP.bit_length()
    g_refs = (gf_ref, gb_ref)
    o_refs = (of_ref, ob_ref)

    r = lax.broadcasted_iota(jnp.int32, (GRP, GRP), 0)
    c = lax.broadcasted_iota(jnp.int32, (GRP, GRP), 1)
    lev = jnp.zeros((GRP, GRP), jnp.int32)
    for j in range(nlev - 1):
        lev = lev + jnp.where((r >> j) != (c >> j), 1, 0)
    lv_scr[0] = jnp.where(c <= r, lev, -1)
    lv_scr[1] = jnp.where(c >= r, lev, -1)
    st_scr[...] = jnp.zeros_like(st_scr)

    def body(i, carry):
        for d in (0, 1):
            grp = i if d == 0 else ngrp - 1 - i
            r0 = pl.multiple_of(grp * GRP, GRP)
            g = g_refs[d][pl.ds(r0, GRP), :]
            qb = q_ref[pl.ds(r0, GRP), :]
            vb = v_ref[pl.ds(r0, GRP), :]
            qf = qb.astype(F32)
            kk = 1.0 - jnp.exp(g)
            b = _group_cumsum(g, d)
            edge = GRP - 1 if d == 0 else 0
            tot = b[edge:edge + 1]
            lv = lv_scr[d]
            att = jnp.where(lv == 0, _dot_nt(qb, kk.astype(BF16)), 0.0)
            for l in range(1, nlev):
                x = jnp.exp(-jnp.abs(b - _boundary(b, 1 << (l - 1), d)))
                att = jnp.where(lv == l, _dot_nt((qf * x).astype(BF16), (kk * x).astype(BF16)), att)
            st = st_scr[d]
            o = _dot(att.astype(BF16), vb) + _dot_nt((qf * jnp.exp(b)).astype(BF16), st.astype(BF16))
            o_refs[d][pl.ds(r0, GRP), :] = o
            st_scr[d] = st * jnp.exp(tot) + _dot_tn(vb, (kk * jnp.exp(tot - b)).astype(BF16))
        return carry

    lax.fori_loop(0, ngrp, body, 0)


def _hgrn(proj, glog, batch, seq, heads, hk):
    m = proj.shape[0]
    nh = hk // LANES
    vcol = 3 * nh
    blk = lambda off: pl.BlockSpec((seq, LANES), lambda b, h: (b, off + h))
    out = jax.ShapeDtypeStruct((m, hk), F32)
    return pl.pallas_call(
        _hgrn_kernel,
        out_shape=(out, out),
        grid=(batch, heads),
        in_specs=[blk(0), blk(vcol), blk(0), blk(nh)],
        out_specs=(blk(0), blk(0)),
        scratch_shapes=[pltpu.VMEM((2, GRP, GRP), jnp.int32),
                        pltpu.VMEM((2, LANES, LANES), F32)],
        compiler_params=_cparams("parallel", "parallel"),
        name="hgrn2_scan",
    )(proj, proj, glog, glog)


def _mla_proj_kernel(cq_ref, ckv_ref, kpe_ref, cs_ref, qag_ref, kvag_ref, wq_ref, wkv_ref,
                     qgn_ref, qgr_ref, kgn_ref, kgr_ref, q_out, k_out, v_out,
                     *, scale, qk_dim, rope, heads):
    cq = cq_ref[...].astype(F32)
    a = (cq * lax.rsqrt(jnp.mean(cq * cq, axis=-1, keepdims=True) + EPS) * qag_ref[...]).astype(BF16)
    ckv = ckv_ref[...].astype(F32)
    c = (ckv * lax.rsqrt(jnp.mean(ckv * ckv, axis=-1, keepdims=True) + EPS)
         * kvag_ref[...]).astype(BF16)
    qall = _dot(a, wq_ref[...])
    kvall = _dot(c, wkv_ref[...])

    cs = cs_ref[...]
    lane = lax.broadcasted_iota(jnp.int32, cs.shape, 1)
    lo = lane < rope

    def rope_sumsq(rr):
        return jnp.sum(jnp.where(lo, rr * rr, 0.0), axis=-1, keepdims=True)

    def rotate(rr, gr):
        y = rr * gr * cs
        return y + pltpu.roll(y, rope, 1)

    kpe = kpe_ref[...]
    k_ss = rope_sumsq(kpe)
    k_rot = rotate(kpe, kgr_ref[...])
    for h in range(heads):
        base = 2 * LANES * h
        qn = qall[:, base:base + LANES]
        qr = qall[:, base + LANES:base + 2 * LANES]
        rq = lax.rsqrt((jnp.sum(qn * qn, axis=-1, keepdims=True) + rope_sumsq(qr)) / qk_dim + EPS) * scale
        q_out[h, :, :LANES] = (qn * qgn_ref[...] * rq).astype(BF16)
        q_out[h, :, LANES:] = jnp.where(lo, rotate(qr, qgr_ref[...]) * rq, 0.0).astype(BF16)
        kn = kvall[:, base:base + LANES]
        rk = lax.rsqrt((jnp.sum(kn * kn, axis=-1, keepdims=True) + k_ss) / qk_dim + EPS)
        k_out[h, :, :LANES] = (kn * kgn_ref[...] * rk).astype(BF16)
        k_out[h, :, LANES:] = jnp.where(lo, k_rot * rk, 0.0).astype(BF16)
        v_out[h] = kvall[:, base + LANES:base + 2 * LANES].astype(BF16)


def _mla_proj(proj, kpe2, cs, qag, kvag, wq_all, wkv_all, qgn, qgr, kgn, kgr,
              batch, seq, heads, cq_off, ckv_off, qk_dim, rope):
    m = proj.shape[0]
    ql, kvl = wq_all.shape[0], wkv_all.shape[0]
    tm = min(seq, 256)
    tpb = seq // tm
    assert cq_off % ql == 0 and ckv_off % kvl == 0
    vec = lambda n: pl.BlockSpec((1, n), lambda i: (0, 0))
    full = lambda w: pl.BlockSpec(w.shape, lambda i: (0, 0))
    hspec = lambda n: pl.BlockSpec((None, heads, tm, n), lambda i: (i // tpb, 0, i % tpb, 0))
    scale = qk_dim ** -0.5 * math.log2(math.e)
    return pl.pallas_call(
        functools.partial(_mla_proj_kernel, scale=scale, qk_dim=float(qk_dim), rope=rope, heads=heads),
        out_shape=(jax.ShapeDtypeStruct((batch, heads, seq, 2 * LANES), BF16),
                   jax.ShapeDtypeStruct((batch, heads, seq, 2 * LANES), BF16),
                   jax.ShapeDtypeStruct((batch, heads, seq, LANES), BF16)),
        grid=(m // tm,),
        in_specs=[pl.BlockSpec((tm, ql), lambda i: (i, cq_off // ql)),
                  pl.BlockSpec((tm, kvl), lambda i: (i, ckv_off // kvl)),
                  pl.BlockSpec((tm, LANES), lambda i: (i, 0)),
                  pl.BlockSpec((tm, LANES), lambda i: (i, 0)),
                  vec(ql), vec(kvl), full(wq_all), full(wkv_all),
                  vec(LANES), vec(LANES), vec(LANES), vec(LANES)],
        out_specs=(hspec(2 * LANES), hspec(2 * LANES), hspec(LANES)),
        compiler_params=_cparams("parallel"),
        name="mla_head_proj",
    )(proj, proj, kpe2, cs, qag, kvag, wq_all, wkv_all, qgn, qgr, kgn, kgr)


ATTN_KEYS = 512


def _attn_kernel(q_ref, k_ref, v_ref, o_ref):
    q = q_ref[...]
    seq = k_ref.shape[0]
    kc = min(ATTN_KEYS, seq)
    m = l = acc = None
    for c in range(seq // kc):
        rows = slice(c * kc, (c + 1) * kc)
        s = _dot_nt(q, k_ref[rows, :])
        mc = jnp.max(s, axis=-1, keepdims=True)
        if c == 0:
            m = mc
            p = jnp.exp2(s - m)
            l = jnp.sum(p, axis=-1, keepdims=True)
            acc = _dot(p.astype(BF16), v_ref[rows, :])
        else:
            m_new = jnp.maximum(m, mc)
            alpha = jnp.exp2(m - m_new)
            p = jnp.exp2(s - m_new)
            l = l * alpha + jnp.sum(p, axis=-1, keepdims=True)
            acc = acc * alpha + _dot(p.astype(BF16), v_ref[rows, :])
            m = m_new
    o_ref[...] = (acc / l).astype(BF16)


def _attention(qh, kh, vh):
    batch, mh, seq, dq = qh.shape
    dv = vh.shape[-1]
    tq = min(seq, 512)
    nq = seq // tq
    return pl.pallas_call(
        _attn_kernel,
        out_shape=jax.ShapeDtypeStruct((batch * seq, mh * dv), BF16),
        grid=(batch, mh, nq),
        in_specs=[pl.BlockSpec((None, None, tq, dq), lambda b, h, i: (b, h, i, 0)),
                  pl.BlockSpec((None, None, seq, dq), lambda b, h, i: (b, h, 0, 0)),
                  pl.BlockSpec((None, None, seq, dv), lambda b, h, i: (b, h, 0, 0))],
        out_specs=pl.BlockSpec((tq, dv), lambda b, h, i: (b * nq + i, h)),
        compiler_params=_cparams("parallel", "parallel", "arbitrary"),
        name="mla_attention",
    )(qh, kh, vh)


def _eye(rows, cols):
    r = lax.broadcasted_iota(jnp.int32, (rows, cols), 0)
    c = lax.broadcasted_iota(jnp.int32, (rows, cols), 1)
    return jnp.where(r == c, 1.0, 0.0).astype(BF16)


def _outproj_kernel(of_ref, ob_ref, hg_ref, og_ref, om_ref, w_ref, x_ref, mod_ref, g2_ref, wr_ref,
                    x1_ref, h2_ref, aff_ref, lat_ref, mix_scr, *, heads, n_exp):
    hw = of_ref.shape[1]
    o = of_ref[...] + ob_ref[...]
    gate = _silu(hg_ref[...].astype(F32))
    for h in range(heads):
        sl = slice(h * LANES, (h + 1) * LANES)
        oh = o[:, sl]
        r = lax.rsqrt(jnp.mean(oh * oh, axis=-1, keepdims=True) + EPS)
        mix_scr[:, sl] = (oh * r * og_ref[:, sl] * gate[:, sl]).astype(BF16)
    mix_scr[:, hw:] = om_ref[...]
    x1 = x_ref[...] + mod_ref[2:3, :] * _dot(mix_scr[...], w_ref[...])
    x1_ref[...] = x1
    r2 = lax.rsqrt(jnp.mean(x1 * x1, axis=-1, keepdims=True) + EPS)
    h2 = x1 * r2 * g2_ref[...] * (1.0 + mod_ref[4:5, :]) + mod_ref[3:4, :]
    h2_ref[...] = h2.astype(BF16)
    logits = _dot_hi(h2, wr_ref[...])
    lane = lax.broadcasted_iota(jnp.int32, logits.shape, 1)
    logits = jnp.where(lane < n_exp, logits, -jnp.inf)
    z = logits - jnp.max(logits, axis=-1, keepdims=True)
    p = jnp.exp(z)
    sp = jnp.sum(p, axis=-1, keepdims=True)
    aff_ref[...] = p / sp
    la = jnp.where(lane < n_exp, z - jnp.log(sp), 0.0)
    eye = _eye(n_exp, la.shape[1])
    p1, p2, p3 = _split3(la)
    lat_ref[...] = (_dot_nt(eye, p1) + _dot_nt(eye, p2)) + _dot_nt(eye, p3)


def _outproj(o_f, o_b, proj, og, o_mla, w_out_b, xf, mod6, g2, wr_pad, seq, heads, hk, n_exp):
    m, d = xf.shape
    hw = o_f.shape[1]
    mw = o_mla.shape[1]
    tm = min(seq, 256)
    tpb = seq // tm
    gcol = (3 * hk + hw) // hw
    assert (3 * hk + hw) % hw == 0
    row = lambda n: pl.BlockSpec((tm, n), lambda i: (i, 0))
    return pl.pallas_call(
        functools.partial(_outproj_kernel, heads=heads, n_exp=n_exp),
        out_shape=(jax.ShapeDtypeStruct((m, d), F32),
                   jax.ShapeDtypeStruct((m, d), BF16),
                   jax.ShapeDtypeStruct((m, LANES), F32),
                   jax.ShapeDtypeStruct((m // seq, n_exp, seq), F32)),
        grid=(m // tm,),
        in_specs=[row(hw), row(hw),
                  pl.BlockSpec((tm, hw), lambda i: (i, gcol)),
                  pl.BlockSpec((1, hw), lambda i: (0, 0)),
                  row(mw),
                  pl.BlockSpec((hw + mw, d), lambda i: (0, 0)),
                  row(d),
                  pl.BlockSpec((None, 6, d), lambda i: (i // tpb, 0, 0)),
                  pl.BlockSpec((1, d), lambda i: (0, 0)),
                  pl.BlockSpec((d, LANES), lambda i: (0, 0))],
        out_specs=(row(d), row(d), row(LANES),
                   pl.BlockSpec((None, n_exp, tm), lambda i: (i // tpb, 0, i % tpb))),
        scratch_shapes=[pltpu.VMEM((tm, hw + mw), BF16)],
        compiler_params=_cparams("parallel"),
        name="outproj_norm2_router",
    )(o_f, o_b, proj, og, o_mla, w_out_b, xf, mod6, g2, wr_pad)


BISECT_STEPS = 64


def _topk_kernel(la_ref, slot_es_ref, slot_se_ref, tri_scr, *, cap, n_exp):
    nrow, seq = la_ref.shape
    ep = slot_se_ref.shape[1]
    rows = 256
    for k in range(seq // rows):
        r = lax.broadcasted_iota(jnp.int32, (rows, seq), 0) + k * rows
        c = lax.broadcasted_iota(jnp.int32, (rows, seq), 1)
        tri_scr[k * rows:(k + 1) * rows, :] = jnp.where(r < c, 1.0, 0.0).astype(BF16)

    def count(mask):
        return jnp.sum(jnp.where(mask, 1.0, 0.0), axis=-1, keepdims=True)

    def body(_, lh):
        lo, hi = lh
        mid = 0.5 * (lo + hi)
        ok = count(la_ref[...] >= mid) >= cap
        return jnp.where(ok, mid, lo), jnp.where(ok, hi, mid)

    la = la_ref[...]
    lo0 = jnp.min(la, axis=-1, keepdims=True)
    lo, hi = lax.fori_loop(0, BISECT_STEPS, body, (lo0, jnp.ones_like(lo0)))
    above = la >= hi
    tie = (la >= lo) & (la < hi)
    need = cap - count(above)
    tri = tri_scr[...]
    rank = _dot(jnp.where(tie, 1.0, 0.0).astype(BF16), tri)
    sel = above | (tie & (rank < need))
    pos = _dot(jnp.where(sel, 1.0, 0.0).astype(BF16), tri)
    slot = jnp.where(sel, pos, -1.0)
    slot_es_ref[...] = slot
    eye = _eye(n_exp, ep)
    for b in range(nrow // n_exp):
        slot_se_ref[b * seq:(b + 1) * seq, :] = _dot_tn(
            slot[b * n_exp:(b + 1) * n_exp, :].astype(BF16), eye)


def _topk(lat, batch, seq, n_exp, cap):
    return pl.pallas_call(
        functools.partial(_topk_kernel, cap=cap, n_exp=n_exp),
        out_shape=(jax.ShapeDtypeStruct((batch * n_exp, seq), F32),
                   jax.ShapeDtypeStruct((batch * seq, LANES), F32)),
        scratch_shapes=[pltpu.VMEM((seq, seq), BF16)],
        compiler_params=pltpu.CompilerParams(vmem_limit_bytes=VMEM_LIMIT),
        name="expert_choice_topk",
    )(lat.reshape(batch * n_exp, seq))


def _gather_kernel(slot_ref, h2_ref, xe_ref, *, cap):
    e = pl.program_id(1)
    seq = h2_ref.shape[0]
    srow = slot_ref[pl.ds(e, 1), :]
    cidx = lax.broadcasted_iota(jnp.int32, (cap, seq), 0).astype(F32)
    onehot = jnp.where(cidx == srow, 1.0, 0.0).astype(BF16)
    xe_ref[...] = _dot(onehot, h2_ref[...]).astype(BF16)


def _gather(slot_es, h2, batch, seq, n_exp, cap):
    d = h2.shape[1]
    return pl.pallas_call(
        functools.partial(_gather_kernel, cap=cap),
        out_shape=jax.ShapeDtypeStruct((n_exp, batch, cap, d), BF16),
        grid=(batch, n_exp),
        in_specs=[pl.BlockSpec((None, n_exp, seq), lambda b, e: (b, 0, 0)),
                  pl.BlockSpec((seq, d), lambda b, e: (b, 0))],
        out_specs=pl.BlockSpec((None, None, cap, d), lambda b, e: (e, b, 0, 0)),
        compiler_params=_cparams("parallel", "arbitrary"),
        name="expert_gather",
    )(slot_es, h2)


def _ffn_kernel(xe_ref, wg_ref, wu_ref, wd_ref, ye_ref, acc_scr):
    f = pl.program_id(1)
    xe = xe_ref[...]
    a = _dot(xe, wg_ref[...].astype(BF16))
    u = _dot(xe, wu_ref[...].astype(BF16))
    hmid = (_silu(a) * u).astype(BF16)
    y = _dot(hmid, wd_ref[...].astype(BF16))

    @pl.when(f == 0)
    def _():
        acc_scr[...] = y

    @pl.when(f > 0)
    def _():
        acc_scr[...] += y

    @pl.when(f == pl.num_programs(1) - 1)
    def _():
        ye_ref[...] = acc_scr[...].astype(BF16)


def _ffn(xe, w_gate, w_up, w_down):
    n_exp, rows, d = xe.shape
    ff = w_gate.shape[2]
    tf = min(ff, 256)
    return pl.pallas_call(
        _ffn_kernel,
        out_shape=jax.ShapeDtypeStruct((n_exp, rows, d), BF16),
        grid=(n_exp, ff // tf),
        in_specs=[pl.BlockSpec((None, rows, d), lambda e, f: (e, 0, 0)),
                  pl.BlockSpec((None, d, tf), lambda e, f: (e, 0, f)),
                  pl.BlockSpec((None, d, tf), lambda e, f: (e, 0, f)),
                  pl.BlockSpec((None, tf, d), lambda e, f: (e, f, 0))],
        out_specs=pl.BlockSpec((None, rows, d), lambda e, f: (e, 0, 0)),
        scratch_shapes=[pltpu.VMEM((rows, d), F32)],
        compiler_params=_cparams("parallel", "arbitrary"),
        name="expert_swiglu",
    )(xe, w_gate, w_up, w_down)


def _combine_kernel(slot_ref, aff_ref, ye_ref, x1_ref, mod_ref, out_ref, *, n_exp, cap):
    tt = x1_ref.shape[0]
    cidx = lax.broadcasted_iota(jnp.int32, (tt, cap), 1).astype(F32)
    acc = jnp.zeros(x1_ref.shape, F32)
    for e in range(n_exp):
        onehot = jnp.where(cidx == slot_ref[:, e:e + 1], 1.0, 0.0).astype(BF16)
        acc = acc + aff_ref[:, e:e + 1] * _dot(onehot, ye_ref[e])
    out_ref[...] = x1_ref[...] + mod_ref[5:6, :] * acc


def _combine(slot_se, aff, ye4, x1, mod6, seq, n_exp, cap):
    m, d = x1.shape
    ep = aff.shape[1]
    batch = m // seq
    tt = min(seq, 256)
    tpb = seq // tt
    return pl.pallas_call(
        functools.partial(_combine_kernel, n_exp=n_exp, cap=cap),
        out_shape=jax.ShapeDtypeStruct((m, d), F32),
        grid=(batch, tpb),
        in_specs=[pl.BlockSpec((tt, ep), lambda b, t: (b * tpb + t, 0)),
                  pl.BlockSpec((tt, ep), lambda b, t: (b * tpb + t, 0)),
                  pl.BlockSpec((n_exp, None, cap, d), lambda b, t: (0, b, 0, 0)),
                  pl.BlockSpec((tt, d), lambda b, t: (b * tpb + t, 0)),
                  pl.BlockSpec((None, 6, d), lambda b, t: (b, 0, 0))],
        out_specs=pl.BlockSpec((tt, d), lambda b, t: (b * tpb + t, 0)),
        compiler_params=_cparams("parallel", "arbitrary"),
        name="expert_combine",
    )(slot_se, aff, ye4, x1, mod6)


def kernel(x, c, positions, w_ada, b_ada, norm1_g, w_in, lb_logits, hgrn_out_g, qa_norm_g, w_uq,
           kva_norm_g, w_ukv, q_head_g, k_head_g, w_out, norm2_g, w_router, w_gate, w_up, w_down):
    batch, seq, d = x.shape
    depth = w_ada.shape[0]
    m = batch * seq
    hk = lb_logits.shape[2]
    heads, dv = hgrn_out_g.shape[1], hgrn_out_g.shape[2]
    hw = heads * dv
    ql, kvl = qa_norm_g.shape[1], kva_norm_g.shape[1]
    qk_dim = q_head_g.shape[1]
    mh = w_uq.shape[2] // qk_dim
    d_in = w_in.shape[2]
    rope = d_in - (3 * hk + 2 * hw + ql + kvl)
    nope = qk_dim - rope
    vdim = w_ukv.shape[2] // mh - nope
    n_exp = w_router.shape[2]
    cap = EC_CAPACITY * seq // n_exp
    assert dv == LANES and hk == hw and nope == LANES and vdim == LANES and 2 * rope == LANES
    assert ql + kvl + rope <= hk and seq % GRP == 0 and n_exp <= LANES and cap % 8 == 0

    cq_off = 3 * hk + 2 * hw
    ckv_off = cq_off + ql
    kpe_off = ckv_off + kvl
    swap = jnp.concatenate([jnp.arange(rope // 2, rope), jnp.arange(0, rope // 2)])

    def both(v):
        return jnp.concatenate([v, v[..., swap]], axis=-1)

    cs = _rope_tables(positions, rope)
    c8 = jnp.pad(c, ((0, (-batch) % 8), (0, 0)))
    xf = x.reshape(m, d)
    for l in range(depth):
        mod6 = _ada(c8, w_ada[l], b_ada[l])[:batch].reshape(batch, 6, d)

        w_in_b = w_in[l].astype(BF16)
        wk_b = both(w_in[l][:, kpe_off:kpe_off + rope]).astype(BF16)
        proj, glog, kpe2 = _inproj(xf, mod6, norm1_g[l].reshape(1, d), w_in_b, wk_b, lb_logits,
                                   seq, hk, l)

        o_f, o_b = _hgrn(proj, glog, batch, seq, heads, hk)

        wq = w_uq[l].reshape(ql, mh, qk_dim)
        wq_all = jnp.concatenate([wq[..., :nope], both(wq[..., nope:])], axis=-1)
        qh, kh, vh = _mla_proj(
            proj, kpe2, cs, qa_norm_g[l].reshape(1, ql), kva_norm_g[l].reshape(1, kvl),
            wq_all.reshape(ql, mh * 2 * LANES).astype(BF16), w_ukv[l].astype(BF16),
            q_head_g[l][:nope].reshape(1, nope), both(q_head_g[l][nope:]).reshape(1, 2 * rope),
            k_head_g[l][:nope].reshape(1, nope), both(k_head_g[l][nope:]).reshape(1, 2 * rope),
            batch, seq, mh, cq_off, ckv_off, qk_dim, rope)
        o_mla = _attention(qh, kh, vh)

        wr_pad = jnp.pad(w_router[l], ((0, 0), (0, LANES - n_exp)))
        x1, h2, aff, lat = _outproj(o_f, o_b, proj, hgrn_out_g[l].reshape(1, hw), o_mla,
                                    w_out[l].astype(BF16), xf, mod6, norm2_g[l].reshape(1, d),
                                    wr_pad, seq, heads, hk, n_exp)

        slot_es, slot_se = _topk(lat, batch, seq, n_exp, cap)
        xe = _gather(slot_es.reshape(batch, n_exp, seq), h2, batch, seq, n_exp, cap)
        ye = _ffn(xe.reshape(n_exp, batch * cap, d), w_gate[l], w_up[l], w_down[l])
        xf = _combine(slot_se, aff, ye.reshape(n_exp, batch, cap, d), x1, mod6, seq, n_exp, cap)
    return xf.reshape(batch, seq, d)
```

```python
import functools
import math

import jax
import jax.numpy as jnp
from jax import lax
from jax.experimental import pallas as pl
from jax.experimental.pallas import tpu as pltpu

F32 = jnp.float32
BF16 = jnp.bfloat16
EPS = 1e-6
ROPE_BASE = 10000.0
EC_CAPACITY = 2
LANES = 128
TILE = 8
GRP = 128
VMEM_LIMIT = 56 * 1024 * 1024


def _cparams(*sem):
    return pltpu.CompilerParams(dimension_semantics=sem, vmem_limit_bytes=VMEM_LIMIT)


def _dot(a, b):
    return jnp.dot(a, b, preferred_element_type=F32)


def _dot_nt(a, b):
    return lax.dot_general(a, b, (((1,), (1,)), ((), ())), preferred_element_type=F32)


def _dot_tn(a, b):
    return lax.dot_general(a, b, (((0,), (0,)), ((), ())), preferred_element_type=F32)


def _split2(a):
    hi = a.astype(BF16)
    lo = (a - hi.astype(F32)).astype(BF16)
    return hi, lo


def _split3(a):
    p1 = a.astype(BF16)
    r1 = a - p1.astype(F32)
    p2 = r1.astype(BF16)
    p3 = (r1 - p2.astype(F32)).astype(BF16)
    return p1, p2, p3


def _dot_hi(a, b):
    ah, al = _split2(a)
    bh, bl = _split2(b)
    return _dot(ah, bh) + (_dot(ah, bl) + _dot(al, bh))


def _silu(x):
    return x * jax.nn.sigmoid(x)


def _rope_kernel(pos_ref, cs_ref, *, half):
    pos = pos_ref[...].astype(F32)
    lane = lax.broadcasted_iota(jnp.int32, (1, 4 * half), 1)
    j = (lane & (half - 1)).astype(F32)
    inv_freq = jnp.exp(j * (-2.0 * math.log(ROPE_BASE) / (2 * half)))
    ang = pos * inv_freq
    c = jnp.cos(ang)
    s = jnp.sin(ang)
    cs_ref[...] = jnp.where(lane < 2 * half, c, jnp.where(lane < 3 * half, -s, s))


def _rope_tables(positions, rope):
    m = positions.size
    tm = min(m, 1024)
    half = rope // 2
    return pl.pallas_call(
        functools.partial(_rope_kernel, half=half),
        out_shape=jax.ShapeDtypeStruct((m, 2 * rope), F32),
        grid=(m // tm,),
        in_specs=[pl.BlockSpec((tm, 1), lambda i: (i, 0))],
        out_specs=pl.BlockSpec((tm, 2 * rope), lambda i: (i, 0)),
        compiler_params=_cparams("parallel"),
        name="rope_tables",
    )(positions.reshape(m, 1))


def _ada_kernel(c_ref, w_ref, b_ref, o_ref):
    o_ref[...] = _dot_hi(_silu(c_ref[...]), w_ref[...]) + b_ref[...]


def _ada(c8, w, b):
    d, n = w.shape
    tn = min(d, 1024)
    assert n % tn == 0
    return pl.pallas_call(
        _ada_kernel,
        out_shape=jax.ShapeDtypeStruct((c8.shape[0], n), F32),
        grid=(n // tn,),
        in_specs=[pl.BlockSpec((c8.shape[0], d), lambda j: (0, 0)),
                  pl.BlockSpec((d, tn), lambda j: (0, j)),
                  pl.BlockSpec((1, tn), lambda j: (0, j))],
        out_specs=pl.BlockSpec((c8.shape[0], tn), lambda j: (0, j)),
        compiler_params=_cparams("parallel"),
        name="ada_mod",
    )(c8, w, b.reshape(1, n))


def _inproj_kernel(x_ref, mod_ref, g_ref, w_ref, wk_ref, lbl_ref,
                   proj_ref, glog_ref, kpe_ref, h_scr, *, layer):
    j = pl.program_id(1)

    @pl.when(j == 0)
    def _():
        x = x_ref[...]
        r = lax.rsqrt(jnp.mean(x * x, axis=-1, keepdims=True) + EPS)
        h = x * r * g_ref[...] * (1.0 + mod_ref[1:2, :]) + mod_ref[0:1, :]
        hb = h.astype(BF16)
        h_scr[...] = hb
        kpe_ref[...] = _dot(hb, wk_ref[...])

    acc = _dot(h_scr[...], w_ref[...])
    proj_ref[...] = acc.astype(BF16)

    @pl.when((j == 1) | (j == 2))
    def _():
        lg = lbl_ref[...]
        e = jnp.exp(lg - jnp.max(lg, axis=0, keepdims=True))
        lb = jnp.sum(e[:layer + 1], axis=0, keepdims=True) / jnp.sum(e, axis=0, keepdims=True)
        glog_ref[...] = jnp.log(lb + (1.0 - lb) * jax.nn.sigmoid(acc))


def _inproj(xf, mod6, g1, w_in_b, wk_b, lb_logits, seq, hk, layer):
    m, d = xf.shape
    d_in = w_in_b.shape[1]
    tm = min(seq, 1024)
    tpb = seq // tm
    tn = hk
    nj = pl.cdiv(d_in, tn)
    nl = lb_logits.shape[1]
    fdir = lambda j: jnp.clip(j - 1, 0, 1)
    return pl.pallas_call(
        functools.partial(_inproj_kernel, layer=layer),
        out_shape=(jax.ShapeDtypeStruct((m, d_in), BF16),
                   jax.ShapeDtypeStruct((m, 2 * hk), F32),
                   jax.ShapeDtypeStruct((m, LANES), F32)),
        grid=(m // tm, nj),
        in_specs=[pl.BlockSpec((tm, d), lambda i, j: (i, 0)),
                  pl.BlockSpec((None, 6, d), lambda i, j: (i // tpb, 0, 0)),
                  pl.BlockSpec((1, d), lambda i, j: (0, 0)),
                  pl.BlockSpec((d, tn), lambda i, j: (0, j)),
                  pl.BlockSpec((d, LANES), lambda i, j: (0, 0)),
                  pl.BlockSpec((None, nl, tn), lambda i, j: (fdir(j), 0, 0))],
        out_specs=(pl.BlockSpec((tm, tn), lambda i, j: (i, j)),
                   pl.BlockSpec((tm, tn), lambda i, j: (i, fdir(j))),
                   pl.BlockSpec((tm, LANES), lambda i, j: (i, 0))),
        scratch_shapes=[pltpu.VMEM((tm, d), BF16)],
        compiler_params=_cparams("parallel", "arbitrary"),
        name="norm1_inproj",
    )(xf, mod6, g1, w_in_b, wk_b, lb_logits)


def _group_cumsum(g, d):
    rin = lax.broadcasted_iota(jnp.int32, g.shape, 0) & (TILE - 1)
    b = g
    step = 1
    while step < TILE:
        if d == 0:
            b = b + jnp.where(rin >= step, pltpu.roll(b, step, 0), 0.0)
        else:
            b = b + jnp.where(rin < TILE - step, pltpu.roll(b, GRP - step, 0), 0.0)
        step *= 2
    ntile = GRP // TILE
    order = range(ntile) if d == 0 else range(ntile - 1, -1, -1)
    edge = TILE - 1 if d == 0 else 0
    out = [None] * ntile
    carry = None
    for i in order:
        t = b[i * TILE:(i + 1) * TILE]
        out[i] = t if carry is None else t + carry
        tot = t[edge:edge + 1]
        carry = tot if carry is None else carry + tot
    return jnp.concatenate(out, axis=0)


def _boundary(b, h, d):
    idx = h - 1 if d == 0 else h
    if 2 * h >= TILE:
        b3 = b.reshape(GRP // (2 * h), 2 * h, b.shape[1])
        return jnp.broadcast_to(b3[:, idx:idx + 1, :], b3.shape).reshape(b.shape)
    p = lax.broadcasted_iota(jnp.int32, b.shape, 0) & (2 * h - 1)
    out = b
    for pos in range(2 * h):
        shift = pos - idx
        if shift != 0:
            out = jnp.where(p == pos, pltpu.roll(b, shift % GRP, 0), out)
    return out


def _hgrn_kernel(q_ref, v_ref, gf_ref, gb_ref, of_ref, ob_ref, lv_scr, st_scr):
    seq = q_ref.shape[0]
    ngrp = seq // GRP
    nlev = GRP.bit_length()
    g_refs = (gf_ref, gb_ref)
    o_refs = (of_ref, ob_ref)

    r = lax.broadcasted_iota(jnp.int32, (GRP, GRP), 0)
    c = lax.broadcasted_iota(jnp.int32, (GRP, GRP), 1)
    lev = jnp.zeros((GRP, GRP), jnp.int32)
    for j in range(nlev - 1):
        lev = lev + jnp.where((r >> j) != (c >> j), 1, 0)
    lv_scr[0] = jnp.where(c <= r, lev, -1)
    lv_scr[1] = jnp.where(c >= r, lev, -1)
    st_scr[...] = jnp.zeros_like(st_scr)

    def body(i, carry):
        for d in (0, 1):
            grp = i if d == 0 else ngrp - 1 - i
            r0 = pl.multiple_of(grp * GRP, GRP)
            g = g_refs[d][pl.ds(r0, GRP), :]
            qb = q_ref[pl.ds(r0, GRP), :]
            vb = v_ref[pl.ds(r0, GRP), :]
            qf = qb.astype(F32)
            kk = 1.0 - jnp.exp(g)
            b = _group_cumsum(g, d)
            edge = GRP - 1 if d == 0 else 0
            tot = b[edge:edge + 1]
            lv = lv_scr[d]
            att = jnp.where(lv == 0, _dot_nt(qb, kk.astype(BF16)), 0.0)
            for l in range(1, nlev):
                x = jnp.exp(-jnp.abs(b - _boundary(b, 1 << (l - 1), d)))
                att = jnp.where(lv == l, _dot_nt((qf * x).astype(BF16), (kk * x).astype(BF16)), att)
            st = st_scr[d]
            o = _dot(att.astype(BF16), vb) + _dot_nt((qf * jnp.exp(b)).astype(BF16), st.astype(BF16))
            o_refs[d][pl.ds(r0, GRP), :] = o
            st_scr[d] = st * jnp.exp(tot) + _dot_tn(vb, (kk * jnp.exp(tot - b)).astype(BF16))
        return carry

    lax.fori_loop(0, ngrp, body, 0)


def _hgrn(proj, glog, batch, seq, heads, hk):
    m = proj.shape[0]
    nh = hk // LANES
    vcol = 3 * nh
    blk = lambda off: pl.BlockSpec((seq, LANES), lambda b, h: (b, off + h))
    out = jax.ShapeDtypeStruct((m, hk), F32)
    return pl.pallas_call(
        _hgrn_kernel,
        out_shape=(out, out),
        grid=(batch, heads),
        in_specs=[blk(0), blk(vcol), blk(0), blk(nh)],
        out_specs=(blk(0), blk(0)),
        scratch_shapes=[pltpu.VMEM((2, GRP, GRP), jnp.int32),
                        pltpu.VMEM((2, LANES, LANES), F32)],
        compiler_params=_cparams("parallel", "parallel"),
        name="hgrn2_scan",
    )(proj, proj, glog, glog)


def _mla_proj_kernel(cq_ref, ckv_ref, kpe_ref, cs_ref, qag_ref, kvag_ref, wq_ref, wkv_ref,
                     qgn_ref, qgr_ref, kgn_ref, kgr_ref, q_out, k_out, v_out,
                     *, scale, qk_dim, rope, heads):
    cq = cq_ref[...].astype(F32)
    a = (cq * lax.rsqrt(jnp.mean(cq * cq, axis=-1, keepdims=True) + EPS) * qag_ref[...]).astype(BF16)
    ckv = ckv_ref[...].astype(F32)
    c = (ckv * lax.rsqrt(jnp.mean(ckv * ckv, axis=-1, keepdims=True) + EPS)
         * kvag_ref[...]).astype(BF16)
    qall = _dot(a, wq_ref[...])
    kvall = _dot(c, wkv_ref[...])

    cs = cs_ref[...]
    lane = lax.broadcasted_iota(jnp.int32, cs.shape, 1)
    lo = lane < rope

    def rope_sumsq(rr):
        return jnp.sum(jnp.where(lo, rr * rr, 0.0), axis=-1, keepdims=True)

    def rotate(rr, gr):
        y = rr * gr * cs
        return y + pltpu.roll(y, rope, 1)

    kpe = kpe_ref[...]
    k_ss = rope_sumsq(kpe)
    k_rot = rotate(kpe, kgr_ref[...])
    for h in range(heads):
        base = 2 * LANES * h
        qn = qall[:, base:base + LANES]
        qr = qall[:, base + LANES:base + 2 * LANES]
        rq = lax.rsqrt((jnp.sum(qn * qn, axis=-1, keepdims=True) + rope_sumsq(qr)) / qk_dim + EPS) * scale
        q_out[h, :, :LANES] = (qn * qgn_ref[...] * rq).astype(BF16)
        q_out[h, :, LANES:] = jnp.where(lo, rotate(qr, qgr_ref[...]) * rq, 0.0).astype(BF16)
        kn = kvall[:, base:base + LANES]
        rk = lax.rsqrt((jnp.sum(kn * kn, axis=-1, keepdims=True) + k_ss) / qk_dim + EPS)
        k_out[h, :, :LANES] = (kn * kgn_ref[...] * rk).astype(BF16)
        k_out[h, :, LANES:] = jnp.where(lo, k_rot * rk, 0.0).astype(BF16)
        v_out[h] = kvall[:, base + LANES:base + 2 * LANES].astype(BF16)


def _mla_proj(proj, kpe2, cs, qag, kvag, wq_all, wkv_all, qgn, qgr, kgn, kgr,
              batch, seq, heads, cq_off, ckv_off, qk_dim, rope):
    m = proj.shape[0]
    ql, kvl = wq_all.shape[0], wkv_all.shape[0]
    tm = min(seq, 256)
    tpb = seq // tm
    assert cq_off % ql == 0 and ckv_off % kvl == 0
    vec = lambda n: pl.BlockSpec((1, n), lambda i: (0, 0))
    full = lambda w: pl.BlockSpec(w.shape, lambda i: (0, 0))
    hspec = lambda n: pl.BlockSpec((None, heads, tm, n), lambda i: (i // tpb, 0, i % tpb, 0))
    scale = qk_dim ** -0.5 * math.log2(math.e)
    return pl.pallas_call(
        functools.partial(_mla_proj_kernel, scale=scale, qk_dim=float(qk_dim), rope=rope, heads=heads),
        out_shape=(jax.ShapeDtypeStruct((batch, heads, seq, 2 * LANES), BF16),
                   jax.ShapeDtypeStruct((batch, heads, seq, 2 * LANES), BF16),
                   jax.ShapeDtypeStruct((batch, heads, seq, LANES), BF16)),
        grid=(m // tm,),
        in_specs=[pl.BlockSpec((tm, ql), lambda i: (i, cq_off // ql)),
                  pl.BlockSpec((tm, kvl), lambda i: (i, ckv_off // kvl)),
                  pl.BlockSpec((tm, LANES), lambda i: (i, 0)),
                  pl.BlockSpec((tm, LANES), lambda i: (i, 0)),
                  vec(ql), vec(kvl), full(wq_all), full(wkv_all),
                  vec(LANES), vec(LANES), vec(LANES), vec(LANES)],
        out_specs=(hspec(2 * LANES), hspec(2 * LANES), hspec(LANES)),
        compiler_params=_cparams("parallel"),
        name="mla_head_proj",
    )(proj, proj, kpe2, cs, qag, kvag, wq_all, wkv_all, qgn, qgr, kgn, kgr)


ATTN_KEYS = 512


def _attn_kernel(q_ref, k_ref, v_ref, o_ref):
    q = q_ref[...]
    seq = k_ref.shape[0]
    kc = min(ATTN_KEYS, seq)
    m = l = acc = None
    for c in range(seq // kc):
        rows = slice(c * kc, (c + 1) * kc)
        s = _dot_nt(q, k_ref[rows, :])
        mc = jnp.max(s, axis=-1, keepdims=True)
        if c == 0:
            m = mc
            p = jnp.exp2(s - m)
            l = jnp.sum(p, axis=-1, keepdims=True)
            acc = _dot(p.astype(BF16), v_ref[rows, :])
        else:
            m_new = jnp.maximum(m, mc)
            alpha = jnp.exp2(m - m_new)
            p = jnp.exp2(s - m_new)
            l = l * alpha + jnp.sum(p, axis=-1, keepdims=True)
            acc = acc * alpha + _dot(p.astype(BF16), v_ref[rows, :])
            m = m_new
    o_ref[...] = (acc / l).astype(BF16)


def _attention(qh, kh, vh):
    batch, mh, seq, dq = qh.shape
    dv = vh.shape[-1]
    tq = min(seq, 512)
    nq = seq // tq
    return pl.pallas_call(
        _attn_kernel,
        out_shape=jax.ShapeDtypeStruct((batch * seq, mh * dv), BF16),
        grid=(batch, mh, nq),
        in_specs=[pl.BlockSpec((None, None, tq, dq), lambda b, h, i: (b, h, i, 0)),
                  pl.BlockSpec((None, None, seq, dq), lambda b, h, i: (b, h, 0, 0)),
                  pl.BlockSpec((None, None, seq, dv), lambda b, h, i: (b, h, 0, 0))],
        out_specs=pl.BlockSpec((tq, dv), lambda b, h, i: (b * nq + i, h)),
        compiler_params=_cparams("parallel", "parallel", "arbitrary"),
        name="mla_attention",
    )(qh, kh, vh)


def _eye(rows, cols):
    r = lax.broadcasted_iota(jnp.int32, (rows, cols), 0)
    c = lax.broadcasted_iota(jnp.int32, (rows, cols), 1)
    return jnp.where(r == c, 1.0, 0.0).astype(BF16)


def _outproj_kernel(of_ref, ob_ref, hg_ref, og_ref, om_ref, w_ref, x_ref, mod_ref, g2_ref, wr_ref,
                    x1_ref, h2_ref, aff_ref, lat_ref, mix_scr, *, heads, n_exp):
    hw = of_ref.shape[1]
    o = of_ref[...] + ob_ref[...]
    gate = _silu(hg_ref[...].astype(F32))
    for h in range(heads):
        sl = slice(h * LANES, (h + 1) * LANES)
        oh = o[:, sl]
        r = lax.rsqrt(jnp.mean(oh * oh, axis=-1, keepdims=True) + EPS)
        mix_scr[:, sl] = (oh * r * og_ref[:, sl] * gate[:, sl]).astype(BF16)
    mix_scr[:, hw:] = om_ref[...]
    x1 = x_ref[...] + mod_ref[2:3, :] * _dot(mix_scr[...], w_ref[...])
    x1_ref[...] = x1
    r2 = lax.rsqrt(jnp.mean(x1 * x1, axis=-1, keepdims=True) + EPS)
    h2 = x1 * r2 * g2_ref[...] * (1.0 + mod_ref[4:5, :]) + mod_ref[3:4, :]
    h2_ref[...] = h2.astype(BF16)
    logits = _dot_hi(h2, wr_ref[...])
    lane = lax.broadcasted_iota(jnp.int32, logits.shape, 1)
    logits = jnp.where(lane < n_exp, logits, -jnp.inf)
    z = logits - jnp.max(logits, axis=-1, keepdims=True)
    p = jnp.exp(z)
    sp = jnp.sum(p, axis=-1, keepdims=True)
    aff_ref[...] = p / sp
    la = jnp.where(lane < n_exp, z - jnp.log(sp), 0.0)
    eye = _eye(n_exp, la.shape[1])
    p1, p2, p3 = _split3(la)
    lat_ref[...] = (_dot_nt(eye, p1) + _dot_nt(eye, p2)) + _dot_nt(eye, p3)


def _outproj(o_f, o_b, proj, og, o_mla, w_out_b, xf, mod6, g2, wr_pad, seq, heads, hk, n_exp):
    m, d = xf.shape
    hw = o_f.shape[1]
    mw = o_mla.shape[1]
    tm = min(seq, 256)
    tpb = seq // tm
    gcol = (3 * hk + hw) // hw
    assert (3 * hk + hw) % hw == 0
    row = lambda n: pl.BlockSpec((tm, n), lambda i: (i, 0))
    return pl.pallas_call(
        functools.partial(_outproj_kernel, heads=heads, n_exp=n_exp),
        out_shape=(jax.ShapeDtypeStruct((m, d), F32),
                   jax.ShapeDtypeStruct((m, d), BF16),
                   jax.ShapeDtypeStruct((m, LANES), F32),
                   jax.ShapeDtypeStruct((m // seq, n_exp, seq), F32)),
        grid=(m // tm,),
        in_specs=[row(hw), row(hw),
                  pl.BlockSpec((tm, hw), lambda i: (i, gcol)),
                  pl.BlockSpec((1, hw), lambda i: (0, 0)),
                  row(mw),
                  pl.BlockSpec((hw + mw, d), lambda i: (0, 0)),
                  row(d),
                  pl.BlockSpec((None, 6, d), lambda i: (i // tpb, 0, 0)),
                  pl.BlockSpec((1, d), lambda i: (0, 0)),
                  pl.BlockSpec((d, LANES), lambda i: (0, 0))],
        out_specs=(row(d), row(d), row(LANES),
                   pl.BlockSpec((None, n_exp, tm), lambda i: (i // tpb, 0, i % tpb))),
        scratch_shapes=[pltpu.VMEM((tm, hw + mw), BF16)],
        compiler_params=_cparams("parallel"),
        name="outproj_norm2_router",
    )(o_f, o_b, proj, og, o_mla, w_out_b, xf, mod6, g2, wr_pad)


BISECT_STEPS = 64


def _topk_kernel(la_ref, slot_es_ref, slot_se_ref, tri_scr, *, cap, n_exp):
    nrow, seq = la_ref.shape
    ep = slot_se_ref.shape[1]
    rows = 256
    for k in range(seq // rows):
        r = lax.broadcasted_iota(jnp.int32, (rows, seq), 0) + k * rows
        c = lax.broadcasted_iota(jnp.int32, (rows, seq), 1)
        tri_scr[k * rows:(k + 1) * rows, :] = jnp.where(r < c, 1.0, 0.0).astype(BF16)

    def count(mask):
        return jnp.sum(jnp.where(mask, 1.0, 0.0), axis=-1, keepdims=True)

    def body(_, lh):
        lo, hi = lh
        mid = 0.5 * (lo + hi)
        ok = count(la_ref[...] >= mid) >= cap
        return jnp.where(ok, mid, lo), jnp.where(ok, hi, mid)

    la = la_ref[...]
    lo0 = jnp.min(la, axis=-1, keepdims=True)
    lo, hi = lax.fori_loop(0, BISECT_STEPS, body, (lo0, jnp.ones_like(lo0)))
    above = la >= hi
    tie = (la >= lo) & (la < hi)
    need = cap - count(above)
    tri = tri_scr[...]
    rank = _dot(jnp.where(tie, 1.0, 0.0).astype(BF16), tri)
    sel = above | (tie & (rank < need))
    pos = _dot(jnp.where(sel, 1.0, 0.0).astype(BF16), tri)
    slot = jnp.where(sel, pos, -1.0)
    slot_es_ref[...] = slot
    eye = _eye(n_exp, ep)
    for b in range(nrow // n_exp):
        slot_se_ref[b * seq:(b + 1) * seq, :] = _dot_tn(
            slot[b * n_exp:(b + 1) * n_exp, :].astype(BF16), eye)


def _topk(lat, batch, seq, n_exp, cap):
    return pl.pallas_call(
        functools.partial(_topk_kernel, cap=cap, n_exp=n_exp),
        out_shape=(jax.ShapeDtypeStruct((batch * n_exp, seq), F32),
                   jax.ShapeDtypeStruct((batch * seq, LANES), F32)),
        scratch_shapes=[pltpu.VMEM((seq, seq), BF16)],
        compiler_params=pltpu.CompilerParams(vmem_limit_bytes=VMEM_LIMIT),
        name="expert_choice_topk",
    )(lat.reshape(batch * n_exp, seq))


def _gather_kernel(slot_ref, h2_ref, xe_ref, *, cap):
    e = pl.program_id(1)
    seq = h2_ref.shape[0]
    srow = slot_ref[pl.ds(e, 1), :]
    cidx = lax.broadcasted_iota(jnp.int32, (cap, seq), 0).astype(F32)
    onehot = jnp.where(cidx == srow, 1.0, 0.0).astype(BF16)
    xe_ref[...] = _dot(onehot, h2_ref[...]).astype(BF16)


def _gather(slot_es, h2, batch, seq, n_exp, cap):
    d = h2.shape[1]
    return pl.pallas_call(
        functools.partial(_gather_kernel, cap=cap),
        out_shape=jax.ShapeDtypeStruct((n_exp, batch, cap, d), BF16),
        grid=(batch, n_exp),
        in_specs=[pl.BlockSpec((None, n_exp, seq), lambda b, e: (b, 0, 0)),
                  pl.BlockSpec((seq, d), lambda b, e: (b, 0))],
        out_specs=pl.BlockSpec((None, None, cap, d), lambda b, e: (e, b, 0, 0)),
        compiler_params=_cparams("parallel", "arbitrary"),
        name="expert_gather",
    )(slot_es, h2)


def _ffn_kernel(xe_ref, wg_ref, wu_ref, wd_ref, ye_ref, hmid_scr, *, nt, tf):
    s = pl.program_id(1)

    @pl.when(s < nt)
    def _():
        xe = xe_ref[...]
        a = _dot(xe, wg_ref[...].astype(BF16))
        u = _dot(xe, wu_ref[...].astype(BF16))
        hmid_scr[s] = (_silu(a) * u).astype(BF16)

    @pl.when(s >= nt)
    def _():
        y = _dot(hmid_scr[0], wd_ref[0:tf, :].astype(BF16))
        for k in range(1, nt):
            y = y + _dot(hmid_scr[k], wd_ref[k * tf:(k + 1) * tf, :].astype(BF16))
        ye_ref[...] = y.astype(BF16)


def _ffn(xe, w_gate, w_up, w_down):
    n_exp, rows, d = xe.shape
    ff = w_gate.shape[2]
    tf = min(ff, 512)
    tn = min(d, 512)
    nt = ff // tf
    assert d // tn == nt
    up = lambda e, s: (e, 0, jnp.minimum(s, nt - 1))
    down = lambda e, s: (e, 0, jnp.maximum(s - nt, 0))
    return pl.pallas_call(
        functools.partial(_ffn_kernel, nt=nt, tf=tf),
        out_shape=jax.ShapeDtypeStruct((n_exp, rows, d), BF16),
        grid=(n_exp, 2 * nt),
        in_specs=[pl.BlockSpec((None, rows, d), lambda e, s: (e, 0, 0)),
                  pl.BlockSpec((None, d, tf), up),
                  pl.BlockSpec((None, d, tf), up),
                  pl.BlockSpec((None, ff, tn), down)],
        out_specs=pl.BlockSpec((None, rows, tn), down),
        scratch_shapes=[pltpu.VMEM((nt, rows, tf), BF16)],
        compiler_params=_cparams("parallel", "arbitrary"),
        name="expert_swiglu",
    )(xe, w_gate, w_up, w_down)


def _combine_kernel(slot_ref, aff_ref, ye_ref, x1_ref, mod_ref, out_ref, *, n_exp, cap):
    tt = x1_ref.shape[0]
    cidx = lax.broadcasted_iota(jnp.int32, (tt, cap), 1).astype(F32)
    acc = jnp.zeros(x1_ref.shape, F32)
    for e in range(n_exp):
        onehot = jnp.where(cidx == slot_ref[:, e:e + 1], 1.0, 0.0).astype(BF16)
        acc = acc + aff_ref[:, e:e + 1] * _dot(onehot, ye_ref[e])
    out_ref[...] = x1_ref[...] + mod_ref[5:6, :] * acc


def _combine(slot_se, aff, ye4, x1, mod6, seq, n_exp, cap):
    m, d = x1.shape
    ep = aff.shape[1]
    batch = m // seq
    tt = min(seq, 256)
    tpb = seq // tt
    return pl.pallas_call(
        functools.partial(_combine_kernel, n_exp=n_exp, cap=cap),
        out_shape=jax.ShapeDtypeStruct((m, d), F32),
        grid=(batch, tpb),
        in_specs=[pl.BlockSpec((tt, ep), lambda b, t: (b * tpb + t, 0)),
                  pl.BlockSpec((tt, ep), lambda b, t: (b * tpb + t, 0)),
                  pl.BlockSpec((n_exp, None, cap, d), lambda b, t: (0, b, 0, 0)),
                  pl.BlockSpec((tt, d), lambda b, t: (b * tpb + t, 0)),
                  pl.BlockSpec((None, 6, d), lambda b, t: (b, 0, 0))],
        out_specs=pl.BlockSpec((tt, d), lambda b, t: (b * tpb + t, 0)),
        compiler_params=_cparams("parallel", "arbitrary"),
        name="expert_combine",
    )(slot_se, aff, ye4, x1, mod6)


def kernel(x, c, positions, w_ada, b_ada, norm1_g, w_in, lb_logits, hgrn_out_g, qa_norm_g, w_uq,
           kva_norm_g, w_ukv, q_head_g, k_head_g, w_out, norm2_g, w_router, w_gate, w_up, w_down):
    batch, seq, d = x.shape
    depth = w_ada.shape[0]
    m = batch * seq
    hk = lb_logits.shape[2]
    heads, dv = hgrn_out_g.shape[1], hgrn_out_g.shape[2]
    hw = heads * dv
    ql, kvl = qa_norm_g.shape[1], kva_norm_g.shape[1]
    qk_dim = q_head_g.shape[1]
    mh = w_uq.shape[2] // qk_dim
    d_in = w_in.shape[2]
    rope = d_in - (3 * hk + 2 * hw + ql + kvl)
    nope = qk_dim - rope
    vdim = w_ukv.shape[2] // mh - nope
    n_exp = w_router.shape[2]
    cap = EC_CAPACITY * seq // n_exp
    assert dv == LANES and hk == hw and nope == LANES and vdim == LANES and 2 * rope == LANES
    assert ql + kvl + rope <= hk and seq % GRP == 0 and n_exp <= LANES and cap % 8 == 0

    cq_off = 3 * hk + 2 * hw
    ckv_off = cq_off + ql
    kpe_off = ckv_off + kvl
    swap = jnp.concatenate([jnp.arange(rope // 2, rope), jnp.arange(0, rope // 2)])

    def both(v):
        return jnp.concatenate([v, v[..., swap]], axis=-1)

    cs = _rope_tables(positions, rope)
    c8 = jnp.pad(c, ((0, (-batch) % 8), (0, 0)))
    xf = x.reshape(m, d)
    for l in range(depth):
        mod6 = _ada(c8, w_ada[l], b_ada[l])[:batch].reshape(batch, 6, d)

        w_in_b = w_in[l].astype(BF16)
        wk_b = both(w_in[l][:, kpe_off:kpe_off + rope]).astype(BF16)
        proj, glog, kpe2 = _inproj(xf, mod6, norm1_g[l].reshape(1, d), w_in_b, wk_b, lb_logits,
                                   seq, hk, l)

        o_f, o_b = _hgrn(proj, glog, batch, seq, heads, hk)

        wq = w_uq[l].reshape(ql, mh, qk_dim)
        wq_all = jnp.concatenate([wq[..., :nope], both(wq[..., nope:])], axis=-1)
        qh, kh, vh = _mla_proj(
            proj, kpe2, cs, qa_norm_g[l].reshape(1, ql), kva_norm_g[l].reshape(1, kvl),
            wq_all.reshape(ql, mh * 2 * LANES).astype(BF16), w_ukv[l].astype(BF16),
            q_head_g[l][:nope].reshape(1, nope), both(q_head_g[l][nope:]).reshape(1, 2 * rope),
            k_head_g[l][:nope].reshape(1, nope), both(k_head_g[l][nope:]).reshape(1, 2 * rope),
            batch, seq, mh, cq_off, ckv_off, qk_dim, rope)
        o_mla = _attention(qh, kh, vh)

        wr_pad = jnp.pad(w_router[l], ((0, 0), (0, LANES - n_exp)))
        x1, h2, aff, lat = _outproj(o_f, o_b, proj, hgrn_out_g[l].reshape(1, hw), o_mla,
                                    w_out[l].astype(BF16), xf, mod6, norm2_g[l].reshape(1, d),
                                    wr_pad, seq, heads, hk, n_exp)

        slot_es, slot_se = _topk(lat, batch, seq, n_exp, cap)
        xe = _gather(slot_es.reshape(batch, n_exp, seq), h2, batch, seq, n_exp, cap)
        ye = _ffn(xe.reshape(n_exp, batch * cap, d), w_gate[l], w_up[l], w_down[l])
        xf = _combine(slot_se, aff, ye.reshape(n_exp, batch, cap, d), x1, mod6, seq, n_exp, cap)
    return xf.reshape(batch, seq, d)
```

```python
import functools
import math

import jax
import jax.numpy as jnp
from jax import lax
from jax.experimental import pallas as pl
from jax.experimental.pallas import tpu as pltpu

F32 = jnp.float32
BF16 = jnp.bfloat16
EPS = 1e-6
ROPE_BASE = 10000.0
EC_CAPACITY = 2
LANES = 128
TILE = 8
GRP = 128
VMEM_LIMIT = 56 * 1024 * 1024


def _cparams(*sem):
    return pltpu.CompilerParams(dimension_semantics=sem, vmem_limit_bytes=VMEM_LIMIT)


def _dot(a, b):
    return jnp.dot(a, b, preferred_element_type=F32)


def _dot_nt(a, b):
    return lax.dot_general(a, b, (((1,), (1,)), ((), ())), preferred_element_type=F32)


def _dot_tn(a, b):
    return lax.dot_general(a, b, (((0,), (0,)), ((), ())), preferred_element_type=F32)


def _split2(a):
    hi = a.astype(BF16)
    lo = (a - hi.astype(F32)).astype(BF16)
    return hi, lo


def _split3(a):
    p1 = a.astype(BF16)
    r1 = a - p1.astype(F32)
    p2 = r1.astype(BF16)
    p3 = (r1 - p2.astype(F32)).astype(BF16)
    return p1, p2, p3


def _dot_hi(a, b):
    ah, al = _split2(a)
    bh, bl = _split2(b)
    return _dot(ah, bh) + (_dot(ah, bl) + _dot(al, bh))


def _silu(x):
    return x * jax.nn.sigmoid(x)


def _rope_kernel(pos_ref, cs_ref, *, half):
    pos = pos_ref[...].astype(F32)
    lane = lax.broadcasted_iota(jnp.int32, (1, 4 * half), 1)
    j = (lane & (half - 1)).astype(F32)
    inv_freq = jnp.exp(j * (-2.0 * math.log(ROPE_BASE) / (2 * half)))
    ang = pos * inv_freq
    c = jnp.cos(ang)
    s = jnp.sin(ang)
    cs_ref[...] = jnp.where(lane < 2 * half, c, jnp.where(lane < 3 * half, -s, s))


def _rope_tables(positions, rope):
    m = positions.size
    tm = min(m, 1024)
    half = rope // 2
    return pl.pallas_call(
        functools.partial(_rope_kernel, half=half),
        out_shape=jax.ShapeDtypeStruct((m, 2 * rope), F32),
        grid=(m // tm,),
        in_specs=[pl.BlockSpec((tm, 1), lambda i: (i, 0))],
        out_specs=pl.BlockSpec((tm, 2 * rope), lambda i: (i, 0)),
        compiler_params=_cparams("parallel"),
        name="rope_tables",
    )(positions.reshape(m, 1))


def _ada_kernel(c_ref, w_ref, b_ref, o_ref):
    o_ref[...] = _dot_hi(_silu(c_ref[...]), w_ref[...]) + b_ref[...]


def _ada(c8, w, b):
    d, n = w.shape
    tn = min(d, 1024)
    assert n % tn == 0
    return pl.pallas_call(
        _ada_kernel,
        out_shape=jax.ShapeDtypeStruct((c8.shape[0], n), F32),
        grid=(n // tn,),
        in_specs=[pl.BlockSpec((c8.shape[0], d), lambda j: (0, 0)),
                  pl.BlockSpec((d, tn), lambda j: (0, j)),
                  pl.BlockSpec((1, tn), lambda j: (0, j))],
        out_specs=pl.BlockSpec((c8.shape[0], tn), lambda j: (0, j)),
        compiler_params=_cparams("parallel"),
        name="ada_mod",
    )(c8, w, b.reshape(1, n))


def _inproj_kernel(x_ref, mod_ref, g_ref, w_ref, wk_ref, lbl_ref,
                   proj_ref, glog_ref, kpe_ref, h_scr, *, layer):
    j = pl.program_id(1)

    @pl.when(j == 0)
    def _():
        x = x_ref[...]
        r = lax.rsqrt(jnp.mean(x * x, axis=-1, keepdims=True) + EPS)
        h = x * r * g_ref[...] * (1.0 + mod_ref[1:2, :]) + mod_ref[0:1, :]
        hb = h.astype(BF16)
        h_scr[...] = hb
        kpe_ref[...] = _dot(hb, wk_ref[...])

    acc = _dot(h_scr[...], w_ref[...])
    proj_ref[...] = acc.astype(BF16)

    @pl.when((j == 1) | (j == 2))
    def _():
        lg = lbl_ref[...]
        e = jnp.exp(lg - jnp.max(lg, axis=0, keepdims=True))
        lb = jnp.sum(e[:layer + 1], axis=0, keepdims=True) / jnp.sum(e, axis=0, keepdims=True)
        glog_ref[...] = jnp.log(lb + (1.0 - lb) * jax.nn.sigmoid(acc))


def _inproj(xf, mod6, g1, w_in_b, wk_b, lb_logits, seq, hk, layer):
    m, d = xf.shape
    d_in = w_in_b.shape[1]
    tm = min(seq, 1024)
    tpb = seq // tm
    tn = hk
    nj = pl.cdiv(d_in, tn)
    nl = lb_logits.shape[1]
    fdir = lambda j: jnp.clip(j - 1, 0, 1)
    return pl.pallas_call(
        functools.partial(_inproj_kernel, layer=layer),
        out_shape=(jax.ShapeDtypeStruct((m, d_in), BF16),
                   jax.ShapeDtypeStruct((m, 2 * hk), F32),
                   jax.ShapeDtypeStruct((m, LANES), F32)),
        grid=(m // tm, nj),
        in_specs=[pl.BlockSpec((tm, d), lambda i, j: (i, 0)),
                  pl.BlockSpec((None, 6, d), lambda i, j: (i // tpb, 0, 0)),
                  pl.BlockSpec((1, d), lambda i, j: (0, 0)),
                  pl.BlockSpec((d, tn), lambda i, j: (0, j)),
                  pl.BlockSpec((d, LANES), lambda i, j: (0, 0)),
                  pl.BlockSpec((None, nl, tn), lambda i, j: (fdir(j), 0, 0))],
        out_specs=(pl.BlockSpec((tm, tn), lambda i, j: (i, j)),
                   pl.BlockSpec((tm, tn), lambda i, j: (i, fdir(j))),
                   pl.BlockSpec((tm, LANES), lambda i, j: (i, 0))),
        scratch_shapes=[pltpu.VMEM((tm, d), BF16)],
        compiler_params=_cparams("parallel", "arbitrary"),
        name="norm1_inproj",
    )(xf, mod6, g1, w_in_b, wk_b, lb_logits)


def _group_cumsum(g, d):
    rin = lax.broadcasted_iota(jnp.int32, g.shape, 0) & (TILE - 1)
    b = g
    step = 1
    while step < TILE:
        if d == 0:
            b = b + jnp.where(rin >= step, pltpu.roll(b, step, 0), 0.0)
        else:
            b = b + jnp.where(rin < TILE - step, pltpu.roll(b, GRP - step, 0), 0.0)
        step *= 2
    ntile = GRP // TILE
    order = range(ntile) if d == 0 else range(ntile - 1, -1, -1)
    edge = TILE - 1 if d == 0 else 0
    out = [None] * ntile
    carry = None
    for i in order:
        t = b[i * TILE:(i + 1) * TILE]
        out[i] = t if carry is None else t + carry
        tot = t[edge:edge + 1]
        carry = tot if carry is None else carry + tot
    return jnp.concatenate(out, axis=0)


def _boundary(b, h, d):
    idx = h - 1 if d == 0 else h
    if 2 * h >= TILE:
        b3 = b.reshape(GRP // (2 * h), 2 * h, b.shape[1])
        return jnp.broadcast_to(b3[:, idx:idx + 1, :], b3.shape).reshape(b.shape)
    p = lax.broadcasted_iota(jnp.int32, b.shape, 0) & (2 * h - 1)
    out = b
    for pos in range(2 * h):
        shift = pos - idx
        if shift != 0:
            out = jnp.where(p == pos, pltpu.roll(b, shift % GRP, 0), out)
    return out


def _hgrn_kernel(q_ref, v_ref, gf_ref, gb_ref, of_ref, ob_ref, lv_scr, st_scr):
    seq = q_ref.shape[0]
    ngrp = seq // GRP
    nlev = GRP.bit_length()
    g_refs = (gf_ref, gb_ref)
    o_refs = (of_ref, ob_ref)

    r = lax.broadcasted_iota(jnp.int32, (GRP, GRP), 0)
    c = lax.broadcasted_iota(jnp.int32, (GRP, GRP), 1)
    lev = jnp.zeros((GRP, GRP), jnp.int32)
    for j in range(nlev - 1):
        lev = lev + jnp.where((r >> j) != (c >> j), 1, 0)
    lv_scr[0] = jnp.where(c <= r, lev, -1)
    lv_scr[1] = jnp.where(c >= r, lev, -1)
    st_scr[...] = jnp.zeros_like(st_scr)

    def body(i, carry):
        for d in (0, 1):
            grp = i if d == 0 else ngrp - 1 - i
            r0 = pl.multiple_of(grp * GRP, GRP)
            g = g_refs[d][pl.ds(r0, GRP), :]
            qb = q_ref[pl.ds(r0, GRP), :]
            vb = v_ref[pl.ds(r0, GRP), :]
            qf = qb.astype(F32)
            kk = 1.0 - jnp.exp(g)
            b = _group_cumsum(g, d)
            edge = GRP - 1 if d == 0 else 0
            tot = b[edge:edge + 1]
            lv = lv_scr[d]
            att = jnp.where(lv == 0, _dot_nt(qb, kk.astype(BF16)), 0.0)
            for l in range(1, nlev):
                x = jnp.exp(-jnp.abs(b - _boundary(b, 1 << (l - 1), d)))
                att = jnp.where(lv == l, _dot_nt((qf * x).astype(BF16), (kk * x).astype(BF16)), att)
            st = st_scr[d]
            o = _dot(att.astype(BF16), vb) + _dot_nt((qf * jnp.exp(b)).astype(BF16), st.astype(BF16))
            o_refs[d][pl.ds(r0, GRP), :] = o
            st_scr[d] = st * jnp.exp(tot) + _dot_tn(vb, (kk * jnp.exp(tot - b)).astype(BF16))
        return carry

    lax.fori_loop(0, ngrp, body, 0)


def _hgrn(proj, glog, batch, seq, heads, hk):
    m = proj.shape[0]
    nh = hk // LANES
    vcol = 3 * nh
    blk = lambda off: pl.BlockSpec((seq, LANES), lambda b, h: (b, off + h))
    out = jax.ShapeDtypeStruct((m, hk), F32)
    return pl.pallas_call(
        _hgrn_kernel,
        out_shape=(out, out),
        grid=(batch, heads),
        in_specs=[blk(0), blk(vcol), blk(0), blk(nh)],
        out_specs=(blk(0), blk(0)),
        scratch_shapes=[pltpu.VMEM((2, GRP, GRP), jnp.int32),
                        pltpu.VMEM((2, LANES, LANES), F32)],
        compiler_params=_cparams("parallel", "parallel"),
        name="hgrn2_scan",
    )(proj, proj, glog, glog)


def _mla_proj_kernel(cq_ref, ckv_ref, kpe_ref, cs_ref, qag_ref, kvag_ref, wq_ref, wkv_ref,
                     qgn_ref, qgr_ref, kgn_ref, kgr_ref, q_out, k_out, v_out,
                     *, scale, qk_dim, rope, heads):
    cq = cq_ref[...].astype(F32)
    a = (cq * lax.rsqrt(jnp.mean(cq * cq, axis=-1, keepdims=True) + EPS) * qag_ref[...]).astype(BF16)
    ckv = ckv_ref[...].astype(F32)
    c = (ckv * lax.rsqrt(jnp.mean(ckv * ckv, axis=-1, keepdims=True) + EPS)
         * kvag_ref[...]).astype(BF16)
    qall = _dot(a, wq_ref[...])
    kvall = _dot(c, wkv_ref[...])

    cs = cs_ref[...]
    lane = lax.broadcasted_iota(jnp.int32, cs.shape, 1)
    lo = lane < rope

    def rope_sumsq(rr):
        return jnp.sum(jnp.where(lo, rr * rr, 0.0), axis=-1, keepdims=True)

    def rotate(rr, gr):
        y = rr * gr * cs
        return y + pltpu.roll(y, rope, 1)

    kpe = kpe_ref[...]
    k_ss = rope_sumsq(kpe)
    k_rot = rotate(kpe, kgr_ref[...])
    for h in range(heads):
        base = 2 * LANES * h
        qn = qall[:, base:base + LANES]
        qr = qall[:, base + LANES:base + 2 * LANES]
        rq = lax.rsqrt((jnp.sum(qn * qn, axis=-1, keepdims=True) + rope_sumsq(qr)) / qk_dim + EPS) * scale
        q_out[h, :, :LANES] = (qn * qgn_ref[...] * rq).astype(BF16)
        q_out[h, :, LANES:] = jnp.where(lo, rotate(qr, qgr_ref[...]) * rq, 0.0).astype(BF16)
        kn = kvall[:, base:base + LANES]
        rk = lax.rsqrt((jnp.sum(kn * kn, axis=-1, keepdims=True) + k_ss) / qk_dim + EPS)
        k_out[h, :, :LANES] = (kn * kgn_ref[...] * rk).astype(BF16)
        k_out[h, :, LANES:] = jnp.where(lo, k_rot * rk, 0.0).astype(BF16)
        v_out[h] = kvall[:, base + LANES:base + 2 * LANES].astype(BF16)


def _mla_proj(proj, kpe2, cs, qag, kvag, wq_all, wkv_all, qgn, qgr, kgn, kgr,
              batch, seq, heads, cq_off, ckv_off, qk_dim, rope):
    m = proj.shape[0]
    ql, kvl = wq_all.shape[0], wkv_all.shape[0]
    tm = min(seq, 256)
    tpb = seq // tm
    assert cq_off % ql == 0 and ckv_off % kvl == 0
    vec = lambda n: pl.BlockSpec((1, n), lambda i: (0, 0))
    full = lambda w: pl.BlockSpec(w.shape, lambda i: (0, 0))
    hspec = lambda n: pl.BlockSpec((None, heads, tm, n), lambda i: (i // tpb, 0, i % tpb, 0))
    scale = qk_dim ** -0.5 * math.log2(math.e)
    return pl.pallas_call(
        functools.partial(_mla_proj_kernel, scale=scale, qk_dim=float(qk_dim), rope=rope, heads=heads),
        out_shape=(jax.ShapeDtypeStruct((batch, heads, seq, 2 * LANES), BF16),
                   jax.ShapeDtypeStruct((batch, heads, seq, 2 * LANES), BF16),
                   jax.ShapeDtypeStruct((batch, heads, seq, LANES), BF16)),
        grid=(m // tm,),
        in_specs=[pl.BlockSpec((tm, ql), lambda i: (i, cq_off // ql)),
                  pl.BlockSpec((tm, kvl), lambda i: (i, ckv_off // kvl)),
                  pl.BlockSpec((tm, LANES), lambda i: (i, 0)),
                  pl.BlockSpec((tm, LANES), lambda i: (i, 0)),
                  vec(ql), vec(kvl), full(wq_all), full(wkv_all),
                  vec(LANES), vec(LANES), vec(LANES), vec(LANES)],
        out_specs=(hspec(2 * LANES), hspec(2 * LANES), hspec(LANES)),
        compiler_params=_cparams("parallel"),
        name="mla_head_proj",
    )(proj, proj, kpe2, cs, qag, kvag, wq_all, wkv_all, qgn, qgr, kgn, kgr)


ATTN_KEYS = 512


def _attn_kernel(q_ref, k_ref, v_ref, o_ref):
    q = q_ref[...]
    seq = k_ref.shape[0]
    kc = min(ATTN_KEYS, seq)
    m = l = acc = None
    for c in range(seq // kc):
        rows = slice(c * kc, (c + 1) * kc)
        s = _dot_nt(q, k_ref[rows, :])
        mc = jnp.max(s, axis=-1, keepdims=True)
        if c == 0:
            m = mc
            p = jnp.exp2(s - m)
            l = jnp.sum(p, axis=-1, keepdims=True)
            acc = _dot(p.astype(BF16), v_ref[rows, :])
        else:
            m_new = jnp.maximum(m, mc)
            alpha = jnp.exp2(m - m_new)
            p = jnp.exp2(s - m_new)
            l = l * alpha + jnp.sum(p, axis=-1, keepdims=True)
            acc = acc * alpha + _dot(p.astype(BF16), v_ref[rows, :])
            m = m_new
    o_ref[...] = (acc / l).astype(BF16)


def _attention(qh, kh, vh):
    batch, mh, seq, dq = qh.shape
    dv = vh.shape[-1]
    tq = min(seq, 512)
    nq = seq // tq
    return pl.pallas_call(
        _attn_kernel,
        out_shape=jax.ShapeDtypeStruct((batch * seq, mh * dv), BF16),
        grid=(batch, mh, nq),
        in_specs=[pl.BlockSpec((None, None, tq, dq), lambda b, h, i: (b, h, i, 0)),
                  pl.BlockSpec((None, None, seq, dq), lambda b, h, i: (b, h, 0, 0)),
                  pl.BlockSpec((None, None, seq, dv), lambda b, h, i: (b, h, 0, 0))],
        out_specs=pl.BlockSpec((tq, dv), lambda b, h, i: (b * nq + i, h)),
        compiler_params=_cparams("parallel", "parallel", "arbitrary"),
        name="mla_attention",
    )(qh, kh, vh)


def _eye(rows, cols):
    r = lax.broadcasted_iota(jnp.int32, (rows, cols), 0)
    c = lax.broadcasted_iota(jnp.int32, (rows, cols), 1)
    return jnp.where(r == c, 1.0, 0.0).astype(BF16)


def _outproj_kernel(of_ref, ob_ref, hg_ref, og_ref, om_ref, w_ref, x_ref, mod_ref, g2_ref, wr_ref,
                    x1_ref, h2_ref, aff_ref, lat_ref, mix_scr, *, heads, n_exp):
    hw = of_ref.shape[1]
    o = of_ref[...] + ob_ref[...]
    gate = _silu(hg_ref[...].astype(F32))
    for h in range(heads):
        sl = slice(h * LANES, (h + 1) * LANES)
        oh = o[:, sl]
        r = lax.rsqrt(jnp.mean(oh * oh, axis=-1, keepdims=True) + EPS)
        mix_scr[:, sl] = (oh * r * og_ref[:, sl] * gate[:, sl]).astype(BF16)
    mix_scr[:, hw:] = om_ref[...]
    x1 = x_ref[...] + mod_ref[2:3, :] * _dot(mix_scr[...], w_ref[...])
    x1_ref[...] = x1
    r2 = lax.rsqrt(jnp.mean(x1 * x1, axis=-1, keepdims=True) + EPS)
    h2 = x1 * r2 * g2_ref[...] * (1.0 + mod_ref[4:5, :]) + mod_ref[3:4, :]
    h2_ref[...] = h2
    logits = _dot_hi(h2, wr_ref[...])
    lane = lax.broadcasted_iota(jnp.int32, logits.shape, 1)
    logits = jnp.where(lane < n_exp, logits, -jnp.inf)
    z = logits - jnp.max(logits, axis=-1, keepdims=True)
    p = jnp.exp(z)
    sp = jnp.sum(p, axis=-1, keepdims=True)
    aff_ref[...] = p / sp
    la = jnp.where(lane < n_exp, z - jnp.log(sp), 0.0)
    eye = _eye(n_exp, la.shape[1])
    p1, p2, p3 = _split3(la)
    lat_ref[...] = (_dot_nt(eye, p1) + _dot_nt(eye, p2)) + _dot_nt(eye, p3)


def _outproj(o_f, o_b, proj, og, o_mla, w_out_b, xf, mod6, g2, wr_pad, seq, heads, hk, n_exp):
    m, d = xf.shape
    hw = o_f.shape[1]
    mw = o_mla.shape[1]
    tm = min(seq, 256)
    tpb = seq // tm
    gcol = (3 * hk + hw) // hw
    assert (3 * hk + hw) % hw == 0
    row = lambda n: pl.BlockSpec((tm, n), lambda i: (i, 0))
    return pl.pallas_call(
        functools.partial(_outproj_kernel, heads=heads, n_exp=n_exp),
        out_shape=(jax.ShapeDtypeStruct((m, d), F32),
                   jax.ShapeDtypeStruct((m, d), F32),
                   jax.ShapeDtypeStruct((m, LANES), F32),
                   jax.ShapeDtypeStruct((m // seq, n_exp, seq), F32)),
        grid=(m // tm,),
        in_specs=[row(hw), row(hw),
                  pl.BlockSpec((tm, hw), lambda i: (i, gcol)),
                  pl.BlockSpec((1, hw), lambda i: (0, 0)),
                  row(mw),
                  pl.BlockSpec((hw + mw, d), lambda i: (0, 0)),
                  row(d),
                  pl.BlockSpec((None, 6, d), lambda i: (i // tpb, 0, 0)),
                  pl.BlockSpec((1, d), lambda i: (0, 0)),
                  pl.BlockSpec((d, LANES), lambda i: (0, 0))],
        out_specs=(row(d), row(d), row(LANES),
                   pl.BlockSpec((None, n_exp, tm), lambda i: (i // tpb, 0, i % tpb))),
        scratch_shapes=[pltpu.VMEM((tm, hw + mw), BF16)],
        compiler_params=_cparams("parallel"),
        name="outproj_norm2_router",
    )(o_f, o_b, proj, og, o_mla, w_out_b, xf, mod6, g2, wr_pad)


BISECT_STEPS = 64


def _topk_kernel(la_ref, slot_se_ref, idx_ref, tri_scr, cum_scr, *, cap, n_exp):
    nrow, seq = la_ref.shape
    ep = slot_se_ref.shape[1]
    rows = 256
    for k in range(seq // rows):
        r = lax.broadcasted_iota(jnp.int32, (rows, seq), 0) + k * rows
        c = lax.broadcasted_iota(jnp.int32, (rows, seq), 1)
        tri_scr[k * rows:(k + 1) * rows, :] = jnp.where(r < c, 1.0, 0.0).astype(BF16)

    def count(mask):
        return jnp.sum(jnp.where(mask, 1.0, 0.0), axis=-1, keepdims=True)

    def body(_, lh):
        lo, hi = lh
        mid = 0.5 * (lo + hi)
        ok = count(la_ref[...] >= mid) >= cap
        return jnp.where(ok, mid, lo), jnp.where(ok, hi, mid)

    la = la_ref[...]
    lo0 = jnp.min(la, axis=-1, keepdims=True)
    lo, hi = lax.fori_loop(0, BISECT_STEPS, body, (lo0, jnp.ones_like(lo0)))
    above = la >= hi
    tie = (la >= lo) & (la < hi)
    need = cap - count(above)
    tri = tri_scr[...]
    rank = _dot(jnp.where(tie, 1.0, 0.0).astype(BF16), tri)
    sel = above | (tie & (rank < need))
    pos = _dot(jnp.where(sel, 1.0, 0.0).astype(BF16), tri)
    slot = jnp.where(sel, pos, -1.0)
    eye = _eye(n_exp, ep)
    for b in range(nrow // n_exp):
        slot_se_ref[b * seq:(b + 1) * seq, :] = _dot_tn(
            slot[b * n_exp:(b + 1) * n_exp, :].astype(BF16), eye)
    cum_scr[...] = pos + jnp.where(sel, 1.0, 0.0)
    lane = lax.broadcasted_iota(jnp.int32, (nrow, cap), 1)

    def slot_body(c, acc):
        cnt = count(cum_scr[...] <= lax.convert_element_type(c, F32))
        return jnp.where(lane == c, cnt, acc)

    idx = lax.fori_loop(0, cap, slot_body, jnp.zeros((nrow, cap), F32), unroll=4)
    idx_ref[...] = idx.astype(jnp.int32)


def _topk(lat, batch, seq, n_exp, cap):
    return pl.pallas_call(
        functools.partial(_topk_kernel, cap=cap, n_exp=n_exp),
        out_shape=(jax.ShapeDtypeStruct((batch * seq, LANES), F32),
                   jax.ShapeDtypeStruct((batch * n_exp, cap), jnp.int32)),
        scratch_shapes=[pltpu.VMEM((seq, seq), BF16), pltpu.VMEM((batch * n_exp, seq), F32)],
        compiler_params=pltpu.CompilerParams(vmem_limit_bytes=VMEM_LIMIT),
        name="expert_choice_topk",
    )(lat.reshape(batch * n_exp, seq))


def _ffn_kernel(idx_ref, h2_hbm, wg_ref, wu_ref, wd_ref, ye_ref, xe_scr, hmid_scr, sem, *, nt, tf):
    e = pl.program_id(0)
    s = pl.program_id(1)
    rows = xe_scr.shape[0]

    def start_gather(expert):
        def body(r, carry):
            pltpu.make_async_copy(h2_hbm.at[pl.ds(idx_ref[expert, r], 1), :],
                                  xe_scr.at[pl.ds(r, 1), :], sem.at[0]).start()
            return carry
        lax.fori_loop(0, rows, body, 0, unroll=8)

    @pl.when((e == 0) & (s == 0))
    def _():
        start_gather(0)

    @pl.when(s == 0)
    def _():
        pltpu.make_async_copy(h2_hbm.at[pl.ds(0, rows), :], xe_scr, sem.at[0]).wait()

    @pl.when(s < nt)
    def _():
        xe = xe_scr[...].astype(BF16)
        a = _dot(xe, wg_ref[...].astype(BF16))
        u = _dot(xe, wu_ref[...].astype(BF16))
        hmid_scr[s] = (_silu(a) * u).astype(BF16)

    @pl.when((s == nt) & (e + 1 < pl.num_programs(0)))
    def _():
        start_gather(e + 1)

    @pl.when(s >= nt)
    def _():
        y = _dot(hmid_scr[0], wd_ref[0:tf, :].astype(BF16))
        for k in range(1, nt):
            y = y + _dot(hmid_scr[k], wd_ref[k * tf:(k + 1) * tf, :].astype(BF16))
        ye_ref[...] = y.astype(BF16)


def _ffn(idx, h2, w_gate, w_up, w_down):
    n_exp, rows = idx.shape
    d = h2.shape[1]
    ff = w_gate.shape[2]
    tf = min(ff, 512)
    tn = min(d, 512)
    nt = ff // tf
    assert d // tn == nt
    up = lambda e, s, idx: (e, 0, jnp.minimum(s, nt - 1))
    down = lambda e, s, idx: (e, 0, jnp.maximum(s - nt, 0))
    return pl.pallas_call(
        functools.partial(_ffn_kernel, nt=nt, tf=tf),
        out_shape=jax.ShapeDtypeStruct((n_exp, rows, d), BF16),
        grid_spec=pltpu.PrefetchScalarGridSpec(
            num_scalar_prefetch=1,
            grid=(n_exp, 2 * nt),
            in_specs=[pl.BlockSpec(memory_space=pl.ANY),
                      pl.BlockSpec((None, d, tf), up),
                      pl.BlockSpec((None, d, tf), up),
                      pl.BlockSpec((None, ff, tn), down)],
            out_specs=pl.BlockSpec((None, rows, tn), down),
            scratch_shapes=[pltpu.VMEM((rows, d), F32),
                            pltpu.VMEM((nt, rows, tf), BF16),
                            pltpu.SemaphoreType.DMA((1,))]),
        compiler_params=_cparams("arbitrary", "arbitrary"),
        name="expert_swiglu",
    )(idx, h2, w_gate, w_up, w_down)


def _combine_kernel(slot_ref, aff_ref, ye_ref, x1_ref, mod_ref, out_ref, *, n_exp, cap):
    tt = x1_ref.shape[0]
    cidx = lax.broadcasted_iota(jnp.int32, (tt, cap), 1).astype(F32)
    acc = jnp.zeros(x1_ref.shape, F32)
    for e in range(n_exp):
        onehot = jnp.where(cidx == slot_ref[:, e:e + 1], 1.0, 0.0).astype(BF16)
        acc = acc + aff_ref[:, e:e + 1] * _dot(onehot, ye_ref[e])
    out_ref[...] = x1_ref[...] + mod_ref[5:6, :] * acc


def _combine(slot_se, aff, ye4, x1, mod6, seq, n_exp, cap):
    m, d = x1.shape
    ep = aff.shape[1]
    batch = m // seq
    tt = min(seq, 256)
    tpb = seq // tt
    return pl.pallas_call(
        functools.partial(_combine_kernel, n_exp=n_exp, cap=cap),
        out_shape=jax.ShapeDtypeStruct((m, d), F32),
        grid=(batch, tpb),
        in_specs=[pl.BlockSpec((tt, ep), lambda b, t: (b * tpb + t, 0)),
                  pl.BlockSpec((tt, ep), lambda b, t: (b * tpb + t, 0)),
                  pl.BlockSpec((n_exp, None, cap, d), lambda b, t: (0, b, 0, 0)),
                  pl.BlockSpec((tt, d), lambda b, t: (b * tpb + t, 0)),
                  pl.BlockSpec((None, 6, d), lambda b, t: (b, 0, 0))],
        out_specs=pl.BlockSpec((tt, d), lambda b, t: (b * tpb + t, 0)),
        compiler_params=_cparams("parallel", "arbitrary"),
        name="expert_combine",
    )(slot_se, aff, ye4, x1, mod6)


def kernel(x, c, positions, w_ada, b_ada, norm1_g, w_in, lb_logits, hgrn_out_g, qa_norm_g, w_uq,
           kva_norm_g, w_ukv, q_head_g, k_head_g, w_out, norm2_g, w_router, w_gate, w_up, w_down):
    batch, seq, d = x.shape
    depth = w_ada.shape[0]
    m = batch * seq
    hk = lb_logits.shape[2]
    heads, dv = hgrn_out_g.shape[1], hgrn_out_g.shape[2]
    hw = heads * dv
    ql, kvl = qa_norm_g.shape[1], kva_norm_g.shape[1]
    qk_dim = q_head_g.shape[1]
    mh = w_uq.shape[2] // qk_dim
    d_in = w_in.shape[2]
    rope = d_in - (3 * hk + 2 * hw + ql + kvl)
    nope = qk_dim - rope
    vdim = w_ukv.shape[2] // mh - nope
    n_exp = w_router.shape[2]
    cap = EC_CAPACITY * seq // n_exp
    assert dv == LANES and hk == hw and nope == LANES and vdim == LANES and 2 * rope == LANES
    assert ql + kvl + rope <= hk and seq % GRP == 0 and n_exp <= LANES and cap % 8 == 0

    cq_off = 3 * hk + 2 * hw
    ckv_off = cq_off + ql
    kpe_off = ckv_off + kvl
    swap = jnp.concatenate([jnp.arange(rope // 2, rope), jnp.arange(0, rope // 2)])

    def both(v):
        return jnp.concatenate([v, v[..., swap]], axis=-1)

    cs = _rope_tables(positions, rope)
    c8 = jnp.pad(c, ((0, (-batch) % 8), (0, 0)))
    xf = x.reshape(m, d)
    for l in range(depth):
        mod6 = _ada(c8, w_ada[l], b_ada[l])[:batch].reshape(batch, 6, d)

        w_in_b = w_in[l].astype(BF16)
        wk_b = both(w_in[l][:, kpe_off:kpe_off + rope]).astype(BF16)
        proj, glog, kpe2 = _inproj(xf, mod6, norm1_g[l].reshape(1, d), w_in_b, wk_b, lb_logits,
                                   seq, hk, l)

        o_f, o_b = _hgrn(proj, glog, batch, seq, heads, hk)

        wq = w_uq[l].reshape(ql, mh, qk_dim)
        wq_all = jnp.concatenate([wq[..., :nope], both(wq[..., nope:])], axis=-1)
        qh, kh, vh = _mla_proj(
            proj, kpe2, cs, qa_norm_g[l].reshape(1, ql), kva_norm_g[l].reshape(1, kvl),
            wq_all.reshape(ql, mh * 2 * LANES).astype(BF16), w_ukv[l].astype(BF16),
            q_head_g[l][:nope].reshape(1, nope), both(q_head_g[l][nope:]).reshape(1, 2 * rope),
            k_head_g[l][:nope].reshape(1, nope), both(k_head_g[l][nope:]).reshape(1, 2 * rope),
            batch, seq, mh, cq_off, ckv_off, qk_dim, rope)
        o_mla = _attention(qh, kh, vh)

        wr_pad = jnp.pad(w_router[l], ((0, 0), (0, LANES - n_exp)))
        x1, h2, aff, lat = _outproj(o_f, o_b, proj, hgrn_out_g[l].reshape(1, hw), o_mla,
                                    w_out[l].astype(BF16), xf, mod6, norm2_g[l].reshape(1, d),
                                    wr_pad, seq, heads, hk, n_exp)

        slot_se, idx = _topk(lat, batch, seq, n_exp, cap)
        rows = idx.reshape(batch, n_exp, cap) + (jnp.arange(batch, dtype=jnp.int32) * seq)[:, None, None]
        rows = rows.transpose(1, 0, 2).reshape(n_exp, batch * cap)
        ye = _ffn(rows, h2, w_gate[l], w_up[l], w_down[l])
        xf = _combine(slot_se, aff, ye.reshape(n_exp, batch, cap, d), x1, mod6, seq, n_exp, cap)
    return xf.reshape(batch, seq, d)
```

```python
import functools
import math

import jax
import jax.numpy as jnp
from jax import lax
from jax.experimental import pallas as pl
from jax.experimental.pallas import tpu as pltpu

F32 = jnp.float32
BF16 = jnp.bfloat16
EPS = 1e-6
ROPE_BASE = 10000.0
LOG2E = math.log2(math.e)
EC_CAPACITY = 2
LANES = 128
TILE = 8
GRP = 128
VMEM_LIMIT = 56 * 1024 * 1024


def _cparams(*sem):
    return pltpu.CompilerParams(dimension_semantics=sem, vmem_limit_bytes=VMEM_LIMIT)


def _dot(a, b):
    return jnp.dot(a, b, preferred_element_type=F32)


def _dot_nt(a, b):
    return lax.dot_general(a, b, (((1,), (1,)), ((), ())), preferred_element_type=F32)


def _dot_tn(a, b):
    return lax.dot_general(a, b, (((0,), (0,)), ((), ())), preferred_element_type=F32)


def _split2(a):
    hi = a.astype(BF16)
    lo = (a - hi.astype(F32)).astype(BF16)
    return hi, lo


def _split3(a):
    p1 = a.astype(BF16)
    r1 = a - p1.astype(F32)
    p2 = r1.astype(BF16)
    p3 = (r1 - p2.astype(F32)).astype(BF16)
    return p1, p2, p3


def _dot_hi(a, b):
    ah, al = _split2(a)
    bh, bl = _split2(b)
    return _dot(ah, bh) + (_dot(ah, bl) + _dot(al, bh))


def _silu(x):
    return x * jax.nn.sigmoid(x)


def _rope_kernel(pos_ref, cs_ref, *, half):
    pos = pos_ref[...].astype(F32)
    lane = lax.broadcasted_iota(jnp.int32, (1, 4 * half), 1)
    j = (lane & (half - 1)).astype(F32)
    inv_freq = jnp.exp(j * (-2.0 * math.log(ROPE_BASE) / (2 * half)))
    ang = pos * inv_freq
    c = jnp.cos(ang)
    s = jnp.sin(ang)
    cs_ref[...] = jnp.where(lane < 2 * half, c, jnp.where(lane < 3 * half, -s, s))


def _rope_tables(positions, rope):
    m = positions.size
    tm = min(m, 1024)
    half = rope // 2
    return pl.pallas_call(
        functools.partial(_rope_kernel, half=half),
        out_shape=jax.ShapeDtypeStruct((m, 2 * rope), F32),
        grid=(m // tm,),
        in_specs=[pl.BlockSpec((tm, 1), lambda i: (i, 0))],
        out_specs=pl.BlockSpec((tm, 2 * rope), lambda i: (i, 0)),
        compiler_params=_cparams("parallel"),
        name="rope_tables",
    )(positions.reshape(m, 1))


def _ada_kernel(c_ref, w_ref, b_ref, o_ref):
    o_ref[...] = _dot_hi(_silu(c_ref[...]), w_ref[...]) + b_ref[...]


def _ada(c8, w, b):
    d, n = w.shape
    tn = min(d, 1024)
    assert n % tn == 0
    return pl.pallas_call(
        _ada_kernel,
        out_shape=jax.ShapeDtypeStruct((c8.shape[0], n), F32),
        grid=(n // tn,),
        in_specs=[pl.BlockSpec((c8.shape[0], d), lambda j: (0, 0)),
                  pl.BlockSpec((d, tn), lambda j: (0, j)),
                  pl.BlockSpec((1, tn), lambda j: (0, j))],
        out_specs=pl.BlockSpec((c8.shape[0], tn), lambda j: (0, j)),
        compiler_params=_cparams("parallel"),
        name="ada_mod",
    )(c8, w, b.reshape(1, n))


def _inproj_kernel(x_ref, mod_ref, g_ref, w_ref, wk_ref, lbl_ref,
                   proj_ref, glog_ref, kpe_ref, h_scr, *, layer):
    j = pl.program_id(1)

    @pl.when(j == 0)
    def _():
        x = x_ref[...]
        r = lax.rsqrt(jnp.mean(x * x, axis=-1, keepdims=True) + EPS)
        h = x * r * g_ref[...] * (1.0 + mod_ref[1:2, :]) + mod_ref[0:1, :]
        hb = h.astype(BF16)
        h_scr[...] = hb
        kpe_ref[...] = _dot(hb, wk_ref[...])

    acc = _dot(h_scr[...], w_ref[...])
    proj_ref[...] = acc.astype(BF16)

    @pl.when((j == 1) | (j == 2))
    def _():
        lg = lbl_ref[...]
        e = jnp.exp(lg - jnp.max(lg, axis=0, keepdims=True))
        lb = jnp.sum(e[:layer + 1], axis=0, keepdims=True) / jnp.sum(e, axis=0, keepdims=True)
        glog_ref[...] = jnp.log(lb + (1.0 - lb) * jax.nn.sigmoid(acc))


def _inproj(xf, mod6, g1, w_in_b, wk_b, lb_logits, seq, hk, layer):
    m, d = xf.shape
    d_in = w_in_b.shape[1]
    tm = min(seq, 1024)
    tpb = seq // tm
    tn = hk
    nj = pl.cdiv(d_in, tn)
    nl = lb_logits.shape[1]
    fdir = lambda j: jnp.clip(j - 1, 0, 1)
    return pl.pallas_call(
        functools.partial(_inproj_kernel, layer=layer),
        out_shape=(jax.ShapeDtypeStruct((m, d_in), BF16),
                   jax.ShapeDtypeStruct((m, 2 * hk), F32),
                   jax.ShapeDtypeStruct((m, LANES), F32)),
        grid=(m // tm, nj),
        in_specs=[pl.BlockSpec((tm, d), lambda i, j: (i, 0)),
                  pl.BlockSpec((None, 6, d), lambda i, j: (i // tpb, 0, 0)),
                  pl.BlockSpec((1, d), lambda i, j: (0, 0)),
                  pl.BlockSpec((d, tn), lambda i, j: (0, j)),
                  pl.BlockSpec((d, LANES), lambda i, j: (0, 0)),
                  pl.BlockSpec((None, nl, tn), lambda i, j: (fdir(j), 0, 0))],
        out_specs=(pl.BlockSpec((tm, tn), lambda i, j: (i, j)),
                   pl.BlockSpec((tm, tn), lambda i, j: (i, fdir(j))),
                   pl.BlockSpec((tm, LANES), lambda i, j: (i, 0))),
        scratch_shapes=[pltpu.VMEM((tm, d), BF16)],
        compiler_params=_cparams("parallel", "arbitrary"),
        name="norm1_inproj",
    )(xf, mod6, g1, w_in_b, wk_b, lb_logits)


def _group_cumsum(g, d):
    rin = lax.broadcasted_iota(jnp.int32, g.shape, 0) & (TILE - 1)
    b = g
    step = 1
    while step < TILE:
        if d == 0:
            b = b + jnp.where(rin >= step, pltpu.roll(b, step, 0), 0.0)
        else:
            b = b + jnp.where(rin < TILE - step, pltpu.roll(b, GRP - step, 0), 0.0)
        step *= 2
    ntile = GRP // TILE
    order = range(ntile) if d == 0 else range(ntile - 1, -1, -1)
    edge = TILE - 1 if d == 0 else 0
    out = [None] * ntile
    carry = None
    for i in order:
        t = b[i * TILE:(i + 1) * TILE]
        out[i] = t if carry is None else t + carry
        tot = t[edge:edge + 1]
        carry = tot if carry is None else carry + tot
    return jnp.concatenate(out, axis=0)


def _boundary(b, h, d):
    idx = h - 1 if d == 0 else h
    if 2 * h >= TILE:
        b3 = b.reshape(GRP // (2 * h), 2 * h, b.shape[1])
        return jnp.broadcast_to(b3[:, idx:idx + 1, :], b3.shape).reshape(b.shape)
    p = lax.broadcasted_iota(jnp.int32, b.shape, 0) & (2 * h - 1)
    out = b
    for pos in range(2 * h):
        shift = pos - idx
        if shift != 0:
            out = jnp.where(p == pos, pltpu.roll(b, shift % GRP, 0), out)
    return out


def _hgrn_kernel(q_ref, v_ref, gf_ref, gb_ref, of_ref, ob_ref, lv_scr, sg_scr, st_scr):
    seq = q_ref.shape[0]
    ngrp = seq // GRP
    nlev = GRP.bit_length()
    g_refs = (gf_ref, gb_ref)
    o_refs = (of_ref, ob_ref)

    r = lax.broadcasted_iota(jnp.int32, (GRP, GRP), 0)
    c = lax.broadcasted_iota(jnp.int32, (GRP, GRP), 1)
    lev = jnp.zeros((GRP, GRP), jnp.int32)
    for j in range(nlev - 1):
        lev = lev + jnp.where((r >> j) != (c >> j), 1, 0)
    lv_scr[0] = jnp.where(c <= r, lev, -1)
    lv_scr[1] = jnp.where(c >= r, lev, -1)
    st_scr[...] = jnp.zeros_like(st_scr)
    rr = lax.broadcasted_iota(jnp.int32, (GRP, LANES), 0)
    for l in range(1, nlev):
        late = (rr & (1 << (l - 1))) != 0
        sg_scr[0, l - 1] = jnp.where(late, 1.0, -1.0)
        sg_scr[1, l - 1] = jnp.where(late, -1.0, 1.0)

    def body(i, carry):
        for d in (0, 1):
            grp = i if d == 0 else ngrp - 1 - i
            r0 = pl.multiple_of(grp * GRP, GRP)
            g = g_refs[d][pl.ds(r0, GRP), :] * LOG2E
            qb = q_ref[pl.ds(r0, GRP), :]
            vb = v_ref[pl.ds(r0, GRP), :]
            qf = qb.astype(F32)
            kk = 1.0 - jnp.exp2(g)
            kb = kk.astype(BF16)
            b = _group_cumsum(g, d)
            edge = GRP - 1 if d == 0 else 0
            tot = b[edge:edge + 1]
            lv = lv_scr[d]
            att = jnp.where(lv == 0, _dot_nt(qb, kb), 0.0)
            for l in range(1, nlev):
                x = jnp.exp2((b - _boundary(b, 1 << (l - 1), d)) * sg_scr[d, l - 1]).astype(BF16)
                att = jnp.where(lv == l, _dot_nt(qb * x, kb * x), att)
            st = st_scr[d]
            o = _dot(att.astype(BF16), vb) + _dot_nt((qf * jnp.exp2(b)).astype(BF16), st.astype(BF16))
            o_refs[d][pl.ds(r0, GRP), :] = o
            st_scr[d] = st * jnp.exp2(tot) + _dot_tn(vb, (kk * jnp.exp2(tot - b)).astype(BF16))
        return carry

    lax.fori_loop(0, ngrp, body, 0, unroll=2)


def _hgrn(proj, glog, batch, seq, heads, hk):
    m = proj.shape[0]
    nh = hk // LANES
    vcol = 3 * nh
    blk = lambda off: pl.BlockSpec((seq, LANES), lambda b, h: (b, off + h))
    out = jax.ShapeDtypeStruct((m, hk), F32)
    return pl.pallas_call(
        _hgrn_kernel,
        out_shape=(out, out),
        grid=(batch, heads),
        in_specs=[blk(0), blk(vcol), blk(0), blk(nh)],
        out_specs=(blk(0), blk(0)),
        scratch_shapes=[pltpu.VMEM((2, GRP, GRP), jnp.int32),
                        pltpu.VMEM((2, GRP.bit_length() - 1, GRP, LANES), F32),
                        pltpu.VMEM((2, LANES, LANES), F32)],
        compiler_params=_cparams("parallel", "parallel"),
        name="hgrn2_scan",
    )(proj, proj, glog, glog)


def _mla_proj_kernel(cq_ref, ckv_ref, kpe_ref, cs_ref, qag_ref, kvag_ref, wq_ref, wkv_ref,
                     qgn_ref, qgr_ref, kgn_ref, kgr_ref, q_out, k_out, v_out,
                     *, scale, qk_dim, rope, heads):
    cq = cq_ref[...].astype(F32)
    a = (cq * lax.rsqrt(jnp.mean(cq * cq, axis=-1, keepdims=True) + EPS) * qag_ref[...]).astype(BF16)
    ckv = ckv_ref[...].astype(F32)
    c = (ckv * lax.rsqrt(jnp.mean(ckv * ckv, axis=-1, keepdims=True) + EPS)
         * kvag_ref[...]).astype(BF16)
    qall = _dot(a, wq_ref[...])
    kvall = _dot(c, wkv_ref[...])

    cs = cs_ref[...]
    lane = lax.broadcasted_iota(jnp.int32, cs.shape, 1)
    lo = lane < rope

    def rope_sumsq(rr):
        return jnp.sum(jnp.where(lo, rr * rr, 0.0), axis=-1, keepdims=True)

    def rotate(rr, gr):
        y = rr * gr * cs
        return y + pltpu.roll(y, rope, 1)

    kpe = kpe_ref[...]
    k_ss = rope_sumsq(kpe)
    k_rot = rotate(kpe, kgr_ref[...])
    for h in range(heads):
        base = 2 * LANES * h
        qn = qall[:, base:base + LANES]
        qr = qall[:, base + LANES:base + 2 * LANES]
        rq = lax.rsqrt((jnp.sum(qn * qn, axis=-1, keepdims=True) + rope_sumsq(qr)) / qk_dim + EPS) * scale
        q_out[h, :, :LANES] = (qn * qgn_ref[...] * rq).astype(BF16)
        q_out[h, :, LANES:] = jnp.where(lo, rotate(qr, qgr_ref[...]) * rq, 0.0).astype(BF16)
        kn = kvall[:, base:base + LANES]
        rk = lax.rsqrt((jnp.sum(kn * kn, axis=-1, keepdims=True) + k_ss) / qk_dim + EPS)
        k_out[h, :, :LANES] = (kn * kgn_ref[...] * rk).astype(BF16)
        k_out[h, :, LANES:] = jnp.where(lo, k_rot * rk, 0.0).astype(BF16)
        v_out[h] = kvall[:, base + LANES:base + 2 * LANES].astype(BF16)


def _mla_proj(proj, kpe2, cs, qag, kvag, wq_all, wkv_all, qgn, qgr, kgn, kgr,
              batch, seq, heads, cq_off, ckv_off, qk_dim, rope):
    m = proj.shape[0]
    ql, kvl = wq_all.shape[0], wkv_all.shape[0]
    tm = min(seq, 256)
    tpb = seq // tm
    assert cq_off % ql == 0 and ckv_off % kvl == 0
    vec = lambda n: pl.BlockSpec((1, n), lambda i: (0, 0))
    full = lambda w: pl.BlockSpec(w.shape, lambda i: (0, 0))
    hspec = lambda n: pl.BlockSpec((None, heads, tm, n), lambda i: (i // tpb, 0, i % tpb, 0))
    scale = qk_dim ** -0.5 * math.log2(math.e)
    return pl.pallas_call(
        functools.partial(_mla_proj_kernel, scale=scale, qk_dim=float(qk_dim), rope=rope, heads=heads),
        out_shape=(jax.ShapeDtypeStruct((batch, heads, seq, 2 * LANES), BF16),
                   jax.ShapeDtypeStruct((batch, heads, seq, 2 * LANES), BF16),
                   jax.ShapeDtypeStruct((batch, heads, seq, LANES), BF16)),
        grid=(m // tm,),
        in_specs=[pl.BlockSpec((tm, ql), lambda i: (i, cq_off // ql)),
                  pl.BlockSpec((tm, kvl), lambda i: (i, ckv_off // kvl)),
                  pl.BlockSpec((tm, LANES), lambda i: (i, 0)),
                  pl.BlockSpec((tm, LANES), lambda i: (i, 0)),
                  vec(ql), vec(kvl), full(wq_all), full(wkv_all),
                  vec(LANES), vec(LANES), vec(LANES), vec(LANES)],
        out_specs=(hspec(2 * LANES), hspec(2 * LANES), hspec(LANES)),
        compiler_params=_cparams("parallel"),
        name="mla_head_proj",
    )(proj, proj, kpe2, cs, qag, kvag, wq_all, wkv_all, qgn, qgr, kgn, kgr)


ATTN_KEYS = 512


def _attn_kernel(q_ref, k_ref, v_ref, o_ref):
    q = q_ref[...]
    seq = k_ref.shape[0]
    kc = min(ATTN_KEYS, seq)
    m = l = acc = None
    for c in range(seq // kc):
        rows = slice(c * kc, (c + 1) * kc)
        s = _dot_nt(q, k_ref[rows, :])
        mc = jnp.max(s, axis=-1, keepdims=True)
        if c == 0:
            m = mc
            p = jnp.exp2(s - m)
            l = jnp.sum(p, axis=-1, keepdims=True)
            acc = _dot(p.astype(BF16), v_ref[rows, :])
        else:
            m_new = jnp.maximum(m, mc)
            alpha = jnp.exp2(m - m_new)
            p = jnp.exp2(s - m_new)
            l = l * alpha + jnp.sum(p, axis=-1, keepdims=True)
            acc = acc * alpha + _dot(p.astype(BF16), v_ref[rows, :])
            m = m_new
    o_ref[...] = (acc / l).astype(BF16)


def _attention(qh, kh, vh):
    batch, mh, seq, dq = qh.shape
    dv = vh.shape[-1]
    tq = min(seq, 1024)
    nq = seq // tq
    return pl.pallas_call(
        _attn_kernel,
        out_shape=jax.ShapeDtypeStruct((batch * seq, mh * dv), BF16),
        grid=(batch, mh, nq),
        in_specs=[pl.BlockSpec((None, None, tq, dq), lambda b, h, i: (b, h, i, 0)),
                  pl.BlockSpec((None, None, seq, dq), lambda b, h, i: (b, h, 0, 0)),
                  pl.BlockSpec((None, None, seq, dv), lambda b, h, i: (b, h, 0, 0))],
        out_specs=pl.BlockSpec((tq, dv), lambda b, h, i: (b * nq + i, h)),
        compiler_params=_cparams("parallel", "parallel", "arbitrary"),
        name="mla_attention",
    )(qh, kh, vh)


def _eye(rows, cols):
    r = lax.broadcasted_iota(jnp.int32, (rows, cols), 0)
    c = lax.broadcasted_iota(jnp.int32, (rows, cols), 1)
    return jnp.where(r == c, 1.0, 0.0).astype(BF16)


def _outproj_kernel(of_ref, ob_ref, hg_ref, og_ref, om_ref, w_ref, x_ref, mod_ref, g2_ref, wr_ref,
                    x1_ref, h2_ref, aff_ref, lat_ref, mix_scr, *, heads, n_exp):
    hw = of_ref.shape[1]
    o = of_ref[...] + ob_ref[...]
    gate = _silu(hg_ref[...].astype(F32))
    for h in range(heads):
        sl = slice(h * LANES, (h + 1) * LANES)
        oh = o[:, sl]
        r = lax.rsqrt(jnp.mean(oh * oh, axis=-1, keepdims=True) + EPS)
        mix_scr[:, sl] = (oh * r * og_ref[:, sl] * gate[:, sl]).astype(BF16)
    mix_scr[:, hw:] = om_ref[...]
    x1 = x_ref[...] + mod_ref[2:3, :] * _dot(mix_scr[...], w_ref[...])
    x1_ref[...] = x1
    r2 = lax.rsqrt(jnp.mean(x1 * x1, axis=-1, keepdims=True) + EPS)
    h2 = x1 * r2 * g2_ref[...] * (1.0 + mod_ref[4:5, :]) + mod_ref[3:4, :]
    h2_ref[...] = h2
    logits = _dot_hi(h2, wr_ref[...])
    lane = lax.broadcasted_iota(jnp.int32, logits.shape, 1)
    logits = jnp.where(lane < n_exp, logits, -jnp.inf)
    z = logits - jnp.max(logits, axis=-1, keepdims=True)
    p = jnp.exp(z)
    sp = jnp.sum(p, axis=-1, keepdims=True)
    aff_ref[...] = p / sp
    la = jnp.where(lane < n_exp, z - jnp.log(sp), 0.0)
    eye = _eye(n_exp, la.shape[1])
    p1, p2, p3 = _split3(la)
    lat_ref[...] = (_dot_nt(eye, p1) + _dot_nt(eye, p2)) + _dot_nt(eye, p3)


def _outproj(o_f, o_b, proj, og, o_mla, w_out_b, xf, mod6, g2, wr_pad, seq, heads, hk, n_exp):
    m, d = xf.shape
    hw = o_f.shape[1]
    mw = o_mla.shape[1]
    tm = min(seq, 256)
    tpb = seq // tm
    gcol = (3 * hk + hw) // hw
    assert (3 * hk + hw) % hw == 0
    row = lambda n: pl.BlockSpec((tm, n), lambda i: (i, 0))
    return pl.pallas_call(
        functools.partial(_outproj_kernel, heads=heads, n_exp=n_exp),
        out_shape=(jax.ShapeDtypeStruct((m, d), F32),
                   jax.ShapeDtypeStruct((m, d), F32),
                   jax.ShapeDtypeStruct((m, LANES), F32),
                   jax.ShapeDtypeStruct((m // seq, n_exp, seq), F32)),
        grid=(m // tm,),
        in_specs=[row(hw), row(hw),
                  pl.BlockSpec((tm, hw), lambda i: (i, gcol)),
                  pl.BlockSpec((1, hw), lambda i: (0, 0)),
                  row(mw),
                  pl.BlockSpec((hw + mw, d), lambda i: (0, 0)),
                  row(d),
                  pl.BlockSpec((None, 6, d), lambda i: (i // tpb, 0, 0)),
                  pl.BlockSpec((1, d), lambda i: (0, 0)),
                  pl.BlockSpec((d, LANES), lambda i: (0, 0))],
        out_specs=(row(d), row(d), row(LANES),
                   pl.BlockSpec((None, n_exp, tm), lambda i: (i // tpb, 0, i % tpb))),
        scratch_shapes=[pltpu.VMEM((tm, hw + mw), BF16)],
        compiler_params=_cparams("parallel"),
        name="outproj_norm2_router",
    )(o_f, o_b, proj, og, o_mla, w_out_b, xf, mod6, g2, wr_pad)


BISECT_STEPS = 64


def _topk_kernel(la_ref, slot_se_ref, idx_ref, tri_scr, cum_scr, *, cap, n_exp):
    nrow, seq = la_ref.shape
    ep = slot_se_ref.shape[1]
    rows = 256
    for k in range(seq // rows):
        r = lax.broadcasted_iota(jnp.int32, (rows, seq), 0) + k * rows
        c = lax.broadcasted_iota(jnp.int32, (rows, seq), 1)
        tri_scr[k * rows:(k + 1) * rows, :] = jnp.where(r < c, 1.0, 0.0).astype(BF16)

    def count(mask):
        return jnp.sum(jnp.where(mask, 1.0, 0.0), axis=-1, keepdims=True)

    def body(_, lh):
        lo, hi = lh
        mid = 0.5 * (lo + hi)
        ok = count(la_ref[...] >= mid) >= cap
        return jnp.where(ok, mid, lo), jnp.where(ok, hi, mid)

    la = la_ref[...]
    lo0 = jnp.min(la, axis=-1, keepdims=True)
    lo, hi = lax.fori_loop(0, BISECT_STEPS, body, (lo0, jnp.ones_like(lo0)))
    above = la >= hi
    tie = (la >= lo) & (la < hi)
    need = cap - count(above)
    tri = tri_scr[...]
    rank = _dot(jnp.where(tie, 1.0, 0.0).astype(BF16), tri)
    sel = above | (tie & (rank < need))
    pos = _dot(jnp.where(sel, 1.0, 0.0).astype(BF16), tri)
    slot = jnp.where(sel, pos, -1.0)
    eye = _eye(n_exp, ep)
    for b in range(nrow // n_exp):
        slot_se_ref[b * seq:(b + 1) * seq, :] = _dot_tn(
            slot[b * n_exp:(b + 1) * n_exp, :].astype(BF16), eye)
    cum_scr[...] = pos + jnp.where(sel, 1.0, 0.0)
    lane = lax.broadcasted_iota(jnp.int32, (nrow, cap), 1)

    def slot_body(c, acc):
        cnt = count(cum_scr[...] <= lax.convert_element_type(c, F32))
        return jnp.where(lane == c, cnt, acc)

    idx = lax.fori_loop(0, cap, slot_body, jnp.zeros((nrow, cap), F32), unroll=4)
    idx_ref[...] = idx.astype(jnp.int32)


def _topk(lat, batch, seq, n_exp, cap):
    return pl.pallas_call(
        functools.partial(_topk_kernel, cap=cap, n_exp=n_exp),
        out_shape=(jax.ShapeDtypeStruct((batch * seq, LANES), F32),
                   jax.ShapeDtypeStruct((batch * n_exp, cap), jnp.int32)),
        scratch_shapes=[pltpu.VMEM((seq, seq), BF16), pltpu.VMEM((batch * n_exp, seq), F32)],
        compiler_params=pltpu.CompilerParams(vmem_limit_bytes=VMEM_LIMIT),
        name="expert_choice_topk",
    )(lat.reshape(batch * n_exp, seq))


def _ffn_kernel(idx_ref, h2_hbm, wg_ref, wu_ref, wd_ref, ye_ref, xe_scr, hmid_scr, sem, *, nt, tf):
    e = pl.program_id(0)
    s = pl.program_id(1)
    rows = xe_scr.shape[0]

    def start_gather(expert):
        base = expert * rows

        def body(k, carry):
            r0 = pl.multiple_of(k * TILE, TILE)
            for j in range(TILE):
                pltpu.make_async_copy(h2_hbm.at[pl.ds(idx_ref[base + r0 + j], 1), :],
                                      xe_scr.at[pl.ds(r0 + j, 1), :], sem.at[0]).start()
            return carry
        lax.fori_loop(0, rows // TILE, body, 0)

    @pl.when((e == 0) & (s == 0))
    def _():
        start_gather(0)

    @pl.when(s == 0)
    def _():
        pltpu.make_async_copy(h2_hbm.at[pl.ds(0, rows), :], xe_scr, sem.at[0]).wait()

    @pl.when(s < nt)
    def _():
        xe = xe_scr[...].astype(BF16)
        a = _dot(xe, wg_ref[...].astype(BF16))
        u = _dot(xe, wu_ref[...].astype(BF16))
        hmid_scr[s] = (_silu(a) * u).astype(BF16)

    def down_step(prefetch):
        per = rows // (nt * nt)
        y = None
        for k in range(nt):
            if prefetch:
                first = (e + 1) * rows + (s - nt) * (per * nt) + k * per
                dst0 = (s - nt) * (per * nt) + k * per
                for j in range(per):
                    pltpu.make_async_copy(h2_hbm.at[pl.ds(idx_ref[first + j], 1), :],
                                          xe_scr.at[pl.ds(dst0 + j, 1), :], sem.at[0]).start()
            part = _dot(hmid_scr[k], wd_ref[k * tf:(k + 1) * tf, :].astype(BF16))
            y = part if y is None else y + part
        ye_ref[...] = y.astype(BF16)

    more = e + 1 < pl.num_programs(0)

    @pl.when((s >= nt) & more)
    def _():
        down_step(True)

    @pl.when((s >= nt) & jnp.logical_not(more))
    def _():
        down_step(False)


def _ffn(idx, h2, w_gate, w_up, w_down):
    n_exp, rows = idx.shape
    idx = idx.reshape(n_exp * rows)
    d = h2.shape[1]
    ff = w_gate.shape[2]
    tf = min(ff, 512)
    tn = min(d, 512)
    nt = ff // tf
    assert d // tn == nt
    up = lambda e, s, idx: (e, 0, jnp.minimum(s, nt - 1))
    down = lambda e, s, idx: (e, 0, jnp.maximum(s - nt, 0))
    return pl.pallas_call(
        functools.partial(_ffn_kernel, nt=nt, tf=tf),
        out_shape=jax.ShapeDtypeStruct((n_exp, rows, d), BF16),
        grid_spec=pltpu.PrefetchScalarGridSpec(
            num_scalar_prefetch=1,
            grid=(n_exp, 2 * nt),
            in_specs=[pl.BlockSpec(memory_space=pl.ANY),
                      pl.BlockSpec((None, d, tf), up),
                      pl.BlockSpec((None, d, tf), up),
                      pl.BlockSpec((None, ff, tn), down)],
            out_specs=pl.BlockSpec((None, rows, tn), down),
            scratch_shapes=[pltpu.VMEM((rows, d), F32),
                            pltpu.VMEM((nt, rows, tf), BF16),
                            pltpu.SemaphoreType.DMA((1,))]),
        compiler_params=_cparams("arbitrary", "arbitrary"),
        name="expert_swiglu",
    )(idx, h2, w_gate, w_up, w_down)


def _combine_kernel(slot_ref, aff_ref, ye_ref, x1_ref, mod_ref, out_ref, *, n_exp, cap):
    tt = x1_ref.shape[0]
    cidx = lax.broadcasted_iota(jnp.int32, (tt, cap), 1).astype(F32)
    acc = jnp.zeros(x1_ref.shape, F32)
    for e in range(n_exp):
        onehot = jnp.where(cidx == slot_ref[:, e:e + 1], 1.0, 0.0).astype(BF16)
        acc = acc + aff_ref[:, e:e + 1] * _dot(onehot, ye_ref[e])
    out_ref[...] = x1_ref[...] + mod_ref[5:6, :] * acc


def _combine(slot_se, aff, ye4, x1, mod6, seq, n_exp, cap):
    m, d = x1.shape
    ep = aff.shape[1]
    batch = m // seq
    tt = min(seq, 256)
    tpb = seq // tt
    return pl.pallas_call(
        functools.partial(_combine_kernel, n_exp=n_exp, cap=cap),
        out_shape=jax.ShapeDtypeStruct((m, d), F32),
        grid=(batch, tpb),
        in_specs=[pl.BlockSpec((tt, ep), lambda b, t: (b * tpb + t, 0)),
                  pl.BlockSpec((tt, ep), lambda b, t: (b * tpb + t, 0)),
                  pl.BlockSpec((n_exp, None, cap, d), lambda b, t: (0, b, 0, 0)),
                  pl.BlockSpec((tt, d), lambda b, t: (b * tpb + t, 0)),
                  pl.BlockSpec((None, 6, d), lambda b, t: (b, 0, 0))],
        out_specs=pl.BlockSpec((tt, d), lambda b, t: (b * tpb + t, 0)),
        compiler_params=_cparams("parallel", "arbitrary"),
        name="expert_combine",
    )(slot_se, aff, ye4, x1, mod6)


def kernel(x, c, positions, w_ada, b_ada, norm1_g, w_in, lb_logits, hgrn_out_g, qa_norm_g, w_uq,
           kva_norm_g, w_ukv, q_head_g, k_head_g, w_out, norm2_g, w_router, w_gate, w_up, w_down):
    batch, seq, d = x.shape
    depth = w_ada.shape[0]
    m = batch * seq
    hk = lb_logits.shape[2]
    heads, dv = hgrn_out_g.shape[1], hgrn_out_g.shape[2]
    hw = heads * dv
    ql, kvl = qa_norm_g.shape[1], kva_norm_g.shape[1]
    qk_dim = q_head_g.shape[1]
    mh = w_uq.shape[2] // qk_dim
    d_in = w_in.shape[2]
    rope = d_in - (3 * hk + 2 * hw + ql + kvl)
    nope = qk_dim - rope
    vdim = w_ukv.shape[2] // mh - nope
    n_exp = w_router.shape[2]
    cap = EC_CAPACITY * seq // n_exp
    assert dv == LANES and hk == hw and nope == LANES and vdim == LANES and 2 * rope == LANES
    assert ql + kvl + rope <= hk and seq % GRP == 0 and n_exp <= LANES and cap % 8 == 0

    cq_off = 3 * hk + 2 * hw
    ckv_off = cq_off + ql
    kpe_off = ckv_off + kvl
    swap = jnp.concatenate([jnp.arange(rope // 2, rope), jnp.arange(0, rope // 2)])

    def both(v):
        return jnp.concatenate([v, v[..., swap]], axis=-1)

    cs = _rope_tables(positions, rope)
    c8 = jnp.pad(c, ((0, (-batch) % 8), (0, 0)))
    xf = x.reshape(m, d)
    for l in range(depth):
        mod6 = _ada(c8, w_ada[l], b_ada[l])[:batch].reshape(batch, 6, d)

        w_in_b = w_in[l].astype(BF16)
        wk_b = both(w_in[l][:, kpe_off:kpe_off + rope]).astype(BF16)
        proj, glog, kpe2 = _inproj(xf, mod6, norm1_g[l].reshape(1, d), w_in_b, wk_b, lb_logits,
                                   seq, hk, l)

        o_f, o_b = _hgrn(proj, glog, batch, seq, heads, hk)

        wq = w_uq[l].reshape(ql, mh, qk_dim)
        wq_all = jnp.concatenate([wq[..., :nope], both(wq[..., nope:])], axis=-1)
        qh, kh, vh = _mla_proj(
            proj, kpe2, cs, qa_norm_g[l].reshape(1, ql), kva_norm_g[l].reshape(1, kvl),
            wq_all.reshape(ql, mh * 2 * LANES).astype(BF16), w_ukv[l].astype(BF16),
            q_head_g[l][:nope].reshape(1, nope), both(q_head_g[l][nope:]).reshape(1, 2 * rope),
            k_head_g[l][:nope].reshape(1, nope), both(k_head_g[l][nope:]).reshape(1, 2 * rope),
            batch, seq, mh, cq_off, ckv_off, qk_dim, rope)
        o_mla = _attention(qh, kh, vh)

        wr_pad = jnp.pad(w_router[l], ((0, 0), (0, LANES - n_exp)))
        x1, h2, aff, lat = _outproj(o_f, o_b, proj, hgrn_out_g[l].reshape(1, hw), o_mla,
                                    w_out[l].astype(BF16), xf, mod6, norm2_g[l].reshape(1, d),
                                    wr_pad, seq, heads, hk, n_exp)

        slot_se, idx = _topk(lat, batch, seq, n_exp, cap)
        rows = idx.reshape(batch, n_exp, cap) + (jnp.arange(batch, dtype=jnp.int32) * seq)[:, None, None]
        rows = rows.transpose(1, 0, 2).reshape(n_exp, batch * cap)
        ye = _ffn(rows, h2, w_gate[l], w_up[l], w_down[l])
        xf = _combine(slot_se, aff, ye.reshape(n_exp, batch, cap, d), x1, mod6, seq, n_exp, cap)
    return xf.reshape(batch, seq, d)
```

```python
import functools
import math

import jax
import jax.numpy as jnp
from jax import lax
from jax.experimental import pallas as pl
from jax.experimental.pallas import tpu as pltpu

F32 = jnp.float32
BF16 = jnp.bfloat16
EPS = 1e-6
ROPE_BASE = 10000.0
LOG2E = math.log2(math.e)
EC_CAPACITY = 2
LANES = 128
TILE = 8
GRP = 128
VMEM_LIMIT = 56 * 1024 * 1024


def _cparams(*sem):
    return pltpu.CompilerParams(dimension_semantics=sem, vmem_limit_bytes=VMEM_LIMIT)


def _dot(a, b):
    return jnp.dot(a, b, preferred_element_type=F32)


def _dot_nt(a, b):
    return lax.dot_general(a, b, (((1,), (1,)), ((), ())), preferred_element_type=F32)


def _dot_tn(a, b):
    return lax.dot_general(a, b, (((0,), (0,)), ((), ())), preferred_element_type=F32)


def _split2(a):
    hi = a.astype(BF16)
    lo = (a - hi.astype(F32)).astype(BF16)
    return hi, lo


def _split3(a):
    p1 = a.astype(BF16)
    r1 = a - p1.astype(F32)
    p2 = r1.astype(BF16)
    p3 = (r1 - p2.astype(F32)).astype(BF16)
    return p1, p2, p3


def _dot_hi(a, b):
    ah, al = _split2(a)
    bh, bl = _split2(b)
    return _dot(ah, bh) + (_dot(ah, bl) + _dot(al, bh))


def _silu(x):
    return x * jax.nn.sigmoid(x)


def _rope_kernel(pos_ref, cs_ref, *, half):
    pos = pos_ref[...].astype(F32)
    lane = lax.broadcasted_iota(jnp.int32, (1, 4 * half), 1)
    j = (lane & (half - 1)).astype(F32)
    inv_freq = jnp.exp(j * (-2.0 * math.log(ROPE_BASE) / (2 * half)))
    ang = pos * inv_freq
    c = jnp.cos(ang)
    s = jnp.sin(ang)
    cs_ref[...] = jnp.where(lane < 2 * half, c, jnp.where(lane < 3 * half, -s, s))


def _rope_tables(positions, rope):
    m = positions.size
    tm = min(m, 1024)
    half = rope // 2
    return pl.pallas_call(
        functools.partial(_rope_kernel, half=half),
        out_shape=jax.ShapeDtypeStruct((m, 2 * rope), F32),
        grid=(m // tm,),
        in_specs=[pl.BlockSpec((tm, 1), lambda i: (i, 0))],
        out_specs=pl.BlockSpec((tm, 2 * rope), lambda i: (i, 0)),
        compiler_params=_cparams("parallel"),
        name="rope_tables",
    )(positions.reshape(m, 1))


def _ada_kernel(c_ref, w_ref, b_ref, o_ref):
    o_ref[...] = _dot_hi(_silu(c_ref[...]), w_ref[...]) + b_ref[...]


def _ada(c8, w, b):
    d, n = w.shape
    tn = min(d, 1024)
    assert n % tn == 0
    return pl.pallas_call(
        _ada_kernel,
        out_shape=jax.ShapeDtypeStruct((c8.shape[0], n), F32),
        grid=(n // tn,),
        in_specs=[pl.BlockSpec((c8.shape[0], d), lambda j: (0, 0)),
                  pl.BlockSpec((d, tn), lambda j: (0, j)),
                  pl.BlockSpec((1, tn), lambda j: (0, j))],
        out_specs=pl.BlockSpec((c8.shape[0], tn), lambda j: (0, j)),
        compiler_params=_cparams("parallel"),
        name="ada_mod",
    )(c8, w, b.reshape(1, n))


def _inproj_kernel(x_ref, mod_ref, g_ref, w_ref, wk_ref, lbl_ref,
                   proj_ref, glog_ref, kpe_ref, h_scr, *, layer):
    j = pl.program_id(1)

    @pl.when(j == 0)
    def _():
        x = x_ref[...]
        r = lax.rsqrt(jnp.mean(x * x, axis=-1, keepdims=True) + EPS)
        h = x * r * g_ref[...] * (1.0 + mod_ref[1:2, :]) + mod_ref[0:1, :]
        hb = h.astype(BF16)
        h_scr[...] = hb
        kpe_ref[...] = _dot(hb, wk_ref[...])

    acc = _dot(h_scr[...], w_ref[...])
    proj_ref[...] = acc.astype(BF16)

    @pl.when((j == 1) | (j == 2))
    def _():
        lg = lbl_ref[...]
        e = jnp.exp(lg - jnp.max(lg, axis=0, keepdims=True))
        lb = jnp.sum(e[:layer + 1], axis=0, keepdims=True) / jnp.sum(e, axis=0, keepdims=True)
        glog_ref[...] = jnp.log(lb + (1.0 - lb) * jax.nn.sigmoid(acc))


def _inproj(xf, mod6, g1, w_in_b, wk_b, lb_logits, seq, hk, layer):
    m, d = xf.shape
    d_in = w_in_b.shape[1]
    tm = min(seq, 1024)
    tpb = seq // tm
    tn = hk
    nj = pl.cdiv(d_in, tn)
    nl = lb_logits.shape[1]
    fdir = lambda j: jnp.clip(j - 1, 0, 1)
    return pl.pallas_call(
        functools.partial(_inproj_kernel, layer=layer),
        out_shape=(jax.ShapeDtypeStruct((m, d_in), BF16),
                   jax.ShapeDtypeStruct((m, 2 * hk), F32),
                   jax.ShapeDtypeStruct((m, LANES), F32)),
        grid=(m // tm, nj),
        in_specs=[pl.BlockSpec((tm, d), lambda i, j: (i, 0)),
                  pl.BlockSpec((None, 6, d), lambda i, j: (i // tpb, 0, 0)),
                  pl.BlockSpec((1, d), lambda i, j: (0, 0)),
                  pl.BlockSpec((d, tn), lambda i, j: (0, j)),
                  pl.BlockSpec((d, LANES), lambda i, j: (0, 0)),
                  pl.BlockSpec((None, nl, tn), lambda i, j: (fdir(j), 0, 0))],
        out_specs=(pl.BlockSpec((tm, tn), lambda i, j: (i, j)),
                   pl.BlockSpec((tm, tn), lambda i, j: (i, fdir(j))),
                   pl.BlockSpec((tm, LANES), lambda i, j: (i, 0))),
        scratch_shapes=[pltpu.VMEM((tm, d), BF16)],
        compiler_params=_cparams("parallel", "arbitrary"),
        name="norm1_inproj",
    )(xf, mod6, g1, w_in_b, wk_b, lb_logits)


def _group_cumsum(g, d):
    rin = lax.broadcasted_iota(jnp.int32, g.shape, 0) & (TILE - 1)
    b = g
    step = 1
    while step < TILE:
        if d == 0:
            b = b + jnp.where(rin >= step, pltpu.roll(b, step, 0), 0.0)
        else:
            b = b + jnp.where(rin < TILE - step, pltpu.roll(b, GRP - step, 0), 0.0)
        step *= 2
    ntile = GRP // TILE
    order = range(ntile) if d == 0 else range(ntile - 1, -1, -1)
    edge = TILE - 1 if d == 0 else 0
    out = [None] * ntile
    carry = None
    for i in order:
        t = b[i * TILE:(i + 1) * TILE]
        out[i] = t if carry is None else t + carry
        tot = t[edge:edge + 1]
        carry = tot if carry is None else carry + tot
    return jnp.concatenate(out, axis=0)


def _boundary(b, h, d):
    idx = h - 1 if d == 0 else h
    if 2 * h >= TILE:
        b3 = b.reshape(GRP // (2 * h), 2 * h, b.shape[1])
        return jnp.broadcast_to(b3[:, idx:idx + 1, :], b3.shape).reshape(b.shape)
    p = lax.broadcasted_iota(jnp.int32, b.shape, 0) & (2 * h - 1)
    out = b
    for pos in range(2 * h):
        shift = pos - idx
        if shift != 0:
            out = jnp.where(p == pos, pltpu.roll(b, shift % GRP, 0), out)
    return out


def _hgrn_kernel(q_ref, v_ref, gf_ref, gb_ref, of_ref, ob_ref, lv_scr, sg_scr, st_scr):
    seq = q_ref.shape[0]
    ngrp = seq // GRP
    nlev = GRP.bit_length()
    g_refs = (gf_ref, gb_ref)
    o_refs = (of_ref, ob_ref)

    r = lax.broadcasted_iota(jnp.int32, (GRP, GRP), 0)
    c = lax.broadcasted_iota(jnp.int32, (GRP, GRP), 1)
    lev = jnp.zeros((GRP, GRP), jnp.int32)
    for j in range(nlev - 1):
        lev = lev + jnp.where((r >> j) != (c >> j), 1, 0)
    lv_scr[0] = jnp.where(c <= r, lev, -1)
    lv_scr[1] = jnp.where(c >= r, lev, -1)
    st_scr[...] = jnp.zeros_like(st_scr)
    rr = lax.broadcasted_iota(jnp.int32, (GRP, LANES), 0)
    for l in range(1, nlev):
        late = (rr & (1 << (l - 1))) != 0
        sg_scr[0, l - 1] = jnp.where(late, 1.0, -1.0)
        sg_scr[1, l - 1] = jnp.where(late, -1.0, 1.0)

    def body(i, carry):
        for d in (0, 1):
            grp = i if d == 0 else ngrp - 1 - i
            r0 = pl.multiple_of(grp * GRP, GRP)
            g = g_refs[d][pl.ds(r0, GRP), :] * LOG2E
            qb = q_ref[pl.ds(r0, GRP), :]
            vb = v_ref[pl.ds(r0, GRP), :]
            qf = qb.astype(F32)
            kk = 1.0 - jnp.exp2(g)
            kb = kk.astype(BF16)
            b = _group_cumsum(g, d)
            edge = GRP - 1 if d == 0 else 0
            tot = b[edge:edge + 1]
            lv = lv_scr[d]
            att = jnp.where(lv == 0, _dot_nt(qb, kb), 0.0)
            for l in range(1, nlev):
                x = jnp.exp2((b - _boundary(b, 1 << (l - 1), d)) * sg_scr[d, l - 1]).astype(BF16)
                att = jnp.where(lv == l, _dot_nt(qb * x, kb * x), att)
            st = st_scr[d]
            o = _dot(att.astype(BF16), vb) + _dot_nt((qf * jnp.exp2(b)).astype(BF16), st.astype(BF16))
            o_refs[d][pl.ds(r0, GRP), :] = o
            st_scr[d] = st * jnp.exp2(tot) + _dot_tn(vb, (kk * jnp.exp2(tot - b)).astype(BF16))
        return carry

    lax.fori_loop(0, ngrp, body, 0, unroll=2)


def _hgrn(proj, glog, batch, seq, heads, hk):
    m = proj.shape[0]
    nh = hk // LANES
    vcol = 3 * nh
    blk = lambda off: pl.BlockSpec((seq, LANES), lambda b, h: (b, off + h))
    out = jax.ShapeDtypeStruct((m, hk), F32)
    return pl.pallas_call(
        _hgrn_kernel,
        out_shape=(out, out),
        grid=(batch, heads),
        in_specs=[blk(0), blk(vcol), blk(0), blk(nh)],
        out_specs=(blk(0), blk(0)),
        scratch_shapes=[pltpu.VMEM((2, GRP, GRP), jnp.int32),
                        pltpu.VMEM((2, GRP.bit_length() - 1, GRP, LANES), F32),
                        pltpu.VMEM((2, LANES, LANES), F32)],
        compiler_params=_cparams("parallel", "parallel"),
        name="hgrn2_scan",
    )(proj, proj, glog, glog)


def _mla_proj_kernel(cq_ref, ckv_ref, kpe_ref, cs_ref, qag_ref, kvag_ref, wq_ref, wkv_ref,
                     qgn_ref, qgr_ref, kgn_ref, kgr_ref, q_out, k_out, v_out,
                     *, scale, qk_dim, rope, heads):
    cq = cq_ref[...].astype(F32)
    a = (cq * lax.rsqrt(jnp.mean(cq * cq, axis=-1, keepdims=True) + EPS) * qag_ref[...]).astype(BF16)
    ckv = ckv_ref[...].astype(F32)
    c = (ckv * lax.rsqrt(jnp.mean(ckv * ckv, axis=-1, keepdims=True) + EPS)
         * kvag_ref[...]).astype(BF16)
    qall = _dot(a, wq_ref[...])
    kvall = _dot(c, wkv_ref[...])

    cs = cs_ref[...]
    lane = lax.broadcasted_iota(jnp.int32, cs.shape, 1)
    lo = lane < rope

    def rope_sumsq(rr):
        return jnp.sum(jnp.where(lo, rr * rr, 0.0), axis=-1, keepdims=True)

    def rotate(rr, gr):
        y = rr * gr * cs
        return y + pltpu.roll(y, rope, 1)

    kpe = kpe_ref[...]
    k_ss = rope_sumsq(kpe)
    k_rot = rotate(kpe, kgr_ref[...])
    for h in range(heads):
        base = 2 * LANES * h
        qn = qall[:, base:base + LANES]
        qr = qall[:, base + LANES:base + 2 * LANES]
        rq = lax.rsqrt((jnp.sum(qn * qn, axis=-1, keepdims=True) + rope_sumsq(qr)) / qk_dim + EPS) * scale
        q_out[h, :, :LANES] = (qn * qgn_ref[...] * rq).astype(BF16)
        q_out[h, :, LANES:] = jnp.where(lo, rotate(qr, qgr_ref[...]) * rq, 0.0).astype(BF16)
        kn = kvall[:, base:base + LANES]
        rk = lax.rsqrt((jnp.sum(kn * kn, axis=-1, keepdims=True) + k_ss) / qk_dim + EPS)
        k_out[h, :, :LANES] = (kn * kgn_ref[...] * rk).astype(BF16)
        k_out[h, :, LANES:] = jnp.where(lo, k_rot * rk, 0.0).astype(BF16)
        v_out[h] = kvall[:, base + LANES:base + 2 * LANES].astype(BF16)


def _mla_proj(proj, kpe2, cs, qag, kvag, wq_all, wkv_all, qgn, qgr, kgn, kgr,
              batch, seq, heads, cq_off, ckv_off, qk_dim, rope):
    m = proj.shape[0]
    ql, kvl = wq_all.shape[0], wkv_all.shape[0]
    tm = min(seq, 256)
    tpb = seq // tm
    assert cq_off % ql == 0 and ckv_off % kvl == 0
    vec = lambda n: pl.BlockSpec((1, n), lambda i: (0, 0))
    full = lambda w: pl.BlockSpec(w.shape, lambda i: (0, 0))
    hspec = lambda n: pl.BlockSpec((None, heads, tm, n), lambda i: (i // tpb, 0, i % tpb, 0))
    scale = qk_dim ** -0.5 * math.log2(math.e)
    return pl.pallas_call(
        functools.partial(_mla_proj_kernel, scale=scale, qk_dim=float(qk_dim), rope=rope, heads=heads),
        out_shape=(jax.ShapeDtypeStruct((batch, heads, seq, 2 * LANES), BF16),
                   jax.ShapeDtypeStruct((batch, heads, seq, 2 * LANES), BF16),
                   jax.ShapeDtypeStruct((batch, heads, seq, LANES), BF16)),
        grid=(m // tm,),
        in_specs=[pl.BlockSpec((tm, ql), lambda i: (i, cq_off // ql)),
                  pl.BlockSpec((tm, kvl), lambda i: (i, ckv_off // kvl)),
                  pl.BlockSpec((tm, LANES), lambda i: (i, 0)),
                  pl.BlockSpec((tm, LANES), lambda i: (i, 0)),
                  vec(ql), vec(kvl), full(wq_all), full(wkv_all),
                  vec(LANES), vec(LANES), vec(LANES), vec(LANES)],
        out_specs=(hspec(2 * LANES), hspec(2 * LANES), hspec(LANES)),
        compiler_params=_cparams("parallel"),
        name="mla_head_proj",
    )(proj, proj, kpe2, cs, qag, kvag, wq_all, wkv_all, qgn, qgr, kgn, kgr)


ATTN_KEYS = 512


def _attn_kernel(q_ref, k_ref, v_ref, o_ref):
    q = q_ref[...]
    seq = k_ref.shape[0]
    kc = min(ATTN_KEYS, seq)
    m = l = acc = None
    for c in range(seq // kc):
        rows = slice(c * kc, (c + 1) * kc)
        s = _dot_nt(q, k_ref[rows, :])
        mc = jnp.max(s, axis=-1, keepdims=True)
        if c == 0:
            m = mc
            p = jnp.exp2(s - m)
            l = jnp.sum(p, axis=-1, keepdims=True)
            acc = _dot(p.astype(BF16), v_ref[rows, :])
        else:
            m_new = jnp.maximum(m, mc)
            alpha = jnp.exp2(m - m_new)
            p = jnp.exp2(s - m_new)
            l = l * alpha + jnp.sum(p, axis=-1, keepdims=True)
            acc = acc * alpha + _dot(p.astype(BF16), v_ref[rows, :])
            m = m_new
    o_ref[...] = (acc / l).astype(BF16)


def _attention(qh, kh, vh):
    batch, mh, seq, dq = qh.shape
    dv = vh.shape[-1]
    tq = min(seq, 1024)
    nq = seq // tq
    return pl.pallas_call(
        _attn_kernel,
        out_shape=jax.ShapeDtypeStruct((batch * seq, mh * dv), BF16),
        grid=(batch, mh, nq),
        in_specs=[pl.BlockSpec((None, None, tq, dq), lambda b, h, i: (b, h, i, 0)),
                  pl.BlockSpec((None, None, seq, dq), lambda b, h, i: (b, h, 0, 0)),
                  pl.BlockSpec((None, None, seq, dv), lambda b, h, i: (b, h, 0, 0))],
        out_specs=pl.BlockSpec((tq, dv), lambda b, h, i: (b * nq + i, h)),
        compiler_params=_cparams("parallel", "parallel", "arbitrary"),
        name="mla_attention",
    )(qh, kh, vh)


def _eye(rows, cols):
    r = lax.broadcasted_iota(jnp.int32, (rows, cols), 0)
    c = lax.broadcasted_iota(jnp.int32, (rows, cols), 1)
    return jnp.where(r == c, 1.0, 0.0).astype(BF16)


def _outproj_kernel(of_ref, ob_ref, hg_ref, og_ref, om_ref, w_ref, x_ref, mod_ref, g2_ref, wr_ref,
                    x1_ref, h2_ref, aff_ref, lat_ref, mix_scr, *, heads, n_exp):
    hw = of_ref.shape[1]
    o = of_ref[...] + ob_ref[...]
    gate = _silu(hg_ref[...].astype(F32))
    for h in range(heads):
        sl = slice(h * LANES, (h + 1) * LANES)
        oh = o[:, sl]
        r = lax.rsqrt(jnp.mean(oh * oh, axis=-1, keepdims=True) + EPS)
        mix_scr[:, sl] = (oh * r * og_ref[:, sl] * gate[:, sl]).astype(BF16)
    mix_scr[:, hw:] = om_ref[...]
    x1 = x_ref[...] + mod_ref[2:3, :] * _dot(mix_scr[...], w_ref[...])
    x1_ref[...] = x1
    r2 = lax.rsqrt(jnp.mean(x1 * x1, axis=-1, keepdims=True) + EPS)
    h2 = x1 * r2 * g2_ref[...] * (1.0 + mod_ref[4:5, :]) + mod_ref[3:4, :]
    h2_ref[...] = h2
    logits = _dot_hi(h2, wr_ref[...])
    lane = lax.broadcasted_iota(jnp.int32, logits.shape, 1)
    logits = jnp.where(lane < n_exp, logits, -jnp.inf)
    z = logits - jnp.max(logits, axis=-1, keepdims=True)
    p = jnp.exp(z)
    sp = jnp.sum(p, axis=-1, keepdims=True)
    aff_ref[...] = p / sp
    la = jnp.where(lane < n_exp, z - jnp.log(sp), 0.0)
    eye = _eye(n_exp, la.shape[1])
    p1, p2, p3 = _split3(la)
    lat_ref[...] = (_dot_nt(eye, p1) + _dot_nt(eye, p2)) + _dot_nt(eye, p3)


def _outproj(o_f, o_b, proj, og, o_mla, w_out_b, xf, mod6, g2, wr_pad, seq, heads, hk, n_exp):
    m, d = xf.shape
    hw = o_f.shape[1]
    mw = o_mla.shape[1]
    tm = min(seq, 256)
    tpb = seq // tm
    gcol = (3 * hk + hw) // hw
    assert (3 * hk + hw) % hw == 0
    row = lambda n: pl.BlockSpec((tm, n), lambda i: (i, 0))
    return pl.pallas_call(
        functools.partial(_outproj_kernel, heads=heads, n_exp=n_exp),
        out_shape=(jax.ShapeDtypeStruct((m, d), F32),
                   jax.ShapeDtypeStruct((m, d), F32),
                   jax.ShapeDtypeStruct((m, LANES), F32),
                   jax.ShapeDtypeStruct((m // seq, n_exp, seq), F32)),
        grid=(m // tm,),
        in_specs=[row(hw), row(hw),
                  pl.BlockSpec((tm, hw), lambda i: (i, gcol)),
                  pl.BlockSpec((1, hw), lambda i: (0, 0)),
                  row(mw),
                  pl.BlockSpec((hw + mw, d), lambda i: (0, 0)),
                  row(d),
                  pl.BlockSpec((None, 6, d), lambda i: (i // tpb, 0, 0)),
                  pl.BlockSpec((1, d), lambda i: (0, 0)),
                  pl.BlockSpec((d, LANES), lambda i: (0, 0))],
        out_specs=(row(d), row(d), row(LANES),
                   pl.BlockSpec((None, n_exp, tm), lambda i: (i // tpb, 0, i % tpb))),
        scratch_shapes=[pltpu.VMEM((tm, hw + mw), BF16)],
        compiler_params=_cparams("parallel"),
        name="outproj_norm2_router",
    )(o_f, o_b, proj, og, o_mla, w_out_b, xf, mod6, g2, wr_pad)


BISECT_STEPS = 64


COMBINE_TILE = 256
COMBINE_WIN = 64


def _topk_kernel(la_ref, slot_se_ref, idx_ref, tab_ref, tri_scr, cum_scr, *, cap, n_exp):
    nrow, seq = la_ref.shape
    ep = slot_se_ref.shape[1]
    rows = 256
    for k in range(seq // rows):
        r = lax.broadcasted_iota(jnp.int32, (rows, seq), 0) + k * rows
        c = lax.broadcasted_iota(jnp.int32, (rows, seq), 1)
        tri_scr[k * rows:(k + 1) * rows, :] = jnp.where(r < c, 1.0, 0.0).astype(BF16)

    def count(mask):
        return jnp.sum(jnp.where(mask, 1.0, 0.0), axis=-1, keepdims=True)

    def body(_, lh):
        lo, hi = lh
        mid = 0.5 * (lo + hi)
        ok = count(la_ref[...] >= mid) >= cap
        return jnp.where(ok, mid, lo), jnp.where(ok, hi, mid)

    la = la_ref[...]
    lo0 = jnp.min(la, axis=-1, keepdims=True)
    lo, hi = lax.fori_loop(0, BISECT_STEPS, body, (lo0, jnp.ones_like(lo0)))
    above = la >= hi
    tie = (la >= lo) & (la < hi)
    need = cap - count(above)
    tri = tri_scr[...]
    rank = _dot(jnp.where(tie, 1.0, 0.0).astype(BF16), tri)
    sel = above | (tie & (rank < need))
    pos = _dot(jnp.where(sel, 1.0, 0.0).astype(BF16), tri)
    slot = jnp.where(sel, pos, -1.0)
    eye = _eye(n_exp, ep)
    for b in range(nrow // n_exp):
        slot_se_ref[b * seq:(b + 1) * seq, :] = _dot_tn(
            slot[b * n_exp:(b + 1) * n_exp, :].astype(BF16), eye)
    cum_scr[...] = pos + jnp.where(sel, 1.0, 0.0)
    lane = lax.broadcasted_iota(jnp.int32, (nrow, cap), 1)

    def slot_body(c, acc):
        cnt = count(cum_scr[...] <= lax.convert_element_type(c, F32))
        return jnp.where(lane == c, cnt, acc)

    idx = lax.fori_loop(0, cap, slot_body, jnp.zeros((nrow, cap), F32), unroll=4)
    idx_ref[...] = idx.astype(jnp.int32)
    tok = lax.broadcasted_iota(jnp.int32, (nrow, seq), 1)
    tlane = lax.broadcasted_iota(jnp.int32, tab_ref.shape, 1)
    tab = jnp.zeros(tab_ref.shape, F32)
    tile = min(seq, COMBINE_TILE)
    for k in range(seq // tile + 1):
        tab = jnp.where(tlane == k, count(sel & (tok < k * tile)), tab)
    tab_ref[...] = tab.astype(jnp.int32)


def _topk(lat, batch, seq, n_exp, cap):
    return pl.pallas_call(
        functools.partial(_topk_kernel, cap=cap, n_exp=n_exp),
        out_shape=(jax.ShapeDtypeStruct((batch * seq, LANES), F32),
                   jax.ShapeDtypeStruct((batch * n_exp, cap), jnp.int32),
                   jax.ShapeDtypeStruct((batch * n_exp, LANES), jnp.int32)),
        scratch_shapes=[pltpu.VMEM((seq, seq), BF16), pltpu.VMEM((batch * n_exp, seq), F32)],
        compiler_params=pltpu.CompilerParams(vmem_limit_bytes=VMEM_LIMIT),
        name="expert_choice_topk",
    )(lat.reshape(batch * n_exp, seq))


def _ffn_kernel(idx_ref, h2_hbm, wg_ref, wu_ref, wd_ref, ye_ref, xe_scr, hmid_scr, sem, *, nt, tf):
    e = pl.program_id(0)
    s = pl.program_id(1)
    rows = xe_scr.shape[0]

    def start_gather(expert):
        base = expert * rows

        def body(k, carry):
            r0 = pl.multiple_of(k * TILE, TILE)
            for j in range(TILE):
                pltpu.make_async_copy(h2_hbm.at[pl.ds(idx_ref[base + r0 + j], 1), :],
                                      xe_scr.at[pl.ds(r0 + j, 1), :], sem.at[0]).start()
            return carry
        lax.fori_loop(0, rows // TILE, body, 0)

    @pl.when((e == 0) & (s == 0))
    def _():
        start_gather(0)

    @pl.when(s == 0)
    def _():
        pltpu.make_async_copy(h2_hbm.at[pl.ds(0, rows), :], xe_scr, sem.at[0]).wait()

    @pl.when(s < nt)
    def _():
        xe = xe_scr[...].astype(BF16)
        a = _dot(xe, wg_ref[...].astype(BF16))
        u = _dot(xe, wu_ref[...].astype(BF16))
        hmid_scr[s] = (_silu(a) * u).astype(BF16)

    def down_step(prefetch):
        per = rows // (nt * nt)
        y = None
        for k in range(nt):
            if prefetch:
                first = (e + 1) * rows + (s - nt) * (per * nt) + k * per
                dst0 = (s - nt) * (per * nt) + k * per
                for j in range(per):
                    pltpu.make_async_copy(h2_hbm.at[pl.ds(idx_ref[first + j], 1), :],
                                          xe_scr.at[pl.ds(dst0 + j, 1), :], sem.at[0]).start()
            part = _dot(hmid_scr[k], wd_ref[k * tf:(k + 1) * tf, :].astype(BF16))
            y = part if y is None else y + part
        ye_ref[...] = y.astype(BF16)

    more = e + 1 < pl.num_programs(0)

    @pl.when((s >= nt) & more)
    def _():
        down_step(True)

    @pl.when((s >= nt) & jnp.logical_not(more))
    def _():
        down_step(False)


def _ffn(idx, h2, w_gate, w_up, w_down):
    n_exp, rows = idx.shape
    idx = idx.reshape(n_exp * rows)
    d = h2.shape[1]
    ff = w_gate.shape[2]
    tf = min(ff, 512)
    tn = min(d, 512)
    nt = ff // tf
    assert d // tn == nt
    up = lambda e, s, idx: (e, 0, jnp.minimum(s, nt - 1))
    down = lambda e, s, idx: (e, 0, jnp.maximum(s - nt, 0))
    return pl.pallas_call(
        functools.partial(_ffn_kernel, nt=nt, tf=tf),
        out_shape=jax.ShapeDtypeStruct((n_exp, rows, d), BF16),
        grid_spec=pltpu.PrefetchScalarGridSpec(
            num_scalar_prefetch=1,
            grid=(n_exp, 2 * nt),
            in_specs=[pl.BlockSpec(memory_space=pl.ANY),
                      pl.BlockSpec((None, d, tf), up),
                      pl.BlockSpec((None, d, tf), up),
                      pl.BlockSpec((None, ff, tn), down)],
            out_specs=pl.BlockSpec((None, rows, tn), down),
            scratch_shapes=[pltpu.VMEM((rows, d), F32),
                            pltpu.VMEM((nt, rows, tf), BF16),
                            pltpu.SemaphoreType.DMA((1,))]),
        compiler_params=_cparams("arbitrary", "arbitrary"),
        name="expert_swiglu",
    )(idx, h2, w_gate, w_up, w_down)


def _combine_kernel(tab_ref, slot_ref, aff_ref, ye_ref, x1_ref, mod_ref, out_ref, y_scr,
                    *, n_exp, cap, win):
    b = pl.program_id(0)
    t = pl.program_id(1)
    tt = x1_ref.shape[0]
    base = (b * (pl.num_programs(1) + 1) + t) * n_exp
    pack = 16
    starts = []
    short = None
    for e in range(n_exp):
        c0 = tab_ref[base + e]
        c1 = tab_ref[base + n_exp + e]
        a = jnp.minimum(c0 & ~(pack - 1), cap - win)
        ok = c1 - a <= win
        starts.append(a)
        short = ok if short is None else short & ok

    def finish(acc):
        out_ref[...] = x1_ref[...] + mod_ref[5:6, :] * acc

    @pl.when(short)
    def _():
        lane = lax.broadcasted_iota(jnp.int32, (tt, LANES), 1).astype(F32)
        per = LANES // win
        blocks = []
        for g in range(n_exp // per):
            blk = jnp.zeros((tt, LANES), F32)
            for j in range(per):
                e = g * per + j
                a = pl.multiple_of(starts[e], pack)
                y_scr[e * win:(e + 1) * win, :] = ye_ref[e, pl.ds(a, win), :]
                slot = slot_ref[:, e:e + 1]
                rel = jnp.where(slot >= 0.0, slot - a.astype(F32) + float(j * win), -1.0)
                blk = jnp.where(lane == rel, aff_ref[:, e:e + 1], blk)
            blocks.append(blk.astype(BF16))
        finish(_dot(jnp.concatenate(blocks, axis=1), y_scr[...]))

    @pl.when(jnp.logical_not(short))
    def _():
        cidx = lax.broadcasted_iota(jnp.int32, (tt, cap), 1).astype(F32)
        acc = jnp.zeros(x1_ref.shape, F32)
        for e in range(n_exp):
            onehot = jnp.where(cidx == slot_ref[:, e:e + 1], 1.0, 0.0).astype(BF16)
            acc = acc + aff_ref[:, e:e + 1] * _dot(onehot, ye_ref[e])
        finish(acc)


def _combine(tab, slot_se, aff, ye4, x1, mod6, seq, n_exp, cap):
    m, d = x1.shape
    ep = slot_se.shape[1]
    batch = m // seq
    tt = min(seq, COMBINE_TILE)
    tpb = seq // tt
    win = min(COMBINE_WIN, cap)
    assert LANES % win == 0 and n_exp % (LANES // win) == 0 and cap % 16 == 0
    return pl.pallas_call(
        functools.partial(_combine_kernel, n_exp=n_exp, cap=cap, win=win),
        out_shape=jax.ShapeDtypeStruct((m, d), F32),
        grid_spec=pltpu.PrefetchScalarGridSpec(
            num_scalar_prefetch=1,
            grid=(batch, tpb),
            in_specs=[pl.BlockSpec((tt, ep), lambda b, t, tab: (b * tpb + t, 0)),
                      pl.BlockSpec((tt, ep), lambda b, t, tab: (b * tpb + t, 0)),
                      pl.BlockSpec((n_exp, None, cap, d), lambda b, t, tab: (0, b, 0, 0)),
                      pl.BlockSpec((tt, d), lambda b, t, tab: (b * tpb + t, 0)),
                      pl.BlockSpec((None, 6, d), lambda b, t, tab: (b, 0, 0))],
            out_specs=pl.BlockSpec((tt, d), lambda b, t, tab: (b * tpb + t, 0)),
            scratch_shapes=[pltpu.VMEM((n_exp * win, d), BF16)]),
        compiler_params=_cparams("parallel", "arbitrary"),
        name="expert_combine",
    )(tab, slot_se, aff, ye4, x1, mod6)


def kernel(x, c, positions, w_ada, b_ada, norm1_g, w_in, lb_logits, hgrn_out_g, qa_norm_g, w_uq,
           kva_norm_g, w_ukv, q_head_g, k_head_g, w_out, norm2_g, w_router, w_gate, w_up, w_down):
    batch, seq, d = x.shape
    depth = w_ada.shape[0]
    m = batch * seq
    hk = lb_logits.shape[2]
    heads, dv = hgrn_out_g.shape[1], hgrn_out_g.shape[2]
    hw = heads * dv
    ql, kvl = qa_norm_g.shape[1], kva_norm_g.shape[1]
    qk_dim = q_head_g.shape[1]
    mh = w_uq.shape[2] // qk_dim
    d_in = w_in.shape[2]
    rope = d_in - (3 * hk + 2 * hw + ql + kvl)
    nope = qk_dim - rope
    vdim = w_ukv.shape[2] // mh - nope
    n_exp = w_router.shape[2]
    cap = EC_CAPACITY * seq // n_exp
    assert dv == LANES and hk == hw and nope == LANES and vdim == LANES and 2 * rope == LANES
    assert ql + kvl + rope <= hk and seq % GRP == 0 and n_exp <= LANES and cap % 8 == 0

    cq_off = 3 * hk + 2 * hw
    ckv_off = cq_off + ql
    kpe_off = ckv_off + kvl
    swap = jnp.concatenate([jnp.arange(rope // 2, rope), jnp.arange(0, rope // 2)])

    def both(v):
        return jnp.concatenate([v, v[..., swap]], axis=-1)

    cs = _rope_tables(positions, rope)
    c8 = jnp.pad(c, ((0, (-batch) % 8), (0, 0)))
    xf = x.reshape(m, d)
    for l in range(depth):
        mod6 = _ada(c8, w_ada[l], b_ada[l])[:batch].reshape(batch, 6, d)

        w_in_b = w_in[l].astype(BF16)
        wk_b = both(w_in[l][:, kpe_off:kpe_off + rope]).astype(BF16)
        proj, glog, kpe2 = _inproj(xf, mod6, norm1_g[l].reshape(1, d), w_in_b, wk_b, lb_logits,
                                   seq, hk, l)

        o_f, o_b = _hgrn(proj, glog, batch, seq, heads, hk)

        wq = w_uq[l].reshape(ql, mh, qk_dim)
        wq_all = jnp.concatenate([wq[..., :nope], both(wq[..., nope:])], axis=-1)
        qh, kh, vh = _mla_proj(
            proj, kpe2, cs, qa_norm_g[l].reshape(1, ql), kva_norm_g[l].reshape(1, kvl),
            wq_all.reshape(ql, mh * 2 * LANES).astype(BF16), w_ukv[l].astype(BF16),
            q_head_g[l][:nope].reshape(1, nope), both(q_head_g[l][nope:]).reshape(1, 2 * rope),
            k_head_g[l][:nope].reshape(1, nope), both(k_head_g[l][nope:]).reshape(1, 2 * rope),
            batch, seq, mh, cq_off, ckv_off, qk_dim, rope)
        o_mla = _attention(qh, kh, vh)

        wr_pad = jnp.pad(w_router[l], ((0, 0), (0, LANES - n_exp)))
        x1, h2, aff, lat = _outproj(o_f, o_b, proj, hgrn_out_g[l].reshape(1, hw), o_mla,
                                    w_out[l].astype(BF16), xf, mod6, norm2_g[l].reshape(1, d),
                                    wr_pad, seq, heads, hk, n_exp)

        slot_se, idx, tab = _topk(lat, batch, seq, n_exp, cap)
        ntile = seq // min(seq, COMBINE_TILE)
        tab = tab[:, :ntile + 1].reshape(batch, n_exp, ntile + 1).transpose(0, 2, 1).reshape(-1)
        rows = idx.reshape(batch, n_exp, cap) + (jnp.arange(batch, dtype=jnp.int32) * seq)[:, None, None]
        rows = rows.transpose(1, 0, 2).reshape(n_exp, batch * cap)
        ye = _ffn(rows, h2, w_gate[l], w_up[l], w_down[l])
        xf = _combine(tab, slot_se, aff, ye.reshape(n_exp, batch, cap, d), x1, mod6, seq, n_exp, cap)
    return xf.reshape(batch, seq, d)
```

```python
import functools
import math

import jax
import jax.numpy as jnp
from jax import lax
from jax.experimental import pallas as pl
from jax.experimental.pallas import tpu as pltpu

F32 = jnp.float32
BF16 = jnp.bfloat16
EPS = 1e-6
ROPE_BASE = 10000.0
LOG2E = math.log2(math.e)
EC_CAPACITY = 2
LANES = 128
TILE = 8
GRP = 128
VMEM_LIMIT = 56 * 1024 * 1024


def _cparams(*sem):
    return pltpu.CompilerParams(dimension_semantics=sem, vmem_limit_bytes=VMEM_LIMIT)


def _dot(a, b):
    return jnp.dot(a, b, preferred_element_type=F32)


def _dot_nt(a, b):
    return lax.dot_general(a, b, (((1,), (1,)), ((), ())), preferred_element_type=F32)


def _dot_tn(a, b):
    return lax.dot_general(a, b, (((0,), (0,)), ((), ())), preferred_element_type=F32)


def _split2(a):
    hi = a.astype(BF16)
    lo = (a - hi.astype(F32)).astype(BF16)
    return hi, lo


def _split3(a):
    p1 = a.astype(BF16)
    r1 = a - p1.astype(F32)
    p2 = r1.astype(BF16)
    p3 = (r1 - p2.astype(F32)).astype(BF16)
    return p1, p2, p3


def _dot_hi(a, b):
    ah, al = _split2(a)
    bh, bl = _split2(b)
    return _dot(ah, bh) + (_dot(ah, bl) + _dot(al, bh))


def _silu(x):
    return x * jax.nn.sigmoid(x)


def _rope_kernel(pos_ref, cs_ref, *, half):
    pos = pos_ref[...].astype(F32)
    lane = lax.broadcasted_iota(jnp.int32, (1, 4 * half), 1)
    j = (lane & (half - 1)).astype(F32)
    inv_freq = jnp.exp(j * (-2.0 * math.log(ROPE_BASE) / (2 * half)))
    ang = pos * inv_freq
    c = jnp.cos(ang)
    s = jnp.sin(ang)
    cs_ref[...] = jnp.where(lane < 2 * half, c, jnp.where(lane < 3 * half, -s, s))


def _rope_tables(positions, rope):
    m = positions.size
    tm = min(m, 1024)
    half = rope // 2
    return pl.pallas_call(
        functools.partial(_rope_kernel, half=half),
        out_shape=jax.ShapeDtypeStruct((m, 2 * rope), F32),
        grid=(m // tm,),
        in_specs=[pl.BlockSpec((tm, 1), lambda i: (i, 0))],
        out_specs=pl.BlockSpec((tm, 2 * rope), lambda i: (i, 0)),
        compiler_params=_cparams("parallel"),
        name="rope_tables",
    )(positions.reshape(m, 1))


def _ada_kernel(c_ref, w_ref, b_ref, o_ref):
    part = _dot_hi(_silu(c_ref[...]), w_ref[...])

    @pl.when(pl.program_id(0) == 0)
    def _():
        o_ref[...] = part + b_ref[...]

    @pl.when(pl.program_id(0) > 0)
    def _():
        o_ref[...] += part


def _ada(c8, w, b):
    d, n = w.shape
    tk = min(d, 128)
    return pl.pallas_call(
        _ada_kernel,
        out_shape=jax.ShapeDtypeStruct((c8.shape[0], n), F32),
        grid=(d // tk,),
        in_specs=[pl.BlockSpec((c8.shape[0], tk), lambda k: (0, k)),
                  pl.BlockSpec((tk, n), lambda k: (k, 0)),
                  pl.BlockSpec((1, n), lambda k: (0, 0))],
        out_specs=pl.BlockSpec((c8.shape[0], n), lambda k: (0, 0)),
        compiler_params=_cparams("arbitrary"),
        name="ada_mod",
    )(c8, w, b.reshape(1, n))


def _inproj_kernel(x_ref, mod_ref, g_ref, w_ref, wk_ref, lbl_ref,
                   proj_ref, glog_ref, kpe_ref, h_scr, *, layer):
    j = pl.program_id(1)

    @pl.when(j == 0)
    def _():
        x = x_ref[...]
        r = lax.rsqrt(jnp.mean(x * x, axis=-1, keepdims=True) + EPS)
        h = x * r * g_ref[...] * (1.0 + mod_ref[1:2, :]) + mod_ref[0:1, :]
        hb = h.astype(BF16)
        h_scr[...] = hb
        kpe_ref[...] = _dot(hb, wk_ref[...])

    acc = _dot(h_scr[...], w_ref[...])
    proj_ref[...] = acc.astype(BF16)

    @pl.when((j == 1) | (j == 2))
    def _():
        lg = lbl_ref[...]
        e = jnp.exp(lg - jnp.max(lg, axis=0, keepdims=True))
        lb = jnp.sum(e[:layer + 1], axis=0, keepdims=True) / jnp.sum(e, axis=0, keepdims=True)
        glog_ref[...] = jnp.log(lb + (1.0 - lb) * jax.nn.sigmoid(acc))


def _inproj(xf, mod6, g1, w_in_b, wk_b, lb_logits, seq, hk, layer):
    m, d = xf.shape
    d_in = w_in_b.shape[1]
    tm = min(seq, 1024)
    tpb = seq // tm
    tn = hk
    nj = pl.cdiv(d_in, tn)
    nl = lb_logits.shape[1]
    fdir = lambda j: jnp.clip(j - 1, 0, 1)
    return pl.pallas_call(
        functools.partial(_inproj_kernel, layer=layer),
        out_shape=(jax.ShapeDtypeStruct((m, d_in), BF16),
                   jax.ShapeDtypeStruct((m, 2 * hk), F32),
                   jax.ShapeDtypeStruct((m, LANES), F32)),
        grid=(m // tm, nj),
        in_specs=[pl.BlockSpec((tm, d), lambda i, j: (i, 0)),
                  pl.BlockSpec((None, 6, d), lambda i, j: (i // tpb, 0, 0)),
                  pl.BlockSpec((1, d), lambda i, j: (0, 0)),
                  pl.BlockSpec((d, tn), lambda i, j: (0, j)),
                  pl.BlockSpec((d, LANES), lambda i, j: (0, 0)),
                  pl.BlockSpec((None, nl, tn), lambda i, j: (fdir(j), 0, 0))],
        out_specs=(pl.BlockSpec((tm, tn), lambda i, j: (i, j)),
                   pl.BlockSpec((tm, tn), lambda i, j: (i, fdir(j))),
                   pl.BlockSpec((tm, LANES), lambda i, j: (i, 0))),
        scratch_shapes=[pltpu.VMEM((tm, d), BF16)],
        compiler_params=_cparams("parallel", "arbitrary"),
        name="norm1_inproj",
    )(xf, mod6, g1, w_in_b, wk_b, lb_logits)


def _group_cumsum(g, d):
    rin = lax.broadcasted_iota(jnp.int32, g.shape, 0) & (TILE - 1)
    b = g
    step = 1
    while step < TILE:
        if d == 0:
            b = b + jnp.where(rin >= step, pltpu.roll(b, step, 0), 0.0)
        else:
            b = b + jnp.where(rin < TILE - step, pltpu.roll(b, GRP - step, 0), 0.0)
        step *= 2
    ntile = GRP // TILE
    order = range(ntile) if d == 0 else range(ntile - 1, -1, -1)
    edge = TILE - 1 if d == 0 else 0
    out = [None] * ntile
    carry = None
    for i in order:
        t = b[i * TILE:(i + 1) * TILE]
        out[i] = t if carry is None else t + carry
        tot = t[edge:edge + 1]
        carry = tot if carry is None else carry + tot
    return jnp.concatenate(out, axis=0)


def _boundary(b, h, d):
    idx = h - 1 if d == 0 else h
    if 2 * h >= TILE:
        b3 = b.reshape(GRP // (2 * h), 2 * h, b.shape[1])
        return jnp.broadcast_to(b3[:, idx:idx + 1, :], b3.shape).reshape(b.shape)
    p = lax.broadcasted_iota(jnp.int32, b.shape, 0) & (2 * h - 1)
    out = b
    for pos in range(2 * h):
        shift = pos - idx
        if shift != 0:
            out = jnp.where(p == pos, pltpu.roll(b, shift % GRP, 0), out)
    return out


def _hgrn_kernel(q_ref, v_ref, gf_ref, gb_ref, o_ref, lv_scr, sg_scr, st_scr):
    seq = q_ref.shape[0]
    ngrp = seq // GRP
    nlev = GRP.bit_length()
    g_refs = (gf_ref, gb_ref)
    assert ngrp % 2 == 0

    r = lax.broadcasted_iota(jnp.int32, (GRP, GRP), 0)
    c = lax.broadcasted_iota(jnp.int32, (GRP, GRP), 1)
    lev = jnp.zeros((GRP, GRP), jnp.int32)
    for j in range(nlev - 1):
        lev = lev + jnp.where((r >> j) != (c >> j), 1, 0)
    lv_scr[0] = jnp.where(c <= r, lev, -1)
    lv_scr[1] = jnp.where(c >= r, lev, -1)
    st_scr[...] = jnp.zeros_like(st_scr)
    rr = lax.broadcasted_iota(jnp.int32, (GRP, LANES), 0)
    for l in range(1, nlev):
        late = (rr & (1 << (l - 1))) != 0
        sg_scr[0, l - 1] = jnp.where(late, 1.0, -1.0)
        sg_scr[1, l - 1] = jnp.where(late, -1.0, 1.0)

    def body(i, carry, first):
        for d in (0, 1):
            grp = i if d == 0 else ngrp - 1 - i
            r0 = pl.multiple_of(grp * GRP, GRP)
            g = g_refs[d][pl.ds(r0, GRP), :] * LOG2E
            qb = q_ref[pl.ds(r0, GRP), :]
            vb = v_ref[pl.ds(r0, GRP), :]
            qf = qb.astype(F32)
            kk = 1.0 - jnp.exp2(g)
            kb = kk.astype(BF16)
            b = _group_cumsum(g, d)
            edge = GRP - 1 if d == 0 else 0
            tot = b[edge:edge + 1]
            lv = lv_scr[d]
            att = jnp.where(lv == 0, _dot_nt(qb, kb), 0.0)
            for l in range(1, nlev):
                x = jnp.exp2((b - _boundary(b, 1 << (l - 1), d)) * sg_scr[d, l - 1]).astype(BF16)
                att = jnp.where(lv == l, _dot_nt(qb * x, kb * x), att)
            st = st_scr[d]
            o = _dot(att.astype(BF16), vb) + _dot_nt((qf * jnp.exp2(b)).astype(BF16), st.astype(BF16))
            if first:
                o_ref[pl.ds(r0, GRP), :] = o
            else:
                o_ref[pl.ds(r0, GRP), :] += o
            st_scr[d] = st * jnp.exp2(tot) + _dot_tn(vb, (kk * jnp.exp2(tot - b)).astype(BF16))
        return carry

    half = ngrp // 2
    lax.fori_loop(0, half, functools.partial(body, first=True), 0, unroll=2)
    lax.fori_loop(half, ngrp, functools.partial(body, first=False), 0, unroll=2)


def _hgrn(proj, glog, batch, seq, heads, hk):
    m = proj.shape[0]
    nh = hk // LANES
    vcol = 3 * nh
    blk = lambda off: pl.BlockSpec((seq, LANES), lambda b, h: (b, off + h))
    out = jax.ShapeDtypeStruct((m, hk), F32)
    return pl.pallas_call(
        _hgrn_kernel,
        out_shape=out,
        grid=(batch, heads),
        in_specs=[blk(0), blk(vcol), blk(0), blk(nh)],
        out_specs=blk(0),
        scratch_shapes=[pltpu.VMEM((2, GRP, GRP), jnp.int32),
                        pltpu.VMEM((2, GRP.bit_length() - 1, GRP, LANES), F32),
                        pltpu.VMEM((2, LANES, LANES), F32)],
        compiler_params=_cparams("parallel", "parallel"),
        name="hgrn2_scan",
    )(proj, proj, glog, glog)


def _mla_proj_kernel(cq_ref, ckv_ref, kpe_ref, cs_ref, qag_ref, kvag_ref, wq_ref, wkv_ref,
                     qgn_ref, qgr_ref, kgn_ref, kgr_ref, q_out, k_out, v_out,
                     *, scale, qk_dim, rope, heads):
    cq = cq_ref[...].astype(F32)
    a = (cq * lax.rsqrt(jnp.mean(cq * cq, axis=-1, keepdims=True) + EPS) * qag_ref[...]).astype(BF16)
    ckv = ckv_ref[...].astype(F32)
    c = (ckv * lax.rsqrt(jnp.mean(ckv * ckv, axis=-1, keepdims=True) + EPS)
         * kvag_ref[...]).astype(BF16)
    qall = _dot(a, wq_ref[...])
    kvall = _dot(c, wkv_ref[...])

    cs = cs_ref[...]
    lane = lax.broadcasted_iota(jnp.int32, cs.shape, 1)
    lo = lane < rope

    def rope_sumsq(rr):
        return jnp.sum(jnp.where(lo, rr * rr, 0.0), axis=-1, keepdims=True)

    def rotate(rr, gr):
        y = rr * gr * cs
        return y + pltpu.roll(y, rope, 1)

    kpe = kpe_ref[...]
    k_ss = rope_sumsq(kpe)
    k_rot = rotate(kpe, kgr_ref[...])
    for h in range(heads):
        base = 2 * LANES * h
        qn = qall[:, base:base + LANES]
        qr = qall[:, base + LANES:base + 2 * LANES]
        rq = lax.rsqrt((jnp.sum(qn * qn, axis=-1, keepdims=True) + rope_sumsq(qr)) / qk_dim + EPS) * scale
        q_out[h, :, :LANES] = (qn * qgn_ref[...] * rq).astype(BF16)
        q_out[h, :, LANES:] = jnp.where(lo, rotate(qr, qgr_ref[...]) * rq, 0.0).astype(BF16)
        kn = kvall[:, base:base + LANES]
        rk = lax.rsqrt((jnp.sum(kn * kn, axis=-1, keepdims=True) + k_ss) / qk_dim + EPS)
        k_out[h, :, :LANES] = (kn * kgn_ref[...] * rk).astype(BF16)
        k_out[h, :, LANES:] = jnp.where(lo, k_rot * rk, 0.0).astype(BF16)
        v_out[h] = kvall[:, base + LANES:base + 2 * LANES].astype(BF16)


def _mla_proj(proj, kpe2, cs, qag, kvag, wq_all, wkv_all, qgn, qgr, kgn, kgr,
              batch, seq, heads, cq_off, ckv_off, qk_dim, rope):
    m = proj.shape[0]
    ql, kvl = wq_all.shape[0], wkv_all.shape[0]
    tm = min(seq, 256)
    tpb = seq // tm
    assert cq_off % ql == 0 and ckv_off % kvl == 0
    vec = lambda n: pl.BlockSpec((1, n), lambda i: (0, 0))
    full = lambda w: pl.BlockSpec(w.shape, lambda i: (0, 0))
    hspec = lambda n: pl.BlockSpec((None, heads, tm, n), lambda i: (i // tpb, 0, i % tpb, 0))
    scale = qk_dim ** -0.5 * math.log2(math.e)
    return pl.pallas_call(
        functools.partial(_mla_proj_kernel, scale=scale, qk_dim=float(qk_dim), rope=rope, heads=heads),
        out_shape=(jax.ShapeDtypeStruct((batch, heads, seq, 2 * LANES), BF16),
                   jax.ShapeDtypeStruct((batch, heads, seq, 2 * LANES), BF16),
                   jax.ShapeDtypeStruct((batch, heads, seq, LANES), BF16)),
        grid=(m // tm,),
        in_specs=[pl.BlockSpec((tm, ql), lambda i: (i, cq_off // ql)),
                  pl.BlockSpec((tm, kvl), lambda i: (i, ckv_off // kvl)),
                  pl.BlockSpec((tm, LANES), lambda i: (i, 0)),
                  pl.BlockSpec((tm, LANES), lambda i: (i, 0)),
                  vec(ql), vec(kvl), full(wq_all), full(wkv_all),
                  vec(LANES), vec(LANES), vec(LANES), vec(LANES)],
        out_specs=(hspec(2 * LANES), hspec(2 * LANES), hspec(LANES)),
        compiler_params=_cparams("parallel"),
        name="mla_head_proj",
    )(proj, proj, kpe2, cs, qag, kvag, wq_all, wkv_all, qgn, qgr, kgn, kgr)


ATTN_KEYS = 512


def _attn_kernel(q_ref, k_ref, v_ref, o_ref):
    q = q_ref[...]
    seq = k_ref.shape[0]
    kc = min(ATTN_KEYS, seq)
    m = l = acc = None
    for c in range(seq // kc):
        rows = slice(c * kc, (c + 1) * kc)
        s = _dot_nt(q, k_ref[rows, :])
        mc = jnp.max(s, axis=-1, keepdims=True)
        if c == 0:
            m = mc
            p = jnp.exp2(s - m)
            l = jnp.sum(p, axis=-1, keepdims=True)
            acc = _dot(p.astype(BF16), v_ref[rows, :])
        else:
            m_new = jnp.maximum(m, mc)
            alpha = jnp.exp2(m - m_new)
            p = jnp.exp2(s - m_new)
            l = l * alpha + jnp.sum(p, axis=-1, keepdims=True)
            acc = acc * alpha + _dot(p.astype(BF16), v_ref[rows, :])
            m = m_new
    o_ref[...] = (acc / l).astype(BF16)


def _attention(qh, kh, vh):
    batch, mh, seq, dq = qh.shape
    dv = vh.shape[-1]
    tq = min(seq, 1024)
    nq = seq // tq
    return pl.pallas_call(
        _attn_kernel,
        out_shape=jax.ShapeDtypeStruct((batch * seq, mh * dv), BF16),
        grid=(batch, mh, nq),
        in_specs=[pl.BlockSpec((None, None, tq, dq), lambda b, h, i: (b, h, i, 0)),
                  pl.BlockSpec((None, None, seq, dq), lambda b, h, i: (b, h, 0, 0)),
                  pl.BlockSpec((None, None, seq, dv), lambda b, h, i: (b, h, 0, 0))],
        out_specs=pl.BlockSpec((tq, dv), lambda b, h, i: (b * nq + i, h)),
        compiler_params=_cparams("parallel", "parallel", "arbitrary"),
        name="mla_attention",
    )(qh, kh, vh)


def _eye(rows, cols):
    r = lax.broadcasted_iota(jnp.int32, (rows, cols), 0)
    c = lax.broadcasted_iota(jnp.int32, (rows, cols), 1)
    return jnp.where(r == c, 1.0, 0.0).astype(BF16)


def _outproj_kernel(o_ref, hg_ref, og_ref, om_ref, w_ref, x_ref, mod_ref, g2_ref, wr_ref,
                    x1_ref, h2_ref, aff_ref, lat_ref, mix_scr, *, heads, n_exp):
    hw = o_ref.shape[1]
    o = o_ref[...]
    gate = _silu(hg_ref[...].astype(F32))
    for h in range(heads):
        sl = slice(h * LANES, (h + 1) * LANES)
        oh = o[:, sl]
        r = lax.rsqrt(jnp.mean(oh * oh, axis=-1, keepdims=True) + EPS)
        mix_scr[:, sl] = (oh * r * og_ref[:, sl] * gate[:, sl]).astype(BF16)
    mix_scr[:, hw:] = om_ref[...]
    x1 = x_ref[...] + mod_ref[2:3, :] * _dot(mix_scr[...], w_ref[...])
    x1_ref[...] = x1
    r2 = lax.rsqrt(jnp.mean(x1 * x1, axis=-1, keepdims=True) + EPS)
    h2 = x1 * r2 * g2_ref[...] * (1.0 + mod_ref[4:5, :]) + mod_ref[3:4, :]
    h2_ref[...] = h2
    logits = _dot_hi(h2, wr_ref[...])
    lane = lax.broadcasted_iota(jnp.int32, logits.shape, 1)
    logits = jnp.where(lane < n_exp, logits, -jnp.inf)
    z = logits - jnp.max(logits, axis=-1, keepdims=True)
    p = jnp.exp(z)
    sp = jnp.sum(p, axis=-1, keepdims=True)
    aff_ref[...] = p / sp
    la = jnp.where(lane < n_exp, z - jnp.log(sp), 0.0)
    eye = _eye(n_exp, la.shape[1])
    p1, p2, p3 = _split3(la)
    lat_ref[...] = (_dot_nt(eye, p1) + _dot_nt(eye, p2)) + _dot_nt(eye, p3)


def _outproj(o_hgrn, proj, og, o_mla, w_out_b, xf, mod6, g2, wr_pad, seq, heads, hk, n_exp):
    m, d = xf.shape
    hw = o_hgrn.shape[1]
    mw = o_mla.shape[1]
    tm = min(seq, 256)
    tpb = seq // tm
    gcol = (3 * hk + hw) // hw
    assert (3 * hk + hw) % hw == 0
    row = lambda n: pl.BlockSpec((tm, n), lambda i: (i, 0))
    return pl.pallas_call(
        functools.partial(_outproj_kernel, heads=heads, n_exp=n_exp),
        out_shape=(jax.ShapeDtypeStruct((m, d), F32),
                   jax.ShapeDtypeStruct((m, d), F32),
                   jax.ShapeDtypeStruct((m, LANES), F32),
                   jax.ShapeDtypeStruct((m // seq, n_exp, seq), F32)),
        grid=(m // tm,),
        in_specs=[row(hw),
                  pl.BlockSpec((tm, hw), lambda i: (i, gcol)),
                  pl.BlockSpec((1, hw), lambda i: (0, 0)),
                  row(mw),
                  pl.BlockSpec((hw + mw, d), lambda i: (0, 0), pipeline_mode=pl.Buffered(1)),
                  row(d),
                  pl.BlockSpec((None, 6, d), lambda i: (i // tpb, 0, 0)),
                  pl.BlockSpec((1, d), lambda i: (0, 0)),
                  pl.BlockSpec((d, LANES), lambda i: (0, 0), pipeline_mode=pl.Buffered(1))],
        out_specs=(row(d), row(d), row(LANES),
                   pl.BlockSpec((None, n_exp, tm), lambda i: (i // tpb, 0, i % tpb))),
        scratch_shapes=[pltpu.VMEM((tm, hw + mw), BF16)],
        compiler_params=_cparams("parallel"),
        name="outproj_norm2_router",
    )(o_hgrn, proj, og, o_mla, w_out_b, xf, mod6, g2, wr_pad)


BISECT_STEPS = 64


COMBINE_TILE = 256
COMBINE_WIN = 64


def _topk_kernel(la_ref, slot_se_ref, idx_ref, tab_ref, tri_scr, cum_scr, *, cap, n_exp):
    nrow, seq = la_ref.shape
    ep = slot_se_ref.shape[1]
    rows = 256
    for k in range(seq // rows):
        r = lax.broadcasted_iota(jnp.int32, (rows, seq), 0) + k * rows
        c = lax.broadcasted_iota(jnp.int32, (rows, seq), 1)
        tri_scr[k * rows:(k + 1) * rows, :] = jnp.where(r < c, 1.0, 0.0).astype(BF16)

    def count(mask):
        return jnp.sum(jnp.where(mask, 1.0, 0.0), axis=-1, keepdims=True)

    def body(_, lh):
        lo, hi = lh
        mid = 0.5 * (lo + hi)
        ok = count(la_ref[...] >= mid) >= cap
        return jnp.where(ok, mid, lo), jnp.where(ok, hi, mid)

    la = la_ref[...]
    lo0 = jnp.min(la, axis=-1, keepdims=True)
    lo, hi = lax.fori_loop(0, BISECT_STEPS, body, (lo0, jnp.ones_like(lo0)))
    above = la >= hi
    tie = (la >= lo) & (la < hi)
    need = cap - count(above)
    tri = tri_scr[...]
    rank = _dot(jnp.where(tie, 1.0, 0.0).astype(BF16), tri)
    sel = above | (tie & (rank < need))
    pos = _dot(jnp.where(sel, 1.0, 0.0).astype(BF16), tri)
    slot = jnp.where(sel, pos, -1.0)
    eye = _eye(n_exp, ep)
    for b in range(nrow // n_exp):
        slot_se_ref[b * seq:(b + 1) * seq, :] = _dot_tn(
            slot[b * n_exp:(b + 1) * n_exp, :].astype(BF16), eye)
    cum_scr[...] = pos + jnp.where(sel, 1.0, 0.0)
    lane = lax.broadcasted_iota(jnp.int32, (nrow, cap), 1)

    def slot_body(c, acc):
        cnt = count(cum_scr[...] <= lax.convert_element_type(c, F32))
        return jnp.where(lane == c, cnt, acc)

    idx = lax.fori_loop(0, cap, slot_body, jnp.zeros((nrow, cap), F32), unroll=4)
    idx_ref[...] = idx.astype(jnp.int32)
    tok = lax.broadcasted_iota(jnp.int32, (nrow, seq), 1)
    tlane = lax.broadcasted_iota(jnp.int32, tab_ref.shape, 1)
    tab = jnp.zeros(tab_ref.shape, F32)
    tile = min(seq, COMBINE_TILE)
    for k in range(seq // tile + 1):
        tab = jnp.where(tlane == k, count(sel & (tok < k * tile)), tab)
    tab_ref[...] = tab.astype(jnp.int32)


def _topk(lat, batch, seq, n_exp, cap):
    return pl.pallas_call(
        functools.partial(_topk_kernel, cap=cap, n_exp=n_exp),
        out_shape=(jax.ShapeDtypeStruct((batch * seq, LANES), F32),
                   jax.ShapeDtypeStruct((batch * n_exp, cap), jnp.int32),
                   jax.ShapeDtypeStruct((batch * n_exp, LANES), jnp.int32)),
        scratch_shapes=[pltpu.VMEM((seq, seq), BF16), pltpu.VMEM((batch * n_exp, seq), F32)],
        compiler_params=pltpu.CompilerParams(vmem_limit_bytes=VMEM_LIMIT),
        name="expert_choice_topk",
    )(lat.reshape(batch * n_exp, seq))


def _ffn_kernel(idx_ref, h2_hbm, wg_ref, wu_ref, wd_ref, ye_ref, xe_scr, hmid_scr, sem, *, nt, tf):
    e = pl.program_id(0)
    s = pl.program_id(1)
    rows = xe_scr.shape[0]

    def start_gather(expert):
        base = expert * rows

        def body(k, carry):
            r0 = pl.multiple_of(k * TILE, TILE)
            for j in range(TILE):
                pltpu.make_async_copy(h2_hbm.at[pl.ds(idx_ref[base + r0 + j], 1), :],
                                      xe_scr.at[pl.ds(r0 + j, 1), :], sem.at[0]).start()
            return carry
        lax.fori_loop(0, rows // TILE, body, 0)

    @pl.when((e == 0) & (s == 0))
    def _():
        start_gather(0)

    @pl.when(s == 0)
    def _():
        pltpu.make_async_copy(h2_hbm.at[pl.ds(0, rows), :], xe_scr, sem.at[0]).wait()

    @pl.when(s < nt)
    def _():
        xe = xe_scr[...].astype(BF16)
        a = _dot(xe, wg_ref[...].astype(BF16))
        u = _dot(xe, wu_ref[...].astype(BF16))
        hmid_scr[s] = (_silu(a) * u).astype(BF16)

    def down_step(prefetch):
        per = rows // (nt * nt)
        y = None
        for k in range(nt):
            if prefetch:
                first = (e + 1) * rows + (s - nt) * (per * nt) + k * per
                dst0 = (s - nt) * (per * nt) + k * per
                for j in range(per):
                    pltpu.make_async_copy(h2_hbm.at[pl.ds(idx_ref[first + j], 1), :],
                                          xe_scr.at[pl.ds(dst0 + j, 1), :], sem.at[0]).start()
            part = _dot(hmid_scr[k], wd_ref[k * tf:(k + 1) * tf, :].astype(BF16))
            y = part if y is None else y + part
        ye_ref[...] = y.astype(BF16)

    more = e + 1 < pl.num_programs(0)

    @pl.when((s >= nt) & more)
    def _():
        down_step(True)

    @pl.when((s >= nt) & jnp.logical_not(more))
    def _():
        down_step(False)


def _ffn(idx, h2, w_gate, w_up, w_down):
    n_exp, rows = idx.shape
    idx = idx.reshape(n_exp * rows)
    d = h2.shape[1]
    ff = w_gate.shape[2]
    tf = min(ff, 512)
    tn = min(d, 512)
    nt = ff // tf
    assert d // tn == nt
    up = lambda e, s, idx: (e, 0, jnp.minimum(s, nt - 1))
    down = lambda e, s, idx: (e, 0, jnp.maximum(s - nt, 0))
    return pl.pallas_call(
        functools.partial(_ffn_kernel, nt=nt, tf=tf),
        out_shape=jax.ShapeDtypeStruct((n_exp, rows, d), BF16),
        grid_spec=pltpu.PrefetchScalarGridSpec(
            num_scalar_prefetch=1,
            grid=(n_exp, 2 * nt),
            in_specs=[pl.BlockSpec(memory_space=pl.ANY),
                      pl.BlockSpec((None, d, tf), up),
                      pl.BlockSpec((None, d, tf), up),
                      pl.BlockSpec((None, ff, tn), down)],
            out_specs=pl.BlockSpec((None, rows, tn), down),
            scratch_shapes=[pltpu.VMEM((rows, d), F32),
                            pltpu.VMEM((nt, rows, tf), BF16),
                            pltpu.SemaphoreType.DMA((1,))]),
        compiler_params=_cparams("arbitrary", "arbitrary"),
        name="expert_swiglu",
    )(idx, h2, w_gate, w_up, w_down)


def _combine_kernel(tab_ref, slot_ref, aff_ref, ye_ref, x1_ref, mod_ref, out_ref, y_scr,
                    *, n_exp, cap, win):
    b = pl.program_id(0)
    t = pl.program_id(1)
    tt = x1_ref.shape[0]
    base = (b * (pl.num_programs(1) + 1) + t) * n_exp
    pack = 16
    starts = []
    short = None
    for e in range(n_exp):
        c0 = tab_ref[base + e]
        c1 = tab_ref[base + n_exp + e]
        a = jnp.minimum(c0 & ~(pack - 1), cap - win)
        ok = c1 - a <= win
        starts.append(a)
        short = ok if short is None else short & ok

    def finish(acc):
        out_ref[...] = x1_ref[...] + mod_ref[5:6, :] * acc

    @pl.when(short)
    def _():
        lane = lax.broadcasted_iota(jnp.int32, (tt, LANES), 1).astype(F32)
        per = LANES // win
        blocks = []
        for g in range(n_exp // per):
            blk = jnp.zeros((tt, LANES), F32)
            for j in range(per):
                e = g * per + j
                a = pl.multiple_of(starts[e], pack)
                y_scr[e * win:(e + 1) * win, :] = ye_ref[e, pl.ds(a, win), :]
                slot = slot_ref[:, e:e + 1]
                rel = jnp.where(slot >= 0.0, slot - a.astype(F32) + float(j * win), -1.0)
                blk = jnp.where(lane == rel, aff_ref[:, e:e + 1], blk)
            blocks.append(blk.astype(BF16))
        finish(_dot(jnp.concatenate(blocks, axis=1), y_scr[...]))

    @pl.when(jnp.logical_not(short))
    def _():
        cidx = lax.broadcasted_iota(jnp.int32, (tt, cap), 1).astype(F32)
        acc = jnp.zeros(x1_ref.shape, F32)
        for e in range(n_exp):
            onehot = jnp.where(cidx == slot_ref[:, e:e + 1], 1.0, 0.0).astype(BF16)
            acc = acc + aff_ref[:, e:e + 1] * _dot(onehot, ye_ref[e])
        finish(acc)


def _combine(tab, slot_se, aff, ye4, x1, mod6, seq, n_exp, cap):
    m, d = x1.shape
    ep = slot_se.shape[1]
    batch = m // seq
    tt = min(seq, COMBINE_TILE)
    tpb = seq // tt
    win = min(COMBINE_WIN, cap)
    assert LANES % win == 0 and n_exp % (LANES // win) == 0 and cap % 16 == 0
    return pl.pallas_call(
        functools.partial(_combine_kernel, n_exp=n_exp, cap=cap, win=win),
        out_shape=jax.ShapeDtypeStruct((m, d), F32),
        grid_spec=pltpu.PrefetchScalarGridSpec(
            num_scalar_prefetch=1,
            grid=(batch, tpb),
            in_specs=[pl.BlockSpec((tt, ep), lambda b, t, tab: (b * tpb + t, 0)),
                      pl.BlockSpec((tt, ep), lambda b, t, tab: (b * tpb + t, 0)),
                      pl.BlockSpec((n_exp, None, cap, d), lambda b, t, tab: (0, b, 0, 0)),
                      pl.BlockSpec((tt, d), lambda b, t, tab: (b * tpb + t, 0)),
                      pl.BlockSpec((None, 6, d), lambda b, t, tab: (b, 0, 0))],
            out_specs=pl.BlockSpec((tt, d), lambda b, t, tab: (b * tpb + t, 0)),
            scratch_shapes=[pltpu.VMEM((n_exp * win, d), BF16)]),
        compiler_params=_cparams("parallel", "arbitrary"),
        name="expert_combine",
    )(tab, slot_se, aff, ye4, x1, mod6)


def kernel(x, c, positions, w_ada, b_ada, norm1_g, w_in, lb_logits, hgrn_out_g, qa_norm_g, w_uq,
           kva_norm_g, w_ukv, q_head_g, k_head_g, w_out, norm2_g, w_router, w_gate, w_up, w_down):
    batch, seq, d = x.shape
    depth = w_ada.shape[0]
    m = batch * seq
    hk = lb_logits.shape[2]
    heads, dv = hgrn_out_g.shape[1], hgrn_out_g.shape[2]
    hw = heads * dv
    ql, kvl = qa_norm_g.shape[1], kva_norm_g.shape[1]
    qk_dim = q_head_g.shape[1]
    mh = w_uq.shape[2] // qk_dim
    d_in = w_in.shape[2]
    rope = d_in - (3 * hk + 2 * hw + ql + kvl)
    nope = qk_dim - rope
    vdim = w_ukv.shape[2] // mh - nope
    n_exp = w_router.shape[2]
    cap = EC_CAPACITY * seq // n_exp
    assert dv == LANES and hk == hw and nope == LANES and vdim == LANES and 2 * rope == LANES
    assert ql + kvl + rope <= hk and seq % GRP == 0 and n_exp <= LANES and cap % 8 == 0

    cq_off = 3 * hk + 2 * hw
    ckv_off = cq_off + ql
    kpe_off = ckv_off + kvl
    swap = jnp.concatenate([jnp.arange(rope // 2, rope), jnp.arange(0, rope // 2)])

    def both(v):
        return jnp.concatenate([v, v[..., swap]], axis=-1)

    cs = _rope_tables(positions, rope)
    c8 = jnp.pad(c, ((0, (-batch) % 8), (0, 0)))
    xf = x.reshape(m, d)
    for l in range(depth):
        mod6 = _ada(c8, w_ada[l], b_ada[l])[:batch].reshape(batch, 6, d)

        w_in_b = w_in[l].astype(BF16)
        wk_b = both(w_in[l][:, kpe_off:kpe_off + rope]).astype(BF16)
        proj, glog, kpe2 = _inproj(xf, mod6, norm1_g[l].reshape(1, d), w_in_b, wk_b, lb_logits,
                                   seq, hk, l)

        o_hgrn = _hgrn(proj, glog, batch, seq, heads, hk)

        wq = w_uq[l].reshape(ql, mh, qk_dim)
        wq_all = jnp.concatenate([wq[..., :nope], both(wq[..., nope:])], axis=-1)
        qh, kh, vh = _mla_proj(
            proj, kpe2, cs, qa_norm_g[l].reshape(1, ql), kva_norm_g[l].reshape(1, kvl),
            wq_all.reshape(ql, mh * 2 * LANES).astype(BF16), w_ukv[l].astype(BF16),
            q_head_g[l][:nope].reshape(1, nope), both(q_head_g[l][nope:]).reshape(1, 2 * rope),
            k_head_g[l][:nope].reshape(1, nope), both(k_head_g[l][nope:]).reshape(1, 2 * rope),
            batch, seq, mh, cq_off, ckv_off, qk_dim, rope)
        o_mla = _attention(qh, kh, vh)

        wr_pad = jnp.pad(w_router[l], ((0, 0), (0, LANES - n_exp)))
        x1, h2, aff, lat = _outproj(o_hgrn, proj, hgrn_out_g[l].reshape(1, hw), o_mla,
                                    w_out[l].astype(BF16), xf, mod6, norm2_g[l].reshape(1, d),
                                    wr_pad, seq, heads, hk, n_exp)

        slot_se, idx, tab = _topk(lat, batch, seq, n_exp, cap)
        ntile = seq // min(seq, COMBINE_TILE)
        tab = tab[:, :ntile + 1].reshape(batch, n_exp, ntile + 1).transpose(0, 2, 1).reshape(-1)
        rows = idx.reshape(batch, n_exp, cap) + (jnp.arange(batch, dtype=jnp.int32) * seq)[:, None, None]
        rows = rows.transpose(1, 0, 2).reshape(n_exp, batch * cap)
        ye = _ffn(rows, h2, w_gate[l], w_up[l], w_down[l])
        xf = _combine(tab, slot_se, aff, ye.reshape(n_exp, batch, cap, d), x1, mod6, seq, n_exp, cap)
    return xf.reshape(batch, seq, d)
```

```python
import functools
import math

import jax
import jax.numpy as jnp
from jax import lax
from jax.experimental import pallas as pl
from jax.experimental.pallas import tpu as pltpu

F32 = jnp.float32
BF16 = jnp.bfloat16
EPS = 1e-6
ROPE_BASE = 10000.0
LOG2E = math.log2(math.e)
EC_CAPACITY = 2
LANES = 128
TILE = 8
GRP = 128
VMEM_LIMIT = 56 * 1024 * 1024


def _cparams(*sem):
    return pltpu.CompilerParams(dimension_semantics=sem, vmem_limit_bytes=VMEM_LIMIT)


def _dot(a, b):
    return jnp.dot(a, b, preferred_element_type=F32)


def _dot_nt(a, b):
    return lax.dot_general(a, b, (((1,), (1,)), ((), ())), preferred_element_type=F32)


def _dot_tn(a, b):
    return lax.dot_general(a, b, (((0,), (0,)), ((), ())), preferred_element_type=F32)


def _split2(a):
    hi = a.astype(BF16)
    lo = (a - hi.astype(F32)).astype(BF16)
    return hi, lo


def _split3(a):
    p1 = a.astype(BF16)
    r1 = a - p1.astype(F32)
    p2 = r1.astype(BF16)
    p3 = (r1 - p2.astype(F32)).astype(BF16)
    return p1, p2, p3


def _dot_hi(a, b):
    ah, al = _split2(a)
    bh, bl = _split2(b)
    return _dot(ah, bh) + (_dot(ah, bl) + _dot(al, bh))


def _silu(x):
    return x * jax.nn.sigmoid(x)


def _rope_kernel(pos_ref, cs_ref, *, half):
    pos = pos_ref[...].astype(F32)
    lane = lax.broadcasted_iota(jnp.int32, (1, 4 * half), 1)
    j = (lane & (half - 1)).astype(F32)
    inv_freq = jnp.exp(j * (-2.0 * math.log(ROPE_BASE) / (2 * half)))
    ang = pos * inv_freq
    c = jnp.cos(ang)
    s = jnp.sin(ang)
    cs_ref[...] = jnp.where(lane < 2 * half, c, jnp.where(lane < 3 * half, -s, s))


def _rope_tables(positions, rope):
    m = positions.size
    tm = min(m, 1024)
    half = rope // 2
    return pl.pallas_call(
        functools.partial(_rope_kernel, half=half),
        out_shape=jax.ShapeDtypeStruct((m, 2 * rope), F32),
        grid=(m // tm,),
        in_specs=[pl.BlockSpec((tm, 1), lambda i: (i, 0))],
        out_specs=pl.BlockSpec((tm, 2 * rope), lambda i: (i, 0)),
        compiler_params=_cparams("parallel"),
        name="rope_tables",
    )(positions.reshape(m, 1))


def _ada_kernel(c_ref, w_ref, b_ref, o_ref):
    part = _dot_hi(_silu(c_ref[...]), w_ref[...])

    @pl.when(pl.program_id(0) == 0)
    def _():
        o_ref[...] = part + b_ref[...]

    @pl.when(pl.program_id(0) > 0)
    def _():
        o_ref[...] += part


def _ada(c8, w, b):
    d, n = w.shape
    tk = min(d, 128)
    return pl.pallas_call(
        _ada_kernel,
        out_shape=jax.ShapeDtypeStruct((c8.shape[0], n), F32),
        grid=(d // tk,),
        in_specs=[pl.BlockSpec((c8.shape[0], tk), lambda k: (0, k)),
                  pl.BlockSpec((tk, n), lambda k: (k, 0)),
                  pl.BlockSpec((1, n), lambda k: (0, 0))],
        out_specs=pl.BlockSpec((c8.shape[0], n), lambda k: (0, 0)),
        compiler_params=_cparams("arbitrary"),
        name="ada_mod",
    )(c8, w, b.reshape(1, n))


def _inproj_kernel(x_ref, mod_ref, g_ref, w_ref, wk_ref, lbl_ref,
                   proj_ref, glog_ref, kpe_ref, h_scr, *, layer):
    j = pl.program_id(1)

    @pl.when(j == 0)
    def _():
        x = x_ref[...]
        r = lax.rsqrt(jnp.mean(x * x, axis=-1, keepdims=True) + EPS)
        h = x * r * g_ref[...] * (1.0 + mod_ref[1:2, :]) + mod_ref[0:1, :]
        hb = h.astype(BF16)
        h_scr[...] = hb
        kpe_ref[...] = _dot(hb, wk_ref[...])

    acc = _dot(h_scr[...], w_ref[...])
    proj_ref[...] = acc.astype(BF16)

    @pl.when((j == 1) | (j == 2))
    def _():
        lg = lbl_ref[...]
        e = jnp.exp(lg - jnp.max(lg, axis=0, keepdims=True))
        lb = jnp.sum(e[:layer + 1], axis=0, keepdims=True) / jnp.sum(e, axis=0, keepdims=True)
        glog_ref[...] = jnp.log(lb + (1.0 - lb) * jax.nn.sigmoid(acc))


def _inproj(xf, mod6, g1, w_in_b, wk_b, lb_logits, seq, hk, layer):
    m, d = xf.shape
    d_in = w_in_b.shape[1]
    tm = min(seq, 1024)
    tpb = seq // tm
    tn = hk
    nj = pl.cdiv(d_in, tn)
    nl = lb_logits.shape[1]
    fdir = lambda j: jnp.clip(j - 1, 0, 1)
    return pl.pallas_call(
        functools.partial(_inproj_kernel, layer=layer),
        out_shape=(jax.ShapeDtypeStruct((m, d_in), BF16),
                   jax.ShapeDtypeStruct((m, 2 * hk), F32),
                   jax.ShapeDtypeStruct((m, LANES), F32)),
        grid=(m // tm, nj),
        in_specs=[pl.BlockSpec((tm, d), lambda i, j: (i, 0)),
                  pl.BlockSpec((None, 6, d), lambda i, j: (i // tpb, 0, 0)),
                  pl.BlockSpec((1, d), lambda i, j: (0, 0)),
                  pl.BlockSpec((d, tn), lambda i, j: (0, j)),
                  pl.BlockSpec((d, LANES), lambda i, j: (0, 0)),
                  pl.BlockSpec((None, nl, tn), lambda i, j: (fdir(j), 0, 0))],
        out_specs=(pl.BlockSpec((tm, tn), lambda i, j: (i, j)),
                   pl.BlockSpec((tm, tn), lambda i, j: (i, fdir(j))),
                   pl.BlockSpec((tm, LANES), lambda i, j: (i, 0))),
        scratch_shapes=[pltpu.VMEM((tm, d), BF16)],
        compiler_params=_cparams("parallel", "arbitrary"),
        name="norm1_inproj",
    )(xf, mod6, g1, w_in_b, wk_b, lb_logits)


def _tile_scan(g, d):
    rin = lax.broadcasted_iota(jnp.int32, g.shape, 0) & (TILE - 1)
    b = g
    step = 1
    while step < TILE:
        if d == 0:
            b = b + jnp.where(rin >= step, pltpu.roll(b, step, 0), 0.0)
        else:
            b = b + jnp.where(rin < TILE - step, pltpu.roll(b, GRP - step, 0), 0.0)
        step *= 2
    return b


def _group_cumsum(g, d):
    b = _tile_scan(g, d)
    ntile = GRP // TILE
    order = range(ntile) if d == 0 else range(ntile - 1, -1, -1)
    edge = TILE - 1 if d == 0 else 0
    out = [None] * ntile
    carry = None
    for i in order:
        t = b[i * TILE:(i + 1) * TILE]
        out[i] = t if carry is None else t + carry
        tot = t[edge:edge + 1]
        carry = tot if carry is None else carry + tot
    return jnp.concatenate(out, axis=0)


def _boundary(b, h, d):
    idx = h - 1 if d == 0 else h
    if 2 * h >= TILE:
        b3 = b.reshape(GRP // (2 * h), 2 * h, b.shape[1])
        return jnp.broadcast_to(b3[:, idx:idx + 1, :], b3.shape).reshape(b.shape)
    p = lax.broadcasted_iota(jnp.int32, b.shape, 0) & (2 * h - 1)
    out = b
    for pos in range(2 * h):
        shift = pos - idx
        if shift != 0:
            out = jnp.where(p == pos, pltpu.roll(b, shift % GRP, 0), out)
    return out


def _hgrn_kernel(q_ref, v_ref, gf_ref, gb_ref, o_ref, lv_scr, sg_scr, st_scr):
    seq = q_ref.shape[0]
    ngrp = seq // GRP
    nlev = GRP.bit_length()
    g_refs = (gf_ref, gb_ref)
    assert ngrp % 2 == 0

    r = lax.broadcasted_iota(jnp.int32, (GRP, GRP), 0)
    c = lax.broadcasted_iota(jnp.int32, (GRP, GRP), 1)
    lev = jnp.zeros((GRP, GRP), jnp.int32)
    for j in range(nlev - 1):
        lev = lev + jnp.where((r >> j) != (c >> j), 1, 0)
    lv_scr[0] = jnp.where(c <= r, lev, -1)
    lv_scr[1] = jnp.where(c >= r, lev, -1)
    st_scr[...] = jnp.zeros_like(st_scr)
    rr = lax.broadcasted_iota(jnp.int32, (GRP, LANES), 0)
    for l in range(1, nlev):
        late = (rr & (1 << (l - 1))) != 0
        sg_scr[0, l - 1] = jnp.where(late, 1.0, -1.0)
        sg_scr[1, l - 1] = jnp.where(late, -1.0, 1.0)

    def body(i, carry, first):
        for d in (0, 1):
            grp = i if d == 0 else ngrp - 1 - i
            r0 = pl.multiple_of(grp * GRP, GRP)
            g = g_refs[d][pl.ds(r0, GRP), :] * LOG2E
            qb = q_ref[pl.ds(r0, GRP), :]
            vb = v_ref[pl.ds(r0, GRP), :]
            qf = qb.astype(F32)
            kk = 1.0 - jnp.exp2(g)
            kb = kk.astype(BF16)
            b = _group_cumsum(g, d)
            edge = GRP - 1 if d == 0 else 0
            tot = b[edge:edge + 1]
            lv = lv_scr[d]
            att = jnp.where(lv == 0, _dot_nt(qb, kb), 0.0)
            for l in range(1, nlev):
                x = jnp.exp2((b - _boundary(b, 1 << (l - 1), d)) * sg_scr[d, l - 1]).astype(BF16)
                att = jnp.where(lv == l, _dot_nt(qb * x, kb * x), att)
            st = st_scr[d]
            o = _dot(att.astype(BF16), vb) + _dot_nt((qf * jnp.exp2(b)).astype(BF16), st.astype(BF16))
            if first:
                o_ref[pl.ds(r0, GRP), :] = o
            else:
                o_ref[pl.ds(r0, GRP), :] += o
            st_scr[d] = st * jnp.exp2(tot) + _dot_tn(vb, (kk * jnp.exp2(tot - b)).astype(BF16))
        return carry

    half = ngrp // 2
    lax.fori_loop(0, half, functools.partial(body, first=True), 0, unroll=2)
    lax.fori_loop(half, ngrp, functools.partial(body, first=False), 0, unroll=2)


def _hgrn(proj, glog, batch, seq, heads, hk):
    m = proj.shape[0]
    nh = hk // LANES
    vcol = 3 * nh
    blk = lambda off: pl.BlockSpec((seq, LANES), lambda b, h: (b, off + h))
    out = jax.ShapeDtypeStruct((m, hk), F32)
    return pl.pallas_call(
        _hgrn_kernel,
        out_shape=out,
        grid=(batch, heads),
        in_specs=[blk(0), blk(vcol), blk(0), blk(nh)],
        out_specs=blk(0),
        scratch_shapes=[pltpu.VMEM((2, GRP, GRP), jnp.int32),
                        pltpu.VMEM((2, GRP.bit_length() - 1, GRP, LANES), F32),
                        pltpu.VMEM((2, LANES, LANES), F32)],
        compiler_params=_cparams("parallel", "parallel"),
        name="hgrn2_scan",
    )(proj, proj, glog, glog)


def _mla_proj_kernel(cq_ref, ckv_ref, kpe_ref, cs_ref, qag_ref, kvag_ref, wq_ref, wkv_ref,
                     qgn_ref, qgr_ref, kgn_ref, kgr_ref, q_out, k_out, v_out,
                     *, scale, qk_dim, rope, heads):
    cq = cq_ref[...].astype(F32)
    a = (cq * lax.rsqrt(jnp.mean(cq * cq, axis=-1, keepdims=True) + EPS) * qag_ref[...]).astype(BF16)
    ckv = ckv_ref[...].astype(F32)
    c = (ckv * lax.rsqrt(jnp.mean(ckv * ckv, axis=-1, keepdims=True) + EPS)
         * kvag_ref[...]).astype(BF16)
    qall = _dot(a, wq_ref[...])
    kvall = _dot(c, wkv_ref[...])

    cs = cs_ref[...]
    lane = lax.broadcasted_iota(jnp.int32, cs.shape, 1)
    lo = lane < rope

    def rope_sumsq(rr):
        return jnp.sum(jnp.where(lo, rr * rr, 0.0), axis=-1, keepdims=True)

    def rotate(rr, gr):
        y = rr * gr * cs
        return y + pltpu.roll(y, rope, 1)

    kpe = kpe_ref[...]
    k_ss = rope_sumsq(kpe)
    k_rot = rotate(kpe, kgr_ref[...])
    for h in range(heads):
        base = 2 * LANES * h
        qn = qall[:, base:base + LANES]
        qr = qall[:, base + LANES:base + 2 * LANES]
        rq = lax.rsqrt((jnp.sum(qn * qn, axis=-1, keepdims=True) + rope_sumsq(qr)) / qk_dim + EPS) * scale
        q_out[h, :, :LANES] = (qn * qgn_ref[...] * rq).astype(BF16)
        q_out[h, :, LANES:] = jnp.where(lo, rotate(qr, qgr_ref[...]) * rq, 0.0).astype(BF16)
        kn = kvall[:, base:base + LANES]
        rk = lax.rsqrt((jnp.sum(kn * kn, axis=-1, keepdims=True) + k_ss) / qk_dim + EPS)
        k_out[h, :, :LANES] = (kn * kgn_ref[...] * rk).astype(BF16)
        k_out[h, :, LANES:] = jnp.where(lo, k_rot * rk, 0.0).astype(BF16)
        v_out[h] = kvall[:, base + LANES:base + 2 * LANES].astype(BF16)


def _mla_proj(proj, kpe2, cs, qag, kvag, wq_all, wkv_all, qgn, qgr, kgn, kgr,
              batch, seq, heads, cq_off, ckv_off, qk_dim, rope):
    m = proj.shape[0]
    ql, kvl = wq_all.shape[0], wkv_all.shape[0]
    tm = min(seq, 256)
    tpb = seq // tm
    assert cq_off % ql == 0 and ckv_off % kvl == 0
    vec = lambda n: pl.BlockSpec((1, n), lambda i: (0, 0))
    full = lambda w: pl.BlockSpec(w.shape, lambda i: (0, 0))
    hspec = lambda n: pl.BlockSpec((None, heads, tm, n), lambda i: (i // tpb, 0, i % tpb, 0))
    scale = qk_dim ** -0.5 * math.log2(math.e)
    return pl.pallas_call(
        functools.partial(_mla_proj_kernel, scale=scale, qk_dim=float(qk_dim), rope=rope, heads=heads),
        out_shape=(jax.ShapeDtypeStruct((batch, heads, seq, 2 * LANES), BF16),
                   jax.ShapeDtypeStruct((batch, heads, seq, 2 * LANES), BF16),
                   jax.ShapeDtypeStruct((batch, heads, seq, LANES), BF16)),
        grid=(m // tm,),
        in_specs=[pl.BlockSpec((tm, ql), lambda i: (i, cq_off // ql)),
                  pl.BlockSpec((tm, kvl), lambda i: (i, ckv_off // kvl)),
                  pl.BlockSpec((tm, LANES), lambda i: (i, 0)),
                  pl.BlockSpec((tm, LANES), lambda i: (i, 0)),
                  vec(ql), vec(kvl), full(wq_all), full(wkv_all),
                  vec(LANES), vec(LANES), vec(LANES), vec(LANES)],
        out_specs=(hspec(2 * LANES), hspec(2 * LANES), hspec(LANES)),
        compiler_params=_cparams("parallel"),
        name="mla_head_proj",
    )(proj, proj, kpe2, cs, qag, kvag, wq_all, wkv_all, qgn, qgr, kgn, kgr)


ATTN_KEYS = 512


def _attn_kernel(q_ref, k_ref, v_ref, o_ref):
    q = q_ref[...]
    seq = k_ref.shape[0]
    kc = min(ATTN_KEYS, seq)
    m = l = acc = None
    for c in range(seq // kc):
        rows = slice(c * kc, (c + 1) * kc)
        s = _dot_nt(q, k_ref[rows, :])
        mc = jnp.max(s, axis=-1, keepdims=True)
        if c == 0:
            m = mc
            p = jnp.exp2(s - m)
            l = jnp.sum(p, axis=-1, keepdims=True)
            acc = _dot(p.astype(BF16), v_ref[rows, :])
        else:
            m_new = jnp.maximum(m, mc)
            alpha = jnp.exp2(m - m_new)
            p = jnp.exp2(s - m_new)
            l = l * alpha + jnp.sum(p, axis=-1, keepdims=True)
            acc = acc * alpha + _dot(p.astype(BF16), v_ref[rows, :])
            m = m_new
    o_ref[...] = (acc / l).astype(BF16)


def _attention(qh, kh, vh):
    batch, mh, seq, dq = qh.shape
    dv = vh.shape[-1]
    tq = min(seq, 2048)
    nq = seq // tq
    return pl.pallas_call(
        _attn_kernel,
        out_shape=jax.ShapeDtypeStruct((batch * seq, mh * dv), BF16),
        grid=(batch, mh, nq),
        in_specs=[pl.BlockSpec((None, None, tq, dq), lambda b, h, i: (b, h, i, 0)),
                  pl.BlockSpec((None, None, seq, dq), lambda b, h, i: (b, h, 0, 0)),
                  pl.BlockSpec((None, None, seq, dv), lambda b, h, i: (b, h, 0, 0))],
        out_specs=pl.BlockSpec((tq, dv), lambda b, h, i: (b * nq + i, h)),
        compiler_params=_cparams("parallel", "parallel", "arbitrary"),
        name="mla_attention",
    )(qh, kh, vh)


def _eye(rows, cols):
    r = lax.broadcasted_iota(jnp.int32, (rows, cols), 0)
    c = lax.broadcasted_iota(jnp.int32, (rows, cols), 1)
    return jnp.where(r == c, 1.0, 0.0).astype(BF16)


def _outproj_kernel(o_ref, hg_ref, og_ref, om_ref, w_ref, x_ref, mod_ref, g2_ref, wr_ref,
                    x1_ref, h2_ref, aff_ref, lat_ref, mix_scr, *, heads, n_exp):
    hw = o_ref.shape[1]
    o = o_ref[...]
    gate = _silu(hg_ref[...].astype(F32))
    for h in range(heads):
        sl = slice(h * LANES, (h + 1) * LANES)
        oh = o[:, sl]
        r = lax.rsqrt(jnp.mean(oh * oh, axis=-1, keepdims=True) + EPS)
        mix_scr[:, sl] = (oh * r * og_ref[:, sl] * gate[:, sl]).astype(BF16)
    mix_scr[:, hw:] = om_ref[...]
    x1 = x_ref[...] + mod_ref[2:3, :] * _dot(mix_scr[...], w_ref[...])
    x1_ref[...] = x1
    r2 = lax.rsqrt(jnp.mean(x1 * x1, axis=-1, keepdims=True) + EPS)
    h2 = x1 * r2 * g2_ref[...] * (1.0 + mod_ref[4:5, :]) + mod_ref[3:4, :]
    h2_ref[...] = h2
    logits = _dot_hi(h2, wr_ref[...])
    lane = lax.broadcasted_iota(jnp.int32, logits.shape, 1)
    logits = jnp.where(lane < n_exp, logits, -jnp.inf)
    z = logits - jnp.max(logits, axis=-1, keepdims=True)
    p = jnp.exp(z)
    sp = jnp.sum(p, axis=-1, keepdims=True)
    aff_ref[...] = p / sp
    la = jnp.where(lane < n_exp, z - jnp.log(sp), 0.0)
    eye = _eye(n_exp, la.shape[1])
    p1, p2, p3 = _split3(la)
    lat_ref[...] = (_dot_nt(eye, p1) + _dot_nt(eye, p2)) + _dot_nt(eye, p3)


def _outproj(o_hgrn, proj, og, o_mla, w_out_b, xf, mod6, g2, wr_pad, seq, heads, hk, n_exp):
    m, d = xf.shape
    hw = o_hgrn.shape[1]
    mw = o_mla.shape[1]
    tm = min(seq, 256)
    tpb = seq // tm
    gcol = (3 * hk + hw) // hw
    assert (3 * hk + hw) % hw == 0
    row = lambda n: pl.BlockSpec((tm, n), lambda i: (i, 0))
    return pl.pallas_call(
        functools.partial(_outproj_kernel, heads=heads, n_exp=n_exp),
        out_shape=(jax.ShapeDtypeStruct((m, d), F32),
                   jax.ShapeDtypeStruct((m, d), F32),
                   jax.ShapeDtypeStruct((m, LANES), F32),
                   jax.ShapeDtypeStruct((m // seq, n_exp, seq), F32)),
        grid=(m // tm,),
        in_specs=[row(hw),
                  pl.BlockSpec((tm, hw), lambda i: (i, gcol)),
                  pl.BlockSpec((1, hw), lambda i: (0, 0)),
                  row(mw),
                  pl.BlockSpec((hw + mw, d), lambda i: (0, 0), pipeline_mode=pl.Buffered(1)),
                  row(d),
                  pl.BlockSpec((None, 6, d), lambda i: (i // tpb, 0, 0)),
                  pl.BlockSpec((1, d), lambda i: (0, 0)),
                  pl.BlockSpec((d, LANES), lambda i: (0, 0), pipeline_mode=pl.Buffered(1))],
        out_specs=(row(d), row(d), row(LANES),
                   pl.BlockSpec((None, n_exp, tm), lambda i: (i // tpb, 0, i % tpb))),
        scratch_shapes=[pltpu.VMEM((tm, hw + mw), BF16)],
        compiler_params=_cparams("parallel"),
        name="outproj_norm2_router",
    )(o_hgrn, proj, og, o_mla, w_out_b, xf, mod6, g2, wr_pad)


BISECT_STEPS = 64


COMBINE_TILE = 256
COMBINE_WIN = 64


def _topk_kernel(la_ref, slot_se_ref, idx_ref, tab_ref, tri_scr, cum_scr, *, cap, n_exp):
    nrow, seq = la_ref.shape
    ep = slot_se_ref.shape[1]
    rows = 256
    for k in range(seq // rows):
        r = lax.broadcasted_iota(jnp.int32, (rows, seq), 0) + k * rows
        c = lax.broadcasted_iota(jnp.int32, (rows, seq), 1)
        tri_scr[k * rows:(k + 1) * rows, :] = jnp.where(r < c, 1.0, 0.0).astype(BF16)

    def count(mask):
        return jnp.sum(jnp.where(mask, 1.0, 0.0), axis=-1, keepdims=True)

    def body(_, lh):
        lo, hi = lh
        mid = 0.5 * (lo + hi)
        ok = count(la_ref[...] >= mid) >= cap
        return jnp.where(ok, mid, lo), jnp.where(ok, hi, mid)

    la = la_ref[...]
    lo0 = jnp.min(la, axis=-1, keepdims=True)
    lo, hi = lax.fori_loop(0, BISECT_STEPS, body, (lo0, jnp.ones_like(lo0)))
    above = la >= hi
    tie = (la >= lo) & (la < hi)
    need = cap - count(above)
    tri = tri_scr[...]
    rank = _dot(jnp.where(tie, 1.0, 0.0).astype(BF16), tri)
    sel = above | (tie & (rank < need))
    pos = _dot(jnp.where(sel, 1.0, 0.0).astype(BF16), tri)
    slot = jnp.where(sel, pos, -1.0)
    eye = _eye(n_exp, ep)
    for b in range(nrow // n_exp):
        slot_se_ref[b * seq:(b + 1) * seq, :] = _dot_tn(
            slot[b * n_exp:(b + 1) * n_exp, :].astype(BF16), eye)
    cum_scr[...] = pos + jnp.where(sel, 1.0, 0.0)
    lane = lax.broadcasted_iota(jnp.int32, (nrow, cap), 1)

    def slot_body(c, acc):
        cnt = count(cum_scr[...] <= lax.convert_element_type(c, F32))
        return jnp.where(lane == c, cnt, acc)

    idx = lax.fori_loop(0, cap, slot_body, jnp.zeros((nrow, cap), F32), unroll=4)
    idx_ref[...] = idx.astype(jnp.int32)
    tok = lax.broadcasted_iota(jnp.int32, (nrow, seq), 1)
    tlane = lax.broadcasted_iota(jnp.int32, tab_ref.shape, 1)
    tab = jnp.zeros(tab_ref.shape, F32)
    tile = min(seq, COMBINE_TILE)
    for k in range(seq // tile + 1):
        tab = jnp.where(tlane == k, count(sel & (tok < k * tile)), tab)
    tab_ref[...] = tab.astype(jnp.int32)


def _topk(lat, batch, seq, n_exp, cap):
    return pl.pallas_call(
        functools.partial(_topk_kernel, cap=cap, n_exp=n_exp),
        out_shape=(jax.ShapeDtypeStruct((batch * seq, LANES), F32),
                   jax.ShapeDtypeStruct((batch * n_exp, cap), jnp.int32),
                   jax.ShapeDtypeStruct((batch * n_exp, LANES), jnp.int32)),
        scratch_shapes=[pltpu.VMEM((seq, seq), BF16), pltpu.VMEM((batch * n_exp, seq), F32)],
        compiler_params=pltpu.CompilerParams(vmem_limit_bytes=VMEM_LIMIT),
        name="expert_choice_topk",
    )(lat.reshape(batch * n_exp, seq))


def _ffn_kernel(idx_ref, h2_hbm, wg_ref, wu_ref, wd_ref, ye_ref, xe_scr, hmid_scr, sem,
                *, nt, nd, tf):
    e = pl.program_id(0)
    s = pl.program_id(1)
    rows = xe_scr.shape[0]

    def start_gather(expert):
        base = expert * rows

        def body(k, carry):
            r0 = pl.multiple_of(k * TILE, TILE)
            for j in range(TILE):
                pltpu.make_async_copy(h2_hbm.at[pl.ds(idx_ref[base + r0 + j], 1), :],
                                      xe_scr.at[pl.ds(r0 + j, 1), :], sem.at[0]).start()
            return carry
        lax.fori_loop(0, rows // TILE, body, 0)

    @pl.when((e == 0) & (s == 0))
    def _():
        start_gather(0)

    @pl.when(s == 0)
    def _():
        pltpu.make_async_copy(h2_hbm.at[pl.ds(0, rows), :], xe_scr, sem.at[0]).wait()

    @pl.when(s < nt)
    def _():
        xe = xe_scr[...].astype(BF16)
        a = _dot(xe, wg_ref[...].astype(BF16))
        u = _dot(xe, wu_ref[...].astype(BF16))
        hmid_scr[s] = (_silu(a) * u).astype(BF16)

    def down_step(prefetch):
        per = rows // (nd * nt)
        y = None
        for k in range(nt):
            if prefetch:
                dst0 = (s - nt) * (per * nt) + k * per
                first = (e + 1) * rows + dst0
                for j in range(per):
                    pltpu.make_async_copy(h2_hbm.at[pl.ds(idx_ref[first + j], 1), :],
                                          xe_scr.at[pl.ds(dst0 + j, 1), :], sem.at[0]).start()
            part = _dot(hmid_scr[k], wd_ref[k * tf:(k + 1) * tf, :].astype(BF16))
            y = part if y is None else y + part
        ye_ref[...] = y.astype(BF16)

    more = e + 1 < pl.num_programs(0)

    @pl.when((s >= nt) & more)
    def _():
        down_step(True)

    @pl.when((s >= nt) & jnp.logical_not(more))
    def _():
        down_step(False)


def _ffn(idx, h2, w_gate, w_up, w_down):
    n_exp, rows = idx.shape
    idx = idx.reshape(n_exp * rows)
    d = h2.shape[1]
    ff = w_gate.shape[2]
    tf = min(ff, 512)
    tn = min(d, 1024)
    nt = ff // tf
    nd = d // tn
    assert rows % (nd * nt) == 0
    up = lambda e, s, idx: (e, 0, jnp.minimum(s, nt - 1))
    down = lambda e, s, idx: (e, 0, jnp.maximum(s - nt, 0))
    return pl.pallas_call(
        functools.partial(_ffn_kernel, nt=nt, nd=nd, tf=tf),
        out_shape=jax.ShapeDtypeStruct((n_exp, rows, d), BF16),
        grid_spec=pltpu.PrefetchScalarGridSpec(
            num_scalar_prefetch=1,
            grid=(n_exp, nt + nd),
            in_specs=[pl.BlockSpec(memory_space=pl.ANY),
                      pl.BlockSpec((None, d, tf), up),
                      pl.BlockSpec((None, d, tf), up),
                      pl.BlockSpec((None, ff, tn), down)],
            out_specs=pl.BlockSpec((None, rows, tn), down),
            scratch_shapes=[pltpu.VMEM((rows, d), F32),
                            pltpu.VMEM((nt, rows, tf), BF16),
                            pltpu.SemaphoreType.DMA((1,))]),
        compiler_params=_cparams("arbitrary", "arbitrary"),
        name="expert_swiglu",
    )(idx, h2, w_gate, w_up, w_down)


def _combine_kernel(tab_ref, slot_ref, aff_ref, ye_ref, x1_ref, mod_ref, out_ref, y_scr,
                    *, n_exp, cap, win):
    b = pl.program_id(0)
    t = pl.program_id(1)
    tt = x1_ref.shape[0]
    base = (b * (pl.num_programs(1) + 1) + t) * n_exp
    pack = 16
    starts = []
    short = None
    for e in range(n_exp):
        c0 = tab_ref[base + e]
        c1 = tab_ref[base + n_exp + e]
        a = jnp.minimum(c0 & ~(pack - 1), cap - win)
        ok = c1 - a <= win
        starts.append(a)
        short = ok if short is None else short & ok

    def finish(acc):
        out_ref[...] = x1_ref[...] + mod_ref[5:6, :] * acc

    @pl.when(short)
    def _():
        lane = lax.broadcasted_iota(jnp.int32, (tt, LANES), 1).astype(F32)
        per = LANES // win
        blocks = []
        for g in range(n_exp // per):
            blk = jnp.zeros((tt, LANES), F32)
            for j in range(per):
                e = g * per + j
                a = pl.multiple_of(starts[e], pack)
                y_scr[e * win:(e + 1) * win, :] = ye_ref[e, pl.ds(a, win), :]
                slot = slot_ref[:, e:e + 1]
                rel = jnp.where(slot >= 0.0, slot - a.astype(F32) + float(j * win), -1.0)
                blk = jnp.where(lane == rel, aff_ref[:, e:e + 1], blk)
            blocks.append(blk.astype(BF16))
        finish(_dot(jnp.concatenate(blocks, axis=1), y_scr[...]))

    @pl.when(jnp.logical_not(short))
    def _():
        cidx = lax.broadcasted_iota(jnp.int32, (tt, cap), 1).astype(F32)
        acc = jnp.zeros(x1_ref.shape, F32)
        for e in range(n_exp):
            onehot = jnp.where(cidx == slot_ref[:, e:e + 1], 1.0, 0.0).astype(BF16)
            acc = acc + aff_ref[:, e:e + 1] * _dot(onehot, ye_ref[e])
        finish(acc)


def _combine(tab, slot_se, aff, ye4, x1, mod6, seq, n_exp, cap):
    m, d = x1.shape
    ep = slot_se.shape[1]
    batch = m // seq
    tt = min(seq, COMBINE_TILE)
    tpb = seq // tt
    win = min(COMBINE_WIN, cap)
    assert LANES % win == 0 and n_exp % (LANES // win) == 0 and cap % 16 == 0
    return pl.pallas_call(
        functools.partial(_combine_kernel, n_exp=n_exp, cap=cap, win=win),
        out_shape=jax.ShapeDtypeStruct((m, d), F32),
        grid_spec=pltpu.PrefetchScalarGridSpec(
            num_scalar_prefetch=1,
            grid=(batch, tpb),
            in_specs=[pl.BlockSpec((tt, ep), lambda b, t, tab: (b * tpb + t, 0)),
                      pl.BlockSpec((tt, ep), lambda b, t, tab: (b * tpb + t, 0)),
                      pl.BlockSpec((n_exp, None, cap, d), lambda b, t, tab: (0, b, 0, 0)),
                      pl.BlockSpec((tt, d), lambda b, t, tab: (b * tpb + t, 0)),
                      pl.BlockSpec((None, 6, d), lambda b, t, tab: (b, 0, 0))],
            out_specs=pl.BlockSpec((tt, d), lambda b, t, tab: (b * tpb + t, 0)),
            scratch_shapes=[pltpu.VMEM((n_exp * win, d), BF16)]),
        compiler_params=_cparams("parallel", "arbitrary"),
        name="expert_combine",
    )(tab, slot_se, aff, ye4, x1, mod6)


def kernel(x, c, positions, w_ada, b_ada, norm1_g, w_in, lb_logits, hgrn_out_g, qa_norm_g, w_uq,
           kva_norm_g, w_ukv, q_head_g, k_head_g, w_out, norm2_g, w_router, w_gate, w_up, w_down):
    batch, seq, d = x.shape
    depth = w_ada.shape[0]
    m = batch * seq
    hk = lb_logits.shape[2]
    heads, dv = hgrn_out_g.shape[1], hgrn_out_g.shape[2]
    hw = heads * dv
    ql, kvl = qa_norm_g.shape[1], kva_norm_g.shape[1]
    qk_dim = q_head_g.shape[1]
    mh = w_uq.shape[2] // qk_dim
    d_in = w_in.shape[2]
    rope = d_in - (3 * hk + 2 * hw + ql + kvl)
    nope = qk_dim - rope
    vdim = w_ukv.shape[2] // mh - nope
    n_exp = w_router.shape[2]
    cap = EC_CAPACITY * seq // n_exp
    assert dv == LANES and hk == hw and nope == LANES and vdim == LANES and 2 * rope == LANES
    assert ql + kvl + rope <= hk and seq % GRP == 0 and n_exp <= LANES and cap % 8 == 0

    cq_off = 3 * hk + 2 * hw
    ckv_off = cq_off + ql
    kpe_off = ckv_off + kvl
    swap = jnp.concatenate([jnp.arange(rope // 2, rope), jnp.arange(0, rope // 2)])

    def both(v):
        return jnp.concatenate([v, v[..., swap]], axis=-1)

    cs = _rope_tables(positions, rope)
    c8 = jnp.pad(c, ((0, (-batch) % 8), (0, 0)))
    xf = x.reshape(m, d)
    for l in range(depth):
        mod6 = _ada(c8, w_ada[l], b_ada[l])[:batch].reshape(batch, 6, d)

        w_in_b = w_in[l].astype(BF16)
        wk_b = both(w_in[l][:, kpe_off:kpe_off + rope]).astype(BF16)
        proj, glog, kpe2 = _inproj(xf, mod6, norm1_g[l].reshape(1, d), w_in_b, wk_b, lb_logits,
                                   seq, hk, l)

        o_hgrn = _hgrn(proj, glog, batch, seq, heads, hk)

        wq = w_uq[l].reshape(ql, mh, qk_dim)
        wq_all = jnp.concatenate([wq[..., :nope], both(wq[..., nope:])], axis=-1)
        qh, kh, vh = _mla_proj(
            proj, kpe2, cs, qa_norm_g[l].reshape(1, ql), kva_norm_g[l].reshape(1, kvl),
            wq_all.reshape(ql, mh * 2 * LANES).astype(BF16), w_ukv[l].astype(BF16),
            q_head_g[l][:nope].reshape(1, nope), both(q_head_g[l][nope:]).reshape(1, 2 * rope),
            k_head_g[l][:nope].reshape(1, nope), both(k_head_g[l][nope:]).reshape(1, 2 * rope),
            batch, seq, mh, cq_off, ckv_off, qk_dim, rope)
        o_mla = _attention(qh, kh, vh)

        wr_pad = jnp.pad(w_router[l], ((0, 0), (0, LANES - n_exp)))
        x1, h2, aff, lat = _outproj(o_hgrn, proj, hgrn_out_g[l].reshape(1, hw), o_mla,
                                    w_out[l].astype(BF16), xf, mod6, norm2_g[l].reshape(1, d),
                                    wr_pad, seq, heads, hk, n_exp)

        slot_se, idx, tab = _topk(lat, batch, seq, n_exp, cap)
        ntile = seq // min(seq, COMBINE_TILE)
        tab = tab[:, :ntile + 1].reshape(batch, n_exp, ntile + 1).transpose(0, 2, 1).reshape(-1)
        rows = idx.reshape(batch, n_exp, cap) + (jnp.arange(batch, dtype=jnp.int32) * seq)[:, None, None]
        rows = rows.transpose(1, 0, 2).reshape(n_exp, batch * cap)
        ye = _ffn(rows, h2, w_gate[l], w_up[l], w_down[l])
        xf = _combine(tab, slot_se, aff, ye.reshape(n_exp, batch, cap, d), x1, mod6, seq, n_exp, cap)
    return xf.reshape(batch, seq, d)
```

```python
import functools
import math

import jax
import jax.numpy as jnp
from jax import lax
from jax.experimental import pallas as pl
from jax.experimental.pallas import tpu as pltpu

F32 = jnp.float32
BF16 = jnp.bfloat16
EPS = 1e-6
ROPE_BASE = 10000.0
LOG2E = math.log2(math.e)
EC_CAPACITY = 2
LANES = 128
TILE = 8
GRP = 128
VMEM_LIMIT = 56 * 1024 * 1024


def _cparams(*sem):
    return pltpu.CompilerParams(dimension_semantics=sem, vmem_limit_bytes=VMEM_LIMIT)


def _dot(a, b):
    return jnp.dot(a, b, preferred_element_type=F32)


def _dot_nt(a, b):
    return lax.dot_general(a, b, (((1,), (1,)), ((), ())), preferred_element_type=F32)


def _dot_tn(a, b):
    return lax.dot_general(a, b, (((0,), (0,)), ((), ())), preferred_element_type=F32)


def _split2(a):
    hi = a.astype(BF16)
    lo = (a - hi.astype(F32)).astype(BF16)
    return hi, lo


def _split3(a):
    p1 = a.astype(BF16)
    r1 = a - p1.astype(F32)
    p2 = r1.astype(BF16)
    p3 = (r1 - p2.astype(F32)).astype(BF16)
    return p1, p2, p3


def _dot_hi(a, b):
    ah, al = _split2(a)
    bh, bl = _split2(b)
    return _dot(ah, bh) + (_dot(ah, bl) + _dot(al, bh))


def _silu(x):
    return x * jax.nn.sigmoid(x)


def _rope_kernel(pos_ref, cs_ref, *, half):
    pos = pos_ref[...].astype(F32)
    lane = lax.broadcasted_iota(jnp.int32, (1, 4 * half), 1)
    j = (lane & (half - 1)).astype(F32)
    inv_freq = jnp.exp(j * (-2.0 * math.log(ROPE_BASE) / (2 * half)))
    ang = pos * inv_freq
    c = jnp.cos(ang)
    s = jnp.sin(ang)
    cs_ref[...] = jnp.where(lane < 2 * half, c, jnp.where(lane < 3 * half, -s, s))


def _rope_tables(positions, rope):
    m = positions.size
    tm = min(m, 1024)
    half = rope // 2
    return pl.pallas_call(
        functools.partial(_rope_kernel, half=half),
        out_shape=jax.ShapeDtypeStruct((m, 2 * rope), F32),
        grid=(m // tm,),
        in_specs=[pl.BlockSpec((tm, 1), lambda i: (i, 0))],
        out_specs=pl.BlockSpec((tm, 2 * rope), lambda i: (i, 0)),
        compiler_params=_cparams("parallel"),
        name="rope_tables",
    )(positions.reshape(m, 1))


def _ada_kernel(c_ref, w_ref, b_ref, o_ref):
    part = _dot_hi(_silu(c_ref[...]), w_ref[...])

    @pl.when(pl.program_id(0) == 0)
    def _():
        o_ref[...] = part + b_ref[...]

    @pl.when(pl.program_id(0) > 0)
    def _():
        o_ref[...] += part


def _ada(c8, w, b):
    d, n = w.shape
    tk = min(d, 128)
    return pl.pallas_call(
        _ada_kernel,
        out_shape=jax.ShapeDtypeStruct((c8.shape[0], n), F32),
        grid=(d // tk,),
        in_specs=[pl.BlockSpec((c8.shape[0], tk), lambda k: (0, k)),
                  pl.BlockSpec((tk, n), lambda k: (k, 0)),
                  pl.BlockSpec((1, n), lambda k: (0, 0))],
        out_specs=pl.BlockSpec((c8.shape[0], n), lambda k: (0, 0)),
        compiler_params=_cparams("arbitrary"),
        name="ada_mod",
    )(c8, w, b.reshape(1, n))


def _inproj_kernel(x_ref, mod_ref, g_ref, w_ref, wk_ref, lbl_ref,
                   proj_ref, fgate_ref, kpe_ref, h_scr, *, layer):
    j = pl.program_id(1)

    @pl.when(j == 0)
    def _():
        x = x_ref[...]
        r = lax.rsqrt(jnp.mean(x * x, axis=-1, keepdims=True) + EPS)
        h = x * r * g_ref[...] * (1.0 + mod_ref[1:2, :]) + mod_ref[0:1, :]
        hb = h.astype(BF16)
        h_scr[...] = hb
        kpe_ref[...] = _dot(hb, wk_ref[...])

    acc = _dot(h_scr[...], w_ref[...].astype(BF16))
    proj_ref[...] = acc.astype(BF16)

    @pl.when((j == 1) | (j == 2))
    def _():
        lg = lbl_ref[...]
        e = jnp.exp(lg - jnp.max(lg, axis=0, keepdims=True))
        lb = jnp.sum(e[:layer + 1], axis=0, keepdims=True) / jnp.sum(e, axis=0, keepdims=True)
        fgate_ref[...] = lb + (1.0 - lb) * jax.nn.sigmoid(acc)


def _inproj(xf, mod6, g1, w_in_b, wk_b, lb_logits, seq, hk, layer):
    m, d = xf.shape
    d_in = w_in_b.shape[1]
    tm = min(seq, 1024)
    tpb = seq // tm
    tn = hk
    nj = pl.cdiv(d_in, tn)
    nl = lb_logits.shape[1]
    fdir = lambda j: jnp.clip(j - 1, 0, 1)
    return pl.pallas_call(
        functools.partial(_inproj_kernel, layer=layer),
        out_shape=(jax.ShapeDtypeStruct((m, d_in), BF16),
                   jax.ShapeDtypeStruct((m, 2 * hk), F32),
                   jax.ShapeDtypeStruct((m, LANES), F32)),
        grid=(m // tm, nj),
        in_specs=[pl.BlockSpec((tm, d), lambda i, j: (i, 0), pipeline_mode=pl.Buffered(1)),
                  pl.BlockSpec((None, 6, d), lambda i, j: (i // tpb, 0, 0)),
                  pl.BlockSpec((1, d), lambda i, j: (0, 0)),
                  pl.BlockSpec((d, tn), lambda i, j: (0, j)),
                  pl.BlockSpec((d, LANES), lambda i, j: (0, 0)),
                  pl.BlockSpec((None, nl, tn), lambda i, j: (fdir(j), 0, 0))],
        out_specs=(pl.BlockSpec((tm, tn), lambda i, j: (i, j)),
                   pl.BlockSpec((tm, tn), lambda i, j: (i, fdir(j))),
                   pl.BlockSpec((tm, LANES), lambda i, j: (i, 0))),
        scratch_shapes=[pltpu.VMEM((tm, d), BF16)],
        compiler_params=_cparams("parallel", "arbitrary"),
        name="norm1_inproj",
    )(xf, mod6, g1, w_in_b, wk_b, lb_logits)


def _tile_scan(g, d):
    rin = lax.broadcasted_iota(jnp.int32, g.shape, 0) & (TILE - 1)
    b = g
    step = 1
    while step < TILE:
        if d == 0:
            b = b + jnp.where(rin >= step, pltpu.roll(b, step, 0), 0.0)
        else:
            b = b + jnp.where(rin < TILE - step, pltpu.roll(b, GRP - step, 0), 0.0)
        step *= 2
    return b


def _group_cumsum(g, d):
    b = _tile_scan(g, d)
    ntile = GRP // TILE
    order = range(ntile) if d == 0 else range(ntile - 1, -1, -1)
    edge = TILE - 1 if d == 0 else 0
    out = [None] * ntile
    carry = None
    for i in order:
        t = b[i * TILE:(i + 1) * TILE]
        out[i] = t if carry is None else t + carry
        tot = t[edge:edge + 1]
        carry = tot if carry is None else carry + tot
    return jnp.concatenate(out, axis=0)


def _boundary(b, h, d):
    idx = h - 1 if d == 0 else h
    if 2 * h >= TILE:
        b3 = b.reshape(GRP // (2 * h), 2 * h, b.shape[1])
        return jnp.broadcast_to(b3[:, idx:idx + 1, :], b3.shape).reshape(b.shape)
    p = lax.broadcasted_iota(jnp.int32, b.shape, 0) & (2 * h - 1)
    out = b
    for pos in range(2 * h):
        shift = pos - idx
        if shift != 0:
            out = jnp.where(p == pos, pltpu.roll(b, shift % GRP, 0), out)
    return out


def _hgrn_kernel(q_ref, v_ref, gf_ref, gb_ref, o_ref, lv_scr, sg_scr, st_scr):
    seq = q_ref.shape[0]
    ngrp = seq // GRP
    nlev = GRP.bit_length()
    g_refs = (gf_ref, gb_ref)
    assert ngrp % 2 == 0

    r = lax.broadcasted_iota(jnp.int32, (GRP, GRP), 0)
    c = lax.broadcasted_iota(jnp.int32, (GRP, GRP), 1)
    lev = jnp.zeros((GRP, GRP), jnp.int32)
    for j in range(nlev - 1):
        lev = lev + jnp.where((r >> j) != (c >> j), 1, 0)
    lv_scr[0] = jnp.where(c <= r, lev, -1)
    lv_scr[1] = jnp.where(c >= r, lev, -1)
    st_scr[...] = jnp.zeros_like(st_scr)
    rr = lax.broadcasted_iota(jnp.int32, (GRP, LANES), 0)
    for l in range(1, nlev):
        late = (rr & (1 << (l - 1))) != 0
        sg_scr[0, l - 1] = jnp.where(late, 1.0, -1.0)
        sg_scr[1, l - 1] = jnp.where(late, -1.0, 1.0)

    def body(i, carry, first):
        for d in (0, 1):
            grp = i if d == 0 else ngrp - 1 - i
            r0 = pl.multiple_of(grp * GRP, GRP)
            f = g_refs[d][pl.ds(r0, GRP), :]
            g = jnp.log2(f)
            qb = q_ref[pl.ds(r0, GRP), :]
            vb = v_ref[pl.ds(r0, GRP), :]
            qf = qb.astype(F32)
            kk = 1.0 - f
            kb = kk.astype(BF16)
            b = _group_cumsum(g, d)
            edge = GRP - 1 if d == 0 else 0
            tot = b[edge:edge + 1]
            lv = lv_scr[d]
            att = jnp.where(lv == 0, _dot_nt(qb, kb), 0.0)
            for l in range(1, nlev):
                x = jnp.exp2((b - _boundary(b, 1 << (l - 1), d)) * sg_scr[d, l - 1]).astype(BF16)
                att = jnp.where(lv == l, _dot_nt(qb * x, kb * x), att)
            st = st_scr[d]
            o = _dot(att.astype(BF16), vb) + _dot_nt((qf * jnp.exp2(b)).astype(BF16), st.astype(BF16))
            if first:
                o_ref[pl.ds(r0, GRP), :] = o
            else:
                o_ref[pl.ds(r0, GRP), :] += o
            st_scr[d] = st * jnp.exp2(tot) + _dot_tn(vb, (kk * jnp.exp2(tot - b)).astype(BF16))
        return carry

    half = ngrp // 2
    lax.fori_loop(0, half, functools.partial(body, first=True), 0, unroll=2)
    lax.fori_loop(half, ngrp, functools.partial(body, first=False), 0, unroll=2)


def _hgrn(proj, fgate, batch, seq, heads, hk):
    m = proj.shape[0]
    nh = hk // LANES
    vcol = 3 * nh
    blk = lambda off: pl.BlockSpec((seq, LANES), lambda b, h: (b, off + h))
    out = jax.ShapeDtypeStruct((m, hk), F32)
    return pl.pallas_call(
        _hgrn_kernel,
        out_shape=out,
        grid=(batch, heads),
        in_specs=[blk(0), blk(vcol), blk(0), blk(nh)],
        out_specs=blk(0),
        scratch_shapes=[pltpu.VMEM((2, GRP, GRP), jnp.int32),
                        pltpu.VMEM((2, GRP.bit_length() - 1, GRP, LANES), F32),
                        pltpu.VMEM((2, LANES, LANES), F32)],
        compiler_params=_cparams("parallel", "parallel"),
        name="hgrn2_scan",
    )(proj, proj, fgate, fgate)


def _mla_proj_kernel(cq_ref, ckv_ref, kpe_ref, cs_ref, qag_ref, kvag_ref, wq_ref, wkv_ref,
                     qgn_ref, qgr_ref, kgn_ref, kgr_ref, q_out, k_out, v_out,
                     *, scale, qk_dim, rope, heads):
    cq = cq_ref[...].astype(F32)
    a = (cq * lax.rsqrt(jnp.mean(cq * cq, axis=-1, keepdims=True) + EPS) * qag_ref[...]).astype(BF16)
    ckv = ckv_ref[...].astype(F32)
    c = (ckv * lax.rsqrt(jnp.mean(ckv * ckv, axis=-1, keepdims=True) + EPS)
         * kvag_ref[...]).astype(BF16)
    qall = _dot(a, wq_ref[...])
    kvall = _dot(c, wkv_ref[...])

    cs = cs_ref[...]
    lane = lax.broadcasted_iota(jnp.int32, cs.shape, 1)
    lo = lane < rope

    def rope_sumsq(rr):
        return jnp.sum(jnp.where(lo, rr * rr, 0.0), axis=-1, keepdims=True)

    def rotate(rr, gr):
        y = rr * gr * cs
        return y + pltpu.roll(y, rope, 1)

    kpe = kpe_ref[...]
    k_ss = rope_sumsq(kpe)
    k_rot = rotate(kpe, kgr_ref[...])
    for h in range(heads):
        base = 2 * LANES * h
        qn = qall[:, base:base + LANES]
        qr = qall[:, base + LANES:base + 2 * LANES]
        rq = lax.rsqrt((jnp.sum(qn * qn, axis=-1, keepdims=True) + rope_sumsq(qr)) / qk_dim + EPS) * scale
        q_out[h, :, :LANES] = (qn * qgn_ref[...] * rq).astype(BF16)
        q_out[h, :, LANES:] = jnp.where(lo, rotate(qr, qgr_ref[...]) * rq, 0.0).astype(BF16)
        kn = kvall[:, base:base + LANES]
        rk = lax.rsqrt((jnp.sum(kn * kn, axis=-1, keepdims=True) + k_ss) / qk_dim + EPS)
        k_out[h, :, :LANES] = (kn * kgn_ref[...] * rk).astype(BF16)
        k_out[h, :, LANES:] = jnp.where(lo, k_rot * rk, 0.0).astype(BF16)
        v_out[h] = kvall[:, base + LANES:base + 2 * LANES].astype(BF16)


def _mla_proj(proj, kpe2, cs, qag, kvag, wq_all, wkv_all, qgn, qgr, kgn, kgr,
              batch, seq, heads, cq_off, ckv_off, qk_dim, rope):
    m = proj.shape[0]
    ql, kvl = wq_all.shape[0], wkv_all.shape[0]
    tm = min(seq, 256)
    tpb = seq // tm
    assert cq_off % ql == 0 and ckv_off % kvl == 0
    vec = lambda n: pl.BlockSpec((1, n), lambda i: (0, 0))
    full = lambda w: pl.BlockSpec(w.shape, lambda i: (0, 0))
    hspec = lambda n: pl.BlockSpec((None, heads, tm, n), lambda i: (i // tpb, 0, i % tpb, 0))
    scale = qk_dim ** -0.5 * LOG2E
    return pl.pallas_call(
        functools.partial(_mla_proj_kernel, scale=scale, qk_dim=float(qk_dim), rope=rope, heads=heads),
        out_shape=(jax.ShapeDtypeStruct((batch, heads, seq, 2 * LANES), BF16),
                   jax.ShapeDtypeStruct((batch, heads, seq, 2 * LANES), BF16),
                   jax.ShapeDtypeStruct((batch, heads, seq, LANES), BF16)),
        grid=(m // tm,),
        in_specs=[pl.BlockSpec((tm, ql), lambda i: (i, cq_off // ql)),
                  pl.BlockSpec((tm, kvl), lambda i: (i, ckv_off // kvl)),
                  pl.BlockSpec((tm, LANES), lambda i: (i, 0)),
                  pl.BlockSpec((tm, LANES), lambda i: (i, 0)),
                  vec(ql), vec(kvl), full(wq_all), full(wkv_all),
                  vec(LANES), vec(LANES), vec(LANES), vec(LANES)],
        out_specs=(hspec(2 * LANES), hspec(2 * LANES), hspec(LANES)),
        compiler_params=_cparams("parallel"),
        name="mla_head_proj",
    )(proj, proj, kpe2, cs, qag, kvag, wq_all, wkv_all, qgn, qgr, kgn, kgr)


ATTN_KEYS = 512


def _attn_kernel(q_ref, k_ref, v_ref, o_ref):
    q = q_ref[...]
    seq = k_ref.shape[0]
    kc = min(ATTN_KEYS, seq)
    m = l = acc = None
    for c in range(seq // kc):
        rows = slice(c * kc, (c + 1) * kc)
        s = _dot_nt(q, k_ref[rows, :])
        mc = jnp.max(s, axis=-1, keepdims=True)
        if c == 0:
            m = mc
            p = jnp.exp2(s - m)
            l = jnp.sum(p, axis=-1, keepdims=True)
            acc = _dot(p.astype(BF16), v_ref[rows, :])
        else:
            m_new = jnp.maximum(m, mc)
            alpha = jnp.exp2(m - m_new)
            p = jnp.exp2(s - m_new)
            l = l * alpha + jnp.sum(p, axis=-1, keepdims=True)
            acc = acc * alpha + _dot(p.astype(BF16), v_ref[rows, :])
            m = m_new
    o_ref[...] = (acc / l).astype(BF16)


def _attention(qh, kh, vh):
    batch, mh, seq, dq = qh.shape
    dv = vh.shape[-1]
    tq = min(seq, 2048)
    nq = seq // tq
    return pl.pallas_call(
        _attn_kernel,
        out_shape=jax.ShapeDtypeStruct((batch * seq, mh * dv), BF16),
        grid=(batch, mh, nq),
        in_specs=[pl.BlockSpec((None, None, tq, dq), lambda b, h, i: (b, h, i, 0)),
                  pl.BlockSpec((None, None, seq, dq), lambda b, h, i: (b, h, 0, 0)),
                  pl.BlockSpec((None, None, seq, dv), lambda b, h, i: (b, h, 0, 0))],
        out_specs=pl.BlockSpec((tq, dv), lambda b, h, i: (b * nq + i, h)),
        compiler_params=_cparams("parallel", "parallel", "arbitrary"),
        name="mla_attention",
    )(qh, kh, vh)


def _eye(rows, cols):
    r = lax.broadcasted_iota(jnp.int32, (rows, cols), 0)
    c = lax.broadcasted_iota(jnp.int32, (rows, cols), 1)
    return jnp.where(r == c, 1.0, 0.0).astype(BF16)


def _outproj_kernel(o_ref, hg_ref, og_ref, om_ref, w_ref, x_ref, mod_ref, g2_ref, wr_ref,
                    x1_ref, h2_ref, aff_ref, lat_ref, mix_scr, *, heads, n_exp):
    hw = o_ref.shape[1]
    o = o_ref[...]
    gate = _silu(hg_ref[...].astype(F32))
    for h in range(heads):
        sl = slice(h * LANES, (h + 1) * LANES)
        oh = o[:, sl]
        r = lax.rsqrt(jnp.mean(oh * oh, axis=-1, keepdims=True) + EPS)
        mix_scr[:, sl] = (oh * r * og_ref[:, sl] * gate[:, sl]).astype(BF16)
    mix_scr[:, hw:] = om_ref[...]
    x1 = x_ref[...] + mod_ref[2:3, :] * _dot(mix_scr[...], w_ref[...])
    x1_ref[...] = x1
    r2 = lax.rsqrt(jnp.mean(x1 * x1, axis=-1, keepdims=True) + EPS)
    h2 = x1 * r2 * g2_ref[...] * (1.0 + mod_ref[4:5, :]) + mod_ref[3:4, :]
    h2_ref[...] = h2
    logits = _dot_hi(h2, wr_ref[...])
    lane = lax.broadcasted_iota(jnp.int32, logits.shape, 1)
    logits = jnp.where(lane < n_exp, logits, -jnp.inf)
    z = logits - jnp.max(logits, axis=-1, keepdims=True)
    p = jnp.exp(z)
    sp = jnp.sum(p, axis=-1, keepdims=True)
    aff_ref[...] = p / sp
    la = jnp.where(lane < n_exp, z - jnp.log(sp), 0.0)
    eye = _eye(n_exp, la.shape[1])
    p1, p2, p3 = _split3(la)
    lat_ref[...] = (_dot_nt(eye, p1) + _dot_nt(eye, p2)) + _dot_nt(eye, p3)


def _outproj(o_hgrn, proj, og, o_mla, w_out_b, xf, mod6, g2, wr_pad, seq, heads, hk, n_exp):
    m, d = xf.shape
    hw = o_hgrn.shape[1]
    mw = o_mla.shape[1]
    tm = min(seq, 256)
    tpb = seq // tm
    gcol = (3 * hk + hw) // hw
    assert (3 * hk + hw) % hw == 0
    row = lambda n: pl.BlockSpec((tm, n), lambda i: (i, 0))
    return pl.pallas_call(
        functools.partial(_outproj_kernel, heads=heads, n_exp=n_exp),
        out_shape=(jax.ShapeDtypeStruct((m, d), F32),
                   jax.ShapeDtypeStruct((m, d), F32),
                   jax.ShapeDtypeStruct((m, LANES), F32),
                   jax.ShapeDtypeStruct((m // seq, n_exp, seq), F32)),
        grid=(m // tm,),
        in_specs=[row(hw),
                  pl.BlockSpec((tm, hw), lambda i: (i, gcol)),
                  pl.BlockSpec((1, hw), lambda i: (0, 0)),
                  row(mw),
                  pl.BlockSpec((hw + mw, d), lambda i: (0, 0), pipeline_mode=pl.Buffered(1)),
                  row(d),
                  pl.BlockSpec((None, 6, d), lambda i: (i // tpb, 0, 0)),
                  pl.BlockSpec((1, d), lambda i: (0, 0)),
                  pl.BlockSpec((d, LANES), lambda i: (0, 0), pipeline_mode=pl.Buffered(1))],
        out_specs=(row(d), row(d), row(LANES),
                   pl.BlockSpec((None, n_exp, tm), lambda i: (i // tpb, 0, i % tpb))),
        scratch_shapes=[pltpu.VMEM((tm, hw + mw), BF16)],
        compiler_params=_cparams("parallel"),
        name="outproj_norm2_router",
    )(o_hgrn, proj, og, o_mla, w_out_b, xf, mod6, g2, wr_pad)


BISECT_STEPS = 64


COMBINE_TILE = 256
COMBINE_WIN = 64


def _topk_kernel(la_ref, slot_se_ref, idx_ref, tab_ref, tri_scr, cum_scr, *, cap, n_exp):
    nrow, seq = la_ref.shape
    ep = slot_se_ref.shape[1]
    rows = 256
    for k in range(seq // rows):
        r = lax.broadcasted_iota(jnp.int32, (rows, seq), 0) + k * rows
        c = lax.broadcasted_iota(jnp.int32, (rows, seq), 1)
        tri_scr[k * rows:(k + 1) * rows, :] = jnp.where(r < c, 1.0, 0.0).astype(BF16)

    def count(mask):
        return jnp.sum(jnp.where(mask, 1.0, 0.0), axis=-1, keepdims=True)

    def body(_, lh):
        lo, hi = lh
        mid = 0.5 * (lo + hi)
        ok = count(la_ref[...] >= mid) >= cap
        return jnp.where(ok, mid, lo), jnp.where(ok, hi, mid)

    la = la_ref[...]
    lo0 = jnp.min(la, axis=-1, keepdims=True)
    lo, hi = lax.fori_loop(0, BISECT_STEPS, body, (lo0, jnp.ones_like(lo0)))
    above = la >= hi
    tie = (la >= lo) & (la < hi)
    need = cap - count(above)
    tri = tri_scr[...]
    rank = _dot(jnp.where(tie, 1.0, 0.0).astype(BF16), tri)
    sel = above | (tie & (rank < need))
    pos = _dot(jnp.where(sel, 1.0, 0.0).astype(BF16), tri)
    slot = jnp.where(sel, pos, -1.0)
    eye = _eye(n_exp, ep)
    for b in range(nrow // n_exp):
        slot_se_ref[b * seq:(b + 1) * seq, :] = _dot_tn(
            slot[b * n_exp:(b + 1) * n_exp, :].astype(BF16), eye)
    cum_scr[...] = pos + jnp.where(sel, 1.0, 0.0)
    lane = lax.broadcasted_iota(jnp.int32, (nrow, cap), 1)

    def slot_body(c, acc):
        cnt = count(cum_scr[...] <= lax.convert_element_type(c, F32))
        return jnp.where(lane == c, cnt, acc)

    idx = lax.fori_loop(0, cap, slot_body, jnp.zeros((nrow, cap), F32), unroll=4)
    idx_ref[...] = idx.astype(jnp.int32)
    tok = lax.broadcasted_iota(jnp.int32, (nrow, seq), 1)
    tlane = lax.broadcasted_iota(jnp.int32, tab_ref.shape, 1)
    tab = jnp.zeros(tab_ref.shape, F32)
    tile = min(seq, COMBINE_TILE)
    for k in range(seq // tile + 1):
        tab = jnp.where(tlane == k, count(sel & (tok < k * tile)), tab)
    tab_ref[...] = tab.astype(jnp.int32)


def _topk(lat, batch, seq, n_exp, cap):
    return pl.pallas_call(
        functools.partial(_topk_kernel, cap=cap, n_exp=n_exp),
        out_shape=(jax.ShapeDtypeStruct((batch * seq, LANES), F32),
                   jax.ShapeDtypeStruct((batch * n_exp, cap), jnp.int32),
                   jax.ShapeDtypeStruct((batch * n_exp, LANES), jnp.int32)),
        scratch_shapes=[pltpu.VMEM((seq, seq), BF16), pltpu.VMEM((batch * n_exp, seq), F32)],
        compiler_params=pltpu.CompilerParams(vmem_limit_bytes=VMEM_LIMIT),
        name="expert_choice_topk",
    )(lat.reshape(batch * n_exp, seq))


def _ffn_kernel(idx_ref, h2_hbm, wg_ref, wu_ref, wd_ref, ye_ref, xe_scr, hmid_scr, sem,
                *, nt, nd, tf):
    e = pl.program_id(0)
    s = pl.program_id(1)
    rows = xe_scr.shape[0]

    def start_gather(expert):
        base = expert * rows

        def body(k, carry):
            r0 = pl.multiple_of(k * TILE, TILE)
            for j in range(TILE):
                pltpu.make_async_copy(h2_hbm.at[pl.ds(idx_ref[base + r0 + j], 1), :],
                                      xe_scr.at[pl.ds(r0 + j, 1), :], sem.at[0]).start()
            return carry
        lax.fori_loop(0, rows // TILE, body, 0)

    @pl.when((e == 0) & (s == 0))
    def _():
        start_gather(0)

    @pl.when(s == 0)
    def _():
        pltpu.make_async_copy(h2_hbm.at[pl.ds(0, rows), :], xe_scr, sem.at[0]).wait()

    @pl.when(s < nt)
    def _():
        xe = xe_scr[...].astype(BF16)
        a = _dot(xe, wg_ref[...].astype(BF16))
        u = _dot(xe, wu_ref[...].astype(BF16))
        hmid_scr[s] = (_silu(a) * u).astype(BF16)

    def down_step(prefetch):
        per = rows // (nd * nt)
        y = None
        for k in range(nt):
            if prefetch:
                dst0 = (s - nt) * (per * nt) + k * per
                first = (e + 1) * rows + dst0
                for j in range(per):
                    pltpu.make_async_copy(h2_hbm.at[pl.ds(idx_ref[first + j], 1), :],
                                          xe_scr.at[pl.ds(dst0 + j, 1), :], sem.at[0]).start()
            part = _dot(hmid_scr[k], wd_ref[k * tf:(k + 1) * tf, :].astype(BF16))
            y = part if y is None else y + part
        ye_ref[...] = y.astype(BF16)

    more = e + 1 < pl.num_programs(0)

    @pl.when((s >= nt) & more)
    def _():
        down_step(True)

    @pl.when((s >= nt) & jnp.logical_not(more))
    def _():
        down_step(False)


def _ffn(idx, h2, w_gate, w_up, w_down):
    n_exp, rows = idx.shape
    idx = idx.reshape(n_exp * rows)
    d = h2.shape[1]
    ff = w_gate.shape[2]
    tf = min(ff, 512)
    tn = min(d, 1024)
    nt = ff // tf
    nd = d // tn
    assert rows % (nd * nt) == 0
    up = lambda e, s, idx: (e, 0, jnp.minimum(s, nt - 1))
    down = lambda e, s, idx: (e, 0, jnp.maximum(s - nt, 0))
    return pl.pallas_call(
        functools.partial(_ffn_kernel, nt=nt, nd=nd, tf=tf),
        out_shape=jax.ShapeDtypeStruct((n_exp, rows, d), BF16),
        grid_spec=pltpu.PrefetchScalarGridSpec(
            num_scalar_prefetch=1,
            grid=(n_exp, nt + nd),
            in_specs=[pl.BlockSpec(memory_space=pl.ANY),
                      pl.BlockSpec((None, d, tf), up),
                      pl.BlockSpec((None, d, tf), up),
                      pl.BlockSpec((None, ff, tn), down)],
            out_specs=pl.BlockSpec((None, rows, tn), down),
            scratch_shapes=[pltpu.VMEM((rows, d), F32),
                            pltpu.VMEM((nt, rows, tf), BF16),
                            pltpu.SemaphoreType.DMA((1,))]),
        compiler_params=_cparams("arbitrary", "arbitrary"),
        name="expert_swiglu",
    )(idx, h2, w_gate, w_up, w_down)


def _combine_kernel(tab_ref, slot_ref, aff_ref, ye_ref, x1_ref, mod_ref, out_ref, y_scr,
                    *, n_exp, cap, win):
    b = pl.program_id(0)
    t = pl.program_id(1)
    tt = x1_ref.shape[0]
    base = (b * (pl.num_programs(1) + 1) + t) * n_exp
    pack = 16
    starts = []
    short = None
    for e in range(n_exp):
        c0 = tab_ref[base + e]
        c1 = tab_ref[base + n_exp + e]
        a = jnp.minimum(c0 & ~(pack - 1), cap - win)
        ok = c1 - a <= win
        starts.append(a)
        short = ok if short is None else short & ok

    def finish(acc):
        out_ref[...] = x1_ref[...] + mod_ref[5:6, :] * acc

    @pl.when(short)
    def _():
        lane = lax.broadcasted_iota(jnp.int32, (tt, LANES), 1).astype(F32)
        per = LANES // win
        blocks = []
        for g in range(n_exp // per):
            blk = jnp.zeros((tt, LANES), F32)
            for j in range(per):
                e = g * per + j
                a = pl.multiple_of(starts[e], pack)
                y_scr[e * win:(e + 1) * win, :] = ye_ref[e, pl.ds(a, win), :]
                slot = slot_ref[:, e:e + 1]
                rel = jnp.where(slot >= 0.0, slot - a.astype(F32) + float(j * win), -1.0)
                blk = jnp.where(lane == rel, aff_ref[:, e:e + 1], blk)
            blocks.append(blk.astype(BF16))
        finish(_dot(jnp.concatenate(blocks, axis=1), y_scr[...]))

    @pl.when(jnp.logical_not(short))
    def _():
        cidx = lax.broadcasted_iota(jnp.int32, (tt, cap), 1).astype(F32)
        acc = jnp.zeros(x1_ref.shape, F32)
        for e in range(n_exp):
            onehot = jnp.where(cidx == slot_ref[:, e:e + 1], 1.0, 0.0).astype(BF16)
            acc = acc + aff_ref[:, e:e + 1] * _dot(onehot, ye_ref[e])
        finish(acc)


def _combine(tab, slot_se, aff, ye4, x1, mod6, seq, n_exp, cap):
    m, d = x1.shape
    ep = slot_se.shape[1]
    batch = m // seq
    tt = min(seq, COMBINE_TILE)
    tpb = seq // tt
    win = min(COMBINE_WIN, cap)
    assert LANES % win == 0 and n_exp % (LANES // win) == 0 and cap % 16 == 0
    return pl.pallas_call(
        functools.partial(_combine_kernel, n_exp=n_exp, cap=cap, win=win),
        out_shape=jax.ShapeDtypeStruct((m, d), F32),
        grid_spec=pltpu.PrefetchScalarGridSpec(
            num_scalar_prefetch=1,
            grid=(batch, tpb),
            in_specs=[pl.BlockSpec((tt, ep), lambda b, t, tab: (b * tpb + t, 0)),
                      pl.BlockSpec((tt, ep), lambda b, t, tab: (b * tpb + t, 0)),
                      pl.BlockSpec((n_exp, None, cap, d), lambda b, t, tab: (0, b, 0, 0)),
                      pl.BlockSpec((tt, d), lambda b, t, tab: (b * tpb + t, 0)),
                      pl.BlockSpec((None, 6, d), lambda b, t, tab: (b, 0, 0))],
            out_specs=pl.BlockSpec((tt, d), lambda b, t, tab: (b * tpb + t, 0)),
            scratch_shapes=[pltpu.VMEM((n_exp * win, d), BF16)]),
        compiler_params=_cparams("parallel", "arbitrary"),
        name="expert_combine",
    )(tab, slot_se, aff, ye4, x1, mod6)


def kernel(x, c, positions, w_ada, b_ada, norm1_g, w_in, lb_logits, hgrn_out_g, qa_norm_g, w_uq,
           kva_norm_g, w_ukv, q_head_g, k_head_g, w_out, norm2_g, w_router, w_gate, w_up, w_down):
    batch, seq, d = x.shape
    depth = w_ada.shape[0]
    m = batch * seq
    hk = lb_logits.shape[2]
    heads, dv = hgrn_out_g.shape[1], hgrn_out_g.shape[2]
    hw = heads * dv
    ql, kvl = qa_norm_g.shape[1], kva_norm_g.shape[1]
    qk_dim = q_head_g.shape[1]
    mh = w_uq.shape[2] // qk_dim
    d_in = w_in.shape[2]
    rope = d_in - (3 * hk + 2 * hw + ql + kvl)
    nope = qk_dim - rope
    vdim = w_ukv.shape[2] // mh - nope
    n_exp = w_router.shape[2]
    cap = EC_CAPACITY * seq // n_exp
    assert dv == LANES and hk == hw and nope == LANES and vdim == LANES and 2 * rope == LANES
    assert ql + kvl + rope <= hk and seq % GRP == 0 and n_exp <= LANES and cap % 8 == 0

    cq_off = 3 * hk + 2 * hw
    ckv_off = cq_off + ql
    kpe_off = ckv_off + kvl
    swap = jnp.concatenate([jnp.arange(rope // 2, rope), jnp.arange(0, rope // 2)])

    def both(v):
        return jnp.concatenate([v, v[..., swap]], axis=-1)

    cs = _rope_tables(positions, rope)
    c8 = jnp.pad(c, ((0, (-batch) % 8), (0, 0)))
    xf = x.reshape(m, d)
    for l in range(depth):
        mod6 = _ada(c8, w_ada[l], b_ada[l])[:batch].reshape(batch, 6, d)

        w_in_b = w_in[l]
        wk_b = both(w_in[l][:, kpe_off:kpe_off + rope]).astype(BF16)
        proj, fgate, kpe2 = _inproj(xf, mod6, norm1_g[l].reshape(1, d), w_in_b, wk_b, lb_logits,
                                   seq, hk, l)

        o_hgrn = _hgrn(proj, fgate, batch, seq, heads, hk)

        wq = w_uq[l].reshape(ql, mh, qk_dim)
        wq_all = jnp.concatenate([wq[..., :nope], both(wq[..., nope:])], axis=-1)
        qh, kh, vh = _mla_proj(
            proj, kpe2, cs, qa_norm_g[l].reshape(1, ql), kva_norm_g[l].reshape(1, kvl),
            wq_all.reshape(ql, mh * 2 * LANES).astype(BF16), w_ukv[l].astype(BF16),
            q_head_g[l][:nope].reshape(1, nope), both(q_head_g[l][nope:]).reshape(1, 2 * rope),
            k_head_g[l][:nope].reshape(1, nope), both(k_head_g[l][nope:]).reshape(1, 2 * rope),
            batch, seq, mh, cq_off, ckv_off, qk_dim, rope)
        o_mla = _attention(qh, kh, vh)

        wr_pad = jnp.pad(w_router[l], ((0, 0), (0, LANES - n_exp)))
        x1, h2, aff, lat = _outproj(o_hgrn, proj, hgrn_out_g[l].reshape(1, hw), o_mla,
                                    w_out[l].astype(BF16), xf, mod6, norm2_g[l].reshape(1, d),
                                    wr_pad, seq, heads, hk, n_exp)

        slot_se, idx, tab = _topk(lat, batch, seq, n_exp, cap)
        ntile = seq // min(seq, COMBINE_TILE)
        tab = tab[:, :ntile + 1].reshape(batch, n_exp, ntile + 1).transpose(0, 2, 1).reshape(-1)
        rows = idx.reshape(batch, n_exp, cap) + (jnp.arange(batch, dtype=jnp.int32) * seq)[:, None, None]
        rows = rows.transpose(1, 0, 2).reshape(n_exp, batch * cap)
        ye = _ffn(rows, h2, w_gate[l], w_up[l], w_down[l])
        xf = _combine(tab, slot_se, aff, ye.reshape(n_exp, batch, cap, d), x1, mod6, seq, n_exp, cap)
    return xf.reshape(batch, seq, d)
```

```python
import functools
import math

import jax
import jax.numpy as jnp
from jax import lax
from jax.experimental import pallas as pl
from jax.experimental.pallas import tpu as pltpu

F32 = jnp.float32
BF16 = jnp.bfloat16
EPS = 1e-6
ROPE_BASE = 10000.0
LOG2E = math.log2(math.e)
EC_CAPACITY = 2
LANES = 128
TILE = 8
GRP = 128
VMEM_LIMIT = 56 * 1024 * 1024


def _cparams(*sem):
    return pltpu.CompilerParams(dimension_semantics=sem, vmem_limit_bytes=VMEM_LIMIT)


def _dot(a, b):
    return jnp.dot(a, b, preferred_element_type=F32)


def _dot_nt(a, b):
    return lax.dot_general(a, b, (((1,), (1,)), ((), ())), preferred_element_type=F32)


def _dot_tn(a, b):
    return lax.dot_general(a, b, (((0,), (0,)), ((), ())), preferred_element_type=F32)


def _split2(a):
    hi = a.astype(BF16)
    lo = (a - hi.astype(F32)).astype(BF16)
    return hi, lo


def _split3(a):
    p1 = a.astype(BF16)
    r1 = a - p1.astype(F32)
    p2 = r1.astype(BF16)
    p3 = (r1 - p2.astype(F32)).astype(BF16)
    return p1, p2, p3


def _dot_hi(a, b):
    ah, al = _split2(a)
    bh, bl = _split2(b)
    return _dot(ah, bh) + (_dot(ah, bl) + _dot(al, bh))


def _silu(x):
    return x * jax.nn.sigmoid(x)


def _rope_kernel(pos_ref, cs_ref, *, half):
    pos = pos_ref[...].astype(F32)
    lane = lax.broadcasted_iota(jnp.int32, (1, 4 * half), 1)
    j = (lane & (half - 1)).astype(F32)
    inv_freq = jnp.exp(j * (-2.0 * math.log(ROPE_BASE) / (2 * half)))
    ang = pos * inv_freq
    c = jnp.cos(ang)
    s = jnp.sin(ang)
    cs_ref[...] = jnp.where(lane < 2 * half, c, jnp.where(lane < 3 * half, -s, s))


def _rope_tables(positions, rope):
    m = positions.size
    tm = min(m, 1024)
    half = rope // 2
    return pl.pallas_call(
        functools.partial(_rope_kernel, half=half),
        out_shape=jax.ShapeDtypeStruct((m, 2 * rope), F32),
        grid=(m // tm,),
        in_specs=[pl.BlockSpec((tm, 1), lambda i: (i, 0))],
        out_specs=pl.BlockSpec((tm, 2 * rope), lambda i: (i, 0)),
        compiler_params=_cparams("parallel"),
        name="rope_tables",
    )(positions.reshape(m, 1))


def _ada_kernel(c_ref, w_ref, b_ref, o_ref):
    part = _dot_hi(_silu(c_ref[...]), w_ref[...])

    @pl.when(pl.program_id(0) == 0)
    def _():
        o_ref[...] = part + b_ref[...]

    @pl.when(pl.program_id(0) > 0)
    def _():
        o_ref[...] += part


def _ada(c8, w, b):
    d, n = w.shape
    tk = min(d, 128)
    return pl.pallas_call(
        _ada_kernel,
        out_shape=jax.ShapeDtypeStruct((c8.shape[0], n), F32),
        grid=(d // tk,),
        in_specs=[pl.BlockSpec((c8.shape[0], tk), lambda k: (0, k)),
                  pl.BlockSpec((tk, n), lambda k: (k, 0)),
                  pl.BlockSpec((1, n), lambda k: (0, 0))],
        out_specs=pl.BlockSpec((c8.shape[0], n), lambda k: (0, 0)),
        compiler_params=_cparams("arbitrary"),
        name="ada_mod",
    )(c8, w, b.reshape(1, n))


def _inproj_kernel(x_ref, mod_ref, g_ref, w_ref, wk_ref, lbl_ref,
                   proj_ref, fgate_ref, kpe_ref, h_scr, *, layer):
    j = pl.program_id(1)

    @pl.when(j == 0)
    def _():
        x = x_ref[...]
        r = lax.rsqrt(jnp.mean(x * x, axis=-1, keepdims=True) + EPS)
        h = x * r * g_ref[...] * (1.0 + mod_ref[1:2, :]) + mod_ref[0:1, :]
        hb = h.astype(BF16)
        h_scr[...] = hb
        kpe_ref[...] = _dot(hb, wk_ref[...])

    acc = _dot(h_scr[...], w_ref[...])
    proj_ref[...] = acc.astype(BF16)

    @pl.when((j == 1) | (j == 2))
    def _():
        lg = lbl_ref[...]
        e = jnp.exp(lg - jnp.max(lg, axis=0, keepdims=True))
        lb = jnp.sum(e[:layer + 1], axis=0, keepdims=True) / jnp.sum(e, axis=0, keepdims=True)
        fgate_ref[...] = lb + (1.0 - lb) * jax.nn.sigmoid(acc)


def _inproj(xf, mod6, g1, w_in_b, wk_b, lb_logits, seq, hk, layer):
    m, d = xf.shape
    d_in = w_in_b.shape[1]
    tm = min(seq, 1024)
    tpb = seq // tm
    tn = hk
    nj = pl.cdiv(d_in, tn)
    nl = lb_logits.shape[1]
    fdir = lambda j: jnp.clip(j - 1, 0, 1)
    return pl.pallas_call(
        functools.partial(_inproj_kernel, layer=layer),
        out_shape=(jax.ShapeDtypeStruct((m, d_in), BF16),
                   jax.ShapeDtypeStruct((m, 2 * hk), F32),
                   jax.ShapeDtypeStruct((m, LANES), F32)),
        grid=(m // tm, nj),
        in_specs=[pl.BlockSpec((tm, d), lambda i, j: (i, 0)),
                  pl.BlockSpec((None, 6, d), lambda i, j: (i // tpb, 0, 0)),
                  pl.BlockSpec((1, d), lambda i, j: (0, 0)),
                  pl.BlockSpec((d, tn), lambda i, j: (0, j)),
                  pl.BlockSpec((d, LANES), lambda i, j: (0, 0)),
                  pl.BlockSpec((None, nl, tn), lambda i, j: (fdir(j), 0, 0))],
        out_specs=(pl.BlockSpec((tm, tn), lambda i, j: (i, j)),
                   pl.BlockSpec((tm, tn), lambda i, j: (i, fdir(j))),
                   pl.BlockSpec((tm, LANES), lambda i, j: (i, 0))),
        scratch_shapes=[pltpu.VMEM((tm, d), BF16)],
        compiler_params=_cparams("parallel", "arbitrary"),
        name="norm1_inproj",
    )(xf, mod6, g1, w_in_b, wk_b, lb_logits)


def _tile_scan(g, d):
    rin = lax.broadcasted_iota(jnp.int32, g.shape, 0) & (TILE - 1)
    b = g
    step = 1
    while step < TILE:
        if d == 0:
            b = b + jnp.where(rin >= step, pltpu.roll(b, step, 0), 0.0)
        else:
            b = b + jnp.where(rin < TILE - step, pltpu.roll(b, GRP - step, 0), 0.0)
        step *= 2
    return b


def _group_cumsum(g, d):
    b = _tile_scan(g, d)
    ntile = GRP // TILE
    order = range(ntile) if d == 0 else range(ntile - 1, -1, -1)
    edge = TILE - 1 if d == 0 else 0
    out = [None] * ntile
    carry = None
    for i in order:
        t = b[i * TILE:(i + 1) * TILE]
        out[i] = t if carry is None else t + carry
        tot = t[edge:edge + 1]
        carry = tot if carry is None else carry + tot
    return jnp.concatenate(out, axis=0)


def _boundary(b, h, d):
    idx = h - 1 if d == 0 else h
    if 2 * h >= TILE:
        b3 = b.reshape(GRP // (2 * h), 2 * h, b.shape[1])
        return jnp.broadcast_to(b3[:, idx:idx + 1, :], b3.shape).reshape(b.shape)
    p = lax.broadcasted_iota(jnp.int32, b.shape, 0) & (2 * h - 1)
    out = b
    for pos in range(2 * h):
        shift = pos - idx
        if shift != 0:
            out = jnp.where(p == pos, pltpu.roll(b, shift % GRP, 0), out)
    return out


def _hgrn_kernel(q_ref, v_ref, gf_ref, gb_ref, o_ref, lv_scr, sg_scr, st_scr):
    seq = q_ref.shape[0]
    ngrp = seq // GRP
    nlev = GRP.bit_length()
    g_refs = (gf_ref, gb_ref)
    assert ngrp % 2 == 0

    r = lax.broadcasted_iota(jnp.int32, (GRP, GRP), 0)
    c = lax.broadcasted_iota(jnp.int32, (GRP, GRP), 1)
    lev = jnp.zeros((GRP, GRP), jnp.int32)
    for j in range(nlev - 1):
        lev = lev + jnp.where((r >> j) != (c >> j), 1, 0)
    lv_scr[0] = jnp.where(c <= r, lev, -1)
    lv_scr[1] = jnp.where(c >= r, lev, -1)
    st_scr[...] = jnp.zeros_like(st_scr)
    rr = lax.broadcasted_iota(jnp.int32, (GRP, LANES), 0)
    for l in range(1, nlev):
        late = (rr & (1 << (l - 1))) != 0
        sg_scr[0, l - 1] = jnp.where(late, 1.0, -1.0)
        sg_scr[1, l - 1] = jnp.where(late, -1.0, 1.0)

    def body(i, carry, first):
        for d in (0, 1):
            grp = i if d == 0 else ngrp - 1 - i
            r0 = pl.multiple_of(grp * GRP, GRP)
            f = g_refs[d][pl.ds(r0, GRP), :]
            g = jnp.log2(f)
            qb = q_ref[pl.ds(r0, GRP), :]
            vb = v_ref[pl.ds(r0, GRP), :]
            qf = qb.astype(F32)
            kk = 1.0 - f
            kb = kk.astype(BF16)
            b = _group_cumsum(g, d)
            edge = GRP - 1 if d == 0 else 0
            tot = b[edge:edge + 1]
            lv = lv_scr[d]
            att = jnp.where(lv == 0, _dot_nt(qb, kb), 0.0)
            for l in range(1, nlev):
                x = jnp.exp2((b - _boundary(b, 1 << (l - 1), d)) * sg_scr[d, l - 1]).astype(BF16)
                att = jnp.where(lv == l, _dot_nt(qb * x, kb * x), att)
            st = st_scr[d]
            o = _dot(att.astype(BF16), vb) + _dot_nt((qf * jnp.exp2(b)).astype(BF16), st.astype(BF16))
            if first:
                o_ref[pl.ds(r0, GRP), :] = o
            else:
                o_ref[pl.ds(r0, GRP), :] += o
            st_scr[d] = st * jnp.exp2(tot) + _dot_tn(vb, (kk * jnp.exp2(tot - b)).astype(BF16))
        return carry

    half = ngrp // 2
    lax.fori_loop(0, half, functools.partial(body, first=True), 0, unroll=2)
    lax.fori_loop(half, ngrp, functools.partial(body, first=False), 0, unroll=2)


def _hgrn(proj, fgate, batch, seq, heads, hk):
    m = proj.shape[0]
    nh = hk // LANES
    vcol = 3 * nh
    blk = lambda off: pl.BlockSpec((seq, LANES), lambda b, h: (b, off + h))
    out = jax.ShapeDtypeStruct((m, hk), F32)
    return pl.pallas_call(
        _hgrn_kernel,
        out_shape=out,
        grid=(batch, heads),
        in_specs=[blk(0), blk(vcol), blk(0), blk(nh)],
        out_specs=blk(0),
        scratch_shapes=[pltpu.VMEM((2, GRP, GRP), jnp.int32),
                        pltpu.VMEM((2, GRP.bit_length() - 1, GRP, LANES), F32),
                        pltpu.VMEM((2, LANES, LANES), F32)],
        compiler_params=_cparams("parallel", "parallel"),
        name="hgrn2_scan",
    )(proj, proj, fgate, fgate)


def _mla_proj_kernel(cq_ref, ckv_ref, kpe_ref, cs_ref, qag_ref, kvag_ref, wq_ref, wkv_ref,
                     qgn_ref, qgr_ref, kgn_ref, kgr_ref, q_out, k_out, v_out,
                     *, scale, qk_dim, rope, heads):
    cq = cq_ref[...].astype(F32)
    a = (cq * lax.rsqrt(jnp.mean(cq * cq, axis=-1, keepdims=True) + EPS) * qag_ref[...]).astype(BF16)
    ckv = ckv_ref[...].astype(F32)
    c = (ckv * lax.rsqrt(jnp.mean(ckv * ckv, axis=-1, keepdims=True) + EPS)
         * kvag_ref[...]).astype(BF16)
    qall = _dot(a, wq_ref[...])
    kvall = _dot(c, wkv_ref[...])

    cs = cs_ref[...]
    lane = lax.broadcasted_iota(jnp.int32, cs.shape, 1)
    lo = lane < rope

    def rope_sumsq(rr):
        return jnp.sum(jnp.where(lo, rr * rr, 0.0), axis=-1, keepdims=True)

    def rotate(rr, gr):
        y = rr * gr * cs
        return y + pltpu.roll(y, rope, 1)

    kpe = kpe_ref[...]
    k_ss = rope_sumsq(kpe)
    k_rot = rotate(kpe, kgr_ref[...])
    for h in range(heads):
        base = 2 * LANES * h
        qn = qall[:, base:base + LANES]
        qr = qall[:, base + LANES:base + 2 * LANES]
        rq = lax.rsqrt((jnp.sum(qn * qn, axis=-1, keepdims=True) + rope_sumsq(qr)) / qk_dim + EPS) * scale
        q_out[h, :, :LANES] = (qn * qgn_ref[...] * rq).astype(BF16)
        q_out[h, :, LANES:] = jnp.where(lo, rotate(qr, qgr_ref[...]) * rq, 0.0).astype(BF16)
        kn = kvall[:, base:base + LANES]
        rk = lax.rsqrt((jnp.sum(kn * kn, axis=-1, keepdims=True) + k_ss) / qk_dim + EPS)
        k_out[h, :, :LANES] = (kn * kgn_ref[...] * rk).astype(BF16)
        k_out[h, :, LANES:] = jnp.where(lo, k_rot * rk, 0.0).astype(BF16)
        v_out[h] = kvall[:, base + LANES:base + 2 * LANES].astype(BF16)


def _mla_proj(proj, kpe2, cs, qag, kvag, wq_all, wkv_all, qgn, qgr, kgn, kgr,
              batch, seq, heads, cq_off, ckv_off, qk_dim, rope):
    m = proj.shape[0]
    ql, kvl = wq_all.shape[0], wkv_all.shape[0]
    tm = min(seq, 256)
    tpb = seq // tm
    assert cq_off % ql == 0 and ckv_off % kvl == 0
    vec = lambda n: pl.BlockSpec((1, n), lambda i: (0, 0))
    full = lambda w: pl.BlockSpec(w.shape, lambda i: (0, 0))
    hspec = lambda n: pl.BlockSpec((None, heads, tm, n), lambda i: (i // tpb, 0, i % tpb, 0))
    scale = qk_dim ** -0.5 * LOG2E
    return pl.pallas_call(
        functools.partial(_mla_proj_kernel, scale=scale, qk_dim=float(qk_dim), rope=rope, heads=heads),
        out_shape=(jax.ShapeDtypeStruct((batch, heads, seq, 2 * LANES), BF16),
                   jax.ShapeDtypeStruct((batch, heads, seq, 2 * LANES), BF16),
                   jax.ShapeDtypeStruct((batch, heads, seq, LANES), BF16)),
        grid=(m // tm,),
        in_specs=[pl.BlockSpec((tm, ql), lambda i: (i, cq_off // ql)),
                  pl.BlockSpec((tm, kvl), lambda i: (i, ckv_off // kvl)),
                  pl.BlockSpec((tm, LANES), lambda i: (i, 0)),
                  pl.BlockSpec((tm, LANES), lambda i: (i, 0)),
                  vec(ql), vec(kvl), full(wq_all), full(wkv_all),
                  vec(LANES), vec(LANES), vec(LANES), vec(LANES)],
        out_specs=(hspec(2 * LANES), hspec(2 * LANES), hspec(LANES)),
        compiler_params=_cparams("parallel"),
        name="mla_head_proj",
    )(proj, proj, kpe2, cs, qag, kvag, wq_all, wkv_all, qgn, qgr, kgn, kgr)


ATTN_KEYS = 512


def _attn_kernel(q_ref, k_ref, v_ref, o_ref):
    q = q_ref[...]
    seq = k_ref.shape[0]
    kc = min(ATTN_KEYS, seq)
    m = l = acc = None
    for c in range(seq // kc):
        rows = slice(c * kc, (c + 1) * kc)
        s = _dot_nt(q, k_ref[rows, :])
        mc = jnp.max(s, axis=-1, keepdims=True)
        if c == 0:
            m = mc
            p = jnp.exp2(s - m)
            l = jnp.sum(p, axis=-1, keepdims=True)
            acc = _dot(p.astype(BF16), v_ref[rows, :])
        else:
            m_new = jnp.maximum(m, mc)
            alpha = jnp.exp2(m - m_new)
            p = jnp.exp2(s - m_new)
            l = l * alpha + jnp.sum(p, axis=-1, keepdims=True)
            acc = acc * alpha + _dot(p.astype(BF16), v_ref[rows, :])
            m = m_new
    o_ref[...] = (acc / l).astype(BF16)


def _attention(qh, kh, vh):
    batch, mh, seq, dq = qh.shape
    dv = vh.shape[-1]
    tq = min(seq, 2048)
    nq = seq // tq
    return pl.pallas_call(
        _attn_kernel,
        out_shape=jax.ShapeDtypeStruct((batch * seq, mh * dv), BF16),
        grid=(batch, mh, nq),
        in_specs=[pl.BlockSpec((None, None, tq, dq), lambda b, h, i: (b, h, i, 0)),
                  pl.BlockSpec((None, None, seq, dq), lambda b, h, i: (b, h, 0, 0)),
                  pl.BlockSpec((None, None, seq, dv), lambda b, h, i: (b, h, 0, 0))],
        out_specs=pl.BlockSpec((tq, dv), lambda b, h, i: (b * nq + i, h)),
        compiler_params=_cparams("parallel", "parallel", "arbitrary"),
        name="mla_attention",
    )(qh, kh, vh)


def _eye(rows, cols):
    r = lax.broadcasted_iota(jnp.int32, (rows, cols), 0)
    c = lax.broadcasted_iota(jnp.int32, (rows, cols), 1)
    return jnp.where(r == c, 1.0, 0.0).astype(BF16)


def _outproj_kernel(o_ref, hg_ref, og_ref, om_ref, w_ref, x_ref, mod_ref, g2_ref, wr_ref,
                    x1_ref, h2_ref, aff_ref, lat_ref, mix_scr, *, heads, n_exp):
    hw = o_ref.shape[1]
    o = o_ref[...]
    gate = _silu(hg_ref[...].astype(F32))
    for h in range(heads):
        sl = slice(h * LANES, (h + 1) * LANES)
        oh = o[:, sl]
        r = lax.rsqrt(jnp.mean(oh * oh, axis=-1, keepdims=True) + EPS)
        mix_scr[:, sl] = (oh * r * og_ref[:, sl] * gate[:, sl]).astype(BF16)
    mix_scr[:, hw:] = om_ref[...]
    x1 = x_ref[...] + mod_ref[2:3, :] * _dot(mix_scr[...], w_ref[...])
    x1_ref[...] = x1
    r2 = lax.rsqrt(jnp.mean(x1 * x1, axis=-1, keepdims=True) + EPS)
    h2 = x1 * r2 * g2_ref[...] * (1.0 + mod_ref[4:5, :]) + mod_ref[3:4, :]
    h2_ref[...] = h2
    logits = _dot_hi(h2, wr_ref[...])
    lane = lax.broadcasted_iota(jnp.int32, logits.shape, 1)
    logits = jnp.where(lane < n_exp, logits, -jnp.inf)
    z = logits - jnp.max(logits, axis=-1, keepdims=True)
    p = jnp.exp(z)
    sp = jnp.sum(p, axis=-1, keepdims=True)
    aff_ref[...] = p / sp
    la = jnp.where(lane < n_exp, z - jnp.log(sp), 0.0)
    eye = _eye(n_exp, la.shape[1])
    p1, p2, p3 = _split3(la)
    lat_ref[...] = (_dot_nt(eye, p1) + _dot_nt(eye, p2)) + _dot_nt(eye, p3)


def _outproj(o_hgrn, proj, og, o_mla, w_out_b, xf, mod6, g2, wr_pad, seq, heads, hk, n_exp):
    m, d = xf.shape
    hw = o_hgrn.shape[1]
    mw = o_mla.shape[1]
    tm = min(seq, 256)
    tpb = seq // tm
    gcol = (3 * hk + hw) // hw
    assert (3 * hk + hw) % hw == 0
    row = lambda n: pl.BlockSpec((tm, n), lambda i: (i, 0))
    return pl.pallas_call(
        functools.partial(_outproj_kernel, heads=heads, n_exp=n_exp),
        out_shape=(jax.ShapeDtypeStruct((m, d), F32),
                   jax.ShapeDtypeStruct((m, d), F32),
                   jax.ShapeDtypeStruct((m, LANES), F32),
                   jax.ShapeDtypeStruct((m // seq, n_exp, seq), F32)),
        grid=(m // tm,),
        in_specs=[row(hw),
                  pl.BlockSpec((tm, hw), lambda i: (i, gcol)),
                  pl.BlockSpec((1, hw), lambda i: (0, 0)),
                  row(mw),
                  pl.BlockSpec((hw + mw, d), lambda i: (0, 0), pipeline_mode=pl.Buffered(1)),
                  row(d),
                  pl.BlockSpec((None, 6, d), lambda i: (i // tpb, 0, 0)),
                  pl.BlockSpec((1, d), lambda i: (0, 0)),
                  pl.BlockSpec((d, LANES), lambda i: (0, 0), pipeline_mode=pl.Buffered(1))],
        out_specs=(row(d), row(d), row(LANES),
                   pl.BlockSpec((None, n_exp, tm), lambda i: (i // tpb, 0, i % tpb))),
        scratch_shapes=[pltpu.VMEM((tm, hw + mw), BF16)],
        compiler_params=_cparams("parallel"),
        name="outproj_norm2_router",
    )(o_hgrn, proj, og, o_mla, w_out_b, xf, mod6, g2, wr_pad)


BISECT_STEPS = 64


COMBINE_TILE = 256
COMBINE_WIN = 64


def _topk_kernel(la_ref, slot_se_ref, idx_ref, tab_ref, tri_scr, cum_scr, *, cap, n_exp):
    nrow, seq = la_ref.shape
    ep = slot_se_ref.shape[1]
    rows = 256
    for k in range(seq // rows):
        r = lax.broadcasted_iota(jnp.int32, (rows, seq), 0) + k * rows
        c = lax.broadcasted_iota(jnp.int32, (rows, seq), 1)
        tri_scr[k * rows:(k + 1) * rows, :] = jnp.where(r < c, 1.0, 0.0).astype(BF16)

    def count(mask):
        return jnp.sum(jnp.where(mask, 1.0, 0.0), axis=-1, keepdims=True)

    def body(_, lh):
        lo, hi = lh
        mid = 0.5 * (lo + hi)
        ok = count(la_ref[...] >= mid) >= cap
        return jnp.where(ok, mid, lo), jnp.where(ok, hi, mid)

    la = la_ref[...]
    lo0 = jnp.min(la, axis=-1, keepdims=True)
    lo, hi = lax.fori_loop(0, BISECT_STEPS, body, (lo0, jnp.ones_like(lo0)))
    above = la >= hi
    tie = (la >= lo) & (la < hi)
    need = cap - count(above)
    tri = tri_scr[...]
    rank = _dot(jnp.where(tie, 1.0, 0.0).astype(BF16), tri)
    sel = above | (tie & (rank < need))
    pos = _dot(jnp.where(sel, 1.0, 0.0).astype(BF16), tri)
    slot = jnp.where(sel, pos, -1.0)
    eye = _eye(n_exp, ep)
    for b in range(nrow // n_exp):
        slot_se_ref[b * seq:(b + 1) * seq, :] = _dot_tn(
            slot[b * n_exp:(b + 1) * n_exp, :].astype(BF16), eye)
    cum_scr[...] = pos + jnp.where(sel, 1.0, 0.0)
    lane = lax.broadcasted_iota(jnp.int32, (nrow, cap), 1)

    def slot_body(c, acc):
        cnt = count(cum_scr[...] <= lax.convert_element_type(c, F32))
        return jnp.where(lane == c, cnt, acc)

    idx = lax.fori_loop(0, cap, slot_body, jnp.zeros((nrow, cap), F32), unroll=4)
    idx_ref[...] = idx.astype(jnp.int32)
    tok = lax.broadcasted_iota(jnp.int32, (nrow, seq), 1)
    tlane = lax.broadcasted_iota(jnp.int32, tab_ref.shape, 1)
    tab = jnp.zeros(tab_ref.shape, F32)
    tile = min(seq, COMBINE_TILE)
    for k in range(seq // tile + 1):
        tab = jnp.where(tlane == k, count(sel & (tok < k * tile)), tab)
    tab_ref[...] = tab.astype(jnp.int32)


def _topk(lat, batch, seq, n_exp, cap):
    return pl.pallas_call(
        functools.partial(_topk_kernel, cap=cap, n_exp=n_exp),
        out_shape=(jax.ShapeDtypeStruct((batch * seq, LANES), F32),
                   jax.ShapeDtypeStruct((batch * n_exp, cap), jnp.int32),
                   jax.ShapeDtypeStruct((batch * n_exp, LANES), jnp.int32)),
        scratch_shapes=[pltpu.VMEM((seq, seq), BF16), pltpu.VMEM((batch * n_exp, seq), F32)],
        compiler_params=pltpu.CompilerParams(vmem_limit_bytes=VMEM_LIMIT),
        name="expert_choice_topk",
    )(lat.reshape(batch * n_exp, seq))


def _ffn_kernel(idx_ref, h2_hbm, wg_ref, wu_ref, wd_ref, ye_ref, xe_scr, hmid_scr, sem,
                *, nt, nd, tf):
    e = pl.program_id(0)
    s = pl.program_id(1)
    rows = xe_scr.shape[0]

    def start_gather(expert):
        base = expert * rows

        def body(k, carry):
            r0 = pl.multiple_of(k * TILE, TILE)
            for j in range(TILE):
                pltpu.make_async_copy(h2_hbm.at[pl.ds(idx_ref[base + r0 + j], 1), :],
                                      xe_scr.at[pl.ds(r0 + j, 1), :], sem.at[0]).start()
            return carry
        lax.fori_loop(0, rows // TILE, body, 0)

    @pl.when((e == 0) & (s == 0))
    def _():
        start_gather(0)

    @pl.when(s == 0)
    def _():
        pltpu.make_async_copy(h2_hbm.at[pl.ds(0, rows), :], xe_scr, sem.at[0]).wait()

    @pl.when(s < nt)
    def _():
        xe = xe_scr[...].astype(BF16)
        a = _dot(xe, wg_ref[...].astype(BF16))
        u = _dot(xe, wu_ref[...].astype(BF16))
        hmid_scr[s] = (_silu(a) * u).astype(BF16)

    def down_step(prefetch):
        per = rows // (nd * nt)
        y = None
        for k in range(nt):
            if prefetch:
                dst0 = (s - nt) * (per * nt) + k * per
                first = (e + 1) * rows + dst0
                for j in range(per):
                    pltpu.make_async_copy(h2_hbm.at[pl.ds(idx_ref[first + j], 1), :],
                                          xe_scr.at[pl.ds(dst0 + j, 1), :], sem.at[0]).start()
            part = _dot(hmid_scr[k], wd_ref[k * tf:(k + 1) * tf, :].astype(BF16))
            y = part if y is None else y + part
        ye_ref[...] = y.astype(BF16)

    more = e + 1 < pl.num_programs(0)

    @pl.when((s >= nt) & more)
    def _():
        down_step(True)

    @pl.when((s >= nt) & jnp.logical_not(more))
    def _():
        down_step(False)


def _ffn(idx, h2, w_gate, w_up, w_down):
    n_exp, rows = idx.shape
    idx = idx.reshape(n_exp * rows)
    d = h2.shape[1]
    ff = w_gate.shape[2]
    tf = min(ff, 512)
    tn = min(d, 1024)
    nt = ff // tf
    nd = d // tn
    assert rows % (nd * nt) == 0
    up = lambda e, s, idx: (e, 0, jnp.minimum(s, nt - 1))
    down = lambda e, s, idx: (e, 0, jnp.maximum(s - nt, 0))
    return pl.pallas_call(
        functools.partial(_ffn_kernel, nt=nt, nd=nd, tf=tf),
        out_shape=jax.ShapeDtypeStruct((n_exp, rows, d), BF16),
        grid_spec=pltpu.PrefetchScalarGridSpec(
            num_scalar_prefetch=1,
            grid=(n_exp, nt + nd),
            in_specs=[pl.BlockSpec(memory_space=pl.ANY),
                      pl.BlockSpec((None, d, tf), up),
                      pl.BlockSpec((None, d, tf), up),
                      pl.BlockSpec((None, ff, tn), down)],
            out_specs=pl.BlockSpec((None, rows, tn), down),
            scratch_shapes=[pltpu.VMEM((rows, d), F32),
                            pltpu.VMEM((nt, rows, tf), BF16),
                            pltpu.SemaphoreType.DMA((1,))]),
        compiler_params=_cparams("arbitrary", "arbitrary"),
        name="expert_swiglu",
    )(idx, h2, w_gate, w_up, w_down)


def _combine_kernel(tab_ref, slot_ref, aff_ref, ye_ref, x1_ref, mod_ref, out_ref, y_scr,
                    *, n_exp, cap, win):
    b = pl.program_id(0)
    t = pl.program_id(1)
    tt = x1_ref.shape[0]
    base = (b * (pl.num_programs(1) + 1) + t) * n_exp
    pack = 16
    starts = []
    short = None
    for e in range(n_exp):
        c0 = tab_ref[base + e]
        c1 = tab_ref[base + n_exp + e]
        a = jnp.minimum(c0 & ~(pack - 1), cap - win)
        ok = c1 - a <= win
        starts.append(a)
        short = ok if short is None else short & ok

    def finish(acc):
        out_ref[...] = x1_ref[...] + mod_ref[5:6, :] * acc

    @pl.when(short)
    def _():
        lane = lax.broadcasted_iota(jnp.int32, (tt, LANES), 1).astype(F32)
        per = LANES // win
        blocks = []
        for g in range(n_exp // per):
            blk = jnp.zeros((tt, LANES), F32)
            for j in range(per):
                e = g * per + j
                a = pl.multiple_of(starts[e], pack)
                y_scr[e * win:(e + 1) * win, :] = ye_ref[e, pl.ds(a, win), :]
                slot = slot_ref[:, e:e + 1]
                rel = jnp.where(slot >= 0.0, slot - a.astype(F32) + float(j * win), -1.0)
                blk = jnp.where(lane == rel, aff_ref[:, e:e + 1], blk)
            blocks.append(blk.astype(BF16))
        finish(_dot(jnp.concatenate(blocks, axis=1), y_scr[...]))

    @pl.when(jnp.logical_not(short))
    def _():
        cidx = lax.broadcasted_iota(jnp.int32, (tt, cap), 1).astype(F32)
        acc = jnp.zeros(x1_ref.shape, F32)
        for e in range(n_exp):
            onehot = jnp.where(cidx == slot_ref[:, e:e + 1], 1.0, 0.0).astype(BF16)
            acc = acc + aff_ref[:, e:e + 1] * _dot(onehot, ye_ref[e])
        finish(acc)


def _combine(tab, slot_se, aff, ye4, x1, mod6, seq, n_exp, cap):
    m, d = x1.shape
    ep = slot_se.shape[1]
    batch = m // seq
    tt = min(seq, COMBINE_TILE)
    tpb = seq // tt
    win = min(COMBINE_WIN, cap)
    assert LANES % win == 0 and n_exp % (LANES // win) == 0 and cap % 16 == 0
    return pl.pallas_call(
        functools.partial(_combine_kernel, n_exp=n_exp, cap=cap, win=win),
        out_shape=jax.ShapeDtypeStruct((m, d), F32),
        grid_spec=pltpu.PrefetchScalarGridSpec(
            num_scalar_prefetch=1,
            grid=(batch, tpb),
            in_specs=[pl.BlockSpec((tt, ep), lambda b, t, tab: (b * tpb + t, 0)),
                      pl.BlockSpec((tt, ep), lambda b, t, tab: (b * tpb + t, 0)),
                      pl.BlockSpec((n_exp, None, cap, d), lambda b, t, tab: (0, b, 0, 0)),
                      pl.BlockSpec((tt, d), lambda b, t, tab: (b * tpb + t, 0)),
                      pl.BlockSpec((None, 6, d), lambda b, t, tab: (b, 0, 0))],
            out_specs=pl.BlockSpec((tt, d), lambda b, t, tab: (b * tpb + t, 0)),
            scratch_shapes=[pltpu.VMEM((n_exp * win, d), BF16)]),
        compiler_params=_cparams("parallel", "arbitrary"),
        name="expert_combine",
    )(tab, slot_se, aff, ye4, x1, mod6)


def kernel(x, c, positions, w_ada, b_ada, norm1_g, w_in, lb_logits, hgrn_out_g, qa_norm_g, w_uq,
           kva_norm_g, w_ukv, q_head_g, k_head_g, w_out, norm2_g, w_router, w_gate, w_up, w_down):
    batch, seq, d = x.shape
    depth = w_ada.shape[0]
    m = batch * seq
    hk = lb_logits.shape[2]
    heads, dv = hgrn_out_g.shape[1], hgrn_out_g.shape[2]
    hw = heads * dv
    ql, kvl = qa_norm_g.shape[1], kva_norm_g.shape[1]
    qk_dim = q_head_g.shape[1]
    mh = w_uq.shape[2] // qk_dim
    d_in = w_in.shape[2]
    rope = d_in - (3 * hk + 2 * hw + ql + kvl)
    nope = qk_dim - rope
    vdim = w_ukv.shape[2] // mh - nope
    n_exp = w_router.shape[2]
    cap = EC_CAPACITY * seq // n_exp
    assert dv == LANES and hk == hw and nope == LANES and vdim == LANES and 2 * rope == LANES
    assert ql + kvl + rope <= hk and seq % GRP == 0 and n_exp <= LANES and cap % 8 == 0

    cq_off = 3 * hk + 2 * hw
    ckv_off = cq_off + ql
    kpe_off = ckv_off + kvl
    swap = jnp.concatenate([jnp.arange(rope // 2, rope), jnp.arange(0, rope // 2)])

    def both(v):
        return jnp.concatenate([v, v[..., swap]], axis=-1)

    cs = _rope_tables(positions, rope)
    c8 = jnp.pad(c, ((0, (-batch) % 8), (0, 0)))
    xf = x.reshape(m, d)
    for l in range(depth):
        mod6 = _ada(c8, w_ada[l], b_ada[l])[:batch].reshape(batch, 6, d)

        w_in_b = w_in[l].astype(BF16)
        wk_b = both(w_in[l][:, kpe_off:kpe_off + rope]).astype(BF16)
        proj, fgate, kpe2 = _inproj(xf, mod6, norm1_g[l].reshape(1, d), w_in_b, wk_b, lb_logits,
                                   seq, hk, l)

        o_hgrn = _hgrn(proj, fgate, batch, seq, heads, hk)

        wq = w_uq[l].reshape(ql, mh, qk_dim)
        wq_all = jnp.concatenate([wq[..., :nope], both(wq[..., nope:])], axis=-1)
        qh, kh, vh = _mla_proj(
            proj, kpe2, cs, qa_norm_g[l].reshape(1, ql), kva_norm_g[l].reshape(1, kvl),
            wq_all.reshape(ql, mh * 2 * LANES).astype(BF16), w_ukv[l].astype(BF16),
            q_head_g[l][:nope].reshape(1, nope), both(q_head_g[l][nope:]).reshape(1, 2 * rope),
            k_head_g[l][:nope].reshape(1, nope), both(k_head_g[l][nope:]).reshape(1, 2 * rope),
            batch, seq, mh, cq_off, ckv_off, qk_dim, rope)
        o_mla = _attention(qh, kh, vh)

        wr_pad = jnp.pad(w_router[l], ((0, 0), (0, LANES - n_exp)))
        x1, h2, aff, lat = _outproj(o_hgrn, proj, hgrn_out_g[l].reshape(1, hw), o_mla,
                                    w_out[l].astype(BF16), xf, mod6, norm2_g[l].reshape(1, d),
                                    wr_pad, seq, heads, hk, n_exp)

        slot_se, idx, tab = _topk(lat, batch, seq, n_exp, cap)
        ntile = seq // min(seq, COMBINE_TILE)
        tab = tab[:, :ntile + 1].reshape(batch, n_exp, ntile + 1).transpose(0, 2, 1).reshape(-1)
        rows = idx.reshape(batch, n_exp, cap) + (jnp.arange(batch, dtype=jnp.int32) * seq)[:, None, None]
        rows = rows.transpose(1, 0, 2).reshape(n_exp, batch * cap)
        ye = _ffn(rows, h2, w_gate[l], w_up[l], w_down[l])
        xf = _combine(tab, slot_se, aff, ye.reshape(n_exp, batch, cap, d), x1, mod6, seq, n_exp, cap)
    return xf.reshape(batch, seq, d)
```

```python
import functools
import math

import jax
import jax.numpy as jnp
from jax import lax
from jax.experimental import pallas as pl
from jax.experimental.pallas import tpu as pltpu

F32 = jnp.float32
BF16 = jnp.bfloat16
EPS = 1e-6
ROPE_BASE = 10000.0
LOG2E = math.log2(math.e)
EC_CAPACITY = 2
LANES = 128
TILE = 8
GRP = 128
VMEM_LIMIT = 60 * 1024 * 1024


def _cparams(*sem):
    return pltpu.CompilerParams(dimension_semantics=sem, vmem_limit_bytes=VMEM_LIMIT)


def _dot(a, b):
    return jnp.dot(a, b, preferred_element_type=F32)


def _dot_nt(a, b):
    return lax.dot_general(a, b, (((1,), (1,)), ((), ())), preferred_element_type=F32)


def _dot_tn(a, b):
    return lax.dot_general(a, b, (((0,), (0,)), ((), ())), preferred_element_type=F32)


def _split2(a):
    hi = a.astype(BF16)
    lo = (a - hi.astype(F32)).astype(BF16)
    return hi, lo


def _split3(a):
    p1 = a.astype(BF16)
    r1 = a - p1.astype(F32)
    p2 = r1.astype(BF16)
    p3 = (r1 - p2.astype(F32)).astype(BF16)
    return p1, p2, p3


def _dot_hi(a, b):
    ah, al = _split2(a)
    bh, bl = _split2(b)
    return _dot(ah, bh) + (_dot(ah, bl) + _dot(al, bh))


def _silu(x):
    return x * jax.nn.sigmoid(x)


def _rope_kernel(pos_ref, cs_ref, *, half):
    pos = pos_ref[...].astype(F32)
    lane = lax.broadcasted_iota(jnp.int32, (1, 4 * half), 1)
    j = (lane & (half - 1)).astype(F32)
    inv_freq = jnp.exp(j * (-2.0 * math.log(ROPE_BASE) / (2 * half)))
    ang = pos * inv_freq
    c = jnp.cos(ang)
    s = jnp.sin(ang)
    cs_ref[...] = jnp.where(lane < 2 * half, c, jnp.where(lane < 3 * half, -s, s))


def _rope_tables(positions, rope):
    m = positions.size
    tm = min(m, 1024)
    half = rope // 2
    return pl.pallas_call(
        functools.partial(_rope_kernel, half=half),
        out_shape=jax.ShapeDtypeStruct((m, 2 * rope), F32),
        grid=(m // tm,),
        in_specs=[pl.BlockSpec((tm, 1), lambda i: (i, 0))],
        out_specs=pl.BlockSpec((tm, 2 * rope), lambda i: (i, 0)),
        compiler_params=_cparams("parallel"),
        name="rope_tables",
    )(positions.reshape(m, 1))


def _ada_kernel(c_ref, w_ref, b_ref, o_ref):
    part = _dot_hi(_silu(c_ref[...]), w_ref[...])

    @pl.when(pl.program_id(0) == 0)
    def _():
        o_ref[...] = part + b_ref[...]

    @pl.when(pl.program_id(0) > 0)
    def _():
        o_ref[...] += part


def _ada(c8, w, b):
    d, n = w.shape
    tk = min(d, 128)
    return pl.pallas_call(
        _ada_kernel,
        out_shape=jax.ShapeDtypeStruct((c8.shape[0], n), F32),
        grid=(d // tk,),
        in_specs=[pl.BlockSpec((c8.shape[0], tk), lambda k: (0, k)),
                  pl.BlockSpec((tk, n), lambda k: (k, 0)),
                  pl.BlockSpec((1, n), lambda k: (0, 0))],
        out_specs=pl.BlockSpec((c8.shape[0], n), lambda k: (0, 0)),
        compiler_params=_cparams("arbitrary"),
        name="ada_mod",
    )(c8, w, b.reshape(1, n))


def _inproj_kernel(x_ref, mod_ref, g_ref, w_ref, wk_ref, lbl_ref,
                   proj_ref, fgate_ref, kpe_ref, h_scr, *, layer):
    j = pl.program_id(1)

    @pl.when(j == 0)
    def _():
        x = x_ref[...]
        r = lax.rsqrt(jnp.mean(x * x, axis=-1, keepdims=True) + EPS)
        h = x * r * g_ref[...] * (1.0 + mod_ref[1:2, :]) + mod_ref[0:1, :]
        hb = h.astype(BF16)
        h_scr[...] = hb
        kpe_ref[...] = _dot(hb, wk_ref[...])

    acc = _dot(h_scr[...], w_ref[...])
    proj_ref[...] = acc.astype(BF16)

    @pl.when((j == 1) | (j == 2))
    def _():
        lg = lbl_ref[...]
        e = jnp.exp(lg - jnp.max(lg, axis=0, keepdims=True))
        lb = jnp.sum(e[:layer + 1], axis=0, keepdims=True) / jnp.sum(e, axis=0, keepdims=True)
        fgate_ref[...] = lb + (1.0 - lb) * jax.nn.sigmoid(acc)


def _inproj(xf, mod6, g1, w_in_b, wk_b, lb_logits, seq, hk, layer):
    m, d = xf.shape
    d_in = w_in_b.shape[1]
    tm = min(seq, 1024)
    tpb = seq // tm
    tn = hk
    nj = pl.cdiv(d_in, tn)
    nl = lb_logits.shape[1]
    fdir = lambda j: jnp.clip(j - 1, 0, 1)
    return pl.pallas_call(
        functools.partial(_inproj_kernel, layer=layer),
        out_shape=(jax.ShapeDtypeStruct((m, d_in), BF16),
                   jax.ShapeDtypeStruct((m, 2 * hk), F32),
                   jax.ShapeDtypeStruct((m, LANES), F32)),
        grid=(m // tm, nj),
        in_specs=[pl.BlockSpec((tm, d), lambda i, j: (i, 0)),
                  pl.BlockSpec((None, 6, d), lambda i, j: (i // tpb, 0, 0)),
                  pl.BlockSpec((1, d), lambda i, j: (0, 0)),
                  pl.BlockSpec((d, tn), lambda i, j: (0, j)),
                  pl.BlockSpec((d, LANES), lambda i, j: (0, 0)),
                  pl.BlockSpec((None, nl, tn), lambda i, j: (fdir(j), 0, 0))],
        out_specs=(pl.BlockSpec((tm, tn), lambda i, j: (i, j)),
                   pl.BlockSpec((tm, tn), lambda i, j: (i, fdir(j))),
                   pl.BlockSpec((tm, LANES), lambda i, j: (i, 0))),
        scratch_shapes=[pltpu.VMEM((tm, d), BF16)],
        compiler_params=_cparams("parallel", "arbitrary"),
        name="norm1_inproj",
    )(xf, mod6, g1, w_in_b, wk_b, lb_logits)


def _tile_scan(g, d):
    rin = lax.broadcasted_iota(jnp.int32, g.shape, 0) & (TILE - 1)
    b = g
    step = 1
    while step < TILE:
        if d == 0:
            b = b + jnp.where(rin >= step, pltpu.roll(b, step, 0), 0.0)
        else:
            b = b + jnp.where(rin < TILE - step, pltpu.roll(b, GRP - step, 0), 0.0)
        step *= 2
    return b


def _group_cumsum(g, d):
    b = _tile_scan(g, d)
    ntile = GRP // TILE
    order = range(ntile) if d == 0 else range(ntile - 1, -1, -1)
    edge = TILE - 1 if d == 0 else 0
    out = [None] * ntile
    carry = None
    for i in order:
        t = b[i * TILE:(i + 1) * TILE]
        out[i] = t if carry is None else t + carry
        tot = t[edge:edge + 1]
        carry = tot if carry is None else carry + tot
    return jnp.concatenate(out, axis=0)


def _boundary(b, h, d):
    idx = h - 1 if d == 0 else h
    if 2 * h >= TILE:
        b3 = b.reshape(GRP // (2 * h), 2 * h, b.shape[1])
        return jnp.broadcast_to(b3[:, idx:idx + 1, :], b3.shape).reshape(b.shape)
    p = lax.broadcasted_iota(jnp.int32, b.shape, 0) & (2 * h - 1)
    out = b
    for pos in range(2 * h):
        shift = pos - idx
        if shift != 0:
            out = jnp.where(p == pos, pltpu.roll(b, shift % GRP, 0), out)
    return out


def _hgrn_kernel(q_ref, v_ref, gf_ref, gb_ref, o_ref, lv_scr, sg_scr, st_scr):
    seq = q_ref.shape[0]
    ngrp = seq // GRP
    nlev = GRP.bit_length()
    g_refs = (gf_ref, gb_ref)
    assert ngrp % 2 == 0

    r = lax.broadcasted_iota(jnp.int32, (GRP, GRP), 0)
    c = lax.broadcasted_iota(jnp.int32, (GRP, GRP), 1)
    lev = jnp.zeros((GRP, GRP), jnp.int32)
    for j in range(nlev - 1):
        lev = lev + jnp.where((r >> j) != (c >> j), 1, 0)
    lv_scr[0] = jnp.where(c <= r, lev, -1)
    lv_scr[1] = jnp.where(c >= r, lev, -1)
    st_scr[...] = jnp.zeros_like(st_scr)
    rr = lax.broadcasted_iota(jnp.int32, (GRP, LANES), 0)
    for l in range(1, nlev):
        late = (rr & (1 << (l - 1))) != 0
        sg_scr[0, l - 1] = jnp.where(late, 1.0, -1.0)
        sg_scr[1, l - 1] = jnp.where(late, -1.0, 1.0)

    def body(i, carry, first):
        for d in (0, 1):
            grp = i if d == 0 else ngrp - 1 - i
            r0 = pl.multiple_of(grp * GRP, GRP)
            f = g_refs[d][pl.ds(r0, GRP), :]
            g = jnp.log2(f)
            qb = q_ref[pl.ds(r0, GRP), :]
            vb = v_ref[pl.ds(r0, GRP), :]
            qf = qb.astype(F32)
            kk = 1.0 - f
            kb = kk.astype(BF16)
            b = _group_cumsum(g, d)
            edge = GRP - 1 if d == 0 else 0
            tot = b[edge:edge + 1]
            lv = lv_scr[d]
            att = jnp.where(lv == 0, _dot_nt(qb, kb), 0.0)
            for l in range(1, nlev):
                x = jnp.exp2((b - _boundary(b, 1 << (l - 1), d)) * sg_scr[d, l - 1]).astype(BF16)
                att = jnp.where(lv == l, _dot_nt(qb * x, kb * x), att)
            st = st_scr[d]
            o = _dot(att.astype(BF16), vb) + _dot_nt((qf * jnp.exp2(b)).astype(BF16), st.astype(BF16))
            if first:
                o_ref[pl.ds(r0, GRP), :] = o
            else:
                o_ref[pl.ds(r0, GRP), :] += o
            st_scr[d] = st * jnp.exp2(tot) + _dot_tn(vb, (kk * jnp.exp2(tot - b)).astype(BF16))
        return carry

    half = ngrp // 2
    lax.fori_loop(0, half, functools.partial(body, first=True), 0, unroll=2)
    lax.fori_loop(half, ngrp, functools.partial(body, first=False), 0, unroll=2)


def _hgrn(proj, fgate, batch, seq, heads, hk):
    m = proj.shape[0]
    nh = hk // LANES
    vcol = 3 * nh
    blk = lambda off: pl.BlockSpec((seq, LANES), lambda b, h: (b, off + h))
    out = jax.ShapeDtypeStruct((m, hk), F32)
    return pl.pallas_call(
        _hgrn_kernel,
        out_shape=out,
        grid=(batch, heads),
        in_specs=[blk(0), blk(vcol), blk(0), blk(nh)],
        out_specs=blk(0),
        scratch_shapes=[pltpu.VMEM((2, GRP, GRP), jnp.int32),
                        pltpu.VMEM((2, GRP.bit_length() - 1, GRP, LANES), F32),
                        pltpu.VMEM((2, LANES, LANES), F32)],
        compiler_params=_cparams("parallel", "parallel"),
        name="hgrn2_scan",
    )(proj, proj, fgate, fgate)


def _mla_proj_kernel(cq_ref, ckv_ref, kpe_ref, cs_ref, qag_ref, kvag_ref, wq_ref, wkv_ref,
                     qgn_ref, qgr_ref, kgn_ref, kgr_ref, q_out, k_out, v_out,
                     *, scale, qk_dim, rope, heads):
    cq = cq_ref[...].astype(F32)
    a = (cq * lax.rsqrt(jnp.mean(cq * cq, axis=-1, keepdims=True) + EPS) * qag_ref[...]).astype(BF16)
    ckv = ckv_ref[...].astype(F32)
    c = (ckv * lax.rsqrt(jnp.mean(ckv * ckv, axis=-1, keepdims=True) + EPS)
         * kvag_ref[...]).astype(BF16)
    qall = _dot(a, wq_ref[...])
    kvall = _dot(c, wkv_ref[...])

    cs = cs_ref[...]
    lane = lax.broadcasted_iota(jnp.int32, cs.shape, 1)
    lo = lane < rope

    def rope_sumsq(rr):
        return jnp.sum(jnp.where(lo, rr * rr, 0.0), axis=-1, keepdims=True)

    def rotate(rr, gr):
        y = rr * gr * cs
        return y + pltpu.roll(y, rope, 1)

    kpe = kpe_ref[...]
    k_ss = rope_sumsq(kpe)
    k_rot = rotate(kpe, kgr_ref[...])
    for h in range(heads):
        base = 2 * LANES * h
        qn = qall[:, base:base + LANES]
        qr = qall[:, base + LANES:base + 2 * LANES]
        rq = lax.rsqrt((jnp.sum(qn * qn, axis=-1, keepdims=True) + rope_sumsq(qr)) / qk_dim + EPS) * scale
        q_out[h, :, :LANES] = (qn * qgn_ref[...] * rq).astype(BF16)
        q_out[h, :, LANES:] = jnp.where(lo, rotate(qr, qgr_ref[...]) * rq, 0.0).astype(BF16)
        kn = kvall[:, base:base + LANES]
        rk = lax.rsqrt((jnp.sum(kn * kn, axis=-1, keepdims=True) + k_ss) / qk_dim + EPS)
        k_out[h, :, :LANES] = (kn * kgn_ref[...] * rk).astype(BF16)
        k_out[h, :, LANES:] = jnp.where(lo, k_rot * rk, 0.0).astype(BF16)
        v_out[h] = kvall[:, base + LANES:base + 2 * LANES].astype(BF16)


def _mla_proj(proj, kpe2, cs, qag, kvag, wq_all, wkv_all, qgn, qgr, kgn, kgr,
              batch, seq, heads, cq_off, ckv_off, qk_dim, rope):
    m = proj.shape[0]
    ql, kvl = wq_all.shape[0], wkv_all.shape[0]
    tm = min(seq, 256)
    tpb = seq // tm
    assert cq_off % ql == 0 and ckv_off % kvl == 0
    vec = lambda n: pl.BlockSpec((1, n), lambda i: (0, 0))
    full = lambda w: pl.BlockSpec(w.shape, lambda i: (0, 0))
    hspec = lambda n: pl.BlockSpec((None, heads, tm, n), lambda i: (i // tpb, 0, i % tpb, 0))
    scale = qk_dim ** -0.5 * LOG2E
    return pl.pallas_call(
        functools.partial(_mla_proj_kernel, scale=scale, qk_dim=float(qk_dim), rope=rope, heads=heads),
        out_shape=(jax.ShapeDtypeStruct((batch, heads, seq, 2 * LANES), BF16),
                   jax.ShapeDtypeStruct((batch, heads, seq, 2 * LANES), BF16),
                   jax.ShapeDtypeStruct((batch, heads, seq, LANES), BF16)),
        grid=(m // tm,),
        in_specs=[pl.BlockSpec((tm, ql), lambda i: (i, cq_off // ql)),
                  pl.BlockSpec((tm, kvl), lambda i: (i, ckv_off // kvl)),
                  pl.BlockSpec((tm, LANES), lambda i: (i, 0)),
                  pl.BlockSpec((tm, LANES), lambda i: (i, 0)),
                  vec(ql), vec(kvl), full(wq_all), full(wkv_all),
                  vec(LANES), vec(LANES), vec(LANES), vec(LANES)],
        out_specs=(hspec(2 * LANES), hspec(2 * LANES), hspec(LANES)),
        compiler_params=_cparams("parallel"),
        name="mla_head_proj",
    )(proj, proj, kpe2, cs, qag, kvag, wq_all, wkv_all, qgn, qgr, kgn, kgr)


ATTN_KEYS = 512


def _attn_kernel(q_ref, k_ref, v_ref, o_ref):
    q = q_ref[...]
    seq = k_ref.shape[0]
    kc = min(ATTN_KEYS, seq)
    m = l = acc = None
    for c in range(seq // kc):
        rows = slice(c * kc, (c + 1) * kc)
        s = _dot_nt(q, k_ref[rows, :])
        mc = jnp.max(s, axis=-1, keepdims=True)
        if c == 0:
            m = mc
            p = jnp.exp2(s - m)
            l = jnp.sum(p, axis=-1, keepdims=True)
            acc = _dot(p.astype(BF16), v_ref[rows, :])
        else:
            m_new = jnp.maximum(m, mc)
            alpha = jnp.exp2(m - m_new)
            p = jnp.exp2(s - m_new)
            l = l * alpha + jnp.sum(p, axis=-1, keepdims=True)
            acc = acc * alpha + _dot(p.astype(BF16), v_ref[rows, :])
            m = m_new
    o_ref[...] = (acc / l).astype(BF16)


def _attention(qh, kh, vh):
    batch, mh, seq, dq = qh.shape
    dv = vh.shape[-1]
    tq = min(seq, 2048)
    nq = seq // tq
    return pl.pallas_call(
        _attn_kernel,
        out_shape=jax.ShapeDtypeStruct((batch * seq, mh * dv), BF16),
        grid=(batch, mh, nq),
        in_specs=[pl.BlockSpec((None, None, tq, dq), lambda b, h, i: (b, h, i, 0)),
                  pl.BlockSpec((None, None, seq, dq), lambda b, h, i: (b, h, 0, 0)),
                  pl.BlockSpec((None, None, seq, dv), lambda b, h, i: (b, h, 0, 0))],
        out_specs=pl.BlockSpec((tq, dv), lambda b, h, i: (b * nq + i, h)),
        compiler_params=_cparams("parallel", "parallel", "arbitrary"),
        name="mla_attention",
    )(qh, kh, vh)


def _eye(rows, cols):
    r = lax.broadcasted_iota(jnp.int32, (rows, cols), 0)
    c = lax.broadcasted_iota(jnp.int32, (rows, cols), 1)
    return jnp.where(r == c, 1.0, 0.0).astype(BF16)


def _outproj_kernel(o_ref, hg_ref, og_ref, om_ref, w_ref, x_ref, mod_ref, g2_ref, wr_ref,
                    x1_ref, h2_ref, aff_ref, lat_ref, mix_scr, *, heads, n_exp):
    hw = o_ref.shape[1]
    o = o_ref[...]
    gate = _silu(hg_ref[...].astype(F32))
    for h in range(heads):
        sl = slice(h * LANES, (h + 1) * LANES)
        oh = o[:, sl]
        r = lax.rsqrt(jnp.mean(oh * oh, axis=-1, keepdims=True) + EPS)
        mix_scr[:, sl] = (oh * r * og_ref[:, sl] * gate[:, sl]).astype(BF16)
    mix_scr[:, hw:] = om_ref[...]
    x1 = x_ref[...] + mod_ref[2:3, :] * _dot(mix_scr[...], w_ref[...])
    x1_ref[...] = x1
    r2 = lax.rsqrt(jnp.mean(x1 * x1, axis=-1, keepdims=True) + EPS)
    h2 = x1 * r2 * g2_ref[...] * (1.0 + mod_ref[4:5, :]) + mod_ref[3:4, :]
    h2_ref[...] = h2
    logits = _dot_hi(h2, wr_ref[...])
    lane = lax.broadcasted_iota(jnp.int32, logits.shape, 1)
    logits = jnp.where(lane < n_exp, logits, -jnp.inf)
    z = logits - jnp.max(logits, axis=-1, keepdims=True)
    p = jnp.exp(z)
    sp = jnp.sum(p, axis=-1, keepdims=True)
    aff_ref[...] = p / sp
    la = jnp.where(lane < n_exp, z - jnp.log(sp), 0.0)
    eye = _eye(n_exp, la.shape[1])
    p1, p2, p3 = _split3(la)
    lat_ref[...] = (_dot_nt(eye, p1) + _dot_nt(eye, p2)) + _dot_nt(eye, p3)


def _outproj(o_hgrn, proj, og, o_mla, w_out_b, xf, mod6, g2, wr_pad, seq, heads, hk, n_exp):
    m, d = xf.shape
    hw = o_hgrn.shape[1]
    mw = o_mla.shape[1]
    tm = min(seq, 256)
    tpb = seq // tm
    gcol = (3 * hk + hw) // hw
    assert (3 * hk + hw) % hw == 0
    row = lambda n: pl.BlockSpec((tm, n), lambda i: (i, 0))
    return pl.pallas_call(
        functools.partial(_outproj_kernel, heads=heads, n_exp=n_exp),
        out_shape=(jax.ShapeDtypeStruct((m, d), F32),
                   jax.ShapeDtypeStruct((m, d), F32),
                   jax.ShapeDtypeStruct((m, LANES), F32),
                   jax.ShapeDtypeStruct((m // seq, n_exp, seq), F32)),
        grid=(m // tm,),
        in_specs=[row(hw),
                  pl.BlockSpec((tm, hw), lambda i: (i, gcol)),
                  pl.BlockSpec((1, hw), lambda i: (0, 0)),
                  row(mw),
                  pl.BlockSpec((hw + mw, d), lambda i: (0, 0), pipeline_mode=pl.Buffered(1)),
                  row(d),
                  pl.BlockSpec((None, 6, d), lambda i: (i // tpb, 0, 0)),
                  pl.BlockSpec((1, d), lambda i: (0, 0)),
                  pl.BlockSpec((d, LANES), lambda i: (0, 0), pipeline_mode=pl.Buffered(1))],
        out_specs=(row(d), row(d), row(LANES),
                   pl.BlockSpec((None, n_exp, tm), lambda i: (i // tpb, 0, i % tpb))),
        scratch_shapes=[pltpu.VMEM((tm, hw + mw), BF16)],
        compiler_params=_cparams("parallel"),
        name="outproj_norm2_router",
    )(o_hgrn, proj, og, o_mla, w_out_b, xf, mod6, g2, wr_pad)


BISECT_STEPS = 64


COMBINE_TILE = 256
COMBINE_WIN = 64


def _topk_kernel(la_ref, slot_se_ref, idx_ref, tab_ref, tri_scr, cum_scr, *, cap, n_exp):
    nrow, seq = la_ref.shape
    ep = slot_se_ref.shape[1]
    rows = 256
    for k in range(seq // rows):
        r = lax.broadcasted_iota(jnp.int32, (rows, seq), 0) + k * rows
        c = lax.broadcasted_iota(jnp.int32, (rows, seq), 1)
        tri_scr[k * rows:(k + 1) * rows, :] = jnp.where(r < c, 1.0, 0.0).astype(BF16)

    def count(mask):
        return jnp.sum(jnp.where(mask, 1.0, 0.0), axis=-1, keepdims=True)

    def body(_, lh):
        lo, hi = lh
        mid = 0.5 * (lo + hi)
        ok = count(la_ref[...] >= mid) >= cap
        return jnp.where(ok, mid, lo), jnp.where(ok, hi, mid)

    la = la_ref[...]
    lo0 = jnp.min(la, axis=-1, keepdims=True)
    lo, hi = lax.fori_loop(0, BISECT_STEPS, body, (lo0, jnp.ones_like(lo0)))
    above = la >= hi
    tie = (la >= lo) & (la < hi)
    need = cap - count(above)
    tri = tri_scr[...]
    rank = _dot(jnp.where(tie, 1.0, 0.0).astype(BF16), tri)
    sel = above | (tie & (rank < need))
    pos = _dot(jnp.where(sel, 1.0, 0.0).astype(BF16), tri)
    slot = jnp.where(sel, pos, -1.0)
    eye = _eye(n_exp, ep)
    for b in range(nrow // n_exp):
        slot_se_ref[b * seq:(b + 1) * seq, :] = _dot_tn(
            slot[b * n_exp:(b + 1) * n_exp, :].astype(BF16), eye)
    cum_scr[...] = pos + jnp.where(sel, 1.0, 0.0)
    lane = lax.broadcasted_iota(jnp.int32, (nrow, cap), 1)

    def slot_body(c, acc):
        cnt = count(cum_scr[...] <= lax.convert_element_type(c, F32))
        return jnp.where(lane == c, cnt, acc)

    idx = lax.fori_loop(0, cap, slot_body, jnp.zeros((nrow, cap), F32), unroll=4)
    idx_ref[...] = idx.astype(jnp.int32)
    tok = lax.broadcasted_iota(jnp.int32, (nrow, seq), 1)
    tlane = lax.broadcasted_iota(jnp.int32, tab_ref.shape, 1)
    tab = jnp.zeros(tab_ref.shape, F32)
    tile = min(seq, COMBINE_TILE)
    for k in range(seq // tile + 1):
        tab = jnp.where(tlane == k, count(sel & (tok < k * tile)), tab)
    tab_ref[...] = tab.astype(jnp.int32)


def _topk(lat, batch, seq, n_exp, cap):
    return pl.pallas_call(
        functools.partial(_topk_kernel, cap=cap, n_exp=n_exp),
        out_shape=(jax.ShapeDtypeStruct((batch * seq, LANES), F32),
                   jax.ShapeDtypeStruct((batch * n_exp, cap), jnp.int32),
                   jax.ShapeDtypeStruct((batch * n_exp, LANES), jnp.int32)),
        scratch_shapes=[pltpu.VMEM((seq, seq), BF16), pltpu.VMEM((batch * n_exp, seq), F32)],
        compiler_params=pltpu.CompilerParams(vmem_limit_bytes=VMEM_LIMIT),
        name="expert_choice_topk",
    )(lat.reshape(batch * n_exp, seq))


def _ffn_kernel(idx_ref, h2_hbm, wg_ref, wu_ref, wd_ref, ye_ref, xe_scr, xb_scr, hmid_scr, sem,
                *, nt, nd, tf):
    e = pl.program_id(0)
    s = pl.program_id(1)
    rows = xe_scr.shape[0]

    def start_gather(expert):
        base = expert * rows

        def body(k, carry):
            r0 = pl.multiple_of(k * TILE, TILE)
            for j in range(TILE):
                pltpu.make_async_copy(h2_hbm.at[pl.ds(idx_ref[base + r0 + j], 1), :],
                                      xe_scr.at[pl.ds(r0 + j, 1), :], sem.at[0]).start()
            return carry
        lax.fori_loop(0, rows // TILE, body, 0)

    @pl.when((e == 0) & (s == 0))
    def _():
        start_gather(0)

    @pl.when(s == 0)
    def _():
        pltpu.make_async_copy(h2_hbm.at[pl.ds(0, rows), :], xe_scr, sem.at[0]).wait()
        xb_scr[...] = xe_scr[...].astype(BF16)

    @pl.when(s < nt)
    def _():
        xe = xb_scr[...]
        a = _dot(xe, wg_ref[...].astype(BF16))
        u = _dot(xe, wu_ref[...].astype(BF16))
        hmid_scr[s] = (_silu(a) * u).astype(BF16)

    def down_step(prefetch):
        per = rows // (nd * nt)
        y = None
        for k in range(nt):
            if prefetch:
                dst0 = (s - nt) * (per * nt) + k * per
                first = (e + 1) * rows + dst0
                for j in range(per):
                    pltpu.make_async_copy(h2_hbm.at[pl.ds(idx_ref[first + j], 1), :],
                                          xe_scr.at[pl.ds(dst0 + j, 1), :], sem.at[0]).start()
            part = _dot(hmid_scr[k], wd_ref[k * tf:(k + 1) * tf, :].astype(BF16))
            y = part if y is None else y + part
        ye_ref[...] = y.astype(BF16)

    more = e + 1 < pl.num_programs(0)

    @pl.when((s >= nt) & more)
    def _():
        down_step(True)

    @pl.when((s >= nt) & jnp.logical_not(more))
    def _():
        down_step(False)


def _ffn(idx, h2, w_gate, w_up, w_down):
    n_exp, rows = idx.shape
    idx = idx.reshape(n_exp * rows)
    d = h2.shape[1]
    ff = w_gate.shape[2]
    tf = min(ff, 512)
    tn = min(d, 1024)
    nt = ff // tf
    nd = d // tn
    assert rows % (nd * nt) == 0
    up = lambda e, s, idx: (e, 0, jnp.minimum(s, nt - 1))
    down = lambda e, s, idx: (e, 0, jnp.maximum(s - nt, 0))
    return pl.pallas_call(
        functools.partial(_ffn_kernel, nt=nt, nd=nd, tf=tf),
        out_shape=jax.ShapeDtypeStruct((n_exp, rows, d), BF16),
        grid_spec=pltpu.PrefetchScalarGridSpec(
            num_scalar_prefetch=1,
            grid=(n_exp, nt + nd),
            in_specs=[pl.BlockSpec(memory_space=pl.ANY),
                      pl.BlockSpec((None, d, tf), up),
                      pl.BlockSpec((None, d, tf), up),
                      pl.BlockSpec((None, ff, tn), down)],
            out_specs=pl.BlockSpec((None, rows, tn), down),
            scratch_shapes=[pltpu.VMEM((rows, d), F32),
                            pltpu.VMEM((rows, d), BF16),
                            pltpu.VMEM((nt, rows, tf), BF16),
                            pltpu.SemaphoreType.DMA((1,))]),
        compiler_params=_cparams("arbitrary", "arbitrary"),
        name="expert_swiglu",
    )(idx, h2, w_gate, w_up, w_down)


def _combine_kernel(tab_ref, slot_ref, aff_ref, ye_ref, x1_ref, mod_ref, out_ref, y_scr,
                    *, n_exp, cap, win):
    b = pl.program_id(0)
    t = pl.program_id(1)
    tt = x1_ref.shape[0]
    base = (b * (pl.num_programs(1) + 1) + t) * n_exp
    pack = 16
    starts = []
    short = None
    for e in range(n_exp):
        c0 = tab_ref[base + e]
        c1 = tab_ref[base + n_exp + e]
        a = jnp.minimum(c0 & ~(pack - 1), cap - win)
        ok = c1 - a <= win
        starts.append(a)
        short = ok if short is None else short & ok

    def finish(acc):
        out_ref[...] = x1_ref[...] + mod_ref[5:6, :] * acc

    @pl.when(short)
    def _():
        lane = lax.broadcasted_iota(jnp.int32, (tt, LANES), 1).astype(F32)
        per = LANES // win
        blocks = []
        for g in range(n_exp // per):
            blk = jnp.zeros((tt, LANES), F32)
            for j in range(per):
                e = g * per + j
                a = pl.multiple_of(starts[e], pack)
                y_scr[e * win:(e + 1) * win, :] = ye_ref[e, pl.ds(a, win), :]
                slot = slot_ref[:, e:e + 1]
                rel = jnp.where(slot >= 0.0, slot - a.astype(F32) + float(j * win), -1.0)
                blk = jnp.where(lane == rel, aff_ref[:, e:e + 1], blk)
            blocks.append(blk.astype(BF16))
        finish(_dot(jnp.concatenate(blocks, axis=1), y_scr[...]))

    @pl.when(jnp.logical_not(short))
    def _():
        cidx = lax.broadcasted_iota(jnp.int32, (tt, cap), 1).astype(F32)
        acc = jnp.zeros(x1_ref.shape, F32)
        for e in range(n_exp):
            onehot = jnp.where(cidx == slot_ref[:, e:e + 1], 1.0, 0.0).astype(BF16)
            acc = acc + aff_ref[:, e:e + 1] * _dot(onehot, ye_ref[e])
        finish(acc)


def _combine(tab, slot_se, aff, ye4, x1, mod6, seq, n_exp, cap):
    m, d = x1.shape
    ep = slot_se.shape[1]
    batch = m // seq
    tt = min(seq, COMBINE_TILE)
    tpb = seq // tt
    win = min(COMBINE_WIN, cap)
    assert LANES % win == 0 and n_exp % (LANES // win) == 0 and cap % 16 == 0
    return pl.pallas_call(
        functools.partial(_combine_kernel, n_exp=n_exp, cap=cap, win=win),
        out_shape=jax.ShapeDtypeStruct((m, d), F32),
        grid_spec=pltpu.PrefetchScalarGridSpec(
            num_scalar_prefetch=1,
            grid=(batch, tpb),
            in_specs=[pl.BlockSpec((tt, ep), lambda b, t, tab: (b * tpb + t, 0)),
                      pl.BlockSpec((tt, ep), lambda b, t, tab: (b * tpb + t, 0)),
                      pl.BlockSpec((n_exp, None, cap, d), lambda b, t, tab: (0, b, 0, 0)),
                      pl.BlockSpec((tt, d), lambda b, t, tab: (b * tpb + t, 0)),
                      pl.BlockSpec((None, 6, d), lambda b, t, tab: (b, 0, 0))],
            out_specs=pl.BlockSpec((tt, d), lambda b, t, tab: (b * tpb + t, 0)),
            scratch_shapes=[pltpu.VMEM((n_exp * win, d), BF16)]),
        compiler_params=_cparams("parallel", "arbitrary"),
        name="expert_combine",
    )(tab, slot_se, aff, ye4, x1, mod6)


def kernel(x, c, positions, w_ada, b_ada, norm1_g, w_in, lb_logits, hgrn_out_g, qa_norm_g, w_uq,
           kva_norm_g, w_ukv, q_head_g, k_head_g, w_out, norm2_g, w_router, w_gate, w_up, w_down):
    batch, seq, d = x.shape
    depth = w_ada.shape[0]
    m = batch * seq
    hk = lb_logits.shape[2]
    heads, dv = hgrn_out_g.shape[1], hgrn_out_g.shape[2]
    hw = heads * dv
    ql, kvl = qa_norm_g.shape[1], kva_norm_g.shape[1]
    qk_dim = q_head_g.shape[1]
    mh = w_uq.shape[2] // qk_dim
    d_in = w_in.shape[2]
    rope = d_in - (3 * hk + 2 * hw + ql + kvl)
    nope = qk_dim - rope
    vdim = w_ukv.shape[2] // mh - nope
    n_exp = w_router.shape[2]
    cap = EC_CAPACITY * seq // n_exp
    assert dv == LANES and hk == hw and nope == LANES and vdim == LANES and 2 * rope == LANES
    assert ql + kvl + rope <= hk and seq % GRP == 0 and n_exp <= LANES and cap % 8 == 0

    cq_off = 3 * hk + 2 * hw
    ckv_off = cq_off + ql
    kpe_off = ckv_off + kvl
    swap = jnp.concatenate([jnp.arange(rope // 2, rope), jnp.arange(0, rope // 2)])

    def both(v):
        return jnp.concatenate([v, v[..., swap]], axis=-1)

    cs = _rope_tables(positions, rope)
    c8 = jnp.pad(c, ((0, (-batch) % 8), (0, 0)))
    xf = x.reshape(m, d)
    for l in range(depth):
        mod6 = _ada(c8, w_ada[l], b_ada[l])[:batch].reshape(batch, 6, d)

        w_in_b = w_in[l].astype(BF16)
        wk_b = both(w_in[l][:, kpe_off:kpe_off + rope]).astype(BF16)
        proj, fgate, kpe2 = _inproj(xf, mod6, norm1_g[l].reshape(1, d), w_in_b, wk_b, lb_logits,
                                   seq, hk, l)

        o_hgrn = _hgrn(proj, fgate, batch, seq, heads, hk)

        wq = w_uq[l].reshape(ql, mh, qk_dim)
        wq_all = jnp.concatenate([wq[..., :nope], both(wq[..., nope:])], axis=-1)
        qh, kh, vh = _mla_proj(
            proj, kpe2, cs, qa_norm_g[l].reshape(1, ql), kva_norm_g[l].reshape(1, kvl),
            wq_all.reshape(ql, mh * 2 * LANES).astype(BF16), w_ukv[l].astype(BF16),
            q_head_g[l][:nope].reshape(1, nope), both(q_head_g[l][nope:]).reshape(1, 2 * rope),
            k_head_g[l][:nope].reshape(1, nope), both(k_head_g[l][nope:]).reshape(1, 2 * rope),
            batch, seq, mh, cq_off, ckv_off, qk_dim, rope)
        o_mla = _attention(qh, kh, vh)

        wr_pad = jnp.pad(w_router[l], ((0, 0), (0, LANES - n_exp)))
        x1, h2, aff, lat = _outproj(o_hgrn, proj, hgrn_out_g[l].reshape(1, hw), o_mla,
                                    w_out[l].astype(BF16), xf, mod6, norm2_g[l].reshape(1, d),
                                    wr_pad, seq, heads, hk, n_exp)

        slot_se, idx, tab = _topk(lat, batch, seq, n_exp, cap)
        ntile = seq // min(seq, COMBINE_TILE)
        tab = tab[:, :ntile + 1].reshape(batch, n_exp, ntile + 1).transpose(0, 2, 1).reshape(-1)
        rows = idx.reshape(batch, n_exp, cap) + (jnp.arange(batch, dtype=jnp.int32) * seq)[:, None, None]
        rows = rows.transpose(1, 0, 2).reshape(n_exp, batch * cap)
        ye = _ffn(rows, h2, w_gate[l], w_up[l], w_down[l])
        xf = _combine(tab, slot_se, aff, ye.reshape(n_exp, batch, cap, d), x1, mod6, seq, n_exp, cap)
    return xf.reshape(batch, seq, d)
```

```python
import functools
import math

import jax
import jax.numpy as jnp
from jax import lax
from jax.experimental import pallas as pl
from jax.experimental.pallas import tpu as pltpu

F32 = jnp.float32
BF16 = jnp.bfloat16
EPS = 1e-6
ROPE_BASE = 10000.0
LOG2E = math.log2(math.e)
EC_CAPACITY = 2
LANES = 128
TILE = 8
GRP = 128
VMEM_LIMIT = 56 * 1024 * 1024


def _cparams(*sem):
    return pltpu.CompilerParams(dimension_semantics=sem, vmem_limit_bytes=VMEM_LIMIT)


def _dot(a, b):
    return jnp.dot(a, b, preferred_element_type=F32)


def _dot_nt(a, b):
    return lax.dot_general(a, b, (((1,), (1,)), ((), ())), preferred_element_type=F32)


def _dot_tn(a, b):
    return lax.dot_general(a, b, (((0,), (0,)), ((), ())), preferred_element_type=F32)


def _split2(a):
    hi = a.astype(BF16)
    lo = (a - hi.astype(F32)).astype(BF16)
    return hi, lo


def _split3(a):
    p1 = a.astype(BF16)
    r1 = a - p1.astype(F32)
    p2 = r1.astype(BF16)
    p3 = (r1 - p2.astype(F32)).astype(BF16)
    return p1, p2, p3


def _dot_hi(a, b):
    ah, al = _split2(a)
    bh, bl = _split2(b)
    return _dot(ah, bh) + (_dot(ah, bl) + _dot(al, bh))


def _silu(x):
    return x * jax.nn.sigmoid(x)


def _rope_kernel(pos_ref, cs_ref, *, half):
    pos = pos_ref[...].astype(F32)
    lane = lax.broadcasted_iota(jnp.int32, (1, 4 * half), 1)
    j = (lane & (half - 1)).astype(F32)
    inv_freq = jnp.exp(j * (-2.0 * math.log(ROPE_BASE) / (2 * half)))
    ang = pos * inv_freq
    c = jnp.cos(ang)
    s = jnp.sin(ang)
    cs_ref[...] = jnp.where(lane < 2 * half, c, jnp.where(lane < 3 * half, -s, s))


def _rope_tables(positions, rope):
    m = positions.size
    tm = min(m, 1024)
    half = rope // 2
    return pl.pallas_call(
        functools.partial(_rope_kernel, half=half),
        out_shape=jax.ShapeDtypeStruct((m, 2 * rope), F32),
        grid=(m // tm,),
        in_specs=[pl.BlockSpec((tm, 1), lambda i: (i, 0))],
        out_specs=pl.BlockSpec((tm, 2 * rope), lambda i: (i, 0)),
        compiler_params=_cparams("parallel"),
        name="rope_tables",
    )(positions.reshape(m, 1))


def _ada_kernel(c_ref, w_ref, b_ref, o_ref):
    part = _dot_hi(_silu(c_ref[...]), w_ref[...])

    @pl.when(pl.program_id(0) == 0)
    def _():
        o_ref[...] = part + b_ref[...]

    @pl.when(pl.program_id(0) > 0)
    def _():
        o_ref[...] += part


def _ada(c8, w, b):
    d, n = w.shape
    tk = min(d, 128)
    return pl.pallas_call(
        _ada_kernel,
        out_shape=jax.ShapeDtypeStruct((c8.shape[0], n), F32),
        grid=(d // tk,),
        in_specs=[pl.BlockSpec((c8.shape[0], tk), lambda k: (0, k)),
                  pl.BlockSpec((tk, n), lambda k: (k, 0)),
                  pl.BlockSpec((1, n), lambda k: (0, 0))],
        out_specs=pl.BlockSpec((c8.shape[0], n), lambda k: (0, 0)),
        compiler_params=_cparams("arbitrary"),
        name="ada_mod",
    )(c8, w, b.reshape(1, n))


def _inproj_kernel(x_ref, mod_ref, g_ref, w_ref, wk_ref, lbl_ref,
                   proj_ref, fgate_ref, kpe_ref, h_scr, *, layer):
    j = pl.program_id(1)

    @pl.when(j == 0)
    def _():
        x = x_ref[...]
        r = lax.rsqrt(jnp.mean(x * x, axis=-1, keepdims=True) + EPS)
        h = x * r * g_ref[...] * (1.0 + mod_ref[1:2, :]) + mod_ref[0:1, :]
        hb = h.astype(BF16)
        h_scr[...] = hb
        kpe_ref[...] = _dot(hb, wk_ref[...])

    acc = _dot(h_scr[...], w_ref[...])
    proj_ref[...] = acc.astype(BF16)

    @pl.when((j == 1) | (j == 2))
    def _():
        lg = lbl_ref[...]
        e = jnp.exp(lg - jnp.max(lg, axis=0, keepdims=True))
        lb = jnp.sum(e[:layer + 1], axis=0, keepdims=True) / jnp.sum(e, axis=0, keepdims=True)
        fgate_ref[...] = lb + (1.0 - lb) * jax.nn.sigmoid(acc)


def _inproj(xf, mod6, g1, w_in_b, wk_b, lb_logits, seq, hk, layer):
    m, d = xf.shape
    d_in = w_in_b.shape[1]
    tm = min(seq, 1024)
    tpb = seq // tm
    tn = hk
    nj = pl.cdiv(d_in, tn)
    nl = lb_logits.shape[1]
    fdir = lambda j: jnp.clip(j - 1, 0, 1)
    return pl.pallas_call(
        functools.partial(_inproj_kernel, layer=layer),
        out_shape=(jax.ShapeDtypeStruct((m, d_in), BF16),
                   jax.ShapeDtypeStruct((m, 2 * hk), F32),
                   jax.ShapeDtypeStruct((m, LANES), F32)),
        grid=(m // tm, nj),
        in_specs=[pl.BlockSpec((tm, d), lambda i, j: (i, 0)),
                  pl.BlockSpec((None, 6, d), lambda i, j: (i // tpb, 0, 0)),
                  pl.BlockSpec((1, d), lambda i, j: (0, 0)),
                  pl.BlockSpec((d, tn), lambda i, j: (0, j)),
                  pl.BlockSpec((d, LANES), lambda i, j: (0, 0)),
                  pl.BlockSpec((None, nl, tn), lambda i, j: (fdir(j), 0, 0))],
        out_specs=(pl.BlockSpec((tm, tn), lambda i, j: (i, j)),
                   pl.BlockSpec((tm, tn), lambda i, j: (i, fdir(j))),
                   pl.BlockSpec((tm, LANES), lambda i, j: (i, 0))),
        scratch_shapes=[pltpu.VMEM((tm, d), BF16)],
        compiler_params=_cparams("parallel", "arbitrary"),
        name="norm1_inproj",
    )(xf, mod6, g1, w_in_b, wk_b, lb_logits)


def _tile_scan(g, d):
    rin = lax.broadcasted_iota(jnp.int32, g.shape, 0) & (TILE - 1)
    b = g
    step = 1
    while step < TILE:
        if d == 0:
            b = b + jnp.where(rin >= step, pltpu.roll(b, step, 0), 0.0)
        else:
            b = b + jnp.where(rin < TILE - step, pltpu.roll(b, GRP - step, 0), 0.0)
        step *= 2
    return b


def _group_cumsum(g, d):
    b = _tile_scan(g, d)
    ntile = GRP // TILE
    order = range(ntile) if d == 0 else range(ntile - 1, -1, -1)
    edge = TILE - 1 if d == 0 else 0
    out = [None] * ntile
    carry = None
    for i in order:
        t = b[i * TILE:(i + 1) * TILE]
        out[i] = t if carry is None else t + carry
        tot = t[edge:edge + 1]
        carry = tot if carry is None else carry + tot
    return jnp.concatenate(out, axis=0)


def _boundary(b, h, d):
    idx = h - 1 if d == 0 else h
    if 2 * h >= TILE:
        b3 = b.reshape(GRP // (2 * h), 2 * h, b.shape[1])
        return jnp.broadcast_to(b3[:, idx:idx + 1, :], b3.shape).reshape(b.shape)
    p = lax.broadcasted_iota(jnp.int32, b.shape, 0) & (2 * h - 1)
    out = b
    for pos in range(2 * h):
        shift = pos - idx
        if shift != 0:
            out = jnp.where(p == pos, pltpu.roll(b, shift % GRP, 0), out)
    return out


def _hgrn_kernel(q_ref, v_ref, gf_ref, gb_ref, o_ref, lv_scr, sg_scr, st_scr):
    seq = q_ref.shape[0]
    ngrp = seq // GRP
    nlev = GRP.bit_length()
    g_refs = (gf_ref, gb_ref)
    assert ngrp % 2 == 0

    r = lax.broadcasted_iota(jnp.int32, (GRP, GRP), 0)
    c = lax.broadcasted_iota(jnp.int32, (GRP, GRP), 1)
    lev = jnp.zeros((GRP, GRP), jnp.int32)
    for j in range(nlev - 1):
        lev = lev + jnp.where((r >> j) != (c >> j), 1, 0)
    lv_scr[0] = jnp.where(c <= r, lev, -1)
    lv_scr[1] = jnp.where(c >= r, lev, -1)
    st_scr[...] = jnp.zeros_like(st_scr)
    rr = lax.broadcasted_iota(jnp.int32, (GRP, LANES), 0)
    for l in range(1, nlev):
        late = (rr & (1 << (l - 1))) != 0
        sg_scr[0, l - 1] = jnp.where(late, 1.0, -1.0)
        sg_scr[1, l - 1] = jnp.where(late, -1.0, 1.0)

    def body(i, carry, first):
        for d in (0, 1):
            grp = i if d == 0 else ngrp - 1 - i
            r0 = pl.multiple_of(grp * GRP, GRP)
            f = g_refs[d][pl.ds(r0, GRP), :]
            g = jnp.log2(f)
            qb = q_ref[pl.ds(r0, GRP), :]
            vb = v_ref[pl.ds(r0, GRP), :]
            qf = qb.astype(F32)
            kk = 1.0 - f
            kb = kk.astype(BF16)
            b = _group_cumsum(g, d)
            edge = GRP - 1 if d == 0 else 0
            tot = b[edge:edge + 1]
            lv = lv_scr[d]
            att = jnp.where(lv == 0, _dot_nt(qb, kb), 0.0)
            for l in range(1, nlev):
                x = jnp.exp2((b - _boundary(b, 1 << (l - 1), d)) * sg_scr[d, l - 1]).astype(BF16)
                att = jnp.where(lv == l, _dot_nt(qb * x, kb * x), att)
            st = st_scr[d]
            o = _dot(att.astype(BF16), vb) + _dot_nt((qf * jnp.exp2(b)).astype(BF16), st.astype(BF16))
            if first:
                o_ref[pl.ds(r0, GRP), :] = o
            else:
                o_ref[pl.ds(r0, GRP), :] += o
            st_scr[d] = st * jnp.exp2(tot) + _dot_tn(vb, (kk * jnp.exp2(tot - b)).astype(BF16))
        return carry

    half = ngrp // 2
    lax.fori_loop(0, half, functools.partial(body, first=True), 0, unroll=min(8, half))
    lax.fori_loop(half, ngrp, functools.partial(body, first=False), 0, unroll=min(8, half))


def _hgrn(proj, fgate, batch, seq, heads, hk):
    m = proj.shape[0]
    nh = hk // LANES
    vcol = 3 * nh
    blk = lambda off: pl.BlockSpec((seq, LANES), lambda b, h: (b, off + h))
    out = jax.ShapeDtypeStruct((m, hk), F32)
    return pl.pallas_call(
        _hgrn_kernel,
        out_shape=out,
        grid=(batch, heads),
        in_specs=[blk(0), blk(vcol), blk(0), blk(nh)],
        out_specs=blk(0),
        scratch_shapes=[pltpu.VMEM((2, GRP, GRP), jnp.int32),
                        pltpu.VMEM((2, GRP.bit_length() - 1, GRP, LANES), F32),
                        pltpu.VMEM((2, LANES, LANES), F32)],
        compiler_params=_cparams("parallel", "parallel"),
        name="hgrn2_scan",
    )(proj, proj, fgate, fgate)


def _mla_proj_kernel(cq_ref, ckv_ref, kpe_ref, cs_ref, qag_ref, kvag_ref, wq_ref, wkv_ref,
                     qgn_ref, qgr_ref, kgn_ref, kgr_ref, q_out, k_out, v_out,
                     *, scale, qk_dim, rope, heads):
    cq = cq_ref[...].astype(F32)
    a = (cq * lax.rsqrt(jnp.mean(cq * cq, axis=-1, keepdims=True) + EPS) * qag_ref[...]).astype(BF16)
    ckv = ckv_ref[...].astype(F32)
    c = (ckv * lax.rsqrt(jnp.mean(ckv * ckv, axis=-1, keepdims=True) + EPS)
         * kvag_ref[...]).astype(BF16)
    qall = _dot(a, wq_ref[...])
    kvall = _dot(c, wkv_ref[...])

    cs = cs_ref[...]
    lane = lax.broadcasted_iota(jnp.int32, cs.shape, 1)
    lo = lane < rope

    def rope_sumsq(rr):
        return jnp.sum(jnp.where(lo, rr * rr, 0.0), axis=-1, keepdims=True)

    def rotate(rr, gr):
        y = rr * gr * cs
        return y + pltpu.roll(y, rope, 1)

    kpe = kpe_ref[...]
    k_ss = rope_sumsq(kpe)
    k_rot = rotate(kpe, kgr_ref[...])
    for h in range(heads):
        base = 2 * LANES * h
        qn = qall[:, base:base + LANES]
        qr = qall[:, base + LANES:base + 2 * LANES]
        rq = lax.rsqrt((jnp.sum(qn * qn, axis=-1, keepdims=True) + rope_sumsq(qr)) / qk_dim + EPS) * scale
        q_out[h, :, :LANES] = (qn * qgn_ref[...] * rq).astype(BF16)
        q_out[h, :, LANES:] = jnp.where(lo, rotate(qr, qgr_ref[...]) * rq, 0.0).astype(BF16)
        kn = kvall[:, base:base + LANES]
        rk = lax.rsqrt((jnp.sum(kn * kn, axis=-1, keepdims=True) + k_ss) / qk_dim + EPS)
        k_out[h, :, :LANES] = (kn * kgn_ref[...] * rk).astype(BF16)
        k_out[h, :, LANES:] = jnp.where(lo, k_rot * rk, 0.0).astype(BF16)
        v_out[h] = kvall[:, base + LANES:base + 2 * LANES].astype(BF16)


def _mla_proj(proj, kpe2, cs, qag, kvag, wq_all, wkv_all, qgn, qgr, kgn, kgr,
              batch, seq, heads, cq_off, ckv_off, qk_dim, rope):
    m = proj.shape[0]
    ql, kvl = wq_all.shape[0], wkv_all.shape[0]
    tm = min(seq, 256)
    tpb = seq // tm
    assert cq_off % ql == 0 and ckv_off % kvl == 0
    vec = lambda n: pl.BlockSpec((1, n), lambda i: (0, 0))
    full = lambda w: pl.BlockSpec(w.shape, lambda i: (0, 0))
    hspec = lambda n: pl.BlockSpec((None, heads, tm, n), lambda i: (i // tpb, 0, i % tpb, 0))
    scale = qk_dim ** -0.5 * LOG2E
    return pl.pallas_call(
        functools.partial(_mla_proj_kernel, scale=scale, qk_dim=float(qk_dim), rope=rope, heads=heads),
        out_shape=(jax.ShapeDtypeStruct((batch, heads, seq, 2 * LANES), BF16),
                   jax.ShapeDtypeStruct((batch, heads, seq, 2 * LANES), BF16),
                   jax.ShapeDtypeStruct((batch, heads, seq, LANES), BF16)),
        grid=(m // tm,),
        in_specs=[pl.BlockSpec((tm, ql), lambda i: (i, cq_off // ql)),
                  pl.BlockSpec((tm, kvl), lambda i: (i, ckv_off // kvl)),
                  pl.BlockSpec((tm, LANES), lambda i: (i, 0)),
                  pl.BlockSpec((tm, LANES), lambda i: (i, 0)),
                  vec(ql), vec(kvl), full(wq_all), full(wkv_all),
                  vec(LANES), vec(LANES), vec(LANES), vec(LANES)],
        out_specs=(hspec(2 * LANES), hspec(2 * LANES), hspec(LANES)),
        compiler_params=_cparams("parallel"),
        name="mla_head_proj",
    )(proj, proj, kpe2, cs, qag, kvag, wq_all, wkv_all, qgn, qgr, kgn, kgr)


ATTN_KEYS = 1024


def _attn_kernel(q_ref, k_ref, v_ref, o_ref):
    q = q_ref[...]
    seq = k_ref.shape[0]
    kc = min(ATTN_KEYS, seq)
    m = l = acc = None
    for c in range(seq // kc):
        rows = slice(c * kc, (c + 1) * kc)
        s = _dot_nt(q, k_ref[rows, :])
        mc = jnp.max(s, axis=-1, keepdims=True)
        if c == 0:
            m = mc
            p = jnp.exp2(s - m)
            l = jnp.sum(p, axis=-1, keepdims=True)
            acc = _dot(p.astype(BF16), v_ref[rows, :])
        else:
            m_new = jnp.maximum(m, mc)
            alpha = jnp.exp2(m - m_new)
            p = jnp.exp2(s - m_new)
            l = l * alpha + jnp.sum(p, axis=-1, keepdims=True)
            acc = acc * alpha + _dot(p.astype(BF16), v_ref[rows, :])
            m = m_new
    o_ref[...] = (acc / l).astype(BF16)


def _attention(qh, kh, vh):
    batch, mh, seq, dq = qh.shape
    dv = vh.shape[-1]
    tq = min(seq, 2048)
    nq = seq // tq
    return pl.pallas_call(
        _attn_kernel,
        out_shape=jax.ShapeDtypeStruct((batch * seq, mh * dv), BF16),
        grid=(batch, mh, nq),
        in_specs=[pl.BlockSpec((None, None, tq, dq), lambda b, h, i: (b, h, i, 0)),
                  pl.BlockSpec((None, None, seq, dq), lambda b, h, i: (b, h, 0, 0)),
                  pl.BlockSpec((None, None, seq, dv), lambda b, h, i: (b, h, 0, 0))],
        out_specs=pl.BlockSpec((tq, dv), lambda b, h, i: (b * nq + i, h)),
        compiler_params=_cparams("parallel", "parallel", "arbitrary"),
        name="mla_attention",
    )(qh, kh, vh)


def _eye(rows, cols):
    r = lax.broadcasted_iota(jnp.int32, (rows, cols), 0)
    c = lax.broadcasted_iota(jnp.int32, (rows, cols), 1)
    return jnp.where(r == c, 1.0, 0.0).astype(BF16)


def _outproj_kernel(o_ref, hg_ref, og_ref, om_ref, w_ref, x_ref, mod_ref, g2_ref, wr_ref,
                    x1_ref, h2_ref, aff_ref, lat_ref, mix_scr, *, heads, n_exp):
    hw = o_ref.shape[1]
    o = o_ref[...]
    gate = _silu(hg_ref[...].astype(F32))
    for h in range(heads):
        sl = slice(h * LANES, (h + 1) * LANES)
        oh = o[:, sl]
        r = lax.rsqrt(jnp.mean(oh * oh, axis=-1, keepdims=True) + EPS)
        mix_scr[:, sl] = (oh * r * og_ref[:, sl] * gate[:, sl]).astype(BF16)
    mix_scr[:, hw:] = om_ref[...]
    x1 = x_ref[...] + mod_ref[2:3, :] * _dot(mix_scr[...], w_ref[...])
    x1_ref[...] = x1
    r2 = lax.rsqrt(jnp.mean(x1 * x1, axis=-1, keepdims=True) + EPS)
    h2 = x1 * r2 * g2_ref[...] * (1.0 + mod_ref[4:5, :]) + mod_ref[3:4, :]
    h2_ref[...] = h2
    logits = _dot_hi(h2, wr_ref[...])
    lane = lax.broadcasted_iota(jnp.int32, logits.shape, 1)
    logits = jnp.where(lane < n_exp, logits, -jnp.inf)
    z = logits - jnp.max(logits, axis=-1, keepdims=True)
    p = jnp.exp(z)
    sp = jnp.sum(p, axis=-1, keepdims=True)
    aff_ref[...] = p / sp
    la = jnp.where(lane < n_exp, z - jnp.log(sp), 0.0)
    eye = _eye(n_exp, la.shape[1])
    p1, p2, p3 = _split3(la)
    lat_ref[...] = (_dot_nt(eye, p1) + _dot_nt(eye, p2)) + _dot_nt(eye, p3)


def _outproj(o_hgrn, proj, og, o_mla, w_out_b, xf, mod6, g2, wr_pad, seq, heads, hk, n_exp):
    m, d = xf.shape
    hw = o_hgrn.shape[1]
    mw = o_mla.shape[1]
    tm = min(seq, 256)
    tpb = seq // tm
    gcol = (3 * hk + hw) // hw
    assert (3 * hk + hw) % hw == 0
    row = lambda n: pl.BlockSpec((tm, n), lambda i: (i, 0))
    return pl.pallas_call(
        functools.partial(_outproj_kernel, heads=heads, n_exp=n_exp),
        out_shape=(jax.ShapeDtypeStruct((m, d), F32),
                   jax.ShapeDtypeStruct((m, d), F32),
                   jax.ShapeDtypeStruct((m, LANES), F32),
                   jax.ShapeDtypeStruct((m // seq, n_exp, seq), F32)),
        grid=(m // tm,),
        in_specs=[row(hw),
                  pl.BlockSpec((tm, hw), lambda i: (i, gcol)),
                  pl.BlockSpec((1, hw), lambda i: (0, 0)),
                  row(mw),
                  pl.BlockSpec((hw + mw, d), lambda i: (0, 0), pipeline_mode=pl.Buffered(1)),
                  row(d),
                  pl.BlockSpec((None, 6, d), lambda i: (i // tpb, 0, 0)),
                  pl.BlockSpec((1, d), lambda i: (0, 0)),
                  pl.BlockSpec((d, LANES), lambda i: (0, 0), pipeline_mode=pl.Buffered(1))],
        out_specs=(row(d), row(d), row(LANES),
                   pl.BlockSpec((None, n_exp, tm), lambda i: (i // tpb, 0, i % tpb))),
        scratch_shapes=[pltpu.VMEM((tm, hw + mw), BF16)],
        compiler_params=_cparams("parallel"),
        name="outproj_norm2_router",
    )(o_hgrn, proj, og, o_mla, w_out_b, xf, mod6, g2, wr_pad)


BISECT_STEPS = 64


COMBINE_TILE = 256
COMBINE_WIN = 64


def _topk_kernel(la_ref, slot_se_ref, idx_ref, tab_ref, tri_scr, cum_scr, *, cap, n_exp):
    nrow, seq = la_ref.shape
    ep = slot_se_ref.shape[1]
    rows = 256
    for k in range(seq // rows):
        r = lax.broadcasted_iota(jnp.int32, (rows, seq), 0) + k * rows
        c = lax.broadcasted_iota(jnp.int32, (rows, seq), 1)
        tri_scr[k * rows:(k + 1) * rows, :] = jnp.where(r < c, 1.0, 0.0).astype(BF16)

    def count(mask):
        return jnp.sum(jnp.where(mask, 1.0, 0.0), axis=-1, keepdims=True)

    def body(_, lh):
        lo, hi = lh
        mid = 0.5 * (lo + hi)
        ok = count(la_ref[...] >= mid) >= cap
        return jnp.where(ok, mid, lo), jnp.where(ok, hi, mid)

    la = la_ref[...]
    lo0 = jnp.min(la, axis=-1, keepdims=True)
    lo, hi = lax.fori_loop(0, BISECT_STEPS, body, (lo0, jnp.ones_like(lo0)))
    above = la >= hi
    tie = (la >= lo) & (la < hi)
    need = cap - count(above)
    tri = tri_scr[...]
    rank = _dot(jnp.where(tie, 1.0, 0.0).astype(BF16), tri)
    sel = above | (tie & (rank < need))
    pos = _dot(jnp.where(sel, 1.0, 0.0).astype(BF16), tri)
    slot = jnp.where(sel, pos, -1.0)
    eye = _eye(n_exp, ep)
    for b in range(nrow // n_exp):
        slot_se_ref[b * seq:(b + 1) * seq, :] = _dot_tn(
            slot[b * n_exp:(b + 1) * n_exp, :].astype(BF16), eye)
    cum_scr[...] = pos + jnp.where(sel, 1.0, 0.0)
    lane = lax.broadcasted_iota(jnp.int32, (nrow, cap), 1)

    def slot_body(c, acc):
        cnt = count(cum_scr[...] <= lax.convert_element_type(c, F32))
        return jnp.where(lane == c, cnt, acc)

    idx = lax.fori_loop(0, cap, slot_body, jnp.zeros((nrow, cap), F32), unroll=4)
    idx_ref[...] = idx.astype(jnp.int32)
    tok = lax.broadcasted_iota(jnp.int32, (nrow, seq), 1)
    tlane = lax.broadcasted_iota(jnp.int32, tab_ref.shape, 1)
    tab = jnp.zeros(tab_ref.shape, F32)
    tile = min(seq, COMBINE_TILE)
    for k in range(seq // tile + 1):
        tab = jnp.where(tlane == k, count(sel & (tok < k * tile)), tab)
    tab_ref[...] = tab.astype(jnp.int32)


def _topk(lat, batch, seq, n_exp, cap):
    return pl.pallas_call(
        functools.partial(_topk_kernel, cap=cap, n_exp=n_exp),
        out_shape=(jax.ShapeDtypeStruct((batch * seq, LANES), F32),
                   jax.ShapeDtypeStruct((batch * n_exp, cap), jnp.int32),
                   jax.ShapeDtypeStruct((batch * n_exp, LANES), jnp.int32)),
        scratch_shapes=[pltpu.VMEM((seq, seq), BF16), pltpu.VMEM((batch * n_exp, seq), F32)],
        compiler_params=pltpu.CompilerParams(vmem_limit_bytes=VMEM_LIMIT),
        name="expert_choice_topk",
    )(lat.reshape(batch * n_exp, seq))


def _ffn_kernel(idx_ref, h2_hbm, wg_ref, wu_ref, wd_ref, ye_ref, xe_scr, hmid_scr, sem,
                *, nt, nd, tf):
    e = pl.program_id(0)
    s = pl.program_id(1)
    rows = xe_scr.shape[0]

    def start_gather(expert):
        base = expert * rows

        def body(k, carry):
            r0 = pl.multiple_of(k * TILE, TILE)
            for j in range(TILE):
                pltpu.make_async_copy(h2_hbm.at[pl.ds(idx_ref[base + r0 + j], 1), :],
                                      xe_scr.at[pl.ds(r0 + j, 1), :], sem.at[0]).start()
            return carry
        lax.fori_loop(0, rows // TILE, body, 0)

    @pl.when((e == 0) & (s == 0))
    def _():
        start_gather(0)

    @pl.when(s == 0)
    def _():
        pltpu.make_async_copy(h2_hbm.at[pl.ds(0, rows), :], xe_scr, sem.at[0]).wait()

    @pl.when(s < nt)
    def _():
        xe = xe_scr[...].astype(BF16)
        a = _dot(xe, wg_ref[...].astype(BF16))
        u = _dot(xe, wu_ref[...].astype(BF16))
        hmid_scr[s] = (_silu(a) * u).astype(BF16)

    def down_step(prefetch):
        per = rows // (nd * nt)
        y = None
        for k in range(nt):
            if prefetch:
                dst0 = (s - nt) * (per * nt) + k * per
                first = (e + 1) * rows + dst0
                for j in range(per):
                    pltpu.make_async_copy(h2_hbm.at[pl.ds(idx_ref[first + j], 1), :],
                                          xe_scr.at[pl.ds(dst0 + j, 1), :], sem.at[0]).start()
            part = _dot(hmid_scr[k], wd_ref[k * tf:(k + 1) * tf, :].astype(BF16))
            y = part if y is None else y + part
        ye_ref[...] = y.astype(BF16)

    more = e + 1 < pl.num_programs(0)

    @pl.when((s >= nt) & more)
    def _():
        down_step(True)

    @pl.when((s >= nt) & jnp.logical_not(more))
    def _():
        down_step(False)


def _ffn(idx, h2, w_gate, w_up, w_down):
    n_exp, rows = idx.shape
    idx = idx.reshape(n_exp * rows)
    d = h2.shape[1]
    ff = w_gate.shape[2]
    tf = min(ff, 512)
    tn = min(d, 1024)
    nt = ff // tf
    nd = d // tn
    assert rows % (nd * nt) == 0
    up = lambda e, s, idx: (e, 0, jnp.minimum(s, nt - 1))
    down = lambda e, s, idx: (e, 0, jnp.maximum(s - nt, 0))
    return pl.pallas_call(
        functools.partial(_ffn_kernel, nt=nt, nd=nd, tf=tf),
        out_shape=jax.ShapeDtypeStruct((n_exp, rows, d), BF16),
        grid_spec=pltpu.PrefetchScalarGridSpec(
            num_scalar_prefetch=1,
            grid=(n_exp, nt + nd),
            in_specs=[pl.BlockSpec(memory_space=pl.ANY),
                      pl.BlockSpec((None, d, tf), up),
                      pl.BlockSpec((None, d, tf), up),
                      pl.BlockSpec((None, ff, tn), down)],
            out_specs=pl.BlockSpec((None, rows, tn), down),
            scratch_shapes=[pltpu.VMEM((rows, d), F32),
                            pltpu.VMEM((nt, rows, tf), BF16),
                            pltpu.SemaphoreType.DMA((1,))]),
        compiler_params=_cparams("arbitrary", "arbitrary"),
        name="expert_swiglu",
    )(idx, h2, w_gate, w_up, w_down)


def _combine_kernel(tab_ref, slot_ref, aff_ref, ye_ref, x1_ref, mod_ref, out_ref, y_scr,
                    *, n_exp, cap, win):
    b = pl.program_id(0)
    t = pl.program_id(1)
    tt = x1_ref.shape[0]
    base = (b * (pl.num_programs(1) + 1) + t) * n_exp
    pack = 16
    starts = []
    short = None
    for e in range(n_exp):
        c0 = tab_ref[base + e]
        c1 = tab_ref[base + n_exp + e]
        a = jnp.minimum(c0 & ~(pack - 1), cap - win)
        ok = c1 - a <= win
        starts.append(a)
        short = ok if short is None else short & ok

    def finish(acc):
        out_ref[...] = x1_ref[...] + mod_ref[5:6, :] * acc

    @pl.when(short)
    def _():
        lane = lax.broadcasted_iota(jnp.int32, (tt, LANES), 1).astype(F32)
        per = LANES // win
        blocks = []
        for g in range(n_exp // per):
            blk = jnp.zeros((tt, LANES), F32)
            for j in range(per):
                e = g * per + j
                a = pl.multiple_of(starts[e], pack)
                y_scr[e * win:(e + 1) * win, :] = ye_ref[e, pl.ds(a, win), :]
                slot = slot_ref[:, e:e + 1]
                rel = jnp.where(slot >= 0.0, slot - a.astype(F32) + float(j * win), -1.0)
                blk = jnp.where(lane == rel, aff_ref[:, e:e + 1], blk)
            blocks.append(blk.astype(BF16))
        finish(_dot(jnp.concatenate(blocks, axis=1), y_scr[...]))

    @pl.when(jnp.logical_not(short))
    def _():
        cidx = lax.broadcasted_iota(jnp.int32, (tt, cap), 1).astype(F32)
        acc = jnp.zeros(x1_ref.shape, F32)
        for e in range(n_exp):
            onehot = jnp.where(cidx == slot_ref[:, e:e + 1], 1.0, 0.0).astype(BF16)
            acc = acc + aff_ref[:, e:e + 1] * _dot(onehot, ye_ref[e])
        finish(acc)


def _combine(tab, slot_se, aff, ye4, x1, mod6, seq, n_exp, cap):
    m, d = x1.shape
    ep = slot_se.shape[1]
    batch = m // seq
    tt = min(seq, COMBINE_TILE)
    tpb = seq // tt
    win = min(COMBINE_WIN, cap)
    assert LANES % win == 0 and n_exp % (LANES // win) == 0 and cap % 16 == 0
    return pl.pallas_call(
        functools.partial(_combine_kernel, n_exp=n_exp, cap=cap, win=win),
        out_shape=jax.ShapeDtypeStruct((m, d), F32),
        grid_spec=pltpu.PrefetchScalarGridSpec(
            num_scalar_prefetch=1,
            grid=(batch, tpb),
            in_specs=[pl.BlockSpec((tt, ep), lambda b, t, tab: (b * tpb + t, 0)),
                      pl.BlockSpec((tt, ep), lambda b, t, tab: (b * tpb + t, 0)),
                      pl.BlockSpec((n_exp, None, cap, d), lambda b, t, tab: (0, b, 0, 0)),
                      pl.BlockSpec((tt, d), lambda b, t, tab: (b * tpb + t, 0)),
                      pl.BlockSpec((None, 6, d), lambda b, t, tab: (b, 0, 0))],
            out_specs=pl.BlockSpec((tt, d), lambda b, t, tab: (b * tpb + t, 0)),
            scratch_shapes=[pltpu.VMEM((n_exp * win, d), BF16)]),
        compiler_params=_cparams("parallel", "arbitrary"),
        name="expert_combine",
    )(tab, slot_se, aff, ye4, x1, mod6)


def kernel(x, c, positions, w_ada, b_ada, norm1_g, w_in, lb_logits, hgrn_out_g, qa_norm_g, w_uq,
           kva_norm_g, w_ukv, q_head_g, k_head_g, w_out, norm2_g, w_router, w_gate, w_up, w_down):
    batch, seq, d = x.shape
    depth = w_ada.shape[0]
    m = batch * seq
    hk = lb_logits.shape[2]
    heads, dv = hgrn_out_g.shape[1], hgrn_out_g.shape[2]
    hw = heads * dv
    ql, kvl = qa_norm_g.shape[1], kva_norm_g.shape[1]
    qk_dim = q_head_g.shape[1]
    mh = w_uq.shape[2] // qk_dim
    d_in = w_in.shape[2]
    rope = d_in - (3 * hk + 2 * hw + ql + kvl)
    nope = qk_dim - rope
    vdim = w_ukv.shape[2] // mh - nope
    n_exp = w_router.shape[2]
    cap = EC_CAPACITY * seq // n_exp
    assert dv == LANES and hk == hw and nope == LANES and vdim == LANES and 2 * rope == LANES
    assert ql + kvl + rope <= hk and seq % GRP == 0 and n_exp <= LANES and cap % 8 == 0

    cq_off = 3 * hk + 2 * hw
    ckv_off = cq_off + ql
    kpe_off = ckv_off + kvl
    swap = jnp.concatenate([jnp.arange(rope // 2, rope), jnp.arange(0, rope // 2)])

    def both(v):
        return jnp.concatenate([v, v[..., swap]], axis=-1)

    cs = _rope_tables(positions, rope)
    c8 = jnp.pad(c, ((0, (-batch) % 8), (0, 0)))
    xf = x.reshape(m, d)
    for l in range(depth):
        mod6 = _ada(c8, w_ada[l], b_ada[l])[:batch].reshape(batch, 6, d)

        w_in_b = w_in[l].astype(BF16)
        wk_b = both(w_in[l][:, kpe_off:kpe_off + rope]).astype(BF16)
        proj, fgate, kpe2 = _inproj(xf, mod6, norm1_g[l].reshape(1, d), w_in_b, wk_b, lb_logits,
                                   seq, hk, l)

        o_hgrn = _hgrn(proj, fgate, batch, seq, heads, hk)

        wq = w_uq[l].reshape(ql, mh, qk_dim)
        wq_all = jnp.concatenate([wq[..., :nope], both(wq[..., nope:])], axis=-1)
        qh, kh, vh = _mla_proj(
            proj, kpe2, cs, qa_norm_g[l].reshape(1, ql), kva_norm_g[l].reshape(1, kvl),
            wq_all.reshape(ql, mh * 2 * LANES).astype(BF16), w_ukv[l].astype(BF16),
            q_head_g[l][:nope].reshape(1, nope), both(q_head_g[l][nope:]).reshape(1, 2 * rope),
            k_head_g[l][:nope].reshape(1, nope), both(k_head_g[l][nope:]).reshape(1, 2 * rope),
            batch, seq, mh, cq_off, ckv_off, qk_dim, rope)
        o_mla = _attention(qh, kh, vh)

        wr_pad = jnp.pad(w_router[l], ((0, 0), (0, LANES - n_exp)))
        x1, h2, aff, lat = _outproj(o_hgrn, proj, hgrn_out_g[l].reshape(1, hw), o_mla,
                                    w_out[l].astype(BF16), xf, mod6, norm2_g[l].reshape(1, d),
                                    wr_pad, seq, heads, hk, n_exp)

        slot_se, idx, tab = _topk(lat, batch, seq, n_exp, cap)
        ntile = seq // min(seq, COMBINE_TILE)
        tab = tab[:, :ntile + 1].reshape(batch, n_exp, ntile + 1).transpose(0, 2, 1).reshape(-1)
        rows = idx.reshape(batch, n_exp, cap) + (jnp.arange(batch, dtype=jnp.int32) * seq)[:, None, None]
        rows = rows.transpose(1, 0, 2).reshape(n_exp, batch * cap)
        ye = _ffn(rows, h2, w_gate[l], w_up[l], w_down[l])
        xf = _combine(tab, slot_se, aff, ye.reshape(n_exp, batch, cap, d), x1, mod6, seq, n_exp, cap)
    return xf.reshape(batch, seq, d)
```

```python
import functools
import math

import jax
import jax.numpy as jnp
from jax import lax
from jax.experimental import pallas as pl
from jax.experimental.pallas import tpu as pltpu

F32 = jnp.float32
BF16 = jnp.bfloat16
EPS = 1e-6
ROPE_BASE = 10000.0
LOG2E = math.log2(math.e)
EC_CAPACITY = 2
LANES = 128
TILE = 8
GRP = 128
VMEM_LIMIT = 56 * 1024 * 1024


def _cparams(*sem):
    return pltpu.CompilerParams(dimension_semantics=sem, vmem_limit_bytes=VMEM_LIMIT)


def _dot(a, b):
    return jnp.dot(a, b, preferred_element_type=F32)


def _dot_nt(a, b):
    return lax.dot_general(a, b, (((1,), (1,)), ((), ())), preferred_element_type=F32)


def _dot_tn(a, b):
    return lax.dot_general(a, b, (((0,), (0,)), ((), ())), preferred_element_type=F32)


def _split2(a):
    hi = a.astype(BF16)
    lo = (a - hi.astype(F32)).astype(BF16)
    return hi, lo


def _split3(a):
    p1 = a.astype(BF16)
    r1 = a - p1.astype(F32)
    p2 = r1.astype(BF16)
    p3 = (r1 - p2.astype(F32)).astype(BF16)
    return p1, p2, p3


def _dot_hi(a, b):
    ah, al = _split2(a)
    bh, bl = _split2(b)
    return _dot(ah, bh) + (_dot(ah, bl) + _dot(al, bh))


def _silu(x):
    return x * jax.nn.sigmoid(x)


def _cast_kernel(w_ref, o_ref):
    o_ref[...] = w_ref[...].astype(BF16)


def _to_bf16(w):
    rows, cols = w.shape
    tr = min(rows, 256)
    return pl.pallas_call(
        _cast_kernel,
        out_shape=jax.ShapeDtypeStruct((rows, cols), BF16),
        grid=(rows // tr,),
        in_specs=[pl.BlockSpec((tr, cols), lambda i: (i, 0))],
        out_specs=pl.BlockSpec((tr, cols), lambda i: (i, 0)),
        compiler_params=_cparams("parallel"),
        name="weight_to_bf16",
    )(w)


def _rope_kernel(pos_ref, cs_ref, *, half):
    pos = pos_ref[...].astype(F32)
    lane = lax.broadcasted_iota(jnp.int32, (1, 4 * half), 1)
    j = (lane & (half - 1)).astype(F32)
    inv_freq = jnp.exp(j * (-2.0 * math.log(ROPE_BASE) / (2 * half)))
    ang = pos * inv_freq
    c = jnp.cos(ang)
    s = jnp.sin(ang)
    cs_ref[...] = jnp.where(lane < 2 * half, c, jnp.where(lane < 3 * half, -s, s))


def _rope_tables(positions, rope):
    m = positions.size
    tm = min(m, 1024)
    half = rope // 2
    return pl.pallas_call(
        functools.partial(_rope_kernel, half=half),
        out_shape=jax.ShapeDtypeStruct((m, 2 * rope), F32),
        grid=(m // tm,),
        in_specs=[pl.BlockSpec((tm, 1), lambda i: (i, 0))],
        out_specs=pl.BlockSpec((tm, 2 * rope), lambda i: (i, 0)),
        compiler_params=_cparams("parallel"),
        name="rope_tables",
    )(positions.reshape(m, 1))


def _ada_kernel(c_ref, w_ref, b_ref, o_ref):
    part = _dot_hi(_silu(c_ref[...]), w_ref[...])

    @pl.when(pl.program_id(0) == 0)
    def _():
        o_ref[...] = part + b_ref[...]

    @pl.when(pl.program_id(0) > 0)
    def _():
        o_ref[...] += part


def _ada(c8, w, b):
    d, n = w.shape
    tk = min(d, 128)
    return pl.pallas_call(
        _ada_kernel,
        out_shape=jax.ShapeDtypeStruct((c8.shape[0], n), F32),
        grid=(d // tk,),
        in_specs=[pl.BlockSpec((c8.shape[0], tk), lambda k: (0, k)),
                  pl.BlockSpec((tk, n), lambda k: (k, 0)),
                  pl.BlockSpec((1, n), lambda k: (0, 0))],
        out_specs=pl.BlockSpec((c8.shape[0], n), lambda k: (0, 0)),
        compiler_params=_cparams("arbitrary"),
        name="ada_mod",
    )(c8, w, b.reshape(1, n))


def _inproj_kernel(x_ref, mod_ref, g_ref, w_ref, wk_ref, lbl_ref,
                   proj_ref, fgate_ref, kpe_ref, h_scr, *, layer):
    j = pl.program_id(1)

    @pl.when(j == 0)
    def _():
        x = x_ref[...]
        r = lax.rsqrt(jnp.mean(x * x, axis=-1, keepdims=True) + EPS)
        h = x * r * g_ref[...] * (1.0 + mod_ref[1:2, :]) + mod_ref[0:1, :]
        hb = h.astype(BF16)
        h_scr[...] = hb
        kpe_ref[...] = _dot(hb, wk_ref[...])

    acc = _dot(h_scr[...], w_ref[...])
    proj_ref[...] = acc.astype(BF16)

    @pl.when((j == 1) | (j == 2))
    def _():
        lg = lbl_ref[...]
        e = jnp.exp(lg - jnp.max(lg, axis=0, keepdims=True))
        lb = jnp.sum(e[:layer + 1], axis=0, keepdims=True) / jnp.sum(e, axis=0, keepdims=True)
        fgate_ref[...] = lb + (1.0 - lb) * jax.nn.sigmoid(acc)


def _inproj(xf, mod6, g1, w_in_b, wk_b, lb_logits, seq, hk, layer):
    m, d = xf.shape
    d_in = w_in_b.shape[1]
    tm = min(seq, 1024)
    tpb = seq // tm
    tn = hk
    nj = pl.cdiv(d_in, tn)
    nl = lb_logits.shape[1]
    fdir = lambda j: jnp.clip(j - 1, 0, 1)
    return pl.pallas_call(
        functools.partial(_inproj_kernel, layer=layer),
        out_shape=(jax.ShapeDtypeStruct((m, d_in), BF16),
                   jax.ShapeDtypeStruct((m, 2 * hk), F32),
                   jax.ShapeDtypeStruct((m, LANES), F32)),
        grid=(m // tm, nj),
        in_specs=[pl.BlockSpec((tm, d), lambda i, j: (i, 0)),
                  pl.BlockSpec((None, 6, d), lambda i, j: (i // tpb, 0, 0)),
                  pl.BlockSpec((1, d), lambda i, j: (0, 0)),
                  pl.BlockSpec((d, tn), lambda i, j: (0, j)),
                  pl.BlockSpec((d, LANES), lambda i, j: (0, 0)),
                  pl.BlockSpec((None, nl, tn), lambda i, j: (fdir(j), 0, 0))],
        out_specs=(pl.BlockSpec((tm, tn), lambda i, j: (i, j)),
                   pl.BlockSpec((tm, tn), lambda i, j: (i, fdir(j))),
                   pl.BlockSpec((tm, LANES), lambda i, j: (i, 0))),
        scratch_shapes=[pltpu.VMEM((tm, d), BF16)],
        compiler_params=_cparams("parallel", "arbitrary"),
        name="norm1_inproj",
    )(xf, mod6, g1, w_in_b, wk_b, lb_logits)


def _tile_scan(g, d):
    rin = lax.broadcasted_iota(jnp.int32, g.shape, 0) & (TILE - 1)
    b = g
    step = 1
    while step < TILE:
        if d == 0:
            b = b + jnp.where(rin >= step, pltpu.roll(b, step, 0), 0.0)
        else:
            b = b + jnp.where(rin < TILE - step, pltpu.roll(b, GRP - step, 0), 0.0)
        step *= 2
    return b


def _group_cumsum(g, d):
    b = _tile_scan(g, d)
    ntile = GRP // TILE
    order = range(ntile) if d == 0 else range(ntile - 1, -1, -1)
    edge = TILE - 1 if d == 0 else 0
    out = [None] * ntile
    carry = None
    for i in order:
        t = b[i * TILE:(i + 1) * TILE]
        out[i] = t if carry is None else t + carry
        tot = t[edge:edge + 1]
        carry = tot if carry is None else carry + tot
    return jnp.concatenate(out, axis=0)


def _boundary(b, h, d):
    idx = h - 1 if d == 0 else h
    if 2 * h >= TILE:
        b3 = b.reshape(GRP // (2 * h), 2 * h, b.shape[1])
        return jnp.broadcast_to(b3[:, idx:idx + 1, :], b3.shape).reshape(b.shape)
    p = lax.broadcasted_iota(jnp.int32, b.shape, 0) & (2 * h - 1)
    out = b
    for pos in range(2 * h):
        shift = pos - idx
        if shift != 0:
            out = jnp.where(p == pos, pltpu.roll(b, shift % GRP, 0), out)
    return out


def _hgrn_kernel(q_ref, v_ref, gf_ref, gb_ref, o_ref, lv_scr, sg_scr, st_scr):
    seq = q_ref.shape[0]
    ngrp = seq // GRP
    nlev = GRP.bit_length()
    g_refs = (gf_ref, gb_ref)
    assert ngrp % 2 == 0

    r = lax.broadcasted_iota(jnp.int32, (GRP, GRP), 0)
    c = lax.broadcasted_iota(jnp.int32, (GRP, GRP), 1)
    lev = jnp.zeros((GRP, GRP), jnp.int32)
    for j in range(nlev - 1):
        lev = lev + jnp.where((r >> j) != (c >> j), 1, 0)
    lv_scr[0] = jnp.where(c <= r, lev, -1)
    lv_scr[1] = jnp.where(c >= r, lev, -1)
    st_scr[...] = jnp.zeros_like(st_scr)
    rr = lax.broadcasted_iota(jnp.int32, (GRP, LANES), 0)
    for l in range(1, nlev):
        late = (rr & (1 << (l - 1))) != 0
        sg_scr[0, l - 1] = jnp.where(late, 1.0, -1.0)
        sg_scr[1, l - 1] = jnp.where(late, -1.0, 1.0)

    def body(i, carry, first):
        for d in (0, 1):
            grp = i if d == 0 else ngrp - 1 - i
            r0 = pl.multiple_of(grp * GRP, GRP)
            f = g_refs[d][pl.ds(r0, GRP), :]
            g = jnp.log2(f)
            qb = q_ref[pl.ds(r0, GRP), :]
            vb = v_ref[pl.ds(r0, GRP), :]
            qf = qb.astype(F32)
            kk = 1.0 - f
            kb = kk.astype(BF16)
            b = _group_cumsum(g, d)
            edge = GRP - 1 if d == 0 else 0
            tot = b[edge:edge + 1]
            lv = lv_scr[d]
            att = jnp.where(lv == 0, _dot_nt(qb, kb), 0.0)
            for l in range(1, nlev):
                x = jnp.exp2((b - _boundary(b, 1 << (l - 1), d)) * sg_scr[d, l - 1]).astype(BF16)
                att = jnp.where(lv == l, _dot_nt(qb * x, kb * x), att)
            st = st_scr[d]
            o = _dot(att.astype(BF16), vb) + _dot_nt((qf * jnp.exp2(b)).astype(BF16), st.astype(BF16))
            if first:
                o_ref[pl.ds(r0, GRP), :] = o
            else:
                o_ref[pl.ds(r0, GRP), :] += o
            st_scr[d] = st * jnp.exp2(tot) + _dot_tn(vb, (kk * jnp.exp2(tot - b)).astype(BF16))
        return carry

    half = ngrp // 2
    lax.fori_loop(0, half, functools.partial(body, first=True), 0, unroll=min(8, half))
    lax.fori_loop(half, ngrp, functools.partial(body, first=False), 0, unroll=min(8, half))


def _hgrn(proj, fgate, batch, seq, heads, hk):
    m = proj.shape[0]
    nh = hk // LANES
    vcol = 3 * nh
    blk = lambda off: pl.BlockSpec((seq, LANES), lambda b, h: (b, off + h))
    out = jax.ShapeDtypeStruct((m, hk), F32)
    return pl.pallas_call(
        _hgrn_kernel,
        out_shape=out,
        grid=(batch, heads),
        in_specs=[blk(0), blk(vcol), blk(0), blk(nh)],
        out_specs=blk(0),
        scratch_shapes=[pltpu.VMEM((2, GRP, GRP), jnp.int32),
                        pltpu.VMEM((2, GRP.bit_length() - 1, GRP, LANES), F32),
                        pltpu.VMEM((2, LANES, LANES), F32)],
        compiler_params=_cparams("parallel", "parallel"),
        name="hgrn2_scan",
    )(proj, proj, fgate, fgate)


def _mla_proj_kernel(cq_ref, ckv_ref, kpe_ref, cs_ref, qag_ref, kvag_ref, wq_ref, wkv_ref,
                     qgn_ref, qgr_ref, kgn_ref, kgr_ref, q_out, k_out, v_out,
                     *, scale, qk_dim, rope, heads):
    cq = cq_ref[...].astype(F32)
    a = (cq * lax.rsqrt(jnp.mean(cq * cq, axis=-1, keepdims=True) + EPS) * qag_ref[...]).astype(BF16)
    ckv = ckv_ref[...].astype(F32)
    c = (ckv * lax.rsqrt(jnp.mean(ckv * ckv, axis=-1, keepdims=True) + EPS)
         * kvag_ref[...]).astype(BF16)
    qall = _dot(a, wq_ref[...])
    kvall = _dot(c, wkv_ref[...])

    cs = cs_ref[...]
    lane = lax.broadcasted_iota(jnp.int32, cs.shape, 1)
    lo = lane < rope

    def rope_sumsq(rr):
        return jnp.sum(jnp.where(lo, rr * rr, 0.0), axis=-1, keepdims=True)

    def rotate(rr, gr):
        y = rr * gr * cs
        return y + pltpu.roll(y, rope, 1)

    kpe = kpe_ref[...]
    k_ss = rope_sumsq(kpe)
    k_rot = rotate(kpe, kgr_ref[...])
    for h in range(heads):
        base = 2 * LANES * h
        qn = qall[:, base:base + LANES]
        qr = qall[:, base + LANES:base + 2 * LANES]
        rq = lax.rsqrt((jnp.sum(qn * qn, axis=-1, keepdims=True) + rope_sumsq(qr)) / qk_dim + EPS) * scale
        q_out[h, :, :LANES] = (qn * qgn_ref[...] * rq).astype(BF16)
        q_out[h, :, LANES:] = jnp.where(lo, rotate(qr, qgr_ref[...]) * rq, 0.0).astype(BF16)
        kn = kvall[:, base:base + LANES]
        rk = lax.rsqrt((jnp.sum(kn * kn, axis=-1, keepdims=True) + k_ss) / qk_dim + EPS)
        k_out[h, :, :LANES] = (kn * kgn_ref[...] * rk).astype(BF16)
        k_out[h, :, LANES:] = jnp.where(lo, k_rot * rk, 0.0).astype(BF16)
        v_out[h] = kvall[:, base + LANES:base + 2 * LANES].astype(BF16)


def _mla_proj(proj, kpe2, cs, qag, kvag, wq_all, wkv_all, qgn, qgr, kgn, kgr,
              batch, seq, heads, cq_off, ckv_off, qk_dim, rope):
    m = proj.shape[0]
    ql, kvl = wq_all.shape[0], wkv_all.shape[0]
    tm = min(seq, 256)
    tpb = seq // tm
    assert cq_off % ql == 0 and ckv_off % kvl == 0
    vec = lambda n: pl.BlockSpec((1, n), lambda i: (0, 0))
    full = lambda w: pl.BlockSpec(w.shape, lambda i: (0, 0))
    hspec = lambda n: pl.BlockSpec((None, heads, tm, n), lambda i: (i // tpb, 0, i % tpb, 0))
    scale = qk_dim ** -0.5 * LOG2E
    return pl.pallas_call(
        functools.partial(_mla_proj_kernel, scale=scale, qk_dim=float(qk_dim), rope=rope, heads=heads),
        out_shape=(jax.ShapeDtypeStruct((batch, heads, seq, 2 * LANES), BF16),
                   jax.ShapeDtypeStruct((batch, heads, seq, 2 * LANES), BF16),
                   jax.ShapeDtypeStruct((batch, heads, seq, LANES), BF16)),
        grid=(m // tm,),
        in_specs=[pl.BlockSpec((tm, ql), lambda i: (i, cq_off // ql)),
                  pl.BlockSpec((tm, kvl), lambda i: (i, ckv_off // kvl)),
                  pl.BlockSpec((tm, LANES), lambda i: (i, 0)),
                  pl.BlockSpec((tm, LANES), lambda i: (i, 0)),
                  vec(ql), vec(kvl), full(wq_all), full(wkv_all),
                  vec(LANES), vec(LANES), vec(LANES), vec(LANES)],
        out_specs=(hspec(2 * LANES), hspec(2 * LANES), hspec(LANES)),
        compiler_params=_cparams("parallel"),
        name="mla_head_proj",
    )(proj, proj, kpe2, cs, qag, kvag, wq_all, wkv_all, qgn, qgr, kgn, kgr)


ATTN_KEYS = 1024


def _attn_kernel(q_ref, k_ref, v_ref, o_ref):
    q = q_ref[...]
    seq = k_ref.shape[0]
    kc = min(ATTN_KEYS, seq)
    m = l = acc = None
    for c in range(seq // kc):
        rows = slice(c * kc, (c + 1) * kc)
        s = _dot_nt(q, k_ref[rows, :])
        mc = jnp.max(s, axis=-1, keepdims=True)
        if c == 0:
            m = mc
            p = jnp.exp2(s - m)
            l = jnp.sum(p, axis=-1, keepdims=True)
            acc = _dot(p.astype(BF16), v_ref[rows, :])
        else:
            m_new = jnp.maximum(m, mc)
            alpha = jnp.exp2(m - m_new)
            p = jnp.exp2(s - m_new)
            l = l * alpha + jnp.sum(p, axis=-1, keepdims=True)
            acc = acc * alpha + _dot(p.astype(BF16), v_ref[rows, :])
            m = m_new
    o_ref[...] = (acc / l).astype(BF16)


def _attention(qh, kh, vh):
    batch, mh, seq, dq = qh.shape
    dv = vh.shape[-1]
    tq = min(seq, 2048)
    nq = seq // tq
    return pl.pallas_call(
        _attn_kernel,
        out_shape=jax.ShapeDtypeStruct((batch * seq, mh * dv), BF16),
        grid=(batch, mh, nq),
        in_specs=[pl.BlockSpec((None, None, tq, dq), lambda b, h, i: (b, h, i, 0)),
                  pl.BlockSpec((None, None, seq, dq), lambda b, h, i: (b, h, 0, 0)),
                  pl.BlockSpec((None, None, seq, dv), lambda b, h, i: (b, h, 0, 0))],
        out_specs=pl.BlockSpec((tq, dv), lambda b, h, i: (b * nq + i, h)),
        compiler_params=_cparams("parallel", "parallel", "arbitrary"),
        name="mla_attention",
    )(qh, kh, vh)


def _eye(rows, cols):
    r = lax.broadcasted_iota(jnp.int32, (rows, cols), 0)
    c = lax.broadcasted_iota(jnp.int32, (rows, cols), 1)
    return jnp.where(r == c, 1.0, 0.0).astype(BF16)


def _outproj_kernel(o_ref, hg_ref, og_ref, om_ref, w_ref, x_ref, mod_ref, g2_ref, wr_ref,
                    x1_ref, h2_ref, aff_ref, lat_ref, mix_scr, *, heads, n_exp):
    hw = o_ref.shape[1]
    o = o_ref[...]
    gate = _silu(hg_ref[...].astype(F32))
    for h in range(heads):
        sl = slice(h * LANES, (h + 1) * LANES)
        oh = o[:, sl]
        r = lax.rsqrt(jnp.mean(oh * oh, axis=-1, keepdims=True) + EPS)
        mix_scr[:, sl] = (oh * r * og_ref[:, sl] * gate[:, sl]).astype(BF16)
    mix_scr[:, hw:] = om_ref[...]
    x1 = x_ref[...] + mod_ref[2:3, :] * _dot(mix_scr[...], w_ref[...])
    x1_ref[...] = x1
    r2 = lax.rsqrt(jnp.mean(x1 * x1, axis=-1, keepdims=True) + EPS)
    h2 = x1 * r2 * g2_ref[...] * (1.0 + mod_ref[4:5, :]) + mod_ref[3:4, :]
    h2_ref[...] = h2
    logits = _dot_hi(h2, wr_ref[...])
    lane = lax.broadcasted_iota(jnp.int32, logits.shape, 1)
    logits = jnp.where(lane < n_exp, logits, -jnp.inf)
    z = logits - jnp.max(logits, axis=-1, keepdims=True)
    p = jnp.exp(z)
    sp = jnp.sum(p, axis=-1, keepdims=True)
    aff_ref[...] = p / sp
    la = jnp.where(lane < n_exp, z - jnp.log(sp), 0.0)
    eye = _eye(n_exp, la.shape[1])
    p1, p2, p3 = _split3(la)
    lat_ref[...] = (_dot_nt(eye, p1) + _dot_nt(eye, p2)) + _dot_nt(eye, p3)


def _outproj(o_hgrn, proj, og, o_mla, w_out_b, xf, mod6, g2, wr_pad, seq, heads, hk, n_exp):
    m, d = xf.shape
    hw = o_hgrn.shape[1]
    mw = o_mla.shape[1]
    tm = min(seq, 256)
    tpb = seq // tm
    gcol = (3 * hk + hw) // hw
    assert (3 * hk + hw) % hw == 0
    row = lambda n: pl.BlockSpec((tm, n), lambda i: (i, 0))
    return pl.pallas_call(
        functools.partial(_outproj_kernel, heads=heads, n_exp=n_exp),
        out_shape=(jax.ShapeDtypeStruct((m, d), F32),
                   jax.ShapeDtypeStruct((m, d), F32),
                   jax.ShapeDtypeStruct((m, LANES), F32),
                   jax.ShapeDtypeStruct((m // seq, n_exp, seq), F32)),
        grid=(m // tm,),
        in_specs=[row(hw),
                  pl.BlockSpec((tm, hw), lambda i: (i, gcol)),
                  pl.BlockSpec((1, hw), lambda i: (0, 0)),
                  row(mw),
                  pl.BlockSpec((hw + mw, d), lambda i: (0, 0), pipeline_mode=pl.Buffered(1)),
                  row(d),
                  pl.BlockSpec((None, 6, d), lambda i: (i // tpb, 0, 0)),
                  pl.BlockSpec((1, d), lambda i: (0, 0)),
                  pl.BlockSpec((d, LANES), lambda i: (0, 0), pipeline_mode=pl.Buffered(1))],
        out_specs=(row(d), row(d), row(LANES),
                   pl.BlockSpec((None, n_exp, tm), lambda i: (i // tpb, 0, i % tpb))),
        scratch_shapes=[pltpu.VMEM((tm, hw + mw), BF16)],
        compiler_params=_cparams("parallel"),
        name="outproj_norm2_router",
    )(o_hgrn, proj, og, o_mla, w_out_b, xf, mod6, g2, wr_pad)


BISECT_STEPS = 64


COMBINE_TILE = 256
COMBINE_WIN = 64


def _topk_kernel(la_ref, slot_se_ref, idx_ref, tab_ref, tri_scr, cum_scr, *, cap, n_exp):
    nrow, seq = la_ref.shape
    ep = slot_se_ref.shape[1]
    rows = 256
    for k in range(seq // rows):
        r = lax.broadcasted_iota(jnp.int32, (rows, seq), 0) + k * rows
        c = lax.broadcasted_iota(jnp.int32, (rows, seq), 1)
        tri_scr[k * rows:(k + 1) * rows, :] = jnp.where(r < c, 1.0, 0.0).astype(BF16)

    def count(mask):
        return jnp.sum(jnp.where(mask, 1.0, 0.0), axis=-1, keepdims=True)

    def body(_, lh):
        lo, hi = lh
        mid = 0.5 * (lo + hi)
        ok = count(la_ref[...] >= mid) >= cap
        return jnp.where(ok, mid, lo), jnp.where(ok, hi, mid)

    la = la_ref[...]
    lo0 = jnp.min(la, axis=-1, keepdims=True)
    lo, hi = lax.fori_loop(0, BISECT_STEPS, body, (lo0, jnp.ones_like(lo0)))
    above = la >= hi
    tie = (la >= lo) & (la < hi)
    need = cap - count(above)
    tri = tri_scr[...]
    rank = _dot(jnp.where(tie, 1.0, 0.0).astype(BF16), tri)
    sel = above | (tie & (rank < need))
    pos = _dot(jnp.where(sel, 1.0, 0.0).astype(BF16), tri)
    slot = jnp.where(sel, pos, -1.0)
    eye = _eye(n_exp, ep)
    for b in range(nrow // n_exp):
        slot_se_ref[b * seq:(b + 1) * seq, :] = _dot_tn(
            slot[b * n_exp:(b + 1) * n_exp, :].astype(BF16), eye)
    cum_scr[...] = pos + jnp.where(sel, 1.0, 0.0)
    lane = lax.broadcasted_iota(jnp.int32, (nrow, cap), 1)

    def slot_body(c, acc):
        cnt = count(cum_scr[...] <= lax.convert_element_type(c, F32))
        return jnp.where(lane == c, cnt, acc)

    idx = lax.fori_loop(0, cap, slot_body, jnp.zeros((nrow, cap), F32), unroll=4)
    idx_ref[...] = idx.astype(jnp.int32)
    tok = lax.broadcasted_iota(jnp.int32, (nrow, seq), 1)
    tlane = lax.broadcasted_iota(jnp.int32, tab_ref.shape, 1)
    tab = jnp.zeros(tab_ref.shape, F32)
    tile = min(seq, COMBINE_TILE)
    for k in range(seq // tile + 1):
        tab = jnp.where(tlane == k, count(sel & (tok < k * tile)), tab)
    tab_ref[...] = tab.astype(jnp.int32)


def _topk(lat, batch, seq, n_exp, cap):
    return pl.pallas_call(
        functools.partial(_topk_kernel, cap=cap, n_exp=n_exp),
        out_shape=(jax.ShapeDtypeStruct((batch * seq, LANES), F32),
                   jax.ShapeDtypeStruct((batch * n_exp, cap), jnp.int32),
                   jax.ShapeDtypeStruct((batch * n_exp, LANES), jnp.int32)),
        scratch_shapes=[pltpu.VMEM((seq, seq), BF16), pltpu.VMEM((batch * n_exp, seq), F32)],
        compiler_params=pltpu.CompilerParams(vmem_limit_bytes=VMEM_LIMIT),
        name="expert_choice_topk",
    )(lat.reshape(batch * n_exp, seq))


def _ffn_kernel(idx_ref, h2_hbm, wg_ref, wu_ref, wd_ref, ye_ref, xe_scr, hmid_scr, sem,
                *, nt, nd, tf):
    e = pl.program_id(0)
    s = pl.program_id(1)
    rows = xe_scr.shape[0]

    def start_gather(expert):
        base = expert * rows

        def body(k, carry):
            r0 = pl.multiple_of(k * TILE, TILE)
            for j in range(TILE):
                pltpu.make_async_copy(h2_hbm.at[pl.ds(idx_ref[base + r0 + j], 1), :],
                                      xe_scr.at[pl.ds(r0 + j, 1), :], sem.at[0]).start()
            return carry
        lax.fori_loop(0, rows // TILE, body, 0)

    @pl.when((e == 0) & (s == 0))
    def _():
        start_gather(0)

    @pl.when(s == 0)
    def _():
        pltpu.make_async_copy(h2_hbm.at[pl.ds(0, rows), :], xe_scr, sem.at[0]).wait()

    @pl.when(s < nt)
    def _():
        xe = xe_scr[...].astype(BF16)
        a = _dot(xe, wg_ref[...].astype(BF16))
        u = _dot(xe, wu_ref[...].astype(BF16))
        hmid_scr[s] = (_silu(a) * u).astype(BF16)

    def down_step(prefetch):
        per = rows // (nd * nt)
        y = None
        for k in range(nt):
            if prefetch:
                dst0 = (s - nt) * (per * nt) + k * per
                first = (e + 1) * rows + dst0
                for j in range(per):
                    pltpu.make_async_copy(h2_hbm.at[pl.ds(idx_ref[first + j], 1), :],
                                          xe_scr.at[pl.ds(dst0 + j, 1), :], sem.at[0]).start()
            part = _dot(hmid_scr[k], wd_ref[k * tf:(k + 1) * tf, :].astype(BF16))
            y = part if y is None else y + part
        ye_ref[...] = y.astype(BF16)

    more = e + 1 < pl.num_programs(0)

    @pl.when((s >= nt) & more)
    def _():
        down_step(True)

    @pl.when((s >= nt) & jnp.logical_not(more))
    def _():
        down_step(False)


def _ffn(idx, h2, w_gate, w_up, w_down):
    n_exp, rows = idx.shape
    idx = idx.reshape(n_exp * rows)
    d = h2.shape[1]
    ff = w_gate.shape[2]
    tf = min(ff, 512)
    tn = min(d, 1024)
    nt = ff // tf
    nd = d // tn
    assert rows % (nd * nt) == 0
    up = lambda e, s, idx: (e, 0, jnp.minimum(s, nt - 1))
    down = lambda e, s, idx: (e, 0, jnp.maximum(s - nt, 0))
    return pl.pallas_call(
        functools.partial(_ffn_kernel, nt=nt, nd=nd, tf=tf),
        out_shape=jax.ShapeDtypeStruct((n_exp, rows, d), BF16),
        grid_spec=pltpu.PrefetchScalarGridSpec(
            num_scalar_prefetch=1,
            grid=(n_exp, nt + nd),
            in_specs=[pl.BlockSpec(memory_space=pl.ANY),
                      pl.BlockSpec((None, d, tf), up),
                      pl.BlockSpec((None, d, tf), up),
                      pl.BlockSpec((None, ff, tn), down)],
            out_specs=pl.BlockSpec((None, rows, tn), down),
            scratch_shapes=[pltpu.VMEM((rows, d), F32),
                            pltpu.VMEM((nt, rows, tf), BF16),
                            pltpu.SemaphoreType.DMA((1,))]),
        compiler_params=_cparams("arbitrary", "arbitrary"),
        name="expert_swiglu",
    )(idx, h2, w_gate, w_up, w_down)


def _combine_kernel(tab_ref, slot_ref, aff_ref, ye_ref, x1_ref, mod_ref, out_ref, y_scr,
                    *, n_exp, cap, win):
    b = pl.program_id(0)
    t = pl.program_id(1)
    tt = x1_ref.shape[0]
    base = (b * (pl.num_programs(1) + 1) + t) * n_exp
    pack = 16
    starts = []
    short = None
    for e in range(n_exp):
        c0 = tab_ref[base + e]
        c1 = tab_ref[base + n_exp + e]
        a = jnp.minimum(c0 & ~(pack - 1), cap - win)
        ok = c1 - a <= win
        starts.append(a)
        short = ok if short is None else short & ok

    def finish(acc):
        out_ref[...] = x1_ref[...] + mod_ref[5:6, :] * acc

    @pl.when(short)
    def _():
        lane = lax.broadcasted_iota(jnp.int32, (tt, LANES), 1).astype(F32)
        per = LANES // win
        blocks = []
        for g in range(n_exp // per):
            blk = jnp.zeros((tt, LANES), F32)
            for j in range(per):
                e = g * per + j
                a = pl.multiple_of(starts[e], pack)
                y_scr[e * win:(e + 1) * win, :] = ye_ref[e, pl.ds(a, win), :]
                slot = slot_ref[:, e:e + 1]
                rel = jnp.where(slot >= 0.0, slot - a.astype(F32) + float(j * win), -1.0)
                blk = jnp.where(lane == rel, aff_ref[:, e:e + 1], blk)
            blocks.append(blk.astype(BF16))
        finish(_dot(jnp.concatenate(blocks, axis=1), y_scr[...]))

    @pl.when(jnp.logical_not(short))
    def _():
        cidx = lax.broadcasted_iota(jnp.int32, (tt, cap), 1).astype(F32)
        acc = jnp.zeros(x1_ref.shape, F32)
        for e in range(n_exp):
            onehot = jnp.where(cidx == slot_ref[:, e:e + 1], 1.0, 0.0).astype(BF16)
            acc = acc + aff_ref[:, e:e + 1] * _dot(onehot, ye_ref[e])
        finish(acc)


def _combine(tab, slot_se, aff, ye4, x1, mod6, seq, n_exp, cap):
    m, d = x1.shape
    ep = slot_se.shape[1]
    batch = m // seq
    tt = min(seq, COMBINE_TILE)
    tpb = seq // tt
    win = min(COMBINE_WIN, cap)
    assert LANES % win == 0 and n_exp % (LANES // win) == 0 and cap % 16 == 0
    return pl.pallas_call(
        functools.partial(_combine_kernel, n_exp=n_exp, cap=cap, win=win),
        out_shape=jax.ShapeDtypeStruct((m, d), F32),
        grid_spec=pltpu.PrefetchScalarGridSpec(
            num_scalar_prefetch=1,
            grid=(batch, tpb),
            in_specs=[pl.BlockSpec((tt, ep), lambda b, t, tab: (b * tpb + t, 0)),
                      pl.BlockSpec((tt, ep), lambda b, t, tab: (b * tpb + t, 0)),
                      pl.BlockSpec((n_exp, None, cap, d), lambda b, t, tab: (0, b, 0, 0)),
                      pl.BlockSpec((tt, d), lambda b, t, tab: (b * tpb + t, 0)),
                      pl.BlockSpec((None, 6, d), lambda b, t, tab: (b, 0, 0))],
            out_specs=pl.BlockSpec((tt, d), lambda b, t, tab: (b * tpb + t, 0)),
            scratch_shapes=[pltpu.VMEM((n_exp * win, d), BF16)]),
        compiler_params=_cparams("parallel", "arbitrary"),
        name="expert_combine",
    )(tab, slot_se, aff, ye4, x1, mod6)


def kernel(x, c, positions, w_ada, b_ada, norm1_g, w_in, lb_logits, hgrn_out_g, qa_norm_g, w_uq,
           kva_norm_g, w_ukv, q_head_g, k_head_g, w_out, norm2_g, w_router, w_gate, w_up, w_down):
    batch, seq, d = x.shape
    depth = w_ada.shape[0]
    m = batch * seq
    hk = lb_logits.shape[2]
    heads, dv = hgrn_out_g.shape[1], hgrn_out_g.shape[2]
    hw = heads * dv
    ql, kvl = qa_norm_g.shape[1], kva_norm_g.shape[1]
    qk_dim = q_head_g.shape[1]
    mh = w_uq.shape[2] // qk_dim
    d_in = w_in.shape[2]
    rope = d_in - (3 * hk + 2 * hw + ql + kvl)
    nope = qk_dim - rope
    vdim = w_ukv.shape[2] // mh - nope
    n_exp = w_router.shape[2]
    cap = EC_CAPACITY * seq // n_exp
    assert dv == LANES and hk == hw and nope == LANES and vdim == LANES and 2 * rope == LANES
    assert ql + kvl + rope <= hk and seq % GRP == 0 and n_exp <= LANES and cap % 8 == 0

    cq_off = 3 * hk + 2 * hw
    ckv_off = cq_off + ql
    kpe_off = ckv_off + kvl
    swap = jnp.concatenate([jnp.arange(rope // 2, rope), jnp.arange(0, rope // 2)])

    def both(v):
        return jnp.concatenate([v, v[..., swap]], axis=-1)

    cs = _rope_tables(positions, rope)
    c8 = jnp.pad(c, ((0, (-batch) % 8), (0, 0)))
    xf = x.reshape(m, d)
    for l in range(depth):
        mod6 = _ada(c8, w_ada[l], b_ada[l])[:batch].reshape(batch, 6, d)

        w_in_b = _to_bf16(w_in[l])
        wk_b = both(w_in[l][:, kpe_off:kpe_off + rope]).astype(BF16)
        proj, fgate, kpe2 = _inproj(xf, mod6, norm1_g[l].reshape(1, d), w_in_b, wk_b, lb_logits,
                                   seq, hk, l)

        o_hgrn = _hgrn(proj, fgate, batch, seq, heads, hk)

        wq = w_uq[l].reshape(ql, mh, qk_dim)
        wq_all = jnp.concatenate([wq[..., :nope], both(wq[..., nope:])], axis=-1)
        qh, kh, vh = _mla_proj(
            proj, kpe2, cs, qa_norm_g[l].reshape(1, ql), kva_norm_g[l].reshape(1, kvl),
            wq_all.reshape(ql, mh * 2 * LANES).astype(BF16), w_ukv[l].astype(BF16),
            q_head_g[l][:nope].reshape(1, nope), both(q_head_g[l][nope:]).reshape(1, 2 * rope),
            k_head_g[l][:nope].reshape(1, nope), both(k_head_g[l][nope:]).reshape(1, 2 * rope),
            batch, seq, mh, cq_off, ckv_off, qk_dim, rope)
        o_mla = _attention(qh, kh, vh)

        wr_pad = jnp.pad(w_router[l], ((0, 0), (0, LANES - n_exp)))
        x1, h2, aff, lat = _outproj(o_hgrn, proj, hgrn_out_g[l].reshape(1, hw), o_mla,
                                    _to_bf16(w_out[l]), xf, mod6, norm2_g[l].reshape(1, d),
                                    wr_pad, seq, heads, hk, n_exp)

        slot_se, idx, tab = _topk(lat, batch, seq, n_exp, cap)
        ntile = seq // min(seq, COMBINE_TILE)
        tab = tab[:, :ntile + 1].reshape(batch, n_exp, ntile + 1).transpose(0, 2, 1).reshape(-1)
        rows = idx.reshape(batch, n_exp, cap) + (jnp.arange(batch, dtype=jnp.int32) * seq)[:, None, None]
        rows = rows.transpose(1, 0, 2).reshape(n_exp, batch * cap)
        ye = _ffn(rows, h2, w_gate[l], w_up[l], w_down[l])
        xf = _combine(tab, slot_se, aff, ye.reshape(n_exp, batch, cap, d), x1, mod6, seq, n_exp, cap)
    return xf.reshape(batch, seq, d)
```

```python
import functools
import math

import jax
import jax.numpy as jnp
from jax import lax
from jax.experimental import pallas as pl
from jax.experimental.pallas import tpu as pltpu

F32 = jnp.float32
BF16 = jnp.bfloat16
EPS = 1e-6
ROPE_BASE = 10000.0
LOG2E = math.log2(math.e)
EC_CAPACITY = 2
LANES = 128
TILE = 8
GRP = 128
VMEM_LIMIT = 56 * 1024 * 1024


def _cparams(*sem):
    return pltpu.CompilerParams(dimension_semantics=sem, vmem_limit_bytes=VMEM_LIMIT)


def _dot(a, b):
    return jnp.dot(a, b, preferred_element_type=F32)


def _dot_nt(a, b):
    return lax.dot_general(a, b, (((1,), (1,)), ((), ())), preferred_element_type=F32)


def _dot_tn(a, b):
    return lax.dot_general(a, b, (((0,), (0,)), ((), ())), preferred_element_type=F32)


def _split2(a):
    hi = a.astype(BF16)
    lo = (a - hi.astype(F32)).astype(BF16)
    return hi, lo


def _split3(a):
    p1 = a.astype(BF16)
    r1 = a - p1.astype(F32)
    p2 = r1.astype(BF16)
    p3 = (r1 - p2.astype(F32)).astype(BF16)
    return p1, p2, p3


def _dot_hi(a, b):
    ah, al = _split2(a)
    bh, bl = _split2(b)
    return _dot(ah, bh) + (_dot(ah, bl) + _dot(al, bh))


def _silu(x):
    return x * jax.nn.sigmoid(x)


def _rope_kernel(pos_ref, cs_ref, *, half):
    pos = pos_ref[...].astype(F32)
    lane = lax.broadcasted_iota(jnp.int32, (1, 4 * half), 1)
    j = (lane & (half - 1)).astype(F32)
    inv_freq = jnp.exp(j * (-2.0 * math.log(ROPE_BASE) / (2 * half)))
    ang = pos * inv_freq
    c = jnp.cos(ang)
    s = jnp.sin(ang)
    cs_ref[...] = jnp.where(lane < 2 * half, c, jnp.where(lane < 3 * half, -s, s))


def _rope_tables(positions, rope):
    m = positions.size
    tm = min(m, 1024)
    half = rope // 2
    return pl.pallas_call(
        functools.partial(_rope_kernel, half=half),
        out_shape=jax.ShapeDtypeStruct((m, 2 * rope), F32),
        grid=(m // tm,),
        in_specs=[pl.BlockSpec((tm, 1), lambda i: (i, 0))],
        out_specs=pl.BlockSpec((tm, 2 * rope), lambda i: (i, 0)),
        compiler_params=_cparams("parallel"),
        name="rope_tables",
    )(positions.reshape(m, 1))


def _ada_kernel(c_ref, w_ref, b_ref, o_ref):
    part = _dot_hi(_silu(c_ref[...]), w_ref[...])

    @pl.when(pl.program_id(0) == 0)
    def _():
        o_ref[...] = part + b_ref[...]

    @pl.when(pl.program_id(0) > 0)
    def _():
        o_ref[...] += part


def _ada(c8, w, b):
    d, n = w.shape
    tk = min(d, 128)
    return pl.pallas_call(
        _ada_kernel,
        out_shape=jax.ShapeDtypeStruct((c8.shape[0], n), F32),
        grid=(d // tk,),
        in_specs=[pl.BlockSpec((c8.shape[0], tk), lambda k: (0, k)),
                  pl.BlockSpec((tk, n), lambda k: (k, 0)),
                  pl.BlockSpec((1, n), lambda k: (0, 0))],
        out_specs=pl.BlockSpec((c8.shape[0], n), lambda k: (0, 0)),
        compiler_params=_cparams("arbitrary"),
        name="ada_mod",
    )(c8, w, b.reshape(1, n))


def _inproj_kernel(x_ref, mod_ref, g_ref, w_ref, wk_ref, lbl_ref,
                   proj_ref, fgate_ref, kpe_ref, h_scr, *, layer):
    j = pl.program_id(1)

    @pl.when(j == 0)
    def _():
        x = x_ref[...]
        r = lax.rsqrt(jnp.mean(x * x, axis=-1, keepdims=True) + EPS)
        h = x * r * g_ref[...] * (1.0 + mod_ref[1:2, :]) + mod_ref[0:1, :]
        hb = h.astype(BF16)
        h_scr[...] = hb
        kpe_ref[...] = _dot(hb, wk_ref[...])

    acc = _dot(h_scr[...], w_ref[...])
    proj_ref[...] = acc.astype(BF16)

    @pl.when((j == 1) | (j == 2))
    def _():
        lg = lbl_ref[...]
        e = jnp.exp(lg - jnp.max(lg, axis=0, keepdims=True))
        lb = jnp.sum(e[:layer + 1], axis=0, keepdims=True) / jnp.sum(e, axis=0, keepdims=True)
        fgate_ref[...] = lb + (1.0 - lb) * jax.nn.sigmoid(acc)


def _inproj(xf, mod6, g1, w_in_b, wk_b, lb_logits, seq, hk, layer):
    m, d = xf.shape
    d_in = w_in_b.shape[1]
    tm = min(seq, 1024)
    tpb = seq // tm
    tn = hk
    nj = pl.cdiv(d_in, tn)
    nl = lb_logits.shape[1]
    fdir = lambda j: jnp.clip(j - 1, 0, 1)
    return pl.pallas_call(
        functools.partial(_inproj_kernel, layer=layer),
        out_shape=(jax.ShapeDtypeStruct((m, d_in), BF16),
                   jax.ShapeDtypeStruct((m, 2 * hk), F32),
                   jax.ShapeDtypeStruct((m, LANES), F32)),
        grid=(m // tm, nj),
        in_specs=[pl.BlockSpec((tm, d), lambda i, j: (i, 0)),
                  pl.BlockSpec((None, 6, d), lambda i, j: (i // tpb, 0, 0)),
                  pl.BlockSpec((1, d), lambda i, j: (0, 0)),
                  pl.BlockSpec((d, tn), lambda i, j: (0, j)),
                  pl.BlockSpec((d, LANES), lambda i, j: (0, 0)),
                  pl.BlockSpec((None, nl, tn), lambda i, j: (fdir(j), 0, 0))],
        out_specs=(pl.BlockSpec((tm, tn), lambda i, j: (i, j)),
                   pl.BlockSpec((tm, tn), lambda i, j: (i, fdir(j))),
                   pl.BlockSpec((tm, LANES), lambda i, j: (i, 0))),
        scratch_shapes=[pltpu.VMEM((tm, d), BF16)],
        compiler_params=_cparams("parallel", "arbitrary"),
        name="norm1_inproj",
    )(xf, mod6, g1, w_in_b, wk_b, lb_logits)


def _tile_scan(g, d):
    rin = lax.broadcasted_iota(jnp.int32, g.shape, 0) & (TILE - 1)
    b = g
    step = 1
    while step < TILE:
        if d == 0:
            b = b + jnp.where(rin >= step, pltpu.roll(b, step, 0), 0.0)
        else:
            b = b + jnp.where(rin < TILE - step, pltpu.roll(b, GRP - step, 0), 0.0)
        step *= 2
    return b


def _group_cumsum(g, d):
    b = _tile_scan(g, d)
    ntile = GRP // TILE
    order = range(ntile) if d == 0 else range(ntile - 1, -1, -1)
    edge = TILE - 1 if d == 0 else 0
    out = [None] * ntile
    carry = None
    for i in order:
        t = b[i * TILE:(i + 1) * TILE]
        out[i] = t if carry is None else t + carry
        tot = t[edge:edge + 1]
        carry = tot if carry is None else carry + tot
    return jnp.concatenate(out, axis=0)


def _boundary(b, h, d):
    idx = h - 1 if d == 0 else h
    if 2 * h >= TILE:
        b3 = b.reshape(GRP // (2 * h), 2 * h, b.shape[1])
        return jnp.broadcast_to(b3[:, idx:idx + 1, :], b3.shape).reshape(b.shape)
    p = lax.broadcasted_iota(jnp.int32, b.shape, 0) & (2 * h - 1)
    out = b
    for pos in range(2 * h):
        shift = pos - idx
        if shift != 0:
            out = jnp.where(p == pos, pltpu.roll(b, shift % GRP, 0), out)
    return out


def _hgrn_kernel(q_ref, v_ref, gf_ref, gb_ref, o_ref, lv_scr, sg_scr, st_scr):
    seq = q_ref.shape[0]
    ngrp = seq // GRP
    nlev = GRP.bit_length()
    g_refs = (gf_ref, gb_ref)
    assert ngrp % 2 == 0

    r = lax.broadcasted_iota(jnp.int32, (GRP, GRP), 0)
    c = lax.broadcasted_iota(jnp.int32, (GRP, GRP), 1)
    lev = jnp.zeros((GRP, GRP), jnp.int32)
    for j in range(nlev - 1):
        lev = lev + jnp.where((r >> j) != (c >> j), 1, 0)
    lv_scr[0] = jnp.where(c <= r, lev, -1)
    lv_scr[1] = jnp.where(c >= r, lev, -1)
    st_scr[...] = jnp.zeros_like(st_scr)
    rr = lax.broadcasted_iota(jnp.int32, (GRP, LANES), 0)
    for l in range(1, nlev):
        late = (rr & (1 << (l - 1))) != 0
        sg_scr[0, l - 1] = jnp.where(late, 1.0, -1.0)
        sg_scr[1, l - 1] = jnp.where(late, -1.0, 1.0)

    def body(i, carry, first):
        for d in (0, 1):
            grp = i if d == 0 else ngrp - 1 - i
            r0 = pl.multiple_of(grp * GRP, GRP)
            f = g_refs[d][pl.ds(r0, GRP), :]
            g = jnp.log2(f)
            qb = q_ref[pl.ds(r0, GRP), :]
            vb = v_ref[pl.ds(r0, GRP), :]
            qf = qb.astype(F32)
            kk = 1.0 - f
            kb = kk.astype(BF16)
            b = _group_cumsum(g, d)
            edge = GRP - 1 if d == 0 else 0
            tot = b[edge:edge + 1]
            lv = lv_scr[d]
            att = jnp.where(lv == 0, _dot_nt(qb, kb), 0.0)
            for l in range(1, nlev):
                x = jnp.exp2((b - _boundary(b, 1 << (l - 1), d)) * sg_scr[d, l - 1]).astype(BF16)
                att = jnp.where(lv == l, _dot_nt(qb * x, kb * x), att)
            st = st_scr[d]
            o = _dot(att.astype(BF16), vb) + _dot_nt((qf * jnp.exp2(b)).astype(BF16), st.astype(BF16))
            if first:
                o_ref[pl.ds(r0, GRP), :] = o
            else:
                o_ref[pl.ds(r0, GRP), :] += o
            st_scr[d] = st * jnp.exp2(tot) + _dot_tn(vb, (kk * jnp.exp2(tot - b)).astype(BF16))
        return carry

    half = ngrp // 2
    lax.fori_loop(0, half, functools.partial(body, first=True), 0, unroll=min(8, half))
    lax.fori_loop(half, ngrp, functools.partial(body, first=False), 0, unroll=min(8, half))


def _hgrn(proj, fgate, batch, seq, heads, hk):
    m = proj.shape[0]
    nh = hk // LANES
    vcol = 3 * nh
    blk = lambda off: pl.BlockSpec((seq, LANES), lambda b, h: (b, off + h))
    out = jax.ShapeDtypeStruct((m, hk), F32)
    return pl.pallas_call(
        _hgrn_kernel,
        out_shape=out,
        grid=(batch, heads),
        in_specs=[blk(0), blk(vcol), blk(0), blk(nh)],
        out_specs=blk(0),
        scratch_shapes=[pltpu.VMEM((2, GRP, GRP), jnp.int32),
                        pltpu.VMEM((2, GRP.bit_length() - 1, GRP, LANES), F32),
                        pltpu.VMEM((2, LANES, LANES), F32)],
        compiler_params=_cparams("parallel", "parallel"),
        name="hgrn2_scan",
    )(proj, proj, fgate, fgate)


def _mla_proj_kernel(cq_ref, ckv_ref, kpe_ref, cs_ref, qag_ref, kvag_ref, wq_ref, wkv_ref,
                     qgn_ref, qgr_ref, kgn_ref, kgr_ref, q_out, k_out, v_out,
                     *, scale, qk_dim, rope, heads):
    cq = cq_ref[...].astype(F32)
    a = (cq * lax.rsqrt(jnp.mean(cq * cq, axis=-1, keepdims=True) + EPS) * qag_ref[...]).astype(BF16)
    ckv = ckv_ref[...].astype(F32)
    c = (ckv * lax.rsqrt(jnp.mean(ckv * ckv, axis=-1, keepdims=True) + EPS)
         * kvag_ref[...]).astype(BF16)
    cs = cs_ref[...]
    lane = lax.broadcasted_iota(jnp.int32, cs.shape, 1)
    lo = lane < rope

    def rope_sumsq(rr):
        return jnp.sum(jnp.where(lo, rr * rr, 0.0), axis=-1, keepdims=True)

    def rotate(rr, gr):
        y = rr * gr * cs
        return y + pltpu.roll(y, rope, 1)

    kpe = kpe_ref[...]
    k_ss = rope_sumsq(kpe)
    k_rot = rotate(kpe, kgr_ref[...])
    for h in range(heads):
        cols = slice(2 * LANES * h, 2 * LANES * (h + 1))
        qall = _dot(a, wq_ref[:, cols])
        kvall = _dot(c, wkv_ref[:, cols])
        base = 0
        qn = qall[:, base:base + LANES]
        qr = qall[:, base + LANES:base + 2 * LANES]
        rq = lax.rsqrt((jnp.sum(qn * qn, axis=-1, keepdims=True) + rope_sumsq(qr)) / qk_dim + EPS) * scale
        q_out[h, :, :LANES] = (qn * qgn_ref[...] * rq).astype(BF16)
        q_out[h, :, LANES:] = jnp.where(lo, rotate(qr, qgr_ref[...]) * rq, 0.0).astype(BF16)
        kn = kvall[:, base:base + LANES]
        rk = lax.rsqrt((jnp.sum(kn * kn, axis=-1, keepdims=True) + k_ss) / qk_dim + EPS)
        k_out[h, :, :LANES] = (kn * kgn_ref[...] * rk).astype(BF16)
        k_out[h, :, LANES:] = jnp.where(lo, k_rot * rk, 0.0).astype(BF16)
        v_out[h] = kvall[:, base + LANES:base + 2 * LANES].astype(BF16)


def _mla_proj(proj, kpe2, cs, qag, kvag, wq_all, wkv_all, qgn, qgr, kgn, kgr,
              batch, seq, heads, cq_off, ckv_off, qk_dim, rope):
    m = proj.shape[0]
    ql, kvl = wq_all.shape[0], wkv_all.shape[0]
    tm = min(seq, 256)
    tpb = seq // tm
    assert cq_off % ql == 0 and ckv_off % kvl == 0
    vec = lambda n: pl.BlockSpec((1, n), lambda i: (0, 0))
    full = lambda w: pl.BlockSpec(w.shape, lambda i: (0, 0))
    hspec = lambda n: pl.BlockSpec((None, heads, tm, n), lambda i: (i // tpb, 0, i % tpb, 0))
    scale = qk_dim ** -0.5 * LOG2E
    return pl.pallas_call(
        functools.partial(_mla_proj_kernel, scale=scale, qk_dim=float(qk_dim), rope=rope, heads=heads),
        out_shape=(jax.ShapeDtypeStruct((batch, heads, seq, 2 * LANES), BF16),
                   jax.ShapeDtypeStruct((batch, heads, seq, 2 * LANES), BF16),
                   jax.ShapeDtypeStruct((batch, heads, seq, LANES), BF16)),
        grid=(m // tm,),
        in_specs=[pl.BlockSpec((tm, ql), lambda i: (i, cq_off // ql)),
                  pl.BlockSpec((tm, kvl), lambda i: (i, ckv_off // kvl)),
                  pl.BlockSpec((tm, LANES), lambda i: (i, 0)),
                  pl.BlockSpec((tm, LANES), lambda i: (i, 0)),
                  vec(ql), vec(kvl), full(wq_all), full(wkv_all),
                  vec(LANES), vec(LANES), vec(LANES), vec(LANES)],
        out_specs=(hspec(2 * LANES), hspec(2 * LANES), hspec(LANES)),
        compiler_params=_cparams("parallel"),
        name="mla_head_proj",
    )(proj, proj, kpe2, cs, qag, kvag, wq_all, wkv_all, qgn, qgr, kgn, kgr)


ATTN_KEYS = 1024


def _attn_kernel(q_ref, k_ref, v_ref, o_ref):
    q = q_ref[...]
    seq = k_ref.shape[0]
    kc = min(ATTN_KEYS, seq)
    m = l = acc = None
    for c in range(seq // kc):
        rows = slice(c * kc, (c + 1) * kc)
        s = _dot_nt(q, k_ref[rows, :])
        mc = jnp.max(s, axis=-1, keepdims=True)
        if c == 0:
            m = mc
            p = jnp.exp2(s - m)
            l = jnp.sum(p, axis=-1, keepdims=True)
            acc = _dot(p.astype(BF16), v_ref[rows, :])
        else:
            m_new = jnp.maximum(m, mc)
            alpha = jnp.exp2(m - m_new)
            p = jnp.exp2(s - m_new)
            l = l * alpha + jnp.sum(p, axis=-1, keepdims=True)
            acc = acc * alpha + _dot(p.astype(BF16), v_ref[rows, :])
            m = m_new
    o_ref[...] = (acc / l).astype(BF16)


def _attention(qh, kh, vh):
    batch, mh, seq, dq = qh.shape
    dv = vh.shape[-1]
    tq = min(seq, 2048)
    nq = seq // tq
    return pl.pallas_call(
        _attn_kernel,
        out_shape=jax.ShapeDtypeStruct((batch * seq, mh * dv), BF16),
        grid=(batch, mh, nq),
        in_specs=[pl.BlockSpec((None, None, tq, dq), lambda b, h, i: (b, h, i, 0)),
                  pl.BlockSpec((None, None, seq, dq), lambda b, h, i: (b, h, 0, 0)),
                  pl.BlockSpec((None, None, seq, dv), lambda b, h, i: (b, h, 0, 0))],
        out_specs=pl.BlockSpec((tq, dv), lambda b, h, i: (b * nq + i, h)),
        compiler_params=_cparams("parallel", "parallel", "arbitrary"),
        name="mla_attention",
    )(qh, kh, vh)


def _eye(rows, cols):
    r = lax.broadcasted_iota(jnp.int32, (rows, cols), 0)
    c = lax.broadcasted_iota(jnp.int32, (rows, cols), 1)
    return jnp.where(r == c, 1.0, 0.0).astype(BF16)


def _outproj_kernel(o_ref, hg_ref, og_ref, om_ref, w_ref, x_ref, mod_ref, g2_ref, wr_ref,
                    x1_ref, h2_ref, aff_ref, lat_ref, mix_scr, *, heads, n_exp):
    hw = o_ref.shape[1]
    o = o_ref[...]
    gate = _silu(hg_ref[...].astype(F32))
    for h in range(heads):
        sl = slice(h * LANES, (h + 1) * LANES)
        oh = o[:, sl]
        r = lax.rsqrt(jnp.mean(oh * oh, axis=-1, keepdims=True) + EPS)
        mix_scr[:, sl] = (oh * r * og_ref[:, sl] * gate[:, sl]).astype(BF16)
    mix_scr[:, hw:] = om_ref[...]
    x1 = x_ref[...] + mod_ref[2:3, :] * _dot(mix_scr[...], w_ref[...])
    x1_ref[...] = x1
    r2 = lax.rsqrt(jnp.mean(x1 * x1, axis=-1, keepdims=True) + EPS)
    h2 = x1 * r2 * g2_ref[...] * (1.0 + mod_ref[4:5, :]) + mod_ref[3:4, :]
    h2_ref[...] = h2
    logits = _dot_hi(h2, wr_ref[...])
    lane = lax.broadcasted_iota(jnp.int32, logits.shape, 1)
    logits = jnp.where(lane < n_exp, logits, -jnp.inf)
    z = logits - jnp.max(logits, axis=-1, keepdims=True)
    p = jnp.exp(z)
    sp = jnp.sum(p, axis=-1, keepdims=True)
    aff_ref[...] = p / sp
    la = jnp.where(lane < n_exp, z - jnp.log(sp), 0.0)
    eye = _eye(n_exp, la.shape[1])
    p1, p2, p3 = _split3(la)
    lat_ref[...] = (_dot_nt(eye, p1) + _dot_nt(eye, p2)) + _dot_nt(eye, p3)


def _outproj(o_hgrn, proj, og, o_mla, w_out_b, xf, mod6, g2, wr_pad, seq, heads, hk, n_exp):
    m, d = xf.shape
    hw = o_hgrn.shape[1]
    mw = o_mla.shape[1]
    tm = min(seq, 256)
    tpb = seq // tm
    gcol = (3 * hk + hw) // hw
    assert (3 * hk + hw) % hw == 0
    row = lambda n: pl.BlockSpec((tm, n), lambda i: (i, 0))
    return pl.pallas_call(
        functools.partial(_outproj_kernel, heads=heads, n_exp=n_exp),
        out_shape=(jax.ShapeDtypeStruct((m, d), F32),
                   jax.ShapeDtypeStruct((m, d), F32),
                   jax.ShapeDtypeStruct((m, LANES), F32),
                   jax.ShapeDtypeStruct((m // seq, n_exp, seq), F32)),
        grid=(m // tm,),
        in_specs=[row(hw),
                  pl.BlockSpec((tm, hw), lambda i: (i, gcol)),
                  pl.BlockSpec((1, hw), lambda i: (0, 0)),
                  row(mw),
                  pl.BlockSpec((hw + mw, d), lambda i: (0, 0), pipeline_mode=pl.Buffered(1)),
                  row(d),
                  pl.BlockSpec((None, 6, d), lambda i: (i // tpb, 0, 0)),
                  pl.BlockSpec((1, d), lambda i: (0, 0)),
                  pl.BlockSpec((d, LANES), lambda i: (0, 0), pipeline_mode=pl.Buffered(1))],
        out_specs=(row(d), row(d), row(LANES),
                   pl.BlockSpec((None, n_exp, tm), lambda i: (i // tpb, 0, i % tpb))),
        scratch_shapes=[pltpu.VMEM((tm, hw + mw), BF16)],
        compiler_params=_cparams("parallel"),
        name="outproj_norm2_router",
    )(o_hgrn, proj, og, o_mla, w_out_b, xf, mod6, g2, wr_pad)


BISECT_STEPS = 64


COMBINE_TILE = 256
COMBINE_WIN = 64


def _topk_kernel(la_ref, slot_se_ref, idx_ref, tab_ref, tri_scr, cum_scr, *, cap, n_exp):
    nrow, seq = la_ref.shape
    ep = slot_se_ref.shape[1]
    rows = 256
    for k in range(seq // rows):
        r = lax.broadcasted_iota(jnp.int32, (rows, seq), 0) + k * rows
        c = lax.broadcasted_iota(jnp.int32, (rows, seq), 1)
        tri_scr[k * rows:(k + 1) * rows, :] = jnp.where(r < c, 1.0, 0.0).astype(BF16)

    def count(mask):
        return jnp.sum(jnp.where(mask, 1.0, 0.0), axis=-1, keepdims=True)

    def body(_, lh):
        lo, hi = lh
        mid = 0.5 * (lo + hi)
        ok = count(la_ref[...] >= mid) >= cap
        return jnp.where(ok, mid, lo), jnp.where(ok, hi, mid)

    la = la_ref[...]
    lo0 = jnp.min(la, axis=-1, keepdims=True)
    lo, hi = lax.fori_loop(0, BISECT_STEPS, body, (lo0, jnp.ones_like(lo0)))
    above = la >= hi
    tie = (la >= lo) & (la < hi)
    need = cap - count(above)
    tri = tri_scr[...]
    rank = _dot(jnp.where(tie, 1.0, 0.0).astype(BF16), tri)
    sel = above | (tie & (rank < need))
    pos = _dot(jnp.where(sel, 1.0, 0.0).astype(BF16), tri)
    slot = jnp.where(sel, pos, -1.0)
    eye = _eye(n_exp, ep)
    for b in range(nrow // n_exp):
        slot_se_ref[b * seq:(b + 1) * seq, :] = _dot_tn(
            slot[b * n_exp:(b + 1) * n_exp, :].astype(BF16), eye)
    cum_scr[...] = pos + jnp.where(sel, 1.0, 0.0)
    lane = lax.broadcasted_iota(jnp.int32, (nrow, cap), 1)

    def slot_body(c, acc):
        cnt = count(cum_scr[...] <= lax.convert_element_type(c, F32))
        return jnp.where(lane == c, cnt, acc)

    idx = lax.fori_loop(0, cap, slot_body, jnp.zeros((nrow, cap), F32), unroll=4)
    idx_ref[...] = idx.astype(jnp.int32)
    tok = lax.broadcasted_iota(jnp.int32, (nrow, seq), 1)
    tlane = lax.broadcasted_iota(jnp.int32, tab_ref.shape, 1)
    tab = jnp.zeros(tab_ref.shape, F32)
    tile = min(seq, COMBINE_TILE)
    for k in range(seq // tile + 1):
        tab = jnp.where(tlane == k, count(sel & (tok < k * tile)), tab)
    tab_ref[...] = tab.astype(jnp.int32)


def _topk(lat, batch, seq, n_exp, cap):
    return pl.pallas_call(
        functools.partial(_topk_kernel, cap=cap, n_exp=n_exp),
        out_shape=(jax.ShapeDtypeStruct((batch * seq, LANES), F32),
                   jax.ShapeDtypeStruct((batch * n_exp, cap), jnp.int32),
                   jax.ShapeDtypeStruct((batch * n_exp, LANES), jnp.int32)),
        scratch_shapes=[pltpu.VMEM((seq, seq), BF16), pltpu.VMEM((batch * n_exp, seq), F32)],
        compiler_params=pltpu.CompilerParams(vmem_limit_bytes=VMEM_LIMIT),
        name="expert_choice_topk",
    )(lat.reshape(batch * n_exp, seq))


def _ffn_kernel(idx_ref, h2_hbm, wg_ref, wu_ref, wd_ref, ye_ref, xe_scr, hmid_scr, sem,
                *, nt, nd, tf):
    e = pl.program_id(0)
    s = pl.program_id(1)
    rows = xe_scr.shape[0]

    def start_gather(expert):
        base = expert * rows

        def body(k, carry):
            r0 = pl.multiple_of(k * TILE, TILE)
            for j in range(TILE):
                pltpu.make_async_copy(h2_hbm.at[pl.ds(idx_ref[base + r0 + j], 1), :],
                                      xe_scr.at[pl.ds(r0 + j, 1), :], sem.at[0]).start()
            return carry
        lax.fori_loop(0, rows // TILE, body, 0)

    @pl.when((e == 0) & (s == 0))
    def _():
        start_gather(0)

    @pl.when(s == 0)
    def _():
        pltpu.make_async_copy(h2_hbm.at[pl.ds(0, rows), :], xe_scr, sem.at[0]).wait()

    @pl.when(s < nt)
    def _():
        xe = xe_scr[...].astype(BF16)
        a = _dot(xe, wg_ref[...].astype(BF16))
        u = _dot(xe, wu_ref[...].astype(BF16))
        hmid_scr[s] = (_silu(a) * u).astype(BF16)

    def down_step(prefetch):
        per = rows // (nd * nt)
        y = None
        for k in range(nt):
            if prefetch:
                dst0 = (s - nt) * (per * nt) + k * per
                first = (e + 1) * rows + dst0
                for j in range(per):
                    pltpu.make_async_copy(h2_hbm.at[pl.ds(idx_ref[first + j], 1), :],
                                          xe_scr.at[pl.ds(dst0 + j, 1), :], sem.at[0]).start()
            part = _dot(hmid_scr[k], wd_ref[k * tf:(k + 1) * tf, :].astype(BF16))
            y = part if y is None else y + part
        ye_ref[...] = y.astype(BF16)

    more = e + 1 < pl.num_programs(0)

    @pl.when((s >= nt) & more)
    def _():
        down_step(True)

    @pl.when((s >= nt) & jnp.logical_not(more))
    def _():
        down_step(False)


def _ffn(idx, h2, w_gate, w_up, w_down):
    n_exp, rows = idx.shape
    idx = idx.reshape(n_exp * rows)
    d = h2.shape[1]
    ff = w_gate.shape[2]
    tf = min(ff, 512)
    tn = min(d, 1024)
    nt = ff // tf
    nd = d // tn
    assert rows % (nd * nt) == 0
    up = lambda e, s, idx: (e, 0, jnp.minimum(s, nt - 1))
    down = lambda e, s, idx: (e, 0, jnp.maximum(s - nt, 0))
    return pl.pallas_call(
        functools.partial(_ffn_kernel, nt=nt, nd=nd, tf=tf),
        out_shape=jax.ShapeDtypeStruct((n_exp, rows, d), BF16),
        grid_spec=pltpu.PrefetchScalarGridSpec(
            num_scalar_prefetch=1,
            grid=(n_exp, nt + nd),
            in_specs=[pl.BlockSpec(memory_space=pl.ANY),
                      pl.BlockSpec((None, d, tf), up),
                      pl.BlockSpec((None, d, tf), up),
                      pl.BlockSpec((None, ff, tn), down)],
            out_specs=pl.BlockSpec((None, rows, tn), down),
            scratch_shapes=[pltpu.VMEM((rows, d), F32),
                            pltpu.VMEM((nt, rows, tf), BF16),
                            pltpu.SemaphoreType.DMA((1,))]),
        compiler_params=_cparams("arbitrary", "arbitrary"),
        name="expert_swiglu",
    )(idx, h2, w_gate, w_up, w_down)


def _combine_kernel(tab_ref, slot_ref, aff_ref, ye_ref, x1_ref, mod_ref, out_ref, y_scr,
                    *, n_exp, cap, win):
    b = pl.program_id(0)
    t = pl.program_id(1)
    tt = x1_ref.shape[0]
    base = (b * (pl.num_programs(1) + 1) + t) * n_exp
    pack = 16
    starts = []
    short = None
    for e in range(n_exp):
        c0 = tab_ref[base + e]
        c1 = tab_ref[base + n_exp + e]
        a = jnp.minimum(c0 & ~(pack - 1), cap - win)
        ok = c1 - a <= win
        starts.append(a)
        short = ok if short is None else short & ok

    def finish(acc):
        out_ref[...] = x1_ref[...] + mod_ref[5:6, :] * acc

    @pl.when(short)
    def _():
        lane = lax.broadcasted_iota(jnp.int32, (tt, LANES), 1).astype(F32)
        per = LANES // win
        blocks = []
        for g in range(n_exp // per):
            blk = jnp.zeros((tt, LANES), F32)
            for j in range(per):
                e = g * per + j
                a = pl.multiple_of(starts[e], pack)
                y_scr[e * win:(e + 1) * win, :] = ye_ref[e, pl.ds(a, win), :]
                slot = slot_ref[:, e:e + 1]
                rel = jnp.where(slot >= 0.0, slot - a.astype(F32) + float(j * win), -1.0)
                blk = jnp.where(lane == rel, aff_ref[:, e:e + 1], blk)
            blocks.append(blk.astype(BF16))
        finish(_dot(jnp.concatenate(blocks, axis=1), y_scr[...]))

    @pl.when(jnp.logical_not(short))
    def _():
        cidx = lax.broadcasted_iota(jnp.int32, (tt, cap), 1).astype(F32)
        acc = jnp.zeros(x1_ref.shape, F32)
        for e in range(n_exp):
            onehot = jnp.where(cidx == slot_ref[:, e:e + 1], 1.0, 0.0).astype(BF16)
            acc = acc + aff_ref[:, e:e + 1] * _dot(onehot, ye_ref[e])
        finish(acc)


def _combine(tab, slot_se, aff, ye4, x1, mod6, seq, n_exp, cap):
    m, d = x1.shape
    ep = slot_se.shape[1]
    batch = m // seq
    tt = min(seq, COMBINE_TILE)
    tpb = seq // tt
    win = min(COMBINE_WIN, cap)
    assert LANES % win == 0 and n_exp % (LANES // win) == 0 and cap % 16 == 0
    return pl.pallas_call(
        functools.partial(_combine_kernel, n_exp=n_exp, cap=cap, win=win),
        out_shape=jax.ShapeDtypeStruct((m, d), F32),
        grid_spec=pltpu.PrefetchScalarGridSpec(
            num_scalar_prefetch=1,
            grid=(batch, tpb),
            in_specs=[pl.BlockSpec((tt, ep), lambda b, t, tab: (b * tpb + t, 0)),
                      pl.BlockSpec((tt, ep), lambda b, t, tab: (b * tpb + t, 0)),
                      pl.BlockSpec((n_exp, None, cap, d), lambda b, t, tab: (0, b, 0, 0)),
                      pl.BlockSpec((tt, d), lambda b, t, tab: (b * tpb + t, 0)),
                      pl.BlockSpec((None, 6, d), lambda b, t, tab: (b, 0, 0))],
            out_specs=pl.BlockSpec((tt, d), lambda b, t, tab: (b * tpb + t, 0)),
            scratch_shapes=[pltpu.VMEM((n_exp * win, d), BF16)]),
        compiler_params=_cparams("parallel", "arbitrary"),
        name="expert_combine",
    )(tab, slot_se, aff, ye4, x1, mod6)


def kernel(x, c, positions, w_ada, b_ada, norm1_g, w_in, lb_logits, hgrn_out_g, qa_norm_g, w_uq,
           kva_norm_g, w_ukv, q_head_g, k_head_g, w_out, norm2_g, w_router, w_gate, w_up, w_down):
    batch, seq, d = x.shape
    depth = w_ada.shape[0]
    m = batch * seq
    hk = lb_logits.shape[2]
    heads, dv = hgrn_out_g.shape[1], hgrn_out_g.shape[2]
    hw = heads * dv
    ql, kvl = qa_norm_g.shape[1], kva_norm_g.shape[1]
    qk_dim = q_head_g.shape[1]
    mh = w_uq.shape[2] // qk_dim
    d_in = w_in.shape[2]
    rope = d_in - (3 * hk + 2 * hw + ql + kvl)
    nope = qk_dim - rope
    vdim = w_ukv.shape[2] // mh - nope
    n_exp = w_router.shape[2]
    cap = EC_CAPACITY * seq // n_exp
    assert dv == LANES and hk == hw and nope == LANES and vdim == LANES and 2 * rope == LANES
    assert ql + kvl + rope <= hk and seq % GRP == 0 and n_exp <= LANES and cap % 8 == 0

    cq_off = 3 * hk + 2 * hw
    ckv_off = cq_off + ql
    kpe_off = ckv_off + kvl
    swap = jnp.concatenate([jnp.arange(rope // 2, rope), jnp.arange(0, rope // 2)])

    def both(v):
        return jnp.concatenate([v, v[..., swap]], axis=-1)

    cs = _rope_tables(positions, rope)
    c8 = jnp.pad(c, ((0, (-batch) % 8), (0, 0)))
    xf = x.reshape(m, d)
    for l in range(depth):
        mod6 = _ada(c8, w_ada[l], b_ada[l])[:batch].reshape(batch, 6, d)

        w_in_b = w_in[l].astype(BF16)
        wk_b = both(w_in[l][:, kpe_off:kpe_off + rope]).astype(BF16)
        proj, fgate, kpe2 = _inproj(xf, mod6, norm1_g[l].reshape(1, d), w_in_b, wk_b, lb_logits,
                                   seq, hk, l)

        o_hgrn = _hgrn(proj, fgate, batch, seq, heads, hk)

        wq = w_uq[l].reshape(ql, mh, qk_dim)
        wq_all = jnp.concatenate([wq[..., :nope], both(wq[..., nope:])], axis=-1)
        qh, kh, vh = _mla_proj(
            proj, kpe2, cs, qa_norm_g[l].reshape(1, ql), kva_norm_g[l].reshape(1, kvl),
            wq_all.reshape(ql, mh * 2 * LANES).astype(BF16), w_ukv[l].astype(BF16),
            q_head_g[l][:nope].reshape(1, nope), both(q_head_g[l][nope:]).reshape(1, 2 * rope),
            k_head_g[l][:nope].reshape(1, nope), both(k_head_g[l][nope:]).reshape(1, 2 * rope),
            batch, seq, mh, cq_off, ckv_off, qk_dim, rope)
        o_mla = _attention(qh, kh, vh)

        wr_pad = jnp.pad(w_router[l], ((0, 0), (0, LANES - n_exp)))
        x1, h2, aff, lat = _outproj(o_hgrn, proj, hgrn_out_g[l].reshape(1, hw), o_mla,
                                    w_out[l].astype(BF16), xf, mod6, norm2_g[l].reshape(1, d),
                                    wr_pad, seq, heads, hk, n_exp)

        slot_se, idx, tab = _topk(lat, batch, seq, n_exp, cap)
        ntile = seq // min(seq, COMBINE_TILE)
        tab = tab[:, :ntile + 1].reshape(batch, n_exp, ntile + 1).transpose(0, 2, 1).reshape(-1)
        rows = idx.reshape(batch, n_exp, cap) + (jnp.arange(batch, dtype=jnp.int32) * seq)[:, None, None]
        rows = rows.transpose(1, 0, 2).reshape(n_exp, batch * cap)
        ye = _ffn(rows, h2, w_gate[l], w_up[l], w_down[l])
        xf = _combine(tab, slot_se, aff, ye.reshape(n_exp, batch, cap, d), x1, mod6, seq, n_exp, cap)
    return xf.reshape(batch, seq, d)
```

```python
import functools
import math

import jax
import jax.numpy as jnp
from jax import lax
from jax.experimental import pallas as pl
from jax.experimental.pallas import tpu as pltpu

F32 = jnp.float32
BF16 = jnp.bfloat16
EPS = 1e-6
ROPE_BASE = 10000.0
LOG2E = math.log2(math.e)
EC_CAPACITY = 2
LANES = 128
TILE = 8
GRP = 128
VMEM_LIMIT = 56 * 1024 * 1024


def _cparams(*sem):
    return pltpu.CompilerParams(dimension_semantics=sem, vmem_limit_bytes=VMEM_LIMIT)


def _dot(a, b):
    return jnp.dot(a, b, preferred_element_type=F32)


def _dot_nt(a, b):
    return lax.dot_general(a, b, (((1,), (1,)), ((), ())), preferred_element_type=F32)


def _dot_tn(a, b):
    return lax.dot_general(a, b, (((0,), (0,)), ((), ())), preferred_element_type=F32)


def _split2(a):
    hi = a.astype(BF16)
    lo = (a - hi.astype(F32)).astype(BF16)
    return hi, lo


def _split3(a):
    p1 = a.astype(BF16)
    r1 = a - p1.astype(F32)
    p2 = r1.astype(BF16)
    p3 = (r1 - p2.astype(F32)).astype(BF16)
    return p1, p2, p3


def _dot_hi(a, b):
    ah, al = _split2(a)
    bh, bl = _split2(b)
    return _dot(ah, bh) + (_dot(ah, bl) + _dot(al, bh))


def _silu(x):
    return x * jax.nn.sigmoid(x)


def _rope_kernel(pos_ref, cs_ref, *, half):
    pos = pos_ref[...].astype(F32)
    lane = lax.broadcasted_iota(jnp.int32, (1, 4 * half), 1)
    j = (lane & (half - 1)).astype(F32)
    inv_freq = jnp.exp(j * (-2.0 * math.log(ROPE_BASE) / (2 * half)))
    ang = pos * inv_freq
    c = jnp.cos(ang)
    s = jnp.sin(ang)
    cs_ref[...] = jnp.where(lane < 2 * half, c, jnp.where(lane < 3 * half, -s, s))


def _rope_tables(positions, rope):
    m = positions.size
    tm = min(m, 1024)
    half = rope // 2
    return pl.pallas_call(
        functools.partial(_rope_kernel, half=half),
        out_shape=jax.ShapeDtypeStruct((m, 2 * rope), F32),
        grid=(m // tm,),
        in_specs=[pl.BlockSpec((tm, 1), lambda i: (i, 0))],
        out_specs=pl.BlockSpec((tm, 2 * rope), lambda i: (i, 0)),
        compiler_params=_cparams("parallel"),
        name="rope_tables",
    )(positions.reshape(m, 1))


def _ada_kernel(c_ref, w_ref, b_ref, o_ref):
    part = _dot_hi(_silu(c_ref[...]), w_ref[...])

    @pl.when(pl.program_id(0) == 0)
    def _():
        o_ref[...] = part + b_ref[...]

    @pl.when(pl.program_id(0) > 0)
    def _():
        o_ref[...] += part


def _ada(c8, w, b):
    d, n = w.shape
    tk = min(d, 128)
    return pl.pallas_call(
        _ada_kernel,
        out_shape=jax.ShapeDtypeStruct((c8.shape[0], n), F32),
        grid=(d // tk,),
        in_specs=[pl.BlockSpec((c8.shape[0], tk), lambda k: (0, k)),
                  pl.BlockSpec((tk, n), lambda k: (k, 0)),
                  pl.BlockSpec((1, n), lambda k: (0, 0))],
        out_specs=pl.BlockSpec((c8.shape[0], n), lambda k: (0, 0)),
        compiler_params=_cparams("arbitrary"),
        name="ada_mod",
    )(c8, w, b.reshape(1, n))


def _inproj_kernel(x_ref, mod_ref, g_ref, w_ref, wk_ref, lbl_ref,
                   proj_ref, fgate_ref, kpe_ref, h_scr, *, layer):
    j = pl.program_id(1)

    @pl.when(j == 0)
    def _():
        x = x_ref[...]
        r = lax.rsqrt(jnp.mean(x * x, axis=-1, keepdims=True) + EPS)
        h = x * r * g_ref[...] * (1.0 + mod_ref[1:2, :]) + mod_ref[0:1, :]
        hb = h.astype(BF16)
        h_scr[...] = hb
        kpe_ref[...] = _dot(hb, wk_ref[...])

    acc = _dot(h_scr[...], w_ref[...].astype(BF16))
    proj_ref[...] = acc.astype(BF16)

    @pl.when((j == 1) | (j == 2))
    def _():
        lg = lbl_ref[...]
        e = jnp.exp(lg - jnp.max(lg, axis=0, keepdims=True))
        lb = jnp.sum(e[:layer + 1], axis=0, keepdims=True) / jnp.sum(e, axis=0, keepdims=True)
        fgate_ref[...] = lb + (1.0 - lb) * jax.nn.sigmoid(acc)


def _inproj(xf, mod6, g1, w_in_b, wk_b, lb_logits, seq, hk, layer):
    m, d = xf.shape
    d_in = w_in_b.shape[2]
    tm = min(seq, 1024)
    tpb = seq // tm
    tn = hk
    nj = pl.cdiv(d_in, tn)
    nl = lb_logits.shape[1]
    fdir = lambda j: jnp.clip(j - 1, 0, 1)
    return pl.pallas_call(
        functools.partial(_inproj_kernel, layer=layer),
        out_shape=(jax.ShapeDtypeStruct((m, d_in), BF16),
                   jax.ShapeDtypeStruct((m, 2 * hk), F32),
                   jax.ShapeDtypeStruct((m, LANES), F32)),
        grid=(m // tm, nj),
        in_specs=[pl.BlockSpec((tm, d), lambda i, j: (i, 0)),
                  pl.BlockSpec((None, 6, d), lambda i, j: (i // tpb, 0, 0)),
                  pl.BlockSpec((1, d), lambda i, j: (0, 0)),
                  pl.BlockSpec((None, d, tn), lambda i, j: (layer, 0, j)),
                  pl.BlockSpec((d, LANES), lambda i, j: (0, 0)),
                  pl.BlockSpec((None, nl, tn), lambda i, j: (fdir(j), 0, 0))],
        out_specs=(pl.BlockSpec((tm, tn), lambda i, j: (i, j)),
                   pl.BlockSpec((tm, tn), lambda i, j: (i, fdir(j))),
                   pl.BlockSpec((tm, LANES), lambda i, j: (i, 0))),
        scratch_shapes=[pltpu.VMEM((tm, d), BF16)],
        compiler_params=_cparams("parallel", "arbitrary"),
        name="norm1_inproj",
    )(xf, mod6, g1, w_in_b, wk_b, lb_logits)


def _tile_scan(g, d):
    rin = lax.broadcasted_iota(jnp.int32, g.shape, 0) & (TILE - 1)
    b = g
    step = 1
    while step < TILE:
        if d == 0:
            b = b + jnp.where(rin >= step, pltpu.roll(b, step, 0), 0.0)
        else:
            b = b + jnp.where(rin < TILE - step, pltpu.roll(b, GRP - step, 0), 0.0)
        step *= 2
    return b


def _group_cumsum(g, d):
    b = _tile_scan(g, d)
    ntile = GRP // TILE
    order = range(ntile) if d == 0 else range(ntile - 1, -1, -1)
    edge = TILE - 1 if d == 0 else 0
    out = [None] * ntile
    carry = None
    for i in order:
        t = b[i * TILE:(i + 1) * TILE]
        out[i] = t if carry is None else t + carry
        tot = t[edge:edge + 1]
        carry = tot if carry is None else carry + tot
    return jnp.concatenate(out, axis=0)


def _boundary(b, h, d):
    idx = h - 1 if d == 0 else h
    if 2 * h >= TILE:
        b3 = b.reshape(GRP // (2 * h), 2 * h, b.shape[1])
        return jnp.broadcast_to(b3[:, idx:idx + 1, :], b3.shape).reshape(b.shape)
    p = lax.broadcasted_iota(jnp.int32, b.shape, 0) & (2 * h - 1)
    out = b
    for pos in range(2 * h):
        shift = pos - idx
        if shift != 0:
            out = jnp.where(p == pos, pltpu.roll(b, shift % GRP, 0), out)
    return out


def _hgrn_kernel(q_ref, v_ref, gf_ref, gb_ref, o_ref, lv_scr, sg_scr, st_scr):
    seq = q_ref.shape[0]
    ngrp = seq // GRP
    nlev = GRP.bit_length()
    g_refs = (gf_ref, gb_ref)
    assert ngrp % 2 == 0

    r = lax.broadcasted_iota(jnp.int32, (GRP, GRP), 0)
    c = lax.broadcasted_iota(jnp.int32, (GRP, GRP), 1)
    lev = jnp.zeros((GRP, GRP), jnp.int32)
    for j in range(nlev - 1):
        lev = lev + jnp.where((r >> j) != (c >> j), 1, 0)
    lv_scr[0] = jnp.where(c <= r, lev, -1)
    lv_scr[1] = jnp.where(c >= r, lev, -1)
    st_scr[...] = jnp.zeros_like(st_scr)
    rr = lax.broadcasted_iota(jnp.int32, (GRP, LANES), 0)
    for l in range(1, nlev):
        late = (rr & (1 << (l - 1))) != 0
        sg_scr[0, l - 1] = jnp.where(late, 1.0, -1.0)
        sg_scr[1, l - 1] = jnp.where(late, -1.0, 1.0)

    def body(i, carry, first):
        for d in (0, 1):
            grp = i if d == 0 else ngrp - 1 - i
            r0 = pl.multiple_of(grp * GRP, GRP)
            f = g_refs[d][pl.ds(r0, GRP), :]
            g = jnp.log2(f)
            qb = q_ref[pl.ds(r0, GRP), :]
            vb = v_ref[pl.ds(r0, GRP), :]
            qf = qb.astype(F32)
            kk = 1.0 - f
            kb = kk.astype(BF16)
            b = _group_cumsum(g, d)
            edge = GRP - 1 if d == 0 else 0
            tot = b[edge:edge + 1]
            lv = lv_scr[d]
            att = jnp.where(lv == 0, _dot_nt(qb, kb), 0.0)
            for l in range(1, nlev):
                x = jnp.exp2((b - _boundary(b, 1 << (l - 1), d)) * sg_scr[d, l - 1]).astype(BF16)
                att = jnp.where(lv == l, _dot_nt(qb * x, kb * x), att)
            st = st_scr[d]
            o = _dot(att.astype(BF16), vb) + _dot_nt((qf * jnp.exp2(b)).astype(BF16), st.astype(BF16))
            if first:
                o_ref[pl.ds(r0, GRP), :] = o
            else:
                o_ref[pl.ds(r0, GRP), :] += o
            st_scr[d] = st * jnp.exp2(tot) + _dot_tn(vb, (kk * jnp.exp2(tot - b)).astype(BF16))
        return carry

    half = ngrp // 2
    lax.fori_loop(0, half, functools.partial(body, first=True), 0, unroll=min(8, half))
    lax.fori_loop(half, ngrp, functools.partial(body, first=False), 0, unroll=min(8, half))


def _hgrn(proj, fgate, batch, seq, heads, hk):
    m = proj.shape[0]
    nh = hk // LANES
    vcol = 3 * nh
    blk = lambda off: pl.BlockSpec((seq, LANES), lambda b, h: (b, off + h))
    out = jax.ShapeDtypeStruct((m, hk), F32)
    return pl.pallas_call(
        _hgrn_kernel,
        out_shape=out,
        grid=(batch, heads),
        in_specs=[blk(0), blk(vcol), blk(0), blk(nh)],
        out_specs=blk(0),
        scratch_shapes=[pltpu.VMEM((2, GRP, GRP), jnp.int32),
                        pltpu.VMEM((2, GRP.bit_length() - 1, GRP, LANES), F32),
                        pltpu.VMEM((2, LANES, LANES), F32)],
        compiler_params=_cparams("parallel", "parallel"),
        name="hgrn2_scan",
    )(proj, proj, fgate, fgate)


def _mla_proj_kernel(cq_ref, ckv_ref, kpe_ref, cs_ref, qag_ref, kvag_ref, wq_ref, wkv_ref,
                     qgn_ref, qgr_ref, kgn_ref, kgr_ref, q_out, k_out, v_out,
                     *, scale, qk_dim, rope, heads):
    cq = cq_ref[...].astype(F32)
    a = (cq * lax.rsqrt(jnp.mean(cq * cq, axis=-1, keepdims=True) + EPS) * qag_ref[...]).astype(BF16)
    ckv = ckv_ref[...].astype(F32)
    c = (ckv * lax.rsqrt(jnp.mean(ckv * ckv, axis=-1, keepdims=True) + EPS)
         * kvag_ref[...]).astype(BF16)
    cs = cs_ref[...]
    lane = lax.broadcasted_iota(jnp.int32, cs.shape, 1)
    lo = lane < rope

    def rope_sumsq(rr):
        return jnp.sum(jnp.where(lo, rr * rr, 0.0), axis=-1, keepdims=True)

    def rotate(rr, gr):
        y = rr * gr * cs
        return y + pltpu.roll(y, rope, 1)

    kpe = kpe_ref[...]
    k_ss = rope_sumsq(kpe)
    k_rot = rotate(kpe, kgr_ref[...])
    for h in range(heads):
        cols = slice(2 * LANES * h, 2 * LANES * (h + 1))
        qall = _dot(a, wq_ref[:, cols])
        kvall = _dot(c, wkv_ref[:, cols])
        base = 0
        qn = qall[:, base:base + LANES]
        qr = qall[:, base + LANES:base + 2 * LANES]
        rq = lax.rsqrt((jnp.sum(qn * qn, axis=-1, keepdims=True) + rope_sumsq(qr)) / qk_dim + EPS) * scale
        q_out[h, :, :LANES] = (qn * qgn_ref[...] * rq).astype(BF16)
        q_out[h, :, LANES:] = jnp.where(lo, rotate(qr, qgr_ref[...]) * rq, 0.0).astype(BF16)
        kn = kvall[:, base:base + LANES]
        rk = lax.rsqrt((jnp.sum(kn * kn, axis=-1, keepdims=True) + k_ss) / qk_dim + EPS)
        k_out[h, :, :LANES] = (kn * kgn_ref[...] * rk).astype(BF16)
        k_out[h, :, LANES:] = jnp.where(lo, k_rot * rk, 0.0).astype(BF16)
        v_out[h] = kvall[:, base + LANES:base + 2 * LANES].astype(BF16)


def _mla_proj(proj, kpe2, cs, qag, kvag, wq_all, wkv_all, qgn, qgr, kgn, kgr,
              batch, seq, heads, cq_off, ckv_off, qk_dim, rope):
    m = proj.shape[0]
    ql, kvl = wq_all.shape[0], wkv_all.shape[0]
    tm = min(seq, 256)
    tpb = seq // tm
    assert cq_off % ql == 0 and ckv_off % kvl == 0
    vec = lambda n: pl.BlockSpec((1, n), lambda i: (0, 0))
    full = lambda w: pl.BlockSpec(w.shape, lambda i: (0, 0))
    hspec = lambda n: pl.BlockSpec((None, heads, tm, n), lambda i: (i // tpb, 0, i % tpb, 0))
    scale = qk_dim ** -0.5 * LOG2E
    return pl.pallas_call(
        functools.partial(_mla_proj_kernel, scale=scale, qk_dim=float(qk_dim), rope=rope, heads=heads),
        out_shape=(jax.ShapeDtypeStruct((batch, heads, seq, 2 * LANES), BF16),
                   jax.ShapeDtypeStruct((batch, heads, seq, 2 * LANES), BF16),
                   jax.ShapeDtypeStruct((batch, heads, seq, LANES), BF16)),
        grid=(m // tm,),
        in_specs=[pl.BlockSpec((tm, ql), lambda i: (i, cq_off // ql)),
                  pl.BlockSpec((tm, kvl), lambda i: (i, ckv_off // kvl)),
                  pl.BlockSpec((tm, LANES), lambda i: (i, 0)),
                  pl.BlockSpec((tm, LANES), lambda i: (i, 0)),
                  vec(ql), vec(kvl), full(wq_all), full(wkv_all),
                  vec(LANES), vec(LANES), vec(LANES), vec(LANES)],
        out_specs=(hspec(2 * LANES), hspec(2 * LANES), hspec(LANES)),
        compiler_params=_cparams("parallel"),
        name="mla_head_proj",
    )(proj, proj, kpe2, cs, qag, kvag, wq_all, wkv_all, qgn, qgr, kgn, kgr)


ATTN_KEYS = 1024


def _attn_kernel(q_ref, k_ref, v_ref, o_ref):
    q = q_ref[...]
    seq = k_ref.shape[0]
    kc = min(ATTN_KEYS, seq)
    m = l = acc = None
    for c in range(seq // kc):
        rows = slice(c * kc, (c + 1) * kc)
        s = _dot_nt(q, k_ref[rows, :])
        mc = jnp.max(s, axis=-1, keepdims=True)
        if c == 0:
            m = mc
            p = jnp.exp2(s - m)
            l = jnp.sum(p, axis=-1, keepdims=True)
            acc = _dot(p.astype(BF16), v_ref[rows, :])
        else:
            m_new = jnp.maximum(m, mc)
            alpha = jnp.exp2(m - m_new)
            p = jnp.exp2(s - m_new)
            l = l * alpha + jnp.sum(p, axis=-1, keepdims=True)
            acc = acc * alpha + _dot(p.astype(BF16), v_ref[rows, :])
            m = m_new
    o_ref[...] = (acc / l).astype(BF16)


def _attention(qh, kh, vh):
    batch, mh, seq, dq = qh.shape
    dv = vh.shape[-1]
    tq = min(seq, 2048)
    nq = seq // tq
    return pl.pallas_call(
        _attn_kernel,
        out_shape=jax.ShapeDtypeStruct((batch * seq, mh * dv), BF16),
        grid=(batch, mh, nq),
        in_specs=[pl.BlockSpec((None, None, tq, dq), lambda b, h, i: (b, h, i, 0)),
                  pl.BlockSpec((None, None, seq, dq), lambda b, h, i: (b, h, 0, 0)),
                  pl.BlockSpec((None, None, seq, dv), lambda b, h, i: (b, h, 0, 0))],
        out_specs=pl.BlockSpec((tq, dv), lambda b, h, i: (b * nq + i, h)),
        compiler_params=_cparams("parallel", "parallel", "arbitrary"),
        name="mla_attention",
    )(qh, kh, vh)


def _eye(rows, cols):
    r = lax.broadcasted_iota(jnp.int32, (rows, cols), 0)
    c = lax.broadcasted_iota(jnp.int32, (rows, cols), 1)
    return jnp.where(r == c, 1.0, 0.0).astype(BF16)


def _outproj_kernel(o_ref, hg_ref, og_ref, om_ref, w_ref, x_ref, mod_ref, g2_ref, wr_ref,
                    x1_ref, h2_ref, aff_ref, lat_ref, mix_scr, *, heads, n_exp):
    hw = o_ref.shape[1]
    o = o_ref[...]
    gate = _silu(hg_ref[...].astype(F32))
    for h in range(heads):
        sl = slice(h * LANES, (h + 1) * LANES)
        oh = o[:, sl]
        r = lax.rsqrt(jnp.mean(oh * oh, axis=-1, keepdims=True) + EPS)
        mix_scr[:, sl] = (oh * r * og_ref[:, sl] * gate[:, sl]).astype(BF16)
    mix_scr[:, hw:] = om_ref[...]
    x1 = x_ref[...] + mod_ref[2:3, :] * _dot(mix_scr[...], w_ref[...])
    x1_ref[...] = x1
    r2 = lax.rsqrt(jnp.mean(x1 * x1, axis=-1, keepdims=True) + EPS)
    h2 = x1 * r2 * g2_ref[...] * (1.0 + mod_ref[4:5, :]) + mod_ref[3:4, :]
    h2_ref[...] = h2
    logits = _dot_hi(h2, wr_ref[...])
    lane = lax.broadcasted_iota(jnp.int32, logits.shape, 1)
    logits = jnp.where(lane < n_exp, logits, -jnp.inf)
    z = logits - jnp.max(logits, axis=-1, keepdims=True)
    p = jnp.exp(z)
    sp = jnp.sum(p, axis=-1, keepdims=True)
    aff_ref[...] = p / sp
    la = jnp.where(lane < n_exp, z - jnp.log(sp), 0.0)
    eye = _eye(n_exp, la.shape[1])
    p1, p2, p3 = _split3(la)
    lat_ref[...] = (_dot_nt(eye, p1) + _dot_nt(eye, p2)) + _dot_nt(eye, p3)


def _outproj(o_hgrn, proj, og, o_mla, w_out_b, xf, mod6, g2, wr_pad, seq, heads, hk, n_exp):
    m, d = xf.shape
    hw = o_hgrn.shape[1]
    mw = o_mla.shape[1]
    tm = min(seq, 256)
    tpb = seq // tm
    gcol = (3 * hk + hw) // hw
    assert (3 * hk + hw) % hw == 0
    row = lambda n: pl.BlockSpec((tm, n), lambda i: (i, 0))
    return pl.pallas_call(
        functools.partial(_outproj_kernel, heads=heads, n_exp=n_exp),
        out_shape=(jax.ShapeDtypeStruct((m, d), F32),
                   jax.ShapeDtypeStruct((m, d), F32),
                   jax.ShapeDtypeStruct((m, LANES), F32),
                   jax.ShapeDtypeStruct((m // seq, n_exp, seq), F32)),
        grid=(m // tm,),
        in_specs=[row(hw),
                  pl.BlockSpec((tm, hw), lambda i: (i, gcol)),
                  pl.BlockSpec((1, hw), lambda i: (0, 0)),
                  row(mw),
                  pl.BlockSpec((hw + mw, d), lambda i: (0, 0), pipeline_mode=pl.Buffered(1)),
                  row(d),
                  pl.BlockSpec((None, 6, d), lambda i: (i // tpb, 0, 0)),
                  pl.BlockSpec((1, d), lambda i: (0, 0)),
                  pl.BlockSpec((d, LANES), lambda i: (0, 0), pipeline_mode=pl.Buffered(1))],
        out_specs=(row(d), row(d), row(LANES),
                   pl.BlockSpec((None, n_exp, tm), lambda i: (i // tpb, 0, i % tpb))),
        scratch_shapes=[pltpu.VMEM((tm, hw + mw), BF16)],
        compiler_params=_cparams("parallel"),
        name="outproj_norm2_router",
    )(o_hgrn, proj, og, o_mla, w_out_b, xf, mod6, g2, wr_pad)


BISECT_STEPS = 64


COMBINE_TILE = 256
COMBINE_WIN = 64


def _topk_kernel(la_ref, slot_se_ref, idx_ref, tab_ref, tri_scr, cum_scr, *, cap, n_exp):
    nrow, seq = la_ref.shape
    ep = slot_se_ref.shape[1]
    rows = 256
    for k in range(seq // rows):
        r = lax.broadcasted_iota(jnp.int32, (rows, seq), 0) + k * rows
        c = lax.broadcasted_iota(jnp.int32, (rows, seq), 1)
        tri_scr[k * rows:(k + 1) * rows, :] = jnp.where(r < c, 1.0, 0.0).astype(BF16)

    def count(mask):
        return jnp.sum(jnp.where(mask, 1.0, 0.0), axis=-1, keepdims=True)

    def body(_, lh):
        lo, hi = lh
        mid = 0.5 * (lo + hi)
        ok = count(la_ref[...] >= mid) >= cap
        return jnp.where(ok, mid, lo), jnp.where(ok, hi, mid)

    la = la_ref[...]
    lo0 = jnp.min(la, axis=-1, keepdims=True)
    lo, hi = lax.fori_loop(0, BISECT_STEPS, body, (lo0, jnp.ones_like(lo0)))
    above = la >= hi
    tie = (la >= lo) & (la < hi)
    need = cap - count(above)
    tri = tri_scr[...]
    rank = _dot(jnp.where(tie, 1.0, 0.0).astype(BF16), tri)
    sel = above | (tie & (rank < need))
    pos = _dot(jnp.where(sel, 1.0, 0.0).astype(BF16), tri)
    slot = jnp.where(sel, pos, -1.0)
    eye = _eye(n_exp, ep)
    for b in range(nrow // n_exp):
        slot_se_ref[b * seq:(b + 1) * seq, :] = _dot_tn(
            slot[b * n_exp:(b + 1) * n_exp, :].astype(BF16), eye)
    cum_scr[...] = pos + jnp.where(sel, 1.0, 0.0)
    lane = lax.broadcasted_iota(jnp.int32, (nrow, cap), 1)

    def slot_body(c, acc):
        cnt = count(cum_scr[...] <= lax.convert_element_type(c, F32))
        return jnp.where(lane == c, cnt, acc)

    idx = lax.fori_loop(0, cap, slot_body, jnp.zeros((nrow, cap), F32), unroll=4)
    idx_ref[...] = idx.astype(jnp.int32)
    tok = lax.broadcasted_iota(jnp.int32, (nrow, seq), 1)
    tlane = lax.broadcasted_iota(jnp.int32, tab_ref.shape, 1)
    tab = jnp.zeros(tab_ref.shape, F32)
    tile = min(seq, COMBINE_TILE)
    for k in range(seq // tile + 1):
        tab = jnp.where(tlane == k, count(sel & (tok < k * tile)), tab)
    tab_ref[...] = tab.astype(jnp.int32)


def _topk(lat, batch, seq, n_exp, cap):
    return pl.pallas_call(
        functools.partial(_topk_kernel, cap=cap, n_exp=n_exp),
        out_shape=(jax.ShapeDtypeStruct((batch * seq, LANES), F32),
                   jax.ShapeDtypeStruct((batch * n_exp, cap), jnp.int32),
                   jax.ShapeDtypeStruct((batch * n_exp, LANES), jnp.int32)),
        scratch_shapes=[pltpu.VMEM((seq, seq), BF16), pltpu.VMEM((batch * n_exp, seq), F32)],
        compiler_params=pltpu.CompilerParams(vmem_limit_bytes=VMEM_LIMIT),
        name="expert_choice_topk",
    )(lat.reshape(batch * n_exp, seq))


def _ffn_kernel(idx_ref, h2_hbm, wg_ref, wu_ref, wd_ref, ye_ref, xe_scr, hmid_scr, sem,
                *, nt, nd, tf):
    e = pl.program_id(0)
    s = pl.program_id(1)
    rows = xe_scr.shape[0]

    def start_gather(expert):
        base = expert * rows

        def body(k, carry):
            r0 = pl.multiple_of(k * TILE, TILE)
            for j in range(TILE):
                pltpu.make_async_copy(h2_hbm.at[pl.ds(idx_ref[base + r0 + j], 1), :],
                                      xe_scr.at[pl.ds(r0 + j, 1), :], sem.at[0]).start()
            return carry
        lax.fori_loop(0, rows // TILE, body, 0)

    @pl.when((e == 0) & (s == 0))
    def _():
        start_gather(0)

    @pl.when(s == 0)
    def _():
        pltpu.make_async_copy(h2_hbm.at[pl.ds(0, rows), :], xe_scr, sem.at[0]).wait()

    @pl.when(s < nt)
    def _():
        xe = xe_scr[...].astype(BF16)
        a = _dot(xe, wg_ref[...].astype(BF16))
        u = _dot(xe, wu_ref[...].astype(BF16))
        hmid_scr[s] = (_silu(a) * u).astype(BF16)

    def down_step(prefetch):
        per = rows // (nd * nt)
        y = None
        for k in range(nt):
            if prefetch:
                dst0 = (s - nt) * (per * nt) + k * per
                first = (e + 1) * rows + dst0
                for j in range(per):
                    pltpu.make_async_copy(h2_hbm.at[pl.ds(idx_ref[first + j], 1), :],
                                          xe_scr.at[pl.ds(dst0 + j, 1), :], sem.at[0]).start()
            part = _dot(hmid_scr[k], wd_ref[k * tf:(k + 1) * tf, :].astype(BF16))
            y = part if y is None else y + part
        ye_ref[...] = y.astype(BF16)

    more = e + 1 < pl.num_programs(0)

    @pl.when((s >= nt) & more)
    def _():
        down_step(True)

    @pl.when((s >= nt) & jnp.logical_not(more))
    def _():
        down_step(False)


def _ffn(idx, h2, w_gate, w_up, w_down):
    n_exp, rows = idx.shape
    idx = idx.reshape(n_exp * rows)
    d = h2.shape[1]
    ff = w_gate.shape[2]
    tf = min(ff, 512)
    tn = min(d, 1024)
    nt = ff // tf
    nd = d // tn
    assert rows % (nd * nt) == 0
    up = lambda e, s, idx: (e, 0, jnp.minimum(s, nt - 1))
    down = lambda e, s, idx: (e, 0, jnp.maximum(s - nt, 0))
    return pl.pallas_call(
        functools.partial(_ffn_kernel, nt=nt, nd=nd, tf=tf),
        out_shape=jax.ShapeDtypeStruct((n_exp, rows, d), BF16),
        grid_spec=pltpu.PrefetchScalarGridSpec(
            num_scalar_prefetch=1,
            grid=(n_exp, nt + nd),
            in_specs=[pl.BlockSpec(memory_space=pl.ANY),
                      pl.BlockSpec((None, d, tf), up),
                      pl.BlockSpec((None, d, tf), up),
                      pl.BlockSpec((None, ff, tn), down)],
            out_specs=pl.BlockSpec((None, rows, tn), down),
            scratch_shapes=[pltpu.VMEM((rows, d), F32),
                            pltpu.VMEM((nt, rows, tf), BF16),
                            pltpu.SemaphoreType.DMA((1,))]),
        compiler_params=_cparams("arbitrary", "arbitrary"),
        name="expert_swiglu",
    )(idx, h2, w_gate, w_up, w_down)


def _combine_kernel(tab_ref, slot_ref, aff_ref, ye_ref, x1_ref, mod_ref, out_ref, y_scr,
                    *, n_exp, cap, win):
    b = pl.program_id(0)
    t = pl.program_id(1)
    tt = x1_ref.shape[0]
    base = (b * (pl.num_programs(1) + 1) + t) * n_exp
    pack = 16
    starts = []
    short = None
    for e in range(n_exp):
        c0 = tab_ref[base + e]
        c1 = tab_ref[base + n_exp + e]
        a = jnp.minimum(c0 & ~(pack - 1), cap - win)
        ok = c1 - a <= win
        starts.append(a)
        short = ok if short is None else short & ok

    def finish(acc):
        out_ref[...] = x1_ref[...] + mod_ref[5:6, :] * acc

    @pl.when(short)
    def _():
        lane = lax.broadcasted_iota(jnp.int32, (tt, LANES), 1).astype(F32)
        per = LANES // win
        blocks = []
        for g in range(n_exp // per):
            blk = jnp.zeros((tt, LANES), F32)
            for j in range(per):
                e = g * per + j
                a = pl.multiple_of(starts[e], pack)
                y_scr[e * win:(e + 1) * win, :] = ye_ref[e, pl.ds(a, win), :]
                slot = slot_ref[:, e:e + 1]
                rel = jnp.where(slot >= 0.0, slot - a.astype(F32) + float(j * win), -1.0)
                blk = jnp.where(lane == rel, aff_ref[:, e:e + 1], blk)
            blocks.append(blk.astype(BF16))
        finish(_dot(jnp.concatenate(blocks, axis=1), y_scr[...]))

    @pl.when(jnp.logical_not(short))
    def _():
        cidx = lax.broadcasted_iota(jnp.int32, (tt, cap), 1).astype(F32)
        acc = jnp.zeros(x1_ref.shape, F32)
        for e in range(n_exp):
            onehot = jnp.where(cidx == slot_ref[:, e:e + 1], 1.0, 0.0).astype(BF16)
            acc = acc + aff_ref[:, e:e + 1] * _dot(onehot, ye_ref[e])
        finish(acc)


def _combine(tab, slot_se, aff, ye4, x1, mod6, seq, n_exp, cap):
    m, d = x1.shape
    ep = slot_se.shape[1]
    batch = m // seq
    tt = min(seq, COMBINE_TILE)
    tpb = seq // tt
    win = min(COMBINE_WIN, cap)
    assert LANES % win == 0 and n_exp % (LANES // win) == 0 and cap % 16 == 0
    return pl.pallas_call(
        functools.partial(_combine_kernel, n_exp=n_exp, cap=cap, win=win),
        out_shape=jax.ShapeDtypeStruct((m, d), F32),
        grid_spec=pltpu.PrefetchScalarGridSpec(
            num_scalar_prefetch=1,
            grid=(batch, tpb),
            in_specs=[pl.BlockSpec((tt, ep), lambda b, t, tab: (b * tpb + t, 0)),
                      pl.BlockSpec((tt, ep), lambda b, t, tab: (b * tpb + t, 0)),
                      pl.BlockSpec((n_exp, None, cap, d), lambda b, t, tab: (0, b, 0, 0)),
                      pl.BlockSpec((tt, d), lambda b, t, tab: (b * tpb + t, 0)),
                      pl.BlockSpec((None, 6, d), lambda b, t, tab: (b, 0, 0))],
            out_specs=pl.BlockSpec((tt, d), lambda b, t, tab: (b * tpb + t, 0)),
            scratch_shapes=[pltpu.VMEM((n_exp * win, d), BF16)]),
        compiler_params=_cparams("parallel", "arbitrary"),
        name="expert_combine",
    )(tab, slot_se, aff, ye4, x1, mod6)


def kernel(x, c, positions, w_ada, b_ada, norm1_g, w_in, lb_logits, hgrn_out_g, qa_norm_g, w_uq,
           kva_norm_g, w_ukv, q_head_g, k_head_g, w_out, norm2_g, w_router, w_gate, w_up, w_down):
    batch, seq, d = x.shape
    depth = w_ada.shape[0]
    m = batch * seq
    hk = lb_logits.shape[2]
    heads, dv = hgrn_out_g.shape[1], hgrn_out_g.shape[2]
    hw = heads * dv
    ql, kvl = qa_norm_g.shape[1], kva_norm_g.shape[1]
    qk_dim = q_head_g.shape[1]
    mh = w_uq.shape[2] // qk_dim
    d_in = w_in.shape[2]
    rope = d_in - (3 * hk + 2 * hw + ql + kvl)
    nope = qk_dim - rope
    vdim = w_ukv.shape[2] // mh - nope
    n_exp = w_router.shape[2]
    cap = EC_CAPACITY * seq // n_exp
    assert dv == LANES and hk == hw and nope == LANES and vdim == LANES and 2 * rope == LANES
    assert ql + kvl + rope <= hk and seq % GRP == 0 and n_exp <= LANES and cap % 8 == 0

    cq_off = 3 * hk + 2 * hw
    ckv_off = cq_off + ql
    kpe_off = ckv_off + kvl
    swap = jnp.concatenate([jnp.arange(rope // 2, rope), jnp.arange(0, rope // 2)])

    def both(v):
        return jnp.concatenate([v, v[..., swap]], axis=-1)

    cs = _rope_tables(positions, rope)
    c8 = jnp.pad(c, ((0, (-batch) % 8), (0, 0)))
    xf = x.reshape(m, d)
    for l in range(depth):
        mod6 = _ada(c8, w_ada[l], b_ada[l])[:batch].reshape(batch, 6, d)

        w_in_b = w_in
        wk_b = both(w_in[l][:, kpe_off:kpe_off + rope]).astype(BF16)
        proj, fgate, kpe2 = _inproj(xf, mod6, norm1_g[l].reshape(1, d), w_in_b, wk_b, lb_logits,
                                   seq, hk, l)

        o_hgrn = _hgrn(proj, fgate, batch, seq, heads, hk)

        wq = w_uq[l].reshape(ql, mh, qk_dim)
        wq_all = jnp.concatenate([wq[..., :nope], both(wq[..., nope:])], axis=-1)
        qh, kh, vh = _mla_proj(
            proj, kpe2, cs, qa_norm_g[l].reshape(1, ql), kva_norm_g[l].reshape(1, kvl),
            wq_all.reshape(ql, mh * 2 * LANES).astype(BF16), w_ukv[l].astype(BF16),
            q_head_g[l][:nope].reshape(1, nope), both(q_head_g[l][nope:]).reshape(1, 2 * rope),
            k_head_g[l][:nope].reshape(1, nope), both(k_head_g[l][nope:]).reshape(1, 2 * rope),
            batch, seq, mh, cq_off, ckv_off, qk_dim, rope)
        o_mla = _attention(qh, kh, vh)

        wr_pad = jnp.pad(w_router[l], ((0, 0), (0, LANES - n_exp)))
        x1, h2, aff, lat = _outproj(o_hgrn, proj, hgrn_out_g[l].reshape(1, hw), o_mla,
                                    w_out[l].astype(BF16), xf, mod6, norm2_g[l].reshape(1, d),
                                    wr_pad, seq, heads, hk, n_exp)

        slot_se, idx, tab = _topk(lat, batch, seq, n_exp, cap)
        ntile = seq // min(seq, COMBINE_TILE)
        tab = tab[:, :ntile + 1].reshape(batch, n_exp, ntile + 1).transpose(0, 2, 1).reshape(-1)
        rows = idx.reshape(batch, n_exp, cap) + (jnp.arange(batch, dtype=jnp.int32) * seq)[:, None, None]
        rows = rows.transpose(1, 0, 2).reshape(n_exp, batch * cap)
        ye = _ffn(rows, h2, w_gate[l], w_up[l], w_down[l])
        xf = _combine(tab, slot_se, aff, ye.reshape(n_exp, batch, cap, d), x1, mod6, seq, n_exp, cap)
    return xf.reshape(batch, seq, d)
```

```python
import functools
import math

import jax
import jax.numpy as jnp
from jax import lax
from jax.experimental import pallas as pl
from jax.experimental.pallas import tpu as pltpu

F32 = jnp.float32
BF16 = jnp.bfloat16
EPS = 1e-6
ROPE_BASE = 10000.0
LOG2E = math.log2(math.e)
EC_CAPACITY = 2
LANES = 128
TILE = 8
GRP = 128
VMEM_LIMIT = 56 * 1024 * 1024


def _cparams(*sem):
    return pltpu.CompilerParams(dimension_semantics=sem, vmem_limit_bytes=VMEM_LIMIT)


def _dot(a, b):
    return jnp.dot(a, b, preferred_element_type=F32)


def _dot_nt(a, b):
    return lax.dot_general(a, b, (((1,), (1,)), ((), ())), preferred_element_type=F32)


def _dot_tn(a, b):
    return lax.dot_general(a, b, (((0,), (0,)), ((), ())), preferred_element_type=F32)


def _split2(a):
    hi = a.astype(BF16)
    lo = (a - hi.astype(F32)).astype(BF16)
    return hi, lo


def _split3(a):
    p1 = a.astype(BF16)
    r1 = a - p1.astype(F32)
    p2 = r1.astype(BF16)
    p3 = (r1 - p2.astype(F32)).astype(BF16)
    return p1, p2, p3


def _dot_hi(a, b):
    ah, al = _split2(a)
    bh, bl = _split2(b)
    return _dot(ah, bh) + (_dot(ah, bl) + _dot(al, bh))


def _silu(x):
    return x * jax.nn.sigmoid(x)


def _rope_kernel(pos_ref, cs_ref, *, half):
    pos = pos_ref[...].astype(F32)
    lane = lax.broadcasted_iota(jnp.int32, (1, 4 * half), 1)
    j = (lane & (half - 1)).astype(F32)
    inv_freq = jnp.exp(j * (-2.0 * math.log(ROPE_BASE) / (2 * half)))
    ang = pos * inv_freq
    c = jnp.cos(ang)
    s = jnp.sin(ang)
    cs_ref[...] = jnp.where(lane < 2 * half, c, jnp.where(lane < 3 * half, -s, s))


def _rope_tables(positions, rope):
    m = positions.size
    tm = min(m, 1024)
    half = rope // 2
    return pl.pallas_call(
        functools.partial(_rope_kernel, half=half),
        out_shape=jax.ShapeDtypeStruct((m, 2 * rope), F32),
        grid=(m // tm,),
        in_specs=[pl.BlockSpec((tm, 1), lambda i: (i, 0))],
        out_specs=pl.BlockSpec((tm, 2 * rope), lambda i: (i, 0)),
        compiler_params=_cparams("parallel"),
        name="rope_tables",
    )(positions.reshape(m, 1))


def _ada_kernel(c_ref, w_ref, b_ref, o_ref):
    part = _dot_hi(_silu(c_ref[...]), w_ref[...])

    @pl.when(pl.program_id(0) == 0)
    def _():
        o_ref[...] = part + b_ref[...]

    @pl.when(pl.program_id(0) > 0)
    def _():
        o_ref[...] += part


def _ada(c8, w, b):
    d, n = w.shape
    tk = min(d, 128)
    return pl.pallas_call(
        _ada_kernel,
        out_shape=jax.ShapeDtypeStruct((c8.shape[0], n), F32),
        grid=(d // tk,),
        in_specs=[pl.BlockSpec((c8.shape[0], tk), lambda k: (0, k)),
                  pl.BlockSpec((tk, n), lambda k: (k, 0)),
                  pl.BlockSpec((1, n), lambda k: (0, 0))],
        out_specs=pl.BlockSpec((c8.shape[0], n), lambda k: (0, 0)),
        compiler_params=_cparams("arbitrary"),
        name="ada_mod",
    )(c8, w, b.reshape(1, n))


def _inproj_kernel(x_ref, mod_ref, g_ref, w_ref, wk_ref, lbl_ref,
                   proj_ref, fgate_ref, kpe_ref, h_scr, *, layer):
    j = pl.program_id(1)

    @pl.when(j == 0)
    def _():
        x = x_ref[...]
        r = lax.rsqrt(jnp.mean(x * x, axis=-1, keepdims=True) + EPS)
        h = x * r * g_ref[...] * (1.0 + mod_ref[1:2, :]) + mod_ref[0:1, :]
        hb = h.astype(BF16)
        h_scr[...] = hb
        kpe_ref[...] = _dot(hb, wk_ref[...])

    acc = _dot(h_scr[...], w_ref[...])
    proj_ref[...] = acc.astype(BF16)

    @pl.when((j == 1) | (j == 2))
    def _():
        lg = lbl_ref[...]
        e = jnp.exp(lg - jnp.max(lg, axis=0, keepdims=True))
        lb = jnp.sum(e[:layer + 1], axis=0, keepdims=True) / jnp.sum(e, axis=0, keepdims=True)
        fgate_ref[...] = lb + (1.0 - lb) * jax.nn.sigmoid(acc)


def _inproj(xf, mod6, g1, w_in_b, wk_b, lb_logits, seq, hk, layer):
    m, d = xf.shape
    d_in = w_in_b.shape[1]
    tm = min(seq, 1024)
    tpb = seq // tm
    tn = hk
    nj = pl.cdiv(d_in, tn)
    nl = lb_logits.shape[1]
    fdir = lambda j: jnp.clip(j - 1, 0, 1)
    return pl.pallas_call(
        functools.partial(_inproj_kernel, layer=layer),
        out_shape=(jax.ShapeDtypeStruct((m, d_in), BF16),
                   jax.ShapeDtypeStruct((m, 2 * hk), F32),
                   jax.ShapeDtypeStruct((m, LANES), F32)),
        grid=(m // tm, nj),
        in_specs=[pl.BlockSpec((tm, d), lambda i, j: (i, 0)),
                  pl.BlockSpec((None, 6, d), lambda i, j: (i // tpb, 0, 0)),
                  pl.BlockSpec((1, d), lambda i, j: (0, 0)),
                  pl.BlockSpec((d, tn), lambda i, j: (0, j)),
                  pl.BlockSpec((d, LANES), lambda i, j: (0, 0)),
                  pl.BlockSpec((None, nl, tn), lambda i, j: (fdir(j), 0, 0))],
        out_specs=(pl.BlockSpec((tm, tn), lambda i, j: (i, j)),
                   pl.BlockSpec((tm, tn), lambda i, j: (i, fdir(j))),
                   pl.BlockSpec((tm, LANES), lambda i, j: (i, 0))),
        scratch_shapes=[pltpu.VMEM((tm, d), BF16)],
        compiler_params=_cparams("parallel", "arbitrary"),
        name="norm1_inproj",
    )(xf, mod6, g1, w_in_b, wk_b, lb_logits)


def _tile_scan(g, d):
    rin = lax.broadcasted_iota(jnp.int32, g.shape, 0) & (TILE - 1)
    b = g
    step = 1
    while step < TILE:
        if d == 0:
            b = b + jnp.where(rin >= step, pltpu.roll(b, step, 0), 0.0)
        else:
            b = b + jnp.where(rin < TILE - step, pltpu.roll(b, GRP - step, 0), 0.0)
        step *= 2
    return b


def _group_cumsum(g, d):
    b = _tile_scan(g, d)
    ntile = GRP // TILE
    order = range(ntile) if d == 0 else range(ntile - 1, -1, -1)
    edge = TILE - 1 if d == 0 else 0
    out = [None] * ntile
    carry = None
    for i in order:
        t = b[i * TILE:(i + 1) * TILE]
        out[i] = t if carry is None else t + carry
        tot = t[edge:edge + 1]
        carry = tot if carry is None else carry + tot
    return jnp.concatenate(out, axis=0)


def _boundary(b, h, d):
    idx = h - 1 if d == 0 else h
    if 2 * h >= TILE:
        b3 = b.reshape(GRP // (2 * h), 2 * h, b.shape[1])
        return jnp.broadcast_to(b3[:, idx:idx + 1, :], b3.shape).reshape(b.shape)
    p = lax.broadcasted_iota(jnp.int32, b.shape, 0) & (2 * h - 1)
    out = b
    for pos in range(2 * h):
        shift = pos - idx
        if shift != 0:
            out = jnp.where(p == pos, pltpu.roll(b, shift % GRP, 0), out)
    return out


def _hgrn_kernel(q_ref, v_ref, gf_ref, gb_ref, o_ref, lv_scr, sg_scr, st_scr):
    seq = q_ref.shape[0]
    ngrp = seq // GRP
    nlev = GRP.bit_length()
    g_refs = (gf_ref, gb_ref)
    assert ngrp % 2 == 0

    r = lax.broadcasted_iota(jnp.int32, (GRP, GRP), 0)
    c = lax.broadcasted_iota(jnp.int32, (GRP, GRP), 1)
    lev = jnp.zeros((GRP, GRP), jnp.int32)
    for j in range(nlev - 1):
        lev = lev + jnp.where((r >> j) != (c >> j), 1, 0)
    lv_scr[0] = jnp.where(c <= r, lev, -1)
    lv_scr[1] = jnp.where(c >= r, lev, -1)
    st_scr[...] = jnp.zeros_like(st_scr)
    rr = lax.broadcasted_iota(jnp.int32, (GRP, LANES), 0)
    for l in range(1, nlev):
        late = (rr & (1 << (l - 1))) != 0
        sg_scr[0, l - 1] = jnp.where(late, 1.0, -1.0)
        sg_scr[1, l - 1] = jnp.where(late, -1.0, 1.0)

    def body(i, carry, first):
        for d in (0, 1):
            grp = i if d == 0 else ngrp - 1 - i
            r0 = pl.multiple_of(grp * GRP, GRP)
            f = g_refs[d][pl.ds(r0, GRP), :]
            g = jnp.log2(f)
            qb = q_ref[pl.ds(r0, GRP), :]
            vb = v_ref[pl.ds(r0, GRP), :]
            qf = qb.astype(F32)
            kk = 1.0 - f
            kb = kk.astype(BF16)
            b = _group_cumsum(g, d)
            edge = GRP - 1 if d == 0 else 0
            tot = b[edge:edge + 1]
            lv = lv_scr[d]
            att = jnp.where(lv == 0, _dot_nt(qb, kb), 0.0)
            for l in range(1, nlev):
                x = jnp.exp2((b - _boundary(b, 1 << (l - 1), d)) * sg_scr[d, l - 1]).astype(BF16)
                att = jnp.where(lv == l, _dot_nt(qb * x, kb * x), att)
            st = st_scr[d]
            o = _dot(att.astype(BF16), vb) + _dot_nt((qf * jnp.exp2(b)).astype(BF16), st.astype(BF16))
            if first:
                o_ref[pl.ds(r0, GRP), :] = o
            else:
                o_ref[pl.ds(r0, GRP), :] += o
            st_scr[d] = st * jnp.exp2(tot) + _dot_tn(vb, (kk * jnp.exp2(tot - b)).astype(BF16))
        return carry

    half = ngrp // 2
    lax.fori_loop(0, half, functools.partial(body, first=True), 0, unroll=min(8, half))
    lax.fori_loop(half, ngrp, functools.partial(body, first=False), 0, unroll=min(8, half))


def _hgrn(proj, fgate, batch, seq, heads, hk):
    m = proj.shape[0]
    nh = hk // LANES
    vcol = 3 * nh
    blk = lambda off: pl.BlockSpec((seq, LANES), lambda b, h: (b, off + h))
    out = jax.ShapeDtypeStruct((m, hk), F32)
    return pl.pallas_call(
        _hgrn_kernel,
        out_shape=out,
        grid=(batch, heads),
        in_specs=[blk(0), blk(vcol), blk(0), blk(nh)],
        out_specs=blk(0),
        scratch_shapes=[pltpu.VMEM((2, GRP, GRP), jnp.int32),
                        pltpu.VMEM((2, GRP.bit_length() - 1, GRP, LANES), F32),
                        pltpu.VMEM((2, LANES, LANES), F32)],
        compiler_params=_cparams("parallel", "parallel"),
        name="hgrn2_scan",
    )(proj, proj, fgate, fgate)


def _mla_proj_kernel(cq_ref, ckv_ref, kpe_ref, cs_ref, qag_ref, kvag_ref, wq_ref, wkv_ref,
                     qgn_ref, qgr_ref, kgn_ref, kgr_ref, q_out, k_out, v_out,
                     *, scale, qk_dim, rope, heads):
    cq = cq_ref[...].astype(F32)
    a = (cq * lax.rsqrt(jnp.mean(cq * cq, axis=-1, keepdims=True) + EPS) * qag_ref[...]).astype(BF16)
    ckv = ckv_ref[...].astype(F32)
    c = (ckv * lax.rsqrt(jnp.mean(ckv * ckv, axis=-1, keepdims=True) + EPS)
         * kvag_ref[...]).astype(BF16)
    cs = cs_ref[...]
    lane = lax.broadcasted_iota(jnp.int32, cs.shape, 1)
    lo = lane < rope

    def rope_sumsq(rr):
        return jnp.sum(jnp.where(lo, rr * rr, 0.0), axis=-1, keepdims=True)

    def rotate(rr, gr):
        y = rr * gr * cs
        return y + pltpu.roll(y, rope, 1)

    kpe = kpe_ref[...]
    k_ss = rope_sumsq(kpe)
    k_rot = rotate(kpe, kgr_ref[...])
    for h in range(heads):
        cols = slice(2 * LANES * h, 2 * LANES * (h + 1))
        qall = _dot(a, wq_ref[:, cols])
        kvall = _dot(c, wkv_ref[:, cols])
        base = 0
        qn = qall[:, base:base + LANES]
        qr = qall[:, base + LANES:base + 2 * LANES]
        rq = lax.rsqrt((jnp.sum(qn * qn, axis=-1, keepdims=True) + rope_sumsq(qr)) / qk_dim + EPS) * scale
        q_out[h, :, :LANES] = (qn * qgn_ref[...] * rq).astype(BF16)
        q_out[h, :, LANES:] = jnp.where(lo, rotate(qr, qgr_ref[...]) * rq, 0.0).astype(BF16)
        kn = kvall[:, base:base + LANES]
        rk = lax.rsqrt((jnp.sum(kn * kn, axis=-1, keepdims=True) + k_ss) / qk_dim + EPS)
        k_out[h, :, :LANES] = (kn * kgn_ref[...] * rk).astype(BF16)
        k_out[h, :, LANES:] = jnp.where(lo, k_rot * rk, 0.0).astype(BF16)
        v_out[h] = kvall[:, base + LANES:base + 2 * LANES].astype(BF16)


def _mla_proj(proj, kpe2, cs, qag, kvag, wq_all, wkv_all, qgn, qgr, kgn, kgr,
              batch, seq, heads, cq_off, ckv_off, qk_dim, rope):
    m = proj.shape[0]
    ql, kvl = wq_all.shape[0], wkv_all.shape[0]
    tm = min(seq, 256)
    tpb = seq // tm
    assert cq_off % ql == 0 and ckv_off % kvl == 0
    vec = lambda n: pl.BlockSpec((1, n), lambda i: (0, 0))
    full = lambda w: pl.BlockSpec(w.shape, lambda i: (0, 0))
    hspec = lambda n: pl.BlockSpec((None, heads, tm, n), lambda i: (i // tpb, 0, i % tpb, 0))
    scale = qk_dim ** -0.5 * LOG2E
    return pl.pallas_call(
        functools.partial(_mla_proj_kernel, scale=scale, qk_dim=float(qk_dim), rope=rope, heads=heads),
        out_shape=(jax.ShapeDtypeStruct((batch, heads, seq, 2 * LANES), BF16),
                   jax.ShapeDtypeStruct((batch, heads, seq, 2 * LANES), BF16),
                   jax.ShapeDtypeStruct((batch, heads, seq, LANES), BF16)),
        grid=(m // tm,),
        in_specs=[pl.BlockSpec((tm, ql), lambda i: (i, cq_off // ql)),
                  pl.BlockSpec((tm, kvl), lambda i: (i, ckv_off // kvl)),
                  pl.BlockSpec((tm, LANES), lambda i: (i, 0)),
                  pl.BlockSpec((tm, LANES), lambda i: (i, 0)),
                  vec(ql), vec(kvl), full(wq_all), full(wkv_all),
                  vec(LANES), vec(LANES), vec(LANES), vec(LANES)],
        out_specs=(hspec(2 * LANES), hspec(2 * LANES), hspec(LANES)),
        compiler_params=_cparams("parallel"),
        name="mla_head_proj",
    )(proj, proj, kpe2, cs, qag, kvag, wq_all, wkv_all, qgn, qgr, kgn, kgr)


ATTN_KEYS = 1024


def _attn_kernel(q_ref, k_ref, v_ref, o_ref):
    q = q_ref[...]
    seq = k_ref.shape[0]
    kc = min(ATTN_KEYS, seq)
    m = l = acc = None
    for c in range(seq // kc):
        rows = slice(c * kc, (c + 1) * kc)
        s = _dot_nt(q, k_ref[rows, :])
        mc = jnp.max(s, axis=-1, keepdims=True)
        if c == 0:
            m = mc
            p = jnp.exp2(s - m)
            l = jnp.sum(p, axis=-1, keepdims=True)
            acc = _dot(p.astype(BF16), v_ref[rows, :])
        else:
            m_new = jnp.maximum(m, mc)
            alpha = jnp.exp2(m - m_new)
            p = jnp.exp2(s - m_new)
            l = l * alpha + jnp.sum(p, axis=-1, keepdims=True)
            acc = acc * alpha + _dot(p.astype(BF16), v_ref[rows, :])
            m = m_new
    o_ref[...] = (acc / l).astype(BF16)


def _attention(qh, kh, vh):
    batch, mh, seq, dq = qh.shape
    dv = vh.shape[-1]
    tq = min(seq, 2048)
    nq = seq // tq
    return pl.pallas_call(
        _attn_kernel,
        out_shape=jax.ShapeDtypeStruct((batch * seq, mh * dv), BF16),
        grid=(batch, mh, nq),
        in_specs=[pl.BlockSpec((None, None, tq, dq), lambda b, h, i: (b, h, i, 0)),
                  pl.BlockSpec((None, None, seq, dq), lambda b, h, i: (b, h, 0, 0)),
                  pl.BlockSpec((None, None, seq, dv), lambda b, h, i: (b, h, 0, 0))],
        out_specs=pl.BlockSpec((tq, dv), lambda b, h, i: (b * nq + i, h)),
        compiler_params=_cparams("parallel", "parallel", "arbitrary"),
        name="mla_attention",
    )(qh, kh, vh)


def _eye(rows, cols):
    r = lax.broadcasted_iota(jnp.int32, (rows, cols), 0)
    c = lax.broadcasted_iota(jnp.int32, (rows, cols), 1)
    return jnp.where(r == c, 1.0, 0.0).astype(BF16)


def _outproj_kernel(o_ref, hg_ref, og_ref, om_ref, w_ref, x_ref, mod_ref, g2_ref, wr_ref,
                    x1_ref, h2_ref, aff_ref, lat_ref, mix_scr, *, heads, n_exp):
    hw = o_ref.shape[1]
    o = o_ref[...]
    gate = _silu(hg_ref[...].astype(F32))
    for h in range(heads):
        sl = slice(h * LANES, (h + 1) * LANES)
        oh = o[:, sl]
        r = lax.rsqrt(jnp.mean(oh * oh, axis=-1, keepdims=True) + EPS)
        mix_scr[:, sl] = (oh * r * og_ref[:, sl] * gate[:, sl]).astype(BF16)
    mix_scr[:, hw:] = om_ref[...]
    x1 = x_ref[...] + mod_ref[2:3, :] * _dot(mix_scr[...], w_ref[...])
    x1_ref[...] = x1
    r2 = lax.rsqrt(jnp.mean(x1 * x1, axis=-1, keepdims=True) + EPS)
    h2 = x1 * r2 * g2_ref[...] * (1.0 + mod_ref[4:5, :]) + mod_ref[3:4, :]
    h2_ref[...] = h2
    logits = _dot_hi(h2, wr_ref[...])
    lane = lax.broadcasted_iota(jnp.int32, logits.shape, 1)
    logits = jnp.where(lane < n_exp, logits, -jnp.inf)
    z = logits - jnp.max(logits, axis=-1, keepdims=True)
    p = jnp.exp(z)
    sp = jnp.sum(p, axis=-1, keepdims=True)
    aff_ref[...] = p / sp
    la = jnp.where(lane < n_exp, z - jnp.log(sp), 0.0)
    eye = _eye(n_exp, la.shape[1])
    p1, p2, p3 = _split3(la)
    lat_ref[...] = (_dot_nt(eye, p1) + _dot_nt(eye, p2)) + _dot_nt(eye, p3)


def _outproj(o_hgrn, proj, og, o_mla, w_out_b, xf, mod6, g2, wr_pad, seq, heads, hk, n_exp):
    m, d = xf.shape
    hw = o_hgrn.shape[1]
    mw = o_mla.shape[1]
    tm = min(seq, 256)
    tpb = seq // tm
    gcol = (3 * hk + hw) // hw
    assert (3 * hk + hw) % hw == 0
    row = lambda n: pl.BlockSpec((tm, n), lambda i: (i, 0))
    return pl.pallas_call(
        functools.partial(_outproj_kernel, heads=heads, n_exp=n_exp),
        out_shape=(jax.ShapeDtypeStruct((m, d), F32),
                   jax.ShapeDtypeStruct((m, d), F32),
                   jax.ShapeDtypeStruct((m, LANES), F32),
                   jax.ShapeDtypeStruct((m // seq, n_exp, seq), F32)),
        grid=(m // tm,),
        in_specs=[row(hw),
                  pl.BlockSpec((tm, hw), lambda i: (i, gcol)),
                  pl.BlockSpec((1, hw), lambda i: (0, 0)),
                  row(mw),
                  pl.BlockSpec((hw + mw, d), lambda i: (0, 0), pipeline_mode=pl.Buffered(1)),
                  row(d),
                  pl.BlockSpec((None, 6, d), lambda i: (i // tpb, 0, 0)),
                  pl.BlockSpec((1, d), lambda i: (0, 0)),
                  pl.BlockSpec((d, LANES), lambda i: (0, 0), pipeline_mode=pl.Buffered(1))],
        out_specs=(row(d), row(d), row(LANES),
                   pl.BlockSpec((None, n_exp, tm), lambda i: (i // tpb, 0, i % tpb))),
        scratch_shapes=[pltpu.VMEM((tm, hw + mw), BF16)],
        compiler_params=_cparams("parallel"),
        name="outproj_norm2_router",
    )(o_hgrn, proj, og, o_mla, w_out_b, xf, mod6, g2, wr_pad)


BISECT_STEPS = 64


COMBINE_TILE = 256
COMBINE_WIN = 64


def _topk_kernel(la_ref, slot_se_ref, idx_ref, tab_ref, tri_scr, cum_scr, *, cap, n_exp):
    nrow, seq = la_ref.shape
    ep = slot_se_ref.shape[1]
    rows = 256
    for k in range(seq // rows):
        r = lax.broadcasted_iota(jnp.int32, (rows, seq), 0) + k * rows
        c = lax.broadcasted_iota(jnp.int32, (rows, seq), 1)
        tri_scr[k * rows:(k + 1) * rows, :] = jnp.where(r < c, 1.0, 0.0).astype(BF16)

    def count(mask):
        return jnp.sum(jnp.where(mask, 1.0, 0.0), axis=-1, keepdims=True)

    def body(_, lh):
        lo, hi = lh
        mid = 0.5 * (lo + hi)
        ok = count(la_ref[...] >= mid) >= cap
        return jnp.where(ok, mid, lo), jnp.where(ok, hi, mid)

    la = la_ref[...]
    lo0 = jnp.min(la, axis=-1, keepdims=True)
    lo, hi = lax.fori_loop(0, BISECT_STEPS, body, (lo0, jnp.ones_like(lo0)))
    above = la >= hi
    tie = (la >= lo) & (la < hi)
    need = cap - count(above)
    tri = tri_scr[...]
    rank = _dot(jnp.where(tie, 1.0, 0.0).astype(BF16), tri)
    sel = above | (tie & (rank < need))
    pos = _dot(jnp.where(sel, 1.0, 0.0).astype(BF16), tri)
    slot = jnp.where(sel, pos, -1.0)
    eye = _eye(n_exp, ep)
    for b in range(nrow // n_exp):
        slot_se_ref[b * seq:(b + 1) * seq, :] = _dot_tn(
            slot[b * n_exp:(b + 1) * n_exp, :].astype(BF16), eye)
    cum_scr[...] = pos + jnp.where(sel, 1.0, 0.0)
    lane = lax.broadcasted_iota(jnp.int32, (nrow, cap), 1)

    def slot_body(c, acc):
        cnt = count(cum_scr[...] <= lax.convert_element_type(c, F32))
        return jnp.where(lane == c, cnt, acc)

    idx = lax.fori_loop(0, cap, slot_body, jnp.zeros((nrow, cap), F32), unroll=4)
    idx_ref[...] = idx.astype(jnp.int32)
    tok = lax.broadcasted_iota(jnp.int32, (nrow, seq), 1)
    tlane = lax.broadcasted_iota(jnp.int32, tab_ref.shape, 1)
    tab = jnp.zeros(tab_ref.shape, F32)
    tile = min(seq, COMBINE_TILE)
    for k in range(seq // tile + 1):
        tab = jnp.where(tlane == k, count(sel & (tok < k * tile)), tab)
    tab_ref[...] = tab.astype(jnp.int32)


def _topk(lat, batch, seq, n_exp, cap):
    return pl.pallas_call(
        functools.partial(_topk_kernel, cap=cap, n_exp=n_exp),
        out_shape=(jax.ShapeDtypeStruct((batch * seq, LANES), F32),
                   jax.ShapeDtypeStruct((batch * n_exp, cap), jnp.int32),
                   jax.ShapeDtypeStruct((batch * n_exp, LANES), jnp.int32)),
        scratch_shapes=[pltpu.VMEM((seq, seq), BF16), pltpu.VMEM((batch * n_exp, seq), F32)],
        compiler_params=pltpu.CompilerParams(vmem_limit_bytes=VMEM_LIMIT),
        name="expert_choice_topk",
    )(lat.reshape(batch * n_exp, seq))


def _ffn_kernel(idx_ref, h2_hbm, wg_ref, wu_ref, wd_ref, ye_ref, xe_scr, hmid_scr, sem,
                *, nt, nd, tf):
    e = pl.program_id(0)
    s = pl.program_id(1)
    rows = xe_scr.shape[0]

    def start_gather(expert):
        base = expert * rows

        def body(k, carry):
            r0 = pl.multiple_of(k * TILE, TILE)
            for j in range(TILE):
                pltpu.make_async_copy(h2_hbm.at[pl.ds(idx_ref[base + r0 + j], 1), :],
                                      xe_scr.at[pl.ds(r0 + j, 1), :], sem.at[0]).start()
            return carry
        lax.fori_loop(0, rows // TILE, body, 0)

    @pl.when((e == 0) & (s == 0))
    def _():
        start_gather(0)

    @pl.when(s == 0)
    def _():
        pltpu.make_async_copy(h2_hbm.at[pl.ds(0, rows), :], xe_scr, sem.at[0]).wait()

    @pl.when(s < nt)
    def _():
        xe = xe_scr[...].astype(BF16)
        a = _dot(xe, wg_ref[...].astype(BF16))
        u = _dot(xe, wu_ref[...].astype(BF16))
        hmid_scr[s] = (_silu(a) * u).astype(BF16)

    def down_step(prefetch):
        per = rows // (nd * nt)
        y = None
        for k in range(nt):
            if prefetch:
                dst0 = (s - nt) * (per * nt) + k * per
                first = (e + 1) * rows + dst0
                for j in range(per):
                    pltpu.make_async_copy(h2_hbm.at[pl.ds(idx_ref[first + j], 1), :],
                                          xe_scr.at[pl.ds(dst0 + j, 1), :], sem.at[0]).start()
            part = _dot(hmid_scr[k], wd_ref[k * tf:(k + 1) * tf, :].astype(BF16))
            y = part if y is None else y + part
        ye_ref[...] = y.astype(BF16)

    more = e + 1 < pl.num_programs(0)

    @pl.when((s >= nt) & more)
    def _():
        down_step(True)

    @pl.when((s >= nt) & jnp.logical_not(more))
    def _():
        down_step(False)


def _ffn(idx, h2, w_gate, w_up, w_down):
    n_exp, rows = idx.shape
    idx = idx.reshape(n_exp * rows)
    d = h2.shape[1]
    ff = w_gate.shape[2]
    tf = min(ff, 512)
    tn = min(d, 1024)
    nt = ff // tf
    nd = d // tn
    assert rows % (nd * nt) == 0
    up = lambda e, s, idx: (e, 0, jnp.minimum(s, nt - 1))
    down = lambda e, s, idx: (e, 0, jnp.maximum(s - nt, 0))
    return pl.pallas_call(
        functools.partial(_ffn_kernel, nt=nt, nd=nd, tf=tf),
        out_shape=jax.ShapeDtypeStruct((n_exp, rows, d), BF16),
        grid_spec=pltpu.PrefetchScalarGridSpec(
            num_scalar_prefetch=1,
            grid=(n_exp, nt + nd),
            in_specs=[pl.BlockSpec(memory_space=pl.ANY),
                      pl.BlockSpec((None, d, tf), up),
                      pl.BlockSpec((None, d, tf), up),
                      pl.BlockSpec((None, ff, tn), down)],
            out_specs=pl.BlockSpec((None, rows, tn), down),
            scratch_shapes=[pltpu.VMEM((rows, d), F32),
                            pltpu.VMEM((nt, rows, tf), BF16),
                            pltpu.SemaphoreType.DMA((1,))]),
        compiler_params=_cparams("arbitrary", "arbitrary"),
        name="expert_swiglu",
    )(idx, h2, w_gate, w_up, w_down)


def _combine_kernel(tab_ref, slot_ref, aff_ref, ye_ref, x1_ref, mod_ref, out_ref, y_scr,
                    *, n_exp, cap, win):
    b = pl.program_id(0)
    t = pl.program_id(1)
    tt = x1_ref.shape[0]
    base = (b * (pl.num_programs(1) + 1) + t) * n_exp
    pack = 16
    starts = []
    short = None
    for e in range(n_exp):
        c0 = tab_ref[base + e]
        c1 = tab_ref[base + n_exp + e]
        a = jnp.minimum(c0 & ~(pack - 1), cap - win)
        ok = c1 - a <= win
        starts.append(a)
        short = ok if short is None else short & ok

    def finish(acc):
        out_ref[...] = x1_ref[...] + mod_ref[5:6, :] * acc

    @pl.when(short)
    def _():
        lane = lax.broadcasted_iota(jnp.int32, (tt, LANES), 1).astype(F32)
        per = LANES // win
        blocks = []
        for g in range(n_exp // per):
            blk = jnp.zeros((tt, LANES), F32)
            for j in range(per):
                e = g * per + j
                a = pl.multiple_of(starts[e], pack)
                y_scr[e * win:(e + 1) * win, :] = ye_ref[e, pl.ds(a, win), :]
                slot = slot_ref[:, e:e + 1]
                rel = jnp.where(slot >= 0.0, slot - a.astype(F32) + float(j * win), -1.0)
                blk = jnp.where(lane == rel, aff_ref[:, e:e + 1], blk)
            blocks.append(blk.astype(BF16))
        acc = None
        for g in range(0, len(blocks), 2):
            part = _dot(jnp.concatenate(blocks[g:g + 2], axis=1), y_scr[g * LANES:(g + 2) * LANES, :])
            acc = part if acc is None else acc + part
        finish(acc)

    @pl.when(jnp.logical_not(short))
    def _():
        cidx = lax.broadcasted_iota(jnp.int32, (tt, cap), 1).astype(F32)
        acc = jnp.zeros(x1_ref.shape, F32)
        for e in range(n_exp):
            onehot = jnp.where(cidx == slot_ref[:, e:e + 1], 1.0, 0.0).astype(BF16)
            acc = acc + aff_ref[:, e:e + 1] * _dot(onehot, ye_ref[e])
        finish(acc)


def _combine(tab, slot_se, aff, ye4, x1, mod6, seq, n_exp, cap):
    m, d = x1.shape
    ep = slot_se.shape[1]
    batch = m // seq
    tt = min(seq, COMBINE_TILE)
    tpb = seq // tt
    win = min(COMBINE_WIN, cap)
    assert LANES % win == 0 and n_exp % (LANES // win) == 0 and cap % 16 == 0
    return pl.pallas_call(
        functools.partial(_combine_kernel, n_exp=n_exp, cap=cap, win=win),
        out_shape=jax.ShapeDtypeStruct((m, d), F32),
        grid_spec=pltpu.PrefetchScalarGridSpec(
            num_scalar_prefetch=1,
            grid=(batch, tpb),
            in_specs=[pl.BlockSpec((tt, ep), lambda b, t, tab: (b * tpb + t, 0)),
                      pl.BlockSpec((tt, ep), lambda b, t, tab: (b * tpb + t, 0)),
                      pl.BlockSpec((n_exp, None, cap, d), lambda b, t, tab: (0, b, 0, 0)),
                      pl.BlockSpec((tt, d), lambda b, t, tab: (b * tpb + t, 0)),
                      pl.BlockSpec((None, 6, d), lambda b, t, tab: (b, 0, 0))],
            out_specs=pl.BlockSpec((tt, d), lambda b, t, tab: (b * tpb + t, 0)),
            scratch_shapes=[pltpu.VMEM((n_exp * win, d), BF16)]),
        compiler_params=_cparams("parallel", "arbitrary"),
        name="expert_combine",
    )(tab, slot_se, aff, ye4, x1, mod6)


def kernel(x, c, positions, w_ada, b_ada, norm1_g, w_in, lb_logits, hgrn_out_g, qa_norm_g, w_uq,
           kva_norm_g, w_ukv, q_head_g, k_head_g, w_out, norm2_g, w_router, w_gate, w_up, w_down):
    batch, seq, d = x.shape
    depth = w_ada.shape[0]
    m = batch * seq
    hk = lb_logits.shape[2]
    heads, dv = hgrn_out_g.shape[1], hgrn_out_g.shape[2]
    hw = heads * dv
    ql, kvl = qa_norm_g.shape[1], kva_norm_g.shape[1]
    qk_dim = q_head_g.shape[1]
    mh = w_uq.shape[2] // qk_dim
    d_in = w_in.shape[2]
    rope = d_in - (3 * hk + 2 * hw + ql + kvl)
    nope = qk_dim - rope
    vdim = w_ukv.shape[2] // mh - nope
    n_exp = w_router.shape[2]
    cap = EC_CAPACITY * seq // n_exp
    assert dv == LANES and hk == hw and nope == LANES and vdim == LANES and 2 * rope == LANES
    assert ql + kvl + rope <= hk and seq % GRP == 0 and n_exp <= LANES and cap % 8 == 0

    cq_off = 3 * hk + 2 * hw
    ckv_off = cq_off + ql
    kpe_off = ckv_off + kvl
    swap = jnp.concatenate([jnp.arange(rope // 2, rope), jnp.arange(0, rope // 2)])

    def both(v):
        return jnp.concatenate([v, v[..., swap]], axis=-1)

    cs = _rope_tables(positions, rope)
    c8 = jnp.pad(c, ((0, (-batch) % 8), (0, 0)))
    xf = x.reshape(m, d)
    for l in range(depth):
        mod6 = _ada(c8, w_ada[l], b_ada[l])[:batch].reshape(batch, 6, d)

        w_in_b = w_in[l].astype(BF16)
        wk_b = both(w_in[l][:, kpe_off:kpe_off + rope]).astype(BF16)
        proj, fgate, kpe2 = _inproj(xf, mod6, norm1_g[l].reshape(1, d), w_in_b, wk_b, lb_logits,
                                   seq, hk, l)

        o_hgrn = _hgrn(proj, fgate, batch, seq, heads, hk)

        wq = w_uq[l].reshape(ql, mh, qk_dim)
        wq_all = jnp.concatenate([wq[..., :nope], both(wq[..., nope:])], axis=-1)
        qh, kh, vh = _mla_proj(
            proj, kpe2, cs, qa_norm_g[l].reshape(1, ql), kva_norm_g[l].reshape(1, kvl),
            wq_all.reshape(ql, mh * 2 * LANES).astype(BF16), w_ukv[l].astype(BF16),
            q_head_g[l][:nope].reshape(1, nope), both(q_head_g[l][nope:]).reshape(1, 2 * rope),
            k_head_g[l][:nope].reshape(1, nope), both(k_head_g[l][nope:]).reshape(1, 2 * rope),
            batch, seq, mh, cq_off, ckv_off, qk_dim, rope)
        o_mla = _attention(qh, kh, vh)

        wr_pad = jnp.pad(w_router[l], ((0, 0), (0, LANES - n_exp)))
        x1, h2, aff, lat = _outproj(o_hgrn, proj, hgrn_out_g[l].reshape(1, hw), o_mla,
                                    w_out[l].astype(BF16), xf, mod6, norm2_g[l].reshape(1, d),
                                    wr_pad, seq, heads, hk, n_exp)

        slot_se, idx, tab = _topk(lat, batch, seq, n_exp, cap)
        ntile = seq // min(seq, COMBINE_TILE)
        tab = tab[:, :ntile + 1].reshape(batch, n_exp, ntile + 1).transpose(0, 2, 1).reshape(-1)
        rows = idx.reshape(batch, n_exp, cap) + (jnp.arange(batch, dtype=jnp.int32) * seq)[:, None, None]
        rows = rows.transpose(1, 0, 2).reshape(n_exp, batch * cap)
        ye = _ffn(rows, h2, w_gate[l], w_up[l], w_down[l])
        xf = _combine(tab, slot_se, aff, ye.reshape(n_exp, batch, cap, d), x1, mod6, seq, n_exp, cap)
    return xf.reshape(batch, seq, d)
```

```python
import functools
import math

import jax
import jax.numpy as jnp
from jax import lax
from jax.experimental import pallas as pl
from jax.experimental.pallas import tpu as pltpu

F32 = jnp.float32
BF16 = jnp.bfloat16
EPS = 1e-6
ROPE_BASE = 10000.0
LOG2E = math.log2(math.e)
EC_CAPACITY = 2
LANES = 128
TILE = 8
GRP = 128
VMEM_LIMIT = 56 * 1024 * 1024


def _cparams(*sem):
    return pltpu.CompilerParams(dimension_semantics=sem, vmem_limit_bytes=VMEM_LIMIT)


def _dot(a, b):
    return jnp.dot(a, b, preferred_element_type=F32)


def _dot_nt(a, b):
    return lax.dot_general(a, b, (((1,), (1,)), ((), ())), preferred_element_type=F32)


def _dot_tn(a, b):
    return lax.dot_general(a, b, (((0,), (0,)), ((), ())), preferred_element_type=F32)


def _split2(a):
    hi = a.astype(BF16)
    lo = (a - hi.astype(F32)).astype(BF16)
    return hi, lo


def _split3(a):
    p1 = a.astype(BF16)
    r1 = a - p1.astype(F32)
    p2 = r1.astype(BF16)
    p3 = (r1 - p2.astype(F32)).astype(BF16)
    return p1, p2, p3


def _dot_hi(a, b):
    ah, al = _split2(a)
    bh, bl = _split2(b)
    return _dot(ah, bh) + (_dot(ah, bl) + _dot(al, bh))


def _silu(x):
    return x * jax.nn.sigmoid(x)


def _rope_kernel(pos_ref, cs_ref, *, half):
    pos = pos_ref[...].astype(F32)
    lane = lax.broadcasted_iota(jnp.int32, (1, 4 * half), 1)
    j = (lane & (half - 1)).astype(F32)
    inv_freq = jnp.exp(j * (-2.0 * math.log(ROPE_BASE) / (2 * half)))
    ang = pos * inv_freq
    c = jnp.cos(ang)
    s = jnp.sin(ang)
    cs_ref[...] = jnp.where(lane < 2 * half, c, jnp.where(lane < 3 * half, -s, s))


def _rope_tables(positions, rope):
    m = positions.size
    tm = min(m, 1024)
    half = rope // 2
    return pl.pallas_call(
        functools.partial(_rope_kernel, half=half),
        out_shape=jax.ShapeDtypeStruct((m, 2 * rope), F32),
        grid=(m // tm,),
        in_specs=[pl.BlockSpec((tm, 1), lambda i: (i, 0))],
        out_specs=pl.BlockSpec((tm, 2 * rope), lambda i: (i, 0)),
        compiler_params=_cparams("parallel"),
        name="rope_tables",
    )(positions.reshape(m, 1))


def _ada_kernel(c_ref, w_ref, b_ref, o_ref):
    part = _dot_hi(_silu(c_ref[...]), w_ref[...])

    @pl.when(pl.program_id(0) == 0)
    def _():
        o_ref[...] = part + b_ref[...]

    @pl.when(pl.program_id(0) > 0)
    def _():
        o_ref[...] += part


def _ada(c8, w, b):
    d, n = w.shape
    tk = min(d, 128)
    return pl.pallas_call(
        _ada_kernel,
        out_shape=jax.ShapeDtypeStruct((c8.shape[0], n), F32),
        grid=(d // tk,),
        in_specs=[pl.BlockSpec((c8.shape[0], tk), lambda k: (0, k)),
                  pl.BlockSpec((tk, n), lambda k: (k, 0)),
                  pl.BlockSpec((1, n), lambda k: (0, 0))],
        out_specs=pl.BlockSpec((c8.shape[0], n), lambda k: (0, 0)),
        compiler_params=_cparams("arbitrary"),
        name="ada_mod",
    )(c8, w, b.reshape(1, n))


def _inproj_kernel(x_ref, mod_ref, g_ref, w_ref, wk_ref, lbl_ref,
                   proj_ref, fgate_ref, kpe_ref, h_scr, *, layer):
    j = pl.program_id(1)

    @pl.when(j == 0)
    def _():
        x = x_ref[...]
        r = lax.rsqrt(jnp.mean(x * x, axis=-1, keepdims=True) + EPS)
        h = x * r * g_ref[...] * (1.0 + mod_ref[1:2, :]) + mod_ref[0:1, :]
        hb = h.astype(BF16)
        h_scr[...] = hb
        kpe_ref[...] = _dot(hb, wk_ref[...])

    acc = _dot(h_scr[...], w_ref[...])
    proj_ref[...] = acc.astype(BF16)

    @pl.when((j == 1) | (j == 2))
    def _():
        lg = lbl_ref[...]
        e = jnp.exp(lg - jnp.max(lg, axis=0, keepdims=True))
        lb = jnp.sum(e[:layer + 1], axis=0, keepdims=True) / jnp.sum(e, axis=0, keepdims=True)
        fgate_ref[...] = lb + (1.0 - lb) * jax.nn.sigmoid(acc)


def _inproj(xf, mod6, g1, w_in_b, wk_b, lb_logits, seq, hk, layer):
    m, d = xf.shape
    d_in = w_in_b.shape[1]
    tm = min(seq, 1024)
    tpb = seq // tm
    tn = hk
    nj = pl.cdiv(d_in, tn)
    nl = lb_logits.shape[1]
    fdir = lambda j: jnp.clip(j - 1, 0, 1)
    return pl.pallas_call(
        functools.partial(_inproj_kernel, layer=layer),
        out_shape=(jax.ShapeDtypeStruct((m, d_in), BF16),
                   jax.ShapeDtypeStruct((m, 2 * hk), F32),
                   jax.ShapeDtypeStruct((m, LANES), F32)),
        grid=(m // tm, nj),
        in_specs=[pl.BlockSpec((tm, d), lambda i, j: (i, 0)),
                  pl.BlockSpec((None, 6, d), lambda i, j: (i // tpb, 0, 0)),
                  pl.BlockSpec((1, d), lambda i, j: (0, 0)),
                  pl.BlockSpec((d, tn), lambda i, j: (0, j)),
                  pl.BlockSpec((d, LANES), lambda i, j: (0, 0)),
                  pl.BlockSpec((None, nl, tn), lambda i, j: (fdir(j), 0, 0))],
        out_specs=(pl.BlockSpec((tm, tn), lambda i, j: (i, j)),
                   pl.BlockSpec((tm, tn), lambda i, j: (i, fdir(j))),
                   pl.BlockSpec((tm, LANES), lambda i, j: (i, 0))),
        scratch_shapes=[pltpu.VMEM((tm, d), BF16)],
        compiler_params=_cparams("parallel", "arbitrary"),
        name="norm1_inproj",
    )(xf, mod6, g1, w_in_b, wk_b, lb_logits)


def _tile_scan(g, d):
    rin = lax.broadcasted_iota(jnp.int32, g.shape, 0) & (TILE - 1)
    b = g
    step = 1
    while step < TILE:
        if d == 0:
            b = b + jnp.where(rin >= step, pltpu.roll(b, step, 0), 0.0)
        else:
            b = b + jnp.where(rin < TILE - step, pltpu.roll(b, GRP - step, 0), 0.0)
        step *= 2
    return b


def _group_cumsum(g, d):
    b = _tile_scan(g, d)
    ntile = GRP // TILE
    order = range(ntile) if d == 0 else range(ntile - 1, -1, -1)
    edge = TILE - 1 if d == 0 else 0
    out = [None] * ntile
    carry = None
    for i in order:
        t = b[i * TILE:(i + 1) * TILE]
        out[i] = t if carry is None else t + carry
        tot = t[edge:edge + 1]
        carry = tot if carry is None else carry + tot
    return jnp.concatenate(out, axis=0)


def _boundary(b, h, d):
    idx = h - 1 if d == 0 else h
    if 2 * h >= TILE:
        b3 = b.reshape(GRP // (2 * h), 2 * h, b.shape[1])
        return jnp.broadcast_to(b3[:, idx:idx + 1, :], b3.shape).reshape(b.shape)
    p = lax.broadcasted_iota(jnp.int32, b.shape, 0) & (2 * h - 1)
    out = b
    for pos in range(2 * h):
        shift = pos - idx
        if shift != 0:
            out = jnp.where(p == pos, pltpu.roll(b, shift % GRP, 0), out)
    return out


def _hgrn_kernel(q_ref, v_ref, gf_ref, gb_ref, o_ref, lv_scr, sg_scr, st_scr):
    seq = q_ref.shape[0]
    ngrp = seq // GRP
    nlev = GRP.bit_length()
    g_refs = (gf_ref, gb_ref)
    assert ngrp % 2 == 0

    r = lax.broadcasted_iota(jnp.int32, (GRP, GRP), 0)
    c = lax.broadcasted_iota(jnp.int32, (GRP, GRP), 1)
    lev = jnp.zeros((GRP, GRP), jnp.int32)
    for j in range(nlev - 1):
        lev = lev + jnp.where((r >> j) != (c >> j), 1, 0)
    lv_scr[0] = jnp.where(c <= r, lev, -1)
    lv_scr[1] = jnp.where(c >= r, lev, -1)
    st_scr[...] = jnp.zeros_like(st_scr)
    rr = lax.broadcasted_iota(jnp.int32, (GRP, LANES), 0)
    for l in range(1, nlev):
        late = (rr & (1 << (l - 1))) != 0
        sg_scr[0, l - 1] = jnp.where(late, 1.0, -1.0)
        sg_scr[1, l - 1] = jnp.where(late, -1.0, 1.0)

    def body(i, carry, first):
        for d in (0, 1):
            grp = i if d == 0 else ngrp - 1 - i
            r0 = pl.multiple_of(grp * GRP, GRP)
            f = g_refs[d][pl.ds(r0, GRP), :]
            g = jnp.log2(f)
            qb = q_ref[pl.ds(r0, GRP), :]
            vb = v_ref[pl.ds(r0, GRP), :]
            qf = qb.astype(F32)
            kk = 1.0 - f
            kb = kk.astype(BF16)
            b = _group_cumsum(g, d)
            edge = GRP - 1 if d == 0 else 0
            tot = b[edge:edge + 1]
            lv = lv_scr[d]
            att = jnp.where(lv == 0, _dot_nt(qb, kb), 0.0)
            for l in range(1, nlev):
                x = jnp.exp2((b - _boundary(b, 1 << (l - 1), d)) * sg_scr[d, l - 1]).astype(BF16)
                att = jnp.where(lv == l, _dot_nt(qb * x, kb * x), att)
            st = st_scr[d]
            o = _dot(att.astype(BF16), vb) + _dot_nt((qf * jnp.exp2(b)).astype(BF16), st.astype(BF16))
            if first:
                o_ref[pl.ds(r0, GRP), :] = o
            else:
                o_ref[pl.ds(r0, GRP), :] += o
            st_scr[d] = st * jnp.exp2(tot) + _dot_tn(vb, (kk * jnp.exp2(tot - b)).astype(BF16))
        return carry

    half = ngrp // 2
    lax.fori_loop(0, half, functools.partial(body, first=True), 0, unroll=min(8, half))
    lax.fori_loop(half, ngrp, functools.partial(body, first=False), 0, unroll=min(8, half))


def _hgrn(proj, fgate, batch, seq, heads, hk):
    m = proj.shape[0]
    nh = hk // LANES
    vcol = 3 * nh
    blk = lambda off: pl.BlockSpec((seq, LANES), lambda b, h: (b, off + h))
    out = jax.ShapeDtypeStruct((m, hk), F32)
    return pl.pallas_call(
        _hgrn_kernel,
        out_shape=out,
        grid=(batch, heads),
        in_specs=[blk(0), blk(vcol), blk(0), blk(nh)],
        out_specs=blk(0),
        scratch_shapes=[pltpu.VMEM((2, GRP, GRP), jnp.int32),
                        pltpu.VMEM((2, GRP.bit_length() - 1, GRP, LANES), F32),
                        pltpu.VMEM((2, LANES, LANES), F32)],
        compiler_params=_cparams("parallel", "parallel"),
        name="hgrn2_scan",
    )(proj, proj, fgate, fgate)


def _mla_proj_kernel(cq_ref, ckv_ref, kpe_ref, cs_ref, qag_ref, kvag_ref, wq_ref, wkv_ref,
                     qgn_ref, qgr_ref, kgn_ref, kgr_ref, q_out, k_out, v_out,
                     *, scale, qk_dim, rope, heads):
    cq = cq_ref[...].astype(F32)
    a = (cq * lax.rsqrt(jnp.mean(cq * cq, axis=-1, keepdims=True) + EPS) * qag_ref[...]).astype(BF16)
    ckv = ckv_ref[...].astype(F32)
    c = (ckv * lax.rsqrt(jnp.mean(ckv * ckv, axis=-1, keepdims=True) + EPS)
         * kvag_ref[...]).astype(BF16)
    cs = cs_ref[...]
    lane = lax.broadcasted_iota(jnp.int32, cs.shape, 1)
    lo = lane < rope

    def rope_sumsq(rr):
        return jnp.sum(jnp.where(lo, rr * rr, 0.0), axis=-1, keepdims=True)

    def rotate(rr, gr):
        y = rr * gr * cs
        return y + pltpu.roll(y, rope, 1)

    kpe = kpe_ref[...]
    k_ss = rope_sumsq(kpe)
    k_rot = rotate(kpe, kgr_ref[...])
    for h in range(heads):
        cols = slice(2 * LANES * h, 2 * LANES * (h + 1))
        qall = _dot(a, wq_ref[:, cols])
        kvall = _dot(c, wkv_ref[:, cols])
        base = 0
        qn = qall[:, base:base + LANES]
        qr = qall[:, base + LANES:base + 2 * LANES]
        rq = lax.rsqrt((jnp.sum(qn * qn, axis=-1, keepdims=True) + rope_sumsq(qr)) / qk_dim + EPS) * scale
        q_out[h, :, :LANES] = (qn * qgn_ref[...] * rq).astype(BF16)
        q_out[h, :, LANES:] = jnp.where(lo, rotate(qr, qgr_ref[...]) * rq, 0.0).astype(BF16)
        kn = kvall[:, base:base + LANES]
        rk = lax.rsqrt((jnp.sum(kn * kn, axis=-1, keepdims=True) + k_ss) / qk_dim + EPS)
        k_out[h, :, :LANES] = (kn * kgn_ref[...] * rk).astype(BF16)
        k_out[h, :, LANES:] = jnp.where(lo, k_rot * rk, 0.0).astype(BF16)
        v_out[h] = kvall[:, base + LANES:base + 2 * LANES].astype(BF16)


def _mla_proj(proj, kpe2, cs, qag, kvag, wq_all, wkv_all, qgn, qgr, kgn, kgr,
              batch, seq, heads, cq_off, ckv_off, qk_dim, rope):
    m = proj.shape[0]
    ql, kvl = wq_all.shape[0], wkv_all.shape[0]
    tm = min(seq, 256)
    tpb = seq // tm
    assert cq_off % ql == 0 and ckv_off % kvl == 0
    vec = lambda n: pl.BlockSpec((1, n), lambda i: (0, 0))
    full = lambda w: pl.BlockSpec(w.shape, lambda i: (0, 0))
    hspec = lambda n: pl.BlockSpec((None, heads, tm, n), lambda i: (i // tpb, 0, i % tpb, 0))
    scale = qk_dim ** -0.5 * LOG2E
    return pl.pallas_call(
        functools.partial(_mla_proj_kernel, scale=scale, qk_dim=float(qk_dim), rope=rope, heads=heads),
        out_shape=(jax.ShapeDtypeStruct((batch, heads, seq, 2 * LANES), BF16),
                   jax.ShapeDtypeStruct((batch, heads, seq, 2 * LANES), BF16),
                   jax.ShapeDtypeStruct((batch, heads, seq, LANES), BF16)),
        grid=(m // tm,),
        in_specs=[pl.BlockSpec((tm, ql), lambda i: (i, cq_off // ql)),
                  pl.BlockSpec((tm, kvl), lambda i: (i, ckv_off // kvl)),
                  pl.BlockSpec((tm, LANES), lambda i: (i, 0)),
                  pl.BlockSpec((tm, LANES), lambda i: (i, 0)),
                  vec(ql), vec(kvl), full(wq_all), full(wkv_all),
                  vec(LANES), vec(LANES), vec(LANES), vec(LANES)],
        out_specs=(hspec(2 * LANES), hspec(2 * LANES), hspec(LANES)),
        compiler_params=_cparams("parallel"),
        name="mla_head_proj",
    )(proj, proj, kpe2, cs, qag, kvag, wq_all, wkv_all, qgn, qgr, kgn, kgr)


ATTN_KEYS = 1024


def _attn_kernel(q_ref, k_ref, v_ref, o_ref):
    q = q_ref[...]
    seq = k_ref.shape[0]
    kc = min(ATTN_KEYS, seq)
    m = l = acc = None
    for c in range(seq // kc):
        rows = slice(c * kc, (c + 1) * kc)
        s = _dot_nt(q, k_ref[rows, :])
        mc = jnp.max(s, axis=-1, keepdims=True)
        if c == 0:
            m = mc
            p = jnp.exp2(s - m)
            l = jnp.sum(p, axis=-1, keepdims=True)
            acc = _dot(p.astype(BF16), v_ref[rows, :])
        else:
            m_new = jnp.maximum(m, mc)
            alpha = jnp.exp2(m - m_new)
            p = jnp.exp2(s - m_new)
            l = l * alpha + jnp.sum(p, axis=-1, keepdims=True)
            acc = acc * alpha + _dot(p.astype(BF16), v_ref[rows, :])
            m = m_new
    o_ref[...] = (acc / l).astype(BF16)


def _attention(qh, kh, vh):
    batch, mh, seq, dq = qh.shape
    dv = vh.shape[-1]
    tq = min(seq, 2048)
    nq = seq // tq
    return pl.pallas_call(
        _attn_kernel,
        out_shape=jax.ShapeDtypeStruct((batch * seq, mh * dv), BF16),
        grid=(batch, mh, nq),
        in_specs=[pl.BlockSpec((None, None, tq, dq), lambda b, h, i: (b, h, i, 0)),
                  pl.BlockSpec((None, None, seq, dq), lambda b, h, i: (b, h, 0, 0)),
                  pl.BlockSpec((None, None, seq, dv), lambda b, h, i: (b, h, 0, 0))],
        out_specs=pl.BlockSpec((tq, dv), lambda b, h, i: (b * nq + i, h)),
        compiler_params=_cparams("parallel", "parallel", "arbitrary"),
        name="mla_attention",
    )(qh, kh, vh)


def _eye(rows, cols):
    r = lax.broadcasted_iota(jnp.int32, (rows, cols), 0)
    c = lax.broadcasted_iota(jnp.int32, (rows, cols), 1)
    return jnp.where(r == c, 1.0, 0.0).astype(BF16)


def _outproj_kernel(o_ref, hg_ref, og_ref, om_ref, w_ref, x_ref, mod_ref, g2_ref, wr_ref,
                    x1_ref, h2_ref, aff_ref, lat_ref, mix_scr, *, heads, n_exp):
    hw = o_ref.shape[1]
    o = o_ref[...]
    gate = _silu(hg_ref[...].astype(F32))
    for h in range(heads):
        sl = slice(h * LANES, (h + 1) * LANES)
        oh = o[:, sl]
        r = lax.rsqrt(jnp.mean(oh * oh, axis=-1, keepdims=True) + EPS)
        mix_scr[:, sl] = (oh * r * og_ref[:, sl] * gate[:, sl]).astype(BF16)
    mix_scr[:, hw:] = om_ref[...]
    x1 = x_ref[...] + mod_ref[2:3, :] * _dot(mix_scr[...], w_ref[...])
    x1_ref[...] = x1
    r2 = lax.rsqrt(jnp.mean(x1 * x1, axis=-1, keepdims=True) + EPS)
    h2 = x1 * r2 * g2_ref[...] * (1.0 + mod_ref[4:5, :]) + mod_ref[3:4, :]
    h2_ref[...] = h2
    logits = _dot_hi(h2, wr_ref[...])
    lane = lax.broadcasted_iota(jnp.int32, logits.shape, 1)
    logits = jnp.where(lane < n_exp, logits, -jnp.inf)
    z = logits - jnp.max(logits, axis=-1, keepdims=True)
    p = jnp.exp(z)
    sp = jnp.sum(p, axis=-1, keepdims=True)
    aff_ref[...] = p / sp
    la = jnp.where(lane < n_exp, z - jnp.log(sp), 0.0)
    eye = _eye(n_exp, la.shape[1])
    p1, p2, p3 = _split3(la)
    lat_ref[...] = (_dot_nt(eye, p1) + _dot_nt(eye, p2)) + _dot_nt(eye, p3)


def _outproj(o_hgrn, proj, og, o_mla, w_out_b, xf, mod6, g2, wr_pad, seq, heads, hk, n_exp):
    m, d = xf.shape
    hw = o_hgrn.shape[1]
    mw = o_mla.shape[1]
    tm = min(seq, 256)
    tpb = seq // tm
    gcol = (3 * hk + hw) // hw
    assert (3 * hk + hw) % hw == 0
    row = lambda n: pl.BlockSpec((tm, n), lambda i: (i, 0))
    return pl.pallas_call(
        functools.partial(_outproj_kernel, heads=heads, n_exp=n_exp),
        out_shape=(jax.ShapeDtypeStruct((m, d), F32),
                   jax.ShapeDtypeStruct((m, d), F32),
                   jax.ShapeDtypeStruct((m, LANES), F32),
                   jax.ShapeDtypeStruct((m // seq, n_exp, seq), F32)),
        grid=(m // tm,),
        in_specs=[row(hw),
                  pl.BlockSpec((tm, hw), lambda i: (i, gcol)),
                  pl.BlockSpec((1, hw), lambda i: (0, 0)),
                  row(mw),
                  pl.BlockSpec((hw + mw, d), lambda i: (0, 0), pipeline_mode=pl.Buffered(1)),
                  row(d),
                  pl.BlockSpec((None, 6, d), lambda i: (i // tpb, 0, 0)),
                  pl.BlockSpec((1, d), lambda i: (0, 0)),
                  pl.BlockSpec((d, LANES), lambda i: (0, 0), pipeline_mode=pl.Buffered(1))],
        out_specs=(row(d), row(d), row(LANES),
                   pl.BlockSpec((None, n_exp, tm), lambda i: (i // tpb, 0, i % tpb))),
        scratch_shapes=[pltpu.VMEM((tm, hw + mw), BF16)],
        compiler_params=_cparams("parallel"),
        name="outproj_norm2_router",
    )(o_hgrn, proj, og, o_mla, w_out_b, xf, mod6, g2, wr_pad)


BISECT_STEPS = 64


COMBINE_TILE = 256
COMBINE_WIN = 64


def _topk_kernel(la_ref, slot_se_ref, idx_ref, tab_ref, tri_scr, cum_scr, *, cap, n_exp):
    nrow, seq = la_ref.shape
    ep = slot_se_ref.shape[1]
    rows = 256
    for k in range(seq // rows):
        r = lax.broadcasted_iota(jnp.int32, (rows, seq), 0) + k * rows
        c = lax.broadcasted_iota(jnp.int32, (rows, seq), 1)
        tri_scr[k * rows:(k + 1) * rows, :] = jnp.where(r < c, 1.0, 0.0).astype(BF16)

    def count(mask):
        return jnp.sum(jnp.where(mask, 1.0, 0.0), axis=-1, keepdims=True)

    def body(_, lh):
        lo, hi = lh
        mid = 0.5 * (lo + hi)
        ok = count(la_ref[...] >= mid) >= cap
        return jnp.where(ok, mid, lo), jnp.where(ok, hi, mid)

    la = la_ref[...]
    lo0 = jnp.min(la, axis=-1, keepdims=True)
    lo, hi = lax.fori_loop(0, BISECT_STEPS, body, (lo0, jnp.ones_like(lo0)))
    above = la >= hi
    tie = (la >= lo) & (la < hi)
    need = cap - count(above)
    tri = tri_scr[...]
    rank = _dot(jnp.where(tie, 1.0, 0.0).astype(BF16), tri)
    sel = above | (tie & (rank < need))
    pos = _dot(jnp.where(sel, 1.0, 0.0).astype(BF16), tri)
    slot = jnp.where(sel, pos, -1.0)
    eye = _eye(n_exp, ep)
    for b in range(nrow // n_exp):
        slot_se_ref[b * seq:(b + 1) * seq, :] = _dot_tn(
            slot[b * n_exp:(b + 1) * n_exp, :].astype(BF16), eye)
    cum_scr[...] = pos + jnp.where(sel, 1.0, 0.0)
    lane = lax.broadcasted_iota(jnp.int32, (nrow, cap), 1)

    def slot_body(c, acc):
        cnt = count(cum_scr[...] <= lax.convert_element_type(c, F32))
        return jnp.where(lane == c, cnt, acc)

    idx = lax.fori_loop(0, cap, slot_body, jnp.zeros((nrow, cap), F32), unroll=4)
    idx_ref[...] = idx.astype(jnp.int32)
    tok = lax.broadcasted_iota(jnp.int32, (nrow, seq), 1)
    tlane = lax.broadcasted_iota(jnp.int32, tab_ref.shape, 1)
    tab = jnp.zeros(tab_ref.shape, F32)
    tile = min(seq, COMBINE_TILE)
    for k in range(seq // tile + 1):
        tab = jnp.where(tlane == k, count(sel & (tok < k * tile)), tab)
    tab_ref[...] = tab.astype(jnp.int32)


def _topk(lat, batch, seq, n_exp, cap):
    return pl.pallas_call(
        functools.partial(_topk_kernel, cap=cap, n_exp=n_exp),
        out_shape=(jax.ShapeDtypeStruct((batch * seq, LANES), F32),
                   jax.ShapeDtypeStruct((batch * n_exp, cap), jnp.int32),
                   jax.ShapeDtypeStruct((batch * n_exp, LANES), jnp.int32)),
        scratch_shapes=[pltpu.VMEM((seq, seq), BF16), pltpu.VMEM((batch * n_exp, seq), F32)],
        compiler_params=pltpu.CompilerParams(vmem_limit_bytes=VMEM_LIMIT),
        name="expert_choice_topk",
    )(lat.reshape(batch * n_exp, seq))


def _ffn_kernel(idx_ref, h2_hbm, wg_ref, wu_ref, wd_ref, ye_ref, xe_scr, hmid_scr, sem,
                *, nt, nd, tf):
    e = pl.program_id(0)
    s = pl.program_id(1)
    rows = xe_scr.shape[0]

    def start_gather(expert):
        base = expert * rows

        def body(k, carry):
            r0 = pl.multiple_of(k * TILE, TILE)
            for j in range(TILE):
                pltpu.make_async_copy(h2_hbm.at[pl.ds(idx_ref[base + r0 + j], 1), :],
                                      xe_scr.at[pl.ds(r0 + j, 1), :], sem.at[0]).start(priority=j % 2)
            return carry
        lax.fori_loop(0, rows // TILE, body, 0)

    @pl.when((e == 0) & (s == 0))
    def _():
        start_gather(0)

    @pl.when(s == 0)
    def _():
        pltpu.make_async_copy(h2_hbm.at[pl.ds(0, rows), :], xe_scr, sem.at[0]).wait()

    @pl.when(s < nt)
    def _():
        xe = xe_scr[...].astype(BF16)
        a = _dot(xe, wg_ref[...].astype(BF16))
        u = _dot(xe, wu_ref[...].astype(BF16))
        hmid_scr[s] = (_silu(a) * u).astype(BF16)

    def down_step(prefetch):
        per = rows // (nd * nt)
        y = None
        for k in range(nt):
            if prefetch:
                dst0 = (s - nt) * (per * nt) + k * per
                first = (e + 1) * rows + dst0
                for j in range(per):
                    pltpu.make_async_copy(h2_hbm.at[pl.ds(idx_ref[first + j], 1), :],
                                          xe_scr.at[pl.ds(dst0 + j, 1), :], sem.at[0]).start(priority=j % 2)
            part = _dot(hmid_scr[k], wd_ref[k * tf:(k + 1) * tf, :].astype(BF16))
            y = part if y is None else y + part
        ye_ref[...] = y.astype(BF16)

    more = e + 1 < pl.num_programs(0)

    @pl.when((s >= nt) & more)
    def _():
        down_step(True)

    @pl.when((s >= nt) & jnp.logical_not(more))
    def _():
        down_step(False)


def _ffn(idx, h2, w_gate, w_up, w_down):
    n_exp, rows = idx.shape
    idx = idx.reshape(n_exp * rows)
    d = h2.shape[1]
    ff = w_gate.shape[2]
    tf = min(ff, 512)
    tn = min(d, 1024)
    nt = ff // tf
    nd = d // tn
    assert rows % (nd * nt) == 0
    up = lambda e, s, idx: (e, 0, jnp.minimum(s, nt - 1))
    down = lambda e, s, idx: (e, 0, jnp.maximum(s - nt, 0))
    return pl.pallas_call(
        functools.partial(_ffn_kernel, nt=nt, nd=nd, tf=tf),
        out_shape=jax.ShapeDtypeStruct((n_exp, rows, d), BF16),
        grid_spec=pltpu.PrefetchScalarGridSpec(
            num_scalar_prefetch=1,
            grid=(n_exp, nt + nd),
            in_specs=[pl.BlockSpec(memory_space=pl.ANY),
                      pl.BlockSpec((None, d, tf), up),
                      pl.BlockSpec((None, d, tf), up),
                      pl.BlockSpec((None, ff, tn), down)],
            out_specs=pl.BlockSpec((None, rows, tn), down),
            scratch_shapes=[pltpu.VMEM((rows, d), F32),
                            pltpu.VMEM((nt, rows, tf), BF16),
                            pltpu.SemaphoreType.DMA((1,))]),
        compiler_params=_cparams("arbitrary", "arbitrary"),
        name="expert_swiglu",
    )(idx, h2, w_gate, w_up, w_down)


def _combine_kernel(tab_ref, slot_ref, aff_ref, ye_ref, x1_ref, mod_ref, out_ref, y_scr,
                    *, n_exp, cap, win):
    b = pl.program_id(0)
    t = pl.program_id(1)
    tt = x1_ref.shape[0]
    base = (b * (pl.num_programs(1) + 1) + t) * n_exp
    pack = 16
    starts = []
    short = None
    for e in range(n_exp):
        c0 = tab_ref[base + e]
        c1 = tab_ref[base + n_exp + e]
        a = jnp.minimum(c0 & ~(pack - 1), cap - win)
        ok = c1 - a <= win
        starts.append(a)
        short = ok if short is None else short & ok

    def finish(acc):
        out_ref[...] = x1_ref[...] + mod_ref[5:6, :] * acc

    @pl.when(short)
    def _():
        lane = lax.broadcasted_iota(jnp.int32, (tt, LANES), 1).astype(F32)
        per = LANES // win
        blocks = []
        for g in range(n_exp // per):
            blk = jnp.zeros((tt, LANES), F32)
            for j in range(per):
                e = g * per + j
                a = pl.multiple_of(starts[e], pack)
                y_scr[e * win:(e + 1) * win, :] = ye_ref[e, pl.ds(a, win), :]
                slot = slot_ref[:, e:e + 1]
                rel = jnp.where(slot >= 0.0, slot - a.astype(F32) + float(j * win), -1.0)
                blk = jnp.where(lane == rel, aff_ref[:, e:e + 1], blk)
            blocks.append(blk.astype(BF16))
        acc = None
        for g in range(0, len(blocks), 2):
            part = _dot(jnp.concatenate(blocks[g:g + 2], axis=1), y_scr[g * LANES:(g + 2) * LANES, :])
            acc = part if acc is None else acc + part
        finish(acc)

    @pl.when(jnp.logical_not(short))
    def _():
        cidx = lax.broadcasted_iota(jnp.int32, (tt, cap), 1).astype(F32)
        acc = jnp.zeros(x1_ref.shape, F32)
        for e in range(n_exp):
            onehot = jnp.where(cidx == slot_ref[:, e:e + 1], 1.0, 0.0).astype(BF16)
            acc = acc + aff_ref[:, e:e + 1] * _dot(onehot, ye_ref[e])
        finish(acc)


def _combine(tab, slot_se, aff, ye4, x1, mod6, seq, n_exp, cap):
    m, d = x1.shape
    ep = slot_se.shape[1]
    batch = m // seq
    tt = min(seq, COMBINE_TILE)
    tpb = seq // tt
    win = min(COMBINE_WIN, cap)
    assert LANES % win == 0 and n_exp % (LANES // win) == 0 and cap % 16 == 0
    return pl.pallas_call(
        functools.partial(_combine_kernel, n_exp=n_exp, cap=cap, win=win),
        out_shape=jax.ShapeDtypeStruct((m, d), F32),
        grid_spec=pltpu.PrefetchScalarGridSpec(
            num_scalar_prefetch=1,
            grid=(batch, tpb),
            in_specs=[pl.BlockSpec((tt, ep), lambda b, t, tab: (b * tpb + t, 0)),
                      pl.BlockSpec((tt, ep), lambda b, t, tab: (b * tpb + t, 0)),
                      pl.BlockSpec((n_exp, None, cap, d), lambda b, t, tab: (0, b, 0, 0)),
                      pl.BlockSpec((tt, d), lambda b, t, tab: (b * tpb + t, 0)),
                      pl.BlockSpec((None, 6, d), lambda b, t, tab: (b, 0, 0))],
            out_specs=pl.BlockSpec((tt, d), lambda b, t, tab: (b * tpb + t, 0)),
            scratch_shapes=[pltpu.VMEM((n_exp * win, d), BF16)]),
        compiler_params=_cparams("parallel", "arbitrary"),
        name="expert_combine",
    )(tab, slot_se, aff, ye4, x1, mod6)


def kernel(x, c, positions, w_ada, b_ada, norm1_g, w_in, lb_logits, hgrn_out_g, qa_norm_g, w_uq,
           kva_norm_g, w_ukv, q_head_g, k_head_g, w_out, norm2_g, w_router, w_gate, w_up, w_down):
    batch, seq, d = x.shape
    depth = w_ada.shape[0]
    m = batch * seq
    hk = lb_logits.shape[2]
    heads, dv = hgrn_out_g.shape[1], hgrn_out_g.shape[2]
    hw = heads * dv
    ql, kvl = qa_norm_g.shape[1], kva_norm_g.shape[1]
    qk_dim = q_head_g.shape[1]
    mh = w_uq.shape[2] // qk_dim
    d_in = w_in.shape[2]
    rope = d_in - (3 * hk + 2 * hw + ql + kvl)
    nope = qk_dim - rope
    vdim = w_ukv.shape[2] // mh - nope
    n_exp = w_router.shape[2]
    cap = EC_CAPACITY * seq // n_exp
    assert dv == LANES and hk == hw and nope == LANES and vdim == LANES and 2 * rope == LANES
    assert ql + kvl + rope <= hk and seq % GRP == 0 and n_exp <= LANES and cap % 8 == 0

    cq_off = 3 * hk + 2 * hw
    ckv_off = cq_off + ql
    kpe_off = ckv_off + kvl
    swap = jnp.concatenate([jnp.arange(rope // 2, rope), jnp.arange(0, rope // 2)])

    def both(v):
        return jnp.concatenate([v, v[..., swap]], axis=-1)

    cs = _rope_tables(positions, rope)
    c8 = jnp.pad(c, ((0, (-batch) % 8), (0, 0)))
    xf = x.reshape(m, d)
    for l in range(depth):
        mod6 = _ada(c8, w_ada[l], b_ada[l])[:batch].reshape(batch, 6, d)

        w_in_b = w_in[l].astype(BF16)
        wk_b = both(w_in[l][:, kpe_off:kpe_off + rope]).astype(BF16)
        proj, fgate, kpe2 = _inproj(xf, mod6, norm1_g[l].reshape(1, d), w_in_b, wk_b, lb_logits,
                                   seq, hk, l)

        o_hgrn = _hgrn(proj, fgate, batch, seq, heads, hk)

        wq = w_uq[l].reshape(ql, mh, qk_dim)
        wq_all = jnp.concatenate([wq[..., :nope], both(wq[..., nope:])], axis=-1)
        qh, kh, vh = _mla_proj(
            proj, kpe2, cs, qa_norm_g[l].reshape(1, ql), kva_norm_g[l].reshape(1, kvl),
            wq_all.reshape(ql, mh * 2 * LANES).astype(BF16), w_ukv[l].astype(BF16),
            q_head_g[l][:nope].reshape(1, nope), both(q_head_g[l][nope:]).reshape(1, 2 * rope),
            k_head_g[l][:nope].reshape(1, nope), both(k_head_g[l][nope:]).reshape(1, 2 * rope),
            batch, seq, mh, cq_off, ckv_off, qk_dim, rope)
        o_mla = _attention(qh, kh, vh)

        wr_pad = jnp.pad(w_router[l], ((0, 0), (0, LANES - n_exp)))
        x1, h2, aff, lat = _outproj(o_hgrn, proj, hgrn_out_g[l].reshape(1, hw), o_mla,
                                    w_out[l].astype(BF16), xf, mod6, norm2_g[l].reshape(1, d),
                                    wr_pad, seq, heads, hk, n_exp)

        slot_se, idx, tab = _topk(lat, batch, seq, n_exp, cap)
        ntile = seq // min(seq, COMBINE_TILE)
        tab = tab[:, :ntile + 1].reshape(batch, n_exp, ntile + 1).transpose(0, 2, 1).reshape(-1)
        rows = idx.reshape(batch, n_exp, cap) + (jnp.arange(batch, dtype=jnp.int32) * seq)[:, None, None]
        rows = rows.transpose(1, 0, 2).reshape(n_exp, batch * cap)
        ye = _ffn(rows, h2, w_gate[l], w_up[l], w_down[l])
        xf = _combine(tab, slot_se, aff, ye.reshape(n_exp, batch, cap, d), x1, mod6, seq, n_exp, cap)
    return xf.reshape(batch, seq, d)
```
